```python
import jax, jax.numpy as jnp
from jax import lax
import numpy as np

D_MODEL = 1024
BATCH = 8
SEQ = 4096
DEPTH = 1

N_HEADS = 8
N_KV_HEADS = 2
HEAD_DIM = 64
GQA_GROUP = N_HEADS // N_KV_HEADS
ATTN_DIM = N_HEADS * HEAD_DIM
KV_DIM = N_KV_HEADS * HEAD_DIM
WINDOW = 128
ATTN_BLOCK = 128
ROT_DIM = HEAD_DIM // 4
ROPE_THETA = 500000.0
CONV_DIM = D_MODEL // 2
CONV_WIDTH = 3
SPLIT_SIZES = (ATTN_DIM, KV_DIM, KV_DIM, CONV_DIM, CONV_DIM, CONV_DIM, D_MODEL, D_MODEL)
PROJ_WIDTH = sum(SPLIT_SIZES)
SPLIT_POINTS = tuple(int(v) for v in np.cumsum(SPLIT_SIZES)[:-1])
N_GROUPS = 4
EXPERTS_PER_GROUP = 8
N_EXPERTS = N_GROUPS * EXPERTS_PER_GROUP
TOP_K = 2
D_FF_EXPERT = D_MODEL // 4
MOE_BLOCK = 128
NORM_EPS = 1e-6
MASK_VALUE = -1e30

kernel_name = "hybrid_swa_shortconv_hiermoe_encoder"


def rmsnorm(x, g):
    xf = x.astype(jnp.float32)
    xf = xf * lax.rsqrt(jnp.mean(xf * xf, axis=-1, keepdims=True) + NORM_EPS)
    return xf.astype(x.dtype) * g


def rope_tables(seq_len):
    inv_freq = ROPE_THETA ** (-jnp.arange(0, ROT_DIM, 2, dtype=jnp.float32) / ROT_DIM)
    ang = jnp.arange(seq_len, dtype=jnp.float32)[:, None] * inv_freq[None, :]
    return jnp.cos(ang), jnp.sin(ang)


def partial_rope(x, cos, sin):
    half = ROT_DIM // 2
    c = cos[None, :, None, :].astype(x.dtype)
    s = sin[None, :, None, :].astype(x.dtype)
    x1 = x[..., :half]
    x2 = x[..., half:ROT_DIM]
    return jnp.concatenate([x1 * c - x2 * s, x2 * c + x1 * s, x[..., ROT_DIM:]], axis=-1)


def band_keys(t):
    b, s, hk, hd = t.shape
    nb = s // ATTN_BLOCK
    tp = jnp.pad(t, ((0, 0), (WINDOW, WINDOW), (0, 0), (0, 0))).reshape(b, nb + 2, ATTN_BLOCK, hk, hd)
    return jnp.concatenate([tp[:, :-2], tp[:, 1:-1], tp[:, 2:]], axis=2)


def windowed_gqa(q, k, v, sink):
    b, s, _, hd = q.shape
    nb = s // ATTN_BLOCK
    qb = q.reshape(b, nb, ATTN_BLOCK, N_KV_HEADS, GQA_GROUP, hd)
    kw = band_keys(k)
    vw = band_keys(v)
    scores = jnp.einsum('bnqhgd,bnkhd->bnhgqk', qb, kw,
                        preferred_element_type=jnp.float32) * (HEAD_DIM ** -0.5)
    blk = jnp.arange(nb)[:, None, None] * ATTN_BLOCK
    qpos = blk + jnp.arange(ATTN_BLOCK)[None, :, None]
    kpos = blk - WINDOW + jnp.arange(3 * ATTN_BLOCK)[None, None, :]
    valid = (jnp.abs(qpos - kpos) <= WINDOW) & (kpos >= 0) & (kpos < s)
    scores = jnp.where(valid[None, :, None, None], scores, MASK_VALUE)
    sink_b = sink.astype(jnp.float32).reshape(1, 1, N_KV_HEADS, GQA_GROUP, 1, 1)
    m = jnp.maximum(jnp.max(scores, axis=-1, keepdims=True), sink_b)
    e = jnp.exp(scores - m)
    p = e / (jnp.sum(e, axis=-1, keepdims=True) + jnp.exp(sink_b - m))
    o = jnp.einsum('bnhgqk,bnkhd->bnqhgd', p.astype(v.dtype), vw)
    return o.reshape(b, s, N_HEADS * hd)


def short_gated_conv(gate_b, gate_c, xc, conv_w, conv_b):
    u = gate_c * xc
    up = jnp.pad(u, ((0, 0), (1, 1), (0, 0)))
    conv = up[:, :-2] * conv_w[0] + up[:, 1:-1] * conv_w[1] + up[:, 2:] * conv_w[2] + conv_b
    return gate_b * conv


def hierarchical_moe(h, w_rg, b_rg, w_re, b_re, w_gate, w_up, w_down):
    b, s, d = h.shape
    t = b * s
    hf = h.reshape(t, d)
    hf32 = hf.astype(jnp.float32)
    g_logits = hf32 @ w_rg.astype(jnp.float32) + b_rg.astype(jnp.float32)
    grp = jnp.argmax(g_logits, axis=-1)
    p_grp = jnp.take_along_axis(jax.nn.softmax(g_logits, axis=-1), grp[:, None], axis=-1)
    e_logits = (hf32 @ w_re.astype(jnp.float32) + b_re.astype(jnp.float32)).reshape(t, N_GROUPS, EXPERTS_PER_GROUP)
    e_in = jnp.take_along_axis(e_logits, grp[:, None, None], axis=1)[:, 0]
    top_p, top_i = lax.top_k(jax.nn.softmax(e_in, axis=-1), TOP_K)
    top_p = top_p / jnp.sum(top_p, axis=-1, keepdims=True)
    gates = (p_grp * top_p).reshape(t * TOP_K)
    expert = (grp[:, None] * EXPERTS_PER_GROUP + top_i).reshape(t * TOP_K)
    n_assign = t * TOP_K
    order = jnp.argsort(expert)
    e_sorted = expert[order]
    tok = order // TOP_K
    counts = jnp.bincount(expert, length=N_EXPERTS)
    starts = jnp.cumsum(counts) - counts
    padded = ((counts + MOE_BLOCK - 1) // MOE_BLOCK) * MOE_BLOCK
    pad_ends = jnp.cumsum(padded)
    pad_starts = pad_ends - padded
    dest = pad_starts[e_sorted] + (jnp.arange(n_assign) - starts[e_sorted])
    n_rows = n_assign + N_EXPERTS * MOE_BLOCK
    n_blocks = n_rows // MOE_BLOCK
    x_pad = jnp.zeros((n_rows, d), h.dtype).at[dest].set(hf[tok])
    block_e = jnp.minimum(jnp.searchsorted(pad_ends, jnp.arange(n_blocks) * MOE_BLOCK, side='right'),
                          N_EXPERTS - 1)

    def expert_block(args):
        xb, e = args
        hid = jax.nn.silu(xb @ w_gate[e]) * (xb @ w_up[e])
        return hid @ w_down[e]

    y_pad = lax.map(expert_block, (x_pad.reshape(n_blocks, MOE_BLOCK, d), block_e)).reshape(n_rows, d)
    y = y_pad[dest] * gates[order][:, None].astype(h.dtype)
    out = jnp.zeros((t, d), h.dtype).at[tok].add(y)
    return out.reshape(b, s, d)


def setup_inputs(seed: int = 0) -> dict:
    key = jax.random.key(seed)
    ks = jax.random.split(key, 20)
    f32 = jnp.float32

    def nrm(k, shape, fan_in):
        return jax.random.normal(k, shape, f32) * (fan_in ** -0.5)

    def gain(k, shape):
        return 1.0 + 0.02 * jax.random.normal(k, shape, f32)

    return {
        "x": jax.random.normal(ks[0], (BATCH, SEQ, D_MODEL), f32),
        "norm1_g": gain(ks[1], (DEPTH, D_MODEL)),
        "w_in": nrm(ks[2], (DEPTH, D_MODEL, PROJ_WIDTH), D_MODEL),
        "q_norm_g": gain(ks[3], (DEPTH, HEAD_DIM)),
        "k_norm_g": gain(ks[4], (DEPTH, HEAD_DIM)),
        "attn_sink": 0.5 * jax.random.normal(ks[5], (DEPTH, N_HEADS), f32),
        "conv_w": nrm(ks[6], (DEPTH, CONV_WIDTH, CONV_DIM), CONV_WIDTH),
        "conv_b": 0.02 * jax.random.normal(ks[7], (DEPTH, CONV_DIM), f32),
        "w_attn_proj": nrm(ks[8], (DEPTH, ATTN_DIM, D_MODEL), ATTN_DIM),
        "w_conv_proj": nrm(ks[9], (DEPTH, CONV_DIM, D_MODEL), CONV_DIM),
        "w_out": nrm(ks[10], (DEPTH, D_MODEL, D_MODEL), D_MODEL),
        "norm2_g": gain(ks[11], (DEPTH, D_MODEL)),
        "w_router_group": nrm(ks[12], (DEPTH, D_MODEL, N_GROUPS), D_MODEL),
        "b_router_group": 0.01 * jax.random.normal(ks[13], (DEPTH, N_GROUPS), f32),
        "w_router_expert": nrm(ks[14], (DEPTH, D_MODEL, N_EXPERTS), D_MODEL),
        "b_router_expert": 0.01 * jax.random.normal(ks[15], (DEPTH, N_EXPERTS), f32),
        "w_gate_e": nrm(ks[16], (DEPTH, N_EXPERTS, D_MODEL, D_FF_EXPERT), D_MODEL),
        "w_up_e": nrm(ks[17], (DEPTH, N_EXPERTS, D_MODEL, D_FF_EXPERT), D_MODEL),
        "w_down_e": nrm(ks[18], (DEPTH, N_EXPERTS, D_FF_EXPERT, D_MODEL), D_FF_EXPERT),
    }


def reference(x, norm1_g, w_in, q_norm_g, k_norm_g, attn_sink, conv_w, conv_b,
              w_attn_proj, w_conv_proj, w_out, norm2_g, w_router_group, b_router_group,
              w_router_expert, b_router_expert, w_gate_e, w_up_e, w_down_e):
    b, s, _ = x.shape
    cos, sin = rope_tables(s)
    for l in range(DEPTH):
        h = rmsnorm(x, norm1_g[l])
        proj = h @ w_in[l]
        q, k, v, cb, cc, cx, g_attn, g_conv = jnp.split(proj, SPLIT_POINTS, axis=-1)
        q = rmsnorm(q.reshape(b, s, N_HEADS, HEAD_DIM), q_norm_g[l])
        k = rmsnorm(k.reshape(b, s, N_KV_HEADS, HEAD_DIM), k_norm_g[l])
        v = v.reshape(b, s, N_KV_HEADS, HEAD_DIM)
        q = partial_rope(q, cos, sin)
        k = partial_rope(k, cos, sin)
        attn = windowed_gqa(q, k, v, attn_sink[l])
        conv = short_gated_conv(cb, cc, cx, conv_w[l], conv_b[l])
        merged = (jax.nn.sigmoid(g_attn) * (attn @ w_attn_proj[l])
                  + jax.nn.sigmoid(g_conv) * (conv @ w_conv_proj[l]))
        x = x + merged @ w_out[l]
        h2 = rmsnorm(x, norm2_g[l])
        x = x + hierarchical_moe(h2, w_router_group[l], b_router_group[l],
                                 w_router_expert[l], b_router_expert[l],
                                 w_gate_e[l], w_up_e[l], w_down_e[l])
    return x
```

```python
import functools

import jax
import jax.numpy as jnp
import numpy as np
from jax import lax
from jax.experimental import pallas as pl
from jax.experimental.pallas import tpu as pltpu

F32 = jnp.float32
BF16 = jnp.bfloat16
I32 = jnp.int32

D_MODEL = 1024
N_HEADS = 8
N_KV_HEADS = 2
HEAD_DIM = 64
ATTN_DIM = N_HEADS * HEAD_DIM
KV_DIM = N_KV_HEADS * HEAD_DIM
WINDOW = 128
ROT_DIM = HEAD_DIM // 4
ROPE_THETA = 500000.0
CONV_DIM = D_MODEL // 2
N_GROUPS = 4
EXPERTS_PER_GROUP = 8
N_EXPERTS = N_GROUPS * EXPERTS_PER_GROUP
D_FF = D_MODEL // 4
NORM_EPS = 1e-6
MASK_VALUE = -1e30
NEG_BIG = -3.0e38

LANES = 128
ROUTER_ROWS = 8 + N_EXPERTS
MOE_ROWS = 256
VMEM_LIMIT = 56 * 1024 * 1024

_Q0, _K0, _V0, _CB0, _CC0, _CX0, _GA0, _GC0, _END = (
    0, 512, 640, 768, 1280, 1792, 2304, 3328, 4352)


def _rope_chunks(t, c, s1, s2):
    outs = []
    for j in range(t.shape[1] // LANES):
        tj = t[:, j * LANES:(j + 1) * LANES]
        outs.append(tj * c + pltpu.roll(tj, LANES - ROT_DIM // 2, 1) * s1
                    + pltpu.roll(tj, ROT_DIM // 2, 1) * s2)
    return outs[0] if len(outs) == 1 else jnp.concatenate(outs, axis=1)


def _proj_kernel(x_ref, g1_ref, w_ref, bd_ref, qg_ref, kg_ref, c_ref, s1_ref, s2_ref,
                 q_ref, k_ref, v_ref, cb_ref, u_ref, ga_ref, gc_ref):
    x = x_ref[...]
    ms = jnp.mean(x * x, axis=-1, keepdims=True)
    h = (x * lax.rsqrt(ms + NORM_EPS) * g1_ref[...]).astype(BF16)

    def proj(a, b):
        return jnp.dot(h, w_ref[:, a:b], preferred_element_type=F32)

    c, s1, s2 = c_ref[...], s1_ref[...], s2_ref[...]

    q = proj(_Q0, _K0)
    qms = jnp.dot((q * q).astype(BF16), bd_ref[...], preferred_element_type=F32)
    qg = jnp.concatenate([qg_ref[...]] * (ATTN_DIM // LANES), axis=1)
    qn = q * lax.rsqrt(qms + NORM_EPS) * qg
    q_ref[...] = (_rope_chunks(qn, c, s1, s2) * (HEAD_DIM ** -0.5)).astype(BF16)

    k = proj(_K0, _V0)
    kms = jnp.dot((k * k).astype(BF16), bd_ref[0:KV_DIM, 0:KV_DIM], preferred_element_type=F32)
    kn = k * lax.rsqrt(kms + NORM_EPS) * kg_ref[...]
    k_ref[...] = _rope_chunks(kn, c, s1, s2).astype(BF16)

    v_ref[...] = proj(_V0, _CB0).astype(BF16)
    cb_ref[...] = proj(_CB0, _CC0).astype(BF16)
    u_ref[...] = (proj(_CC0, _CX0) * proj(_CX0, _GA0)).astype(BF16)
    ga_ref[...] = jax.nn.sigmoid(proj(_GA0, _GC0)).astype(BF16)
    gc_ref[...] = jax.nn.sigmoid(proj(_GC0, _END)).astype(BF16)


def _mix_kernel(sink_ref, x_ref, q_ref, kp_ref, kc_ref, kn_ref, vp_ref, vc_ref, vn_ref,
                cb_ref, up_ref, uc_ref, un_ref, ga_ref, gc_ref,
                cw_ref, cbias_ref, wa_ref, wc_ref, wo_ref, g2_ref,
                wrh_ref, wrl_ref, br_ref, tri_ref,
                x1_ref, h2_ref, ri_ref, rf_ref, cnt_ref,
                ubuf, base_ref, *, tq, n_seq_blocks):
    b = pl.program_id(0)
    j = pl.program_id(1)
    nj = pl.num_programs(1)
    nsub = tq // WINDOW

    @pl.when((b == 0) & (j == 0))
    def _():
        base_ref[...] = jnp.zeros_like(base_ref)

    kbuf = jnp.concatenate([kp_ref[...], kc_ref[...], kn_ref[...]], axis=0)
    vbuf = jnp.concatenate([vp_ref[...], vc_ref[...], vn_ref[...]], axis=0)
    lane = lax.broadcasted_iota(I32, (WINDOW, LANES), 1)
    low_half = lane < HEAD_DIM
    qrow = lax.broadcasted_iota(I32, (WINDOW, 3 * WINDOW), 0)
    kcol = lax.broadcasted_iota(I32, (WINDOW, 3 * WINDOW), 1)
    dd = kcol - qrow
    band = (dd >= 0) & (dd <= 2 * WINDOW)
    attn_rows = []
    for n in range(nsub):
        lo_lim = jnp.where(j > 0, 0, WINDOW) if n == 0 else 0
        hi_lim = jnp.where(j < nj - 1, 3 * WINDOW, 2 * WINDOW) if n == nsub - 1 else 3 * WINDOW
        valid = band & (kcol >= lo_lim) & (kcol < hi_lim)
        qs = q_ref[n * WINDOW:(n + 1) * WINDOW, :]
        zero = jnp.zeros((WINDOW, LANES), BF16)
        q_stack = []
        for hd in range(N_HEADS):
            col = qs[:, (hd % 4) * LANES:(hd % 4 + 1) * LANES]
            keep = low_half if hd < 4 else jnp.logical_not(low_half)
            q_stack.append(jnp.where(keep, col, zero))
        q_stack = jnp.concatenate(q_stack, axis=0)
        kwin = kbuf[n * WINDOW:(n + 3) * WINDOW, :]
        vwin = vbuf[n * WINDOW:(n + 3) * WINDOW, :]
        s = lax.dot_general(q_stack, kwin, (((1,), (1,)), ((), ())),
                            preferred_element_type=F32)
        p_list, inv_list = [], []
        for hd in range(N_HEADS):
            sink = sink_ref[hd]
            sh = jnp.where(valid, s[hd * WINDOW:(hd + 1) * WINDOW, :], MASK_VALUE)
            m = jnp.maximum(jnp.max(sh, axis=-1, keepdims=True), sink)
            e = jnp.exp(sh - m)
            den = jnp.sum(e, axis=-1, keepdims=True) + jnp.exp(sink - m)
            p_list.append(e.astype(BF16))
            inv_list.append(1.0 / den)
        p = jnp.concatenate(p_list, axis=0)
        o = jnp.dot(p, vwin, preferred_element_type=F32)
        cols = []
        for c4 in range(4):
            oa = o[c4 * WINDOW:(c4 + 1) * WINDOW, :] * inv_list[c4]
            ob = o[(c4 + 4) * WINDOW:(c4 + 5) * WINDOW, :] * inv_list[c4 + 4]
            cols.append(jnp.where(low_half, oa, ob))
        attn_rows.append(jnp.concatenate(cols, axis=1).astype(BF16))
    attn = attn_rows[0] if nsub == 1 else jnp.concatenate(attn_rows, axis=0)

    ubuf[8:8 + tq, :] = uc_ref[...].astype(F32)
    prev_rows = up_ref[...].astype(F32)
    next_rows = un_ref[...].astype(F32)
    ubuf[7:8, :] = prev_rows[15:16, :] * (j > 0).astype(F32)
    ubuf[8 + tq:9 + tq, :] = next_rows[0:1, :] * (j < nj - 1).astype(F32)
    cw = cw_ref[...]
    conv = (ubuf[7:7 + tq, :] * cw[0:1, :] + ubuf[8:8 + tq, :] * cw[1:2, :]
            + ubuf[9:9 + tq, :] * cw[2:3, :] + cbias_ref[...])
    cgate = (cb_ref[...].astype(F32) * conv).astype(BF16)

    merged = (ga_ref[...].astype(F32) * jnp.dot(attn, wa_ref[...], preferred_element_type=F32)
              + gc_ref[...].astype(F32) * jnp.dot(cgate, wc_ref[...], preferred_element_type=F32))
    x1 = x_ref[...] + jnp.dot(merged.astype(BF16), wo_ref[...], preferred_element_type=F32)
    x1_ref[...] = x1

    ms = jnp.mean(x1 * x1, axis=-1, keepdims=True)
    h2 = x1 * lax.rsqrt(ms + NORM_EPS) * g2_ref[...]
    h2_ref[...] = h2
    h_hi = h2.astype(BF16)
    h_lo = (h2 - h_hi.astype(F32)).astype(BF16)
    nt = (((1,), (1,)), ((), ()))
    logits = (lax.dot_general(wrh_ref[...], h_hi, nt, preferred_element_type=F32)
              + lax.dot_general(wrh_ref[...], h_lo, nt, preferred_element_type=F32)
              + lax.dot_general(wrl_ref[...], h_hi, nt, preferred_element_type=F32)
              + br_ref[...])
    grow = lax.broadcasted_iota(I32, (8, tq), 0)
    gl = jnp.where(grow < N_GROUPS, logits[0:8, :], NEG_BIG)
    gmax = jnp.max(gl, axis=0, keepdims=True)
    grp = jnp.min(jnp.where(gl == gmax, grow, 8), axis=0, keepdims=True)
    p_grp = 1.0 / jnp.sum(jnp.exp(gl - gmax), axis=0, keepdims=True)
    el = logits[8:ROUTER_ROWS, :]
    erow = lax.broadcasted_iota(I32, (N_EXPERTS, tq), 0)
    in_grp = (erow // EXPERTS_PER_GROUP) == grp
    l_a = jnp.where(in_grp, el, NEG_BIG)
    l1 = jnp.max(l_a, axis=0, keepdims=True)
    i1 = jnp.min(jnp.where(l_a == l1, erow, N_EXPERTS), axis=0, keepdims=True)
    l_b = jnp.where(erow == i1, NEG_BIG, l_a)
    l2 = jnp.max(l_b, axis=0, keepdims=True)
    i2 = jnp.min(jnp.where(l_b == l2, erow, N_EXPERTS), axis=0, keepdims=True)
    r21 = jnp.exp(l2 - l1)
    t1 = 1.0 / (1.0 + r21)
    gate1 = p_grp * t1
    gate2 = p_grp * (r21 * t1)
    oh1 = (erow == i1).astype(F32)
    oh2 = (erow == i2).astype(F32)
    oh = oh1 + oh2
    before = jnp.dot(oh.astype(BF16), tri_ref[...], preferred_element_type=F32) + base_ref[...][:, 0:1]
    rank1 = jnp.sum(oh1 * before, axis=0, keepdims=True)
    rank2 = jnp.sum(oh2 * before, axis=0, keepdims=True)
    new_base = base_ref[...] + jnp.sum(oh, axis=1, keepdims=True)
    base_ref[...] = new_base
    cnt_ref[...] = new_base
    zi = jnp.zeros((4, tq), I32)
    ri_ref[...] = jnp.concatenate([i1, i2, rank1.astype(I32), rank2.astype(I32), zi], axis=0)
    rf_ref[...] = jnp.concatenate([gate1, gate2, jnp.zeros((6, tq), F32)], axis=0)


def _row_gather(idx_ref, n_rows, src_hbm, dst, sem, wait):
    def body(r, carry):
        t = idx_ref[0, 0, r]
        cp = pltpu.make_async_copy(src_hbm.at[pl.ds(t, 1)], dst.at[pl.ds(r, 1)], sem)
        if wait:
            cp.wait()
        else:
            cp.start()
        return carry
    lax.fori_loop(0, n_rows, body, 0, unroll=8)


def _moe_kernel(be_ref, nused_ref, idx_cur, idx_nxt, h2_hbm, wg_ref, wu_ref, wd_ref,
                y_ref, xbuf, sem):
    i = pl.program_id(0)
    nused = nused_ref[0]
    slot = lax.rem(i, 2)

    @pl.when(i == 0)
    def _():
        _row_gather(idx_cur, MOE_ROWS, h2_hbm, xbuf.at[0], sem.at[0], wait=False)

    @pl.when(i + 1 < nused)
    def _():
        _row_gather(idx_nxt, MOE_ROWS, h2_hbm, xbuf.at[1 - slot], sem.at[1 - slot], wait=False)

    @pl.when(i < nused)
    def _():
        _row_gather(idx_cur, MOE_ROWS, h2_hbm, xbuf.at[slot], sem.at[slot], wait=True)
        xb = xbuf[slot].astype(BF16)
        g = jnp.dot(xb, wg_ref[...], preferred_element_type=F32)
        u = jnp.dot(xb, wu_ref[...], preferred_element_type=F32)
        hid = (g * jax.nn.sigmoid(g) * u).astype(BF16)
        y_ref[...] = jnp.dot(hid, wd_ref[...], preferred_element_type=F32)

    @pl.when(i >= nused)
    def _():
        y_ref[...] = jnp.zeros_like(y_ref)


def _combine_kernel(idx_cur, idx_nxt, x1_ref, gate_ref, y_hbm, o_ref, ybuf, sem, *, tm):
    i = pl.program_id(0)
    n = pl.num_programs(0)
    slot = lax.rem(i, 2)

    @pl.when(i == 0)
    def _():
        _row_gather(idx_cur, 2 * tm, y_hbm, ybuf.at[0], sem.at[0], wait=False)

    @pl.when(i + 1 < n)
    def _():
        _row_gather(idx_nxt, 2 * tm, y_hbm, ybuf.at[1 - slot], sem.at[1 - slot], wait=False)

    _row_gather(idx_cur, 2 * tm, y_hbm, ybuf.at[slot], sem.at[slot], wait=True)
    g = gate_ref[...]
    o_ref[...] = (x1_ref[...] + g[:, 0:1] * ybuf[slot, 0:tm, :] + g[:, 1:2] * ybuf[slot, tm:2 * tm, :])


def _rope_lane_tables(seq_len):
    half = ROT_DIM // 2
    inv_freq = ROPE_THETA ** (-jnp.arange(0, ROT_DIM, 2, dtype=F32) / ROT_DIM)
    ang = jnp.arange(seq_len, dtype=F32)[:, None] * inv_freq[None, :]
    cos, sin = jnp.cos(ang), jnp.sin(ang)
    m = np.arange(LANES) % HEAD_DIM
    f = m % half
    c = jnp.where(m[None, :] < ROT_DIM, cos[:, f], 1.0)
    s1 = jnp.where(m[None, :] < half, -sin[:, f], 0.0)
    s2 = jnp.where((m[None, :] >= half) & (m[None, :] < ROT_DIM), sin[:, f], 0.0)
    return c.astype(F32), s1.astype(F32), s2.astype(F32)


def _layer(x, norm1_g, w_in, q_norm_g, k_norm_g, attn_sink, conv_w, conv_b, w_attn_proj,
           w_conv_proj, w_out, norm2_g, w_rg, b_rg, w_re, b_re, w_gate, w_up, w_down):
    bsz, seq, d = x.shape
    t = bsz * seq
    tm = min(512, t)
    tq = min(256, seq)
    assert d == D_MODEL and seq % tq == 0 and t % tm == 0 and seq % tm == 0 and tq % WINDOW == 0
    xf = x.reshape(t, d)

    head_order = np.array([[j, j + 4] for j in range(4)]).reshape(-1)
    q_perm = (head_order[:, None] * HEAD_DIM + np.arange(HEAD_DIM)[None, :]).reshape(-1)
    w_in_p = jnp.concatenate([w_in[:, q_perm], w_in[:, ATTN_DIM:]], axis=1).astype(BF16)
    wa_p = w_attn_proj[q_perm, :].astype(BF16)
    bd = jnp.asarray(np.kron(np.eye(N_HEADS), np.full((HEAD_DIM, HEAD_DIM), 1.0 / HEAD_DIM)), BF16)
    qg = jnp.tile(q_norm_g, LANES // HEAD_DIM).reshape(1, LANES)
    kg = jnp.tile(k_norm_g, LANES // HEAD_DIM).reshape(1, LANES)
    ctab, s1tab, s2tab = _rope_lane_tables(seq)
    nseq_t = seq // tm

    row = lambda w: pl.BlockSpec((tm, w), lambda i: (i, 0))
    full = lambda a: pl.BlockSpec(a.shape, lambda i: (0,) * a.ndim)
    tab = pl.BlockSpec((tm, LANES), lambda i: (i % nseq_t, 0))
    q, k, v, cb, u, ga, gc = pl.pallas_call(
        _proj_kernel,
        grid=(t // tm,),
        in_specs=[row(d), full(norm1_g.reshape(1, d)), full(w_in_p), full(bd), full(qg), full(kg),
                  tab, tab, tab],
        out_specs=[row(ATTN_DIM), row(KV_DIM), row(KV_DIM), row(CONV_DIM), row(CONV_DIM),
                   row(D_MODEL), row(D_MODEL)],
        out_shape=[jax.ShapeDtypeStruct((t, w), BF16)
                   for w in (ATTN_DIM, KV_DIM, KV_DIM, CONV_DIM, CONV_DIM, D_MODEL, D_MODEL)],
        compiler_params=pltpu.CompilerParams(dimension_semantics=("arbitrary",),
                                             vmem_limit_bytes=VMEM_LIMIT),
        name="proj",
    )(xf, norm1_g.reshape(1, d), w_in_p, bd, qg, kg, ctab, s1tab, s2tab)

    nj = seq // tq
    wpb = tq // WINDOW
    nb_seq = seq // WINDOW
    hpb = tq // 16
    rowq = lambda w: pl.BlockSpec((tq, w), lambda b, j: (b * nj + j, 0))
    prev_blk = pl.BlockSpec((WINDOW, KV_DIM),
                            lambda b, j: (b * nb_seq + jnp.maximum(j * wpb - 1, 0), 0))
    next_blk = pl.BlockSpec((WINDOW, KV_DIM),
                            lambda b, j: (b * nb_seq + jnp.minimum((j + 1) * wpb, nb_seq - 1), 0))
    uprev = pl.BlockSpec((16, CONV_DIM),
                         lambda b, j: (jnp.maximum((b * nj + j) * hpb - 1, 0), 0))
    unext = pl.BlockSpec((16, CONV_DIM),
                         lambda b, j: (jnp.minimum((b * nj + j + 1) * hpb, t // 16 - 1), 0))
    full2 = lambda a: pl.BlockSpec(a.shape, lambda b, j: (0,) * a.ndim)
    colq = lambda r: pl.BlockSpec((r, tq), lambda b, j: (0, b * nj + j))

    wr = jnp.concatenate([w_rg.T, jnp.zeros((8 - N_GROUPS, d), F32), w_re.T], axis=0)
    wr_hi = wr.astype(BF16)
    wr_lo = (wr - wr_hi.astype(F32)).astype(BF16)
    br = jnp.concatenate([b_rg, jnp.zeros((8 - N_GROUPS,), F32), b_re]).reshape(ROUTER_ROWS, 1)
    tri = jnp.asarray(np.triu(np.ones((tq, tq), np.float32), 1), BF16)
    wc_b = w_conv_proj.astype(BF16)
    wo_b = w_out.astype(BF16)
    cbias = conv_b.reshape(1, CONV_DIM)
    g2 = norm2_g.reshape(1, d)

    x1, h2, route_i, route_f, counts = pl.pallas_call(
        functools.partial(_mix_kernel, tq=tq, n_seq_blocks=nb_seq),
        grid=(bsz, nj),
        in_specs=[pl.BlockSpec(memory_space=pltpu.SMEM),
                  rowq(d), rowq(ATTN_DIM), prev_blk, rowq(KV_DIM), next_blk,
                  prev_blk, rowq(KV_DIM), next_blk,
                  rowq(CONV_DIM), uprev, rowq(CONV_DIM), unext, rowq(D_MODEL), rowq(D_MODEL),
                  full2(conv_w), full2(cbias), full2(wa_p), full2(wc_b), full2(wo_b), full2(g2),
                  full2(wr_hi), full2(wr_lo), full2(br), full2(tri)],
        out_specs=[rowq(d), rowq(d), colq(8), colq(8),
                   pl.BlockSpec((N_EXPERTS, LANES), lambda b, j: (0, 0))],
        out_shape=[jax.ShapeDtypeStruct((t, d), F32), jax.ShapeDtypeStruct((t, d), F32),
                   jax.ShapeDtypeStruct((8, t), I32), jax.ShapeDtypeStruct((8, t), F32),
                   jax.ShapeDtypeStruct((N_EXPERTS, LANES), F32)],
        scratch_shapes=[pltpu.VMEM((tq + 16, CONV_DIM), F32), pltpu.VMEM((N_EXPERTS, LANES), F32)],
        compiler_params=pltpu.CompilerParams(dimension_semantics=("arbitrary", "arbitrary"),
                                             vmem_limit_bytes=VMEM_LIMIT),
        name="mix",
    )(attn_sink, xf, q, k, k, k, v, v, v, cb, u, u, u, ga, gc,
      conv_w, cbias, wa_p, wc_b, wo_b, g2, wr_hi, wr_lo, br, tri)

    n_rows = 2 * t + N_EXPERTS * MOE_ROWS
    nblk = n_rows // MOE_ROWS
    cnt = counts[:, 0].astype(I32)
    padded = ((cnt + MOE_ROWS - 1) // MOE_ROWS) * MOE_ROWS
    pad_ends = jnp.cumsum(padded)
    pad_starts = pad_ends - padded
    dest = pad_starts[route_i[0:2]] + route_i[2:4]
    block_e = jnp.minimum(jnp.searchsorted(pad_ends, jnp.arange(nblk, dtype=I32) * MOE_ROWS,
                                           side="right"), N_EXPERTS - 1).astype(I32)
    nused = (pad_ends[-1:] // MOE_ROWS).astype(I32)
    tok = jnp.tile(jnp.arange(t, dtype=I32), 2)
    src_tok = jnp.zeros((n_rows,), I32).at[dest.reshape(-1)].set(tok).reshape(nblk, 1, MOE_ROWS)

    idx_spec = lambda off: pl.BlockSpec(
        (1, 1, MOE_ROWS), lambda i, be, nu: (jnp.minimum(i + off, nblk - 1), 0, 0),
        memory_space=pltpu.SMEM)
    wspec = lambda a, b: pl.BlockSpec((None, a, b), lambda i, be, nu: (be[i], 0, 0))
    y_pad = pl.pallas_call(
        _moe_kernel,
        grid_spec=pltpu.PrefetchScalarGridSpec(
            num_scalar_prefetch=2,
            grid=(nblk,),
            in_specs=[idx_spec(0), idx_spec(1), pl.BlockSpec(memory_space=pl.ANY),
                      wspec(d, D_FF), wspec(d, D_FF), wspec(D_FF, d)],
            out_specs=pl.BlockSpec((MOE_ROWS, d), lambda i, be, nu: (i, 0)),
            scratch_shapes=[pltpu.VMEM((2, MOE_ROWS, d), F32), pltpu.SemaphoreType.DMA((2,))]),
        out_shape=jax.ShapeDtypeStruct((n_rows, d), F32),
        compiler_params=pltpu.CompilerParams(dimension_semantics=("arbitrary",),
                                             vmem_limit_bytes=VMEM_LIMIT),
        name="moe",
    )(block_e, nused, src_tok, src_tok, h2, w_gate.astype(BF16), w_up.astype(BF16),
      w_down.astype(BF16))

    tc = min(256, t)
    nt = t // tc
    dest_tiles = dest.reshape(2, nt, tc).transpose(1, 0, 2).reshape(nt, 1, 2 * tc)
    gates_t = route_f[0:2].T
    cidx = lambda off: pl.BlockSpec((1, 1, 2 * tc), lambda i: (jnp.minimum(i + off, nt - 1), 0, 0),
                                    memory_space=pltpu.SMEM)
    out = pl.pallas_call(
        functools.partial(_combine_kernel, tm=tc),
        grid=(nt,),
        in_specs=[cidx(0), cidx(1), pl.BlockSpec((tc, d), lambda i: (i, 0)),
                  pl.BlockSpec((tc, 2), lambda i: (i, 0)), pl.BlockSpec(memory_space=pl.ANY)],
        out_specs=pl.BlockSpec((tc, d), lambda i: (i, 0)),
        out_shape=jax.ShapeDtypeStruct((t, d), F32),
        scratch_shapes=[pltpu.VMEM((2, 2 * tc, d), F32), pltpu.SemaphoreType.DMA((2,))],
        compiler_params=pltpu.CompilerParams(dimension_semantics=("arbitrary",),
                                             vmem_limit_bytes=VMEM_LIMIT),
        name="combine",
    )(dest_tiles, dest_tiles, x1, gates_t, y_pad)
    return out.reshape(bsz, seq, d)


def kernel(x, norm1_g, w_in, q_norm_g, k_norm_g, attn_sink, conv_w, conv_b, w_attn_proj, w_conv_proj, w_out, norm2_g, w_router_group, b_router_group, w_router_expert, b_router_expert, w_gate_e, w_up_e, w_down_e):
    for l in range(norm1_g.shape[0]):
        x = _layer(x, norm1_g[l], w_in[l], q_norm_g[l], k_norm_g[l], attn_sink[l], conv_w[l],
                   conv_b[l], w_attn_proj[l], w_conv_proj[l], w_out[l], norm2_g[l],
                   w_router_group[l], b_router_group[l], w_router_expert[l], b_router_expert[l],
                   w_gate_e[l], w_up_e[l], w_down_e[l])
    return x
```

```python
import functools

import jax
import jax.numpy as jnp
import numpy as np
from jax import lax
from jax.experimental import pallas as pl
from jax.experimental.pallas import tpu as pltpu

F32 = jnp.float32
BF16 = jnp.bfloat16
I32 = jnp.int32

D_MODEL = 1024
N_HEADS = 8
N_KV_HEADS = 2
HEAD_DIM = 64
ATTN_DIM = N_HEADS * HEAD_DIM
KV_DIM = N_KV_HEADS * HEAD_DIM
WINDOW = 128
ROT_DIM = HEAD_DIM // 4
ROPE_THETA = 500000.0
CONV_DIM = D_MODEL // 2
N_GROUPS = 4
EXPERTS_PER_GROUP = 8
N_EXPERTS = N_GROUPS * EXPERTS_PER_GROUP
D_FF = D_MODEL // 4
NORM_EPS = 1e-6
MASK_VALUE = -1e30
NEG_BIG = -3.0e38

LANES = 128
ROUTER_ROWS = 8 + N_EXPERTS
MOE_ROWS = 256
VMEM_LIMIT = 56 * 1024 * 1024

_Q0, _K0, _V0, _CB0, _CC0, _CX0, _GA0, _GC0, _END = (
    0, 512, 640, 768, 1280, 1792, 2304, 3328, 4352)


def _rope_chunks(t, c, s1, s2):
    outs = []
    for j in range(t.shape[1] // LANES):
        tj = t[:, j * LANES:(j + 1) * LANES]
        outs.append(tj * c + pltpu.roll(tj, LANES - ROT_DIM // 2, 1) * s1
                    + pltpu.roll(tj, ROT_DIM // 2, 1) * s2)
    return outs[0] if len(outs) == 1 else jnp.concatenate(outs, axis=1)


def _proj_kernel(x_ref, g1_ref, w_ref, bd_ref, qg_ref, kg_ref, c_ref, s1_ref, s2_ref,
                 q_ref, k_ref, v_ref, cb_ref, u_ref, ga_ref, gc_ref):
    x = x_ref[...]
    ms = jnp.mean(x * x, axis=-1, keepdims=True)
    h = (x * lax.rsqrt(ms + NORM_EPS) * g1_ref[...]).astype(BF16)

    def proj(a, b):
        return jnp.dot(h, w_ref[:, a:b], preferred_element_type=F32)

    c, s1, s2 = c_ref[...], s1_ref[...], s2_ref[...]

    q = proj(_Q0, _K0)
    qms = jnp.dot((q * q).astype(BF16), bd_ref[...], preferred_element_type=F32)
    qg = jnp.concatenate([qg_ref[...]] * (ATTN_DIM // LANES), axis=1)
    qn = q * lax.rsqrt(qms + NORM_EPS) * qg
    q_ref[...] = (_rope_chunks(qn, c, s1, s2) * (HEAD_DIM ** -0.5)).astype(BF16)

    k = proj(_K0, _V0)
    kms = jnp.dot((k * k).astype(BF16), bd_ref[0:KV_DIM, 0:KV_DIM], preferred_element_type=F32)
    kn = k * lax.rsqrt(kms + NORM_EPS) * kg_ref[...]
    k_ref[...] = _rope_chunks(kn, c, s1, s2).astype(BF16)

    v_ref[...] = proj(_V0, _CB0).astype(BF16)
    cb_ref[...] = proj(_CB0, _CC0).astype(BF16)
    u_ref[...] = (proj(_CC0, _CX0) * proj(_CX0, _GA0)).astype(BF16)
    ga_ref[...] = jax.nn.sigmoid(proj(_GA0, _GC0)).astype(BF16)
    gc_ref[...] = jax.nn.sigmoid(proj(_GC0, _END)).astype(BF16)


def _mix_kernel(sink_ref, x_ref, q_ref, kp_ref, kc_ref, kn_ref, vp_ref, vc_ref, vn_ref,
                cb_ref, up_ref, uc_ref, un_ref, ga_ref, gc_ref,
                cw_ref, cbias_ref, wa_ref, wc_ref, wo_ref, g2_ref,
                wrh_ref, wrl_ref, br_ref, tri_ref,
                x1_ref, h2_ref, ri_ref, rf_ref, cnt_ref,
                ubuf, base_ref, *, tq, n_seq_blocks):
    b = pl.program_id(0)
    j = pl.program_id(1)
    nj = pl.num_programs(1)
    nsub = tq // WINDOW

    @pl.when((b == 0) & (j == 0))
    def _():
        base_ref[...] = jnp.zeros_like(base_ref)

    kbuf = jnp.concatenate([kp_ref[...], kc_ref[...], kn_ref[...]], axis=0)
    vbuf = jnp.concatenate([vp_ref[...], vc_ref[...], vn_ref[...]], axis=0)
    lane = lax.broadcasted_iota(I32, (WINDOW, LANES), 1)
    low_half = lane < HEAD_DIM
    qrow = lax.broadcasted_iota(I32, (WINDOW, 3 * WINDOW), 0)
    kcol = lax.broadcasted_iota(I32, (WINDOW, 3 * WINDOW), 1)
    dd = kcol - qrow
    band = (dd >= 0) & (dd <= 2 * WINDOW)
    attn_rows = []
    for n in range(nsub):
        lo_lim = jnp.where(j > 0, 0, WINDOW) if n == 0 else 0
        hi_lim = jnp.where(j < nj - 1, 3 * WINDOW, 2 * WINDOW) if n == nsub - 1 else 3 * WINDOW
        valid = band & (kcol >= lo_lim) & (kcol < hi_lim)
        qs = q_ref[n * WINDOW:(n + 1) * WINDOW, :]
        zero = jnp.zeros((WINDOW, LANES), BF16)
        q_stack = []
        for hd in range(N_HEADS):
            col = qs[:, (hd % 4) * LANES:(hd % 4 + 1) * LANES]
            keep = low_half if hd < 4 else jnp.logical_not(low_half)
            q_stack.append(jnp.where(keep, col, zero))
        q_stack = jnp.concatenate(q_stack, axis=0)
        kwin = kbuf[n * WINDOW:(n + 3) * WINDOW, :]
        vwin = vbuf[n * WINDOW:(n + 3) * WINDOW, :]
        s = lax.dot_general(q_stack, kwin, (((1,), (1,)), ((), ())),
                            preferred_element_type=F32)
        p_list, inv_list = [], []
        for hd in range(N_HEADS):
            sink = sink_ref[hd]
            sh = jnp.where(valid, s[hd * WINDOW:(hd + 1) * WINDOW, :], MASK_VALUE)
            m = jnp.maximum(jnp.max(sh, axis=-1, keepdims=True), sink)
            e = jnp.exp(sh - m)
            den = jnp.sum(e, axis=-1, keepdims=True) + jnp.exp(sink - m)
            p_list.append(e.astype(BF16))
            inv_list.append(1.0 / den)
        p = jnp.concatenate(p_list, axis=0)
        o = jnp.dot(p, vwin, preferred_element_type=F32)
        cols = []
        for c4 in range(4):
            oa = o[c4 * WINDOW:(c4 + 1) * WINDOW, :] * inv_list[c4]
            ob = o[(c4 + 4) * WINDOW:(c4 + 5) * WINDOW, :] * inv_list[c4 + 4]
            cols.append(jnp.where(low_half, oa, ob))
        attn_rows.append(jnp.concatenate(cols, axis=1).astype(BF16))
    attn = attn_rows[0] if nsub == 1 else jnp.concatenate(attn_rows, axis=0)

    ubuf[8:8 + tq, :] = uc_ref[...].astype(F32)
    prev_rows = up_ref[...].astype(F32)
    next_rows = un_ref[...].astype(F32)
    ubuf[7:8, :] = prev_rows[15:16, :] * (j > 0).astype(F32)
    ubuf[8 + tq:9 + tq, :] = next_rows[0:1, :] * (j < nj - 1).astype(F32)
    cw = cw_ref[...]
    conv = (ubuf[7:7 + tq, :] * cw[0:1, :] + ubuf[8:8 + tq, :] * cw[1:2, :]
            + ubuf[9:9 + tq, :] * cw[2:3, :] + cbias_ref[...])
    cgate = (cb_ref[...].astype(F32) * conv).astype(BF16)

    merged = (ga_ref[...].astype(F32) * jnp.dot(attn, wa_ref[...], preferred_element_type=F32)
              + gc_ref[...].astype(F32) * jnp.dot(cgate, wc_ref[...], preferred_element_type=F32))
    x1 = x_ref[...] + jnp.dot(merged.astype(BF16), wo_ref[...], preferred_element_type=F32)
    x1_ref[...] = x1

    ms = jnp.mean(x1 * x1, axis=-1, keepdims=True)
    h2 = x1 * lax.rsqrt(ms + NORM_EPS) * g2_ref[...]
    h2_ref[...] = h2
    h_hi = h2.astype(BF16)
    h_lo = (h2 - h_hi.astype(F32)).astype(BF16)
    nt = (((1,), (1,)), ((), ()))
    logits = (lax.dot_general(wrh_ref[...], h_hi, nt, preferred_element_type=F32)
              + lax.dot_general(wrh_ref[...], h_lo, nt, preferred_element_type=F32)
              + lax.dot_general(wrl_ref[...], h_hi, nt, preferred_element_type=F32)
              + br_ref[...])
    grow = lax.broadcasted_iota(I32, (8, tq), 0)
    gl = jnp.where(grow < N_GROUPS, logits[0:8, :], NEG_BIG)
    gmax = jnp.max(gl, axis=0, keepdims=True)
    grp = jnp.min(jnp.where(gl == gmax, grow, 8), axis=0, keepdims=True)
    p_grp = 1.0 / jnp.sum(jnp.exp(gl - gmax), axis=0, keepdims=True)
    el = logits[8:ROUTER_ROWS, :]
    erow = lax.broadcasted_iota(I32, (N_EXPERTS, tq), 0)
    in_grp = (erow // EXPERTS_PER_GROUP) == grp
    l_a = jnp.where(in_grp, el, NEG_BIG)
    l1 = jnp.max(l_a, axis=0, keepdims=True)
    i1 = jnp.min(jnp.where(l_a == l1, erow, N_EXPERTS), axis=0, keepdims=True)
    l_b = jnp.where(erow == i1, NEG_BIG, l_a)
    l2 = jnp.max(l_b, axis=0, keepdims=True)
    i2 = jnp.min(jnp.where(l_b == l2, erow, N_EXPERTS), axis=0, keepdims=True)
    r21 = jnp.exp(l2 - l1)
    t1 = 1.0 / (1.0 + r21)
    gate1 = p_grp * t1
    gate2 = p_grp * (r21 * t1)
    oh1 = (erow == i1).astype(F32)
    oh2 = (erow == i2).astype(F32)
    oh = oh1 + oh2
    before = jnp.dot(oh.astype(BF16), tri_ref[...], preferred_element_type=F32) + base_ref[...][:, 0:1]
    rank1 = jnp.sum(oh1 * before, axis=0, keepdims=True)
    rank2 = jnp.sum(oh2 * before, axis=0, keepdims=True)
    new_base = base_ref[...] + jnp.sum(oh, axis=1, keepdims=True)
    base_ref[...] = new_base
    cnt_ref[...] = new_base
    zi = jnp.zeros((4, tq), I32)
    ri_ref[...] = jnp.concatenate([i1, i2, rank1.astype(I32), rank2.astype(I32), zi], axis=0)
    rf_ref[...] = jnp.concatenate([gate1, gate2, jnp.zeros((6, tq), F32)], axis=0)


def _row_gather(idx_ref, n_rows, src_hbm, dst, sem, wait):
    def body(r, carry):
        t = idx_ref[0, 0, r]
        cp = pltpu.make_async_copy(src_hbm.at[pl.ds(t, 1)], dst.at[pl.ds(r, 1)], sem)
        if wait:
            cp.wait()
        else:
            cp.start()
        return carry
    lax.fori_loop(0, n_rows, body, 0, unroll=8)


def _moe_kernel(be_ref, nused_ref, idx_cur, idx_nxt, h2_hbm, wg_ref, wu_ref, wd_ref,
                y_ref, xbuf, sem):
    i = pl.program_id(0)
    nused = nused_ref[0]
    slot = lax.rem(i, 2)

    @pl.when(i == 0)
    def _():
        _row_gather(idx_cur, MOE_ROWS, h2_hbm, xbuf.at[0], sem.at[0], wait=False)

    @pl.when(i + 1 < nused)
    def _():
        _row_gather(idx_nxt, MOE_ROWS, h2_hbm, xbuf.at[1 - slot], sem.at[1 - slot], wait=False)

    @pl.when(i < nused)
    def _():
        _row_gather(idx_cur, MOE_ROWS, h2_hbm, xbuf.at[slot], sem.at[slot], wait=True)
        xb = xbuf[slot].astype(BF16)
        g = jnp.dot(xb, wg_ref[...], preferred_element_type=F32)
        u = jnp.dot(xb, wu_ref[...], preferred_element_type=F32)
        hid = (g * jax.nn.sigmoid(g) * u).astype(BF16)
        y_ref[...] = jnp.dot(hid, wd_ref[...], preferred_element_type=F32)

    @pl.when(i >= nused)
    def _():
        y_ref[...] = jnp.zeros_like(y_ref)


def _combine_kernel(idx_cur, idx_nxt, x1_ref, gate_ref, y_hbm, o_ref, ybuf, sem, *, tm):
    i = pl.program_id(0)
    n = pl.num_programs(0)
    slot = lax.rem(i, 2)

    @pl.when(i == 0)
    def _():
        _row_gather(idx_cur, 2 * tm, y_hbm, ybuf.at[0], sem.at[0], wait=False)

    @pl.when(i + 1 < n)
    def _():
        _row_gather(idx_nxt, 2 * tm, y_hbm, ybuf.at[1 - slot], sem.at[1 - slot], wait=False)

    _row_gather(idx_cur, 2 * tm, y_hbm, ybuf.at[slot], sem.at[slot], wait=True)
    g = gate_ref[...]
    o_ref[...] = (x1_ref[...] + g[:, 0:1] * ybuf[slot, 0:tm, :] + g[:, 1:2] * ybuf[slot, tm:2 * tm, :])


def _rope_lane_tables(seq_len):
    half = ROT_DIM // 2
    inv_freq = ROPE_THETA ** (-jnp.arange(0, ROT_DIM, 2, dtype=F32) / ROT_DIM)
    ang = jnp.arange(seq_len, dtype=F32)[:, None] * inv_freq[None, :]
    cos, sin = jnp.cos(ang), jnp.sin(ang)
    m = np.arange(LANES) % HEAD_DIM
    f = m % half
    c = jnp.where(m[None, :] < ROT_DIM, cos[:, f], 1.0)
    s1 = jnp.where(m[None, :] < half, -sin[:, f], 0.0)
    s2 = jnp.where((m[None, :] >= half) & (m[None, :] < ROT_DIM), sin[:, f], 0.0)
    return c.astype(F32), s1.astype(F32), s2.astype(F32)


def _layer(x, norm1_g, w_in, q_norm_g, k_norm_g, attn_sink, conv_w, conv_b, w_attn_proj,
           w_conv_proj, w_out, norm2_g, w_rg, b_rg, w_re, b_re, w_gate, w_up, w_down):
    bsz, seq, d = x.shape
    t = bsz * seq
    tm = min(512, t)
    tq = min(256, seq)
    assert d == D_MODEL and seq % tq == 0 and t % tm == 0 and seq % tm == 0 and tq % WINDOW == 0
    xf = x.reshape(t, d)

    hpg = N_HEADS // N_KV_HEADS
    wq_p = (w_in[:, :ATTN_DIM].reshape(d, N_KV_HEADS, hpg, HEAD_DIM).transpose(0, 2, 1, 3)
            .reshape(d, ATTN_DIM))
    w_in_p = jnp.concatenate([wq_p, w_in[:, ATTN_DIM:]], axis=1).astype(BF16)
    wa_p = (w_attn_proj.reshape(N_KV_HEADS, hpg, HEAD_DIM, d).transpose(1, 0, 2, 3)
            .reshape(ATTN_DIM, d).astype(BF16))
    bd = jnp.asarray(np.kron(np.eye(N_HEADS), np.full((HEAD_DIM, HEAD_DIM), 1.0 / HEAD_DIM)), BF16)
    qg = jnp.tile(q_norm_g, LANES // HEAD_DIM).reshape(1, LANES)
    kg = jnp.tile(k_norm_g, LANES // HEAD_DIM).reshape(1, LANES)
    ctab, s1tab, s2tab = _rope_lane_tables(seq)
    nseq_t = seq // tm

    row = lambda w: pl.BlockSpec((tm, w), lambda i: (i, 0))
    full = lambda a: pl.BlockSpec(a.shape, lambda i: (0,) * a.ndim)
    tab = pl.BlockSpec((tm, LANES), lambda i: (i % nseq_t, 0))
    q, k, v, cb, u, ga, gc = pl.pallas_call(
        _proj_kernel,
        grid=(t // tm,),
        in_specs=[row(d), full(norm1_g.reshape(1, d)), full(w_in_p), full(bd), full(qg), full(kg),
                  tab, tab, tab],
        out_specs=[row(ATTN_DIM), row(KV_DIM), row(KV_DIM), row(CONV_DIM), row(CONV_DIM),
                   row(D_MODEL), row(D_MODEL)],
        out_shape=[jax.ShapeDtypeStruct((t, w), BF16)
                   for w in (ATTN_DIM, KV_DIM, KV_DIM, CONV_DIM, CONV_DIM, D_MODEL, D_MODEL)],
        compiler_params=pltpu.CompilerParams(dimension_semantics=("arbitrary",),
                                             vmem_limit_bytes=VMEM_LIMIT),
        name="proj",
    )(xf, norm1_g.reshape(1, d), w_in_p, bd, qg, kg, ctab, s1tab, s2tab)

    nj = seq // tq
    wpb = tq // WINDOW
    nb_seq = seq // WINDOW
    hpb = tq // 16
    rowq = lambda w: pl.BlockSpec((tq, w), lambda b, j: (b * nj + j, 0))
    prev_blk = pl.BlockSpec((WINDOW, KV_DIM),
                            lambda b, j: (b * nb_seq + jnp.maximum(j * wpb - 1, 0), 0))
    next_blk = pl.BlockSpec((WINDOW, KV_DIM),
                            lambda b, j: (b * nb_seq + jnp.minimum((j + 1) * wpb, nb_seq - 1), 0))
    uprev = pl.BlockSpec((16, CONV_DIM),
                         lambda b, j: (jnp.maximum((b * nj + j) * hpb - 1, 0), 0))
    unext = pl.BlockSpec((16, CONV_DIM),
                         lambda b, j: (jnp.minimum((b * nj + j + 1) * hpb, t // 16 - 1), 0))
    full2 = lambda a: pl.BlockSpec(a.shape, lambda b, j: (0,) * a.ndim)
    colq = lambda r: pl.BlockSpec((r, tq), lambda b, j: (0, b * nj + j))

    wr = jnp.concatenate([w_rg.T, jnp.zeros((8 - N_GROUPS, d), F32), w_re.T], axis=0)
    wr_hi = wr.astype(BF16)
    wr_lo = (wr - wr_hi.astype(F32)).astype(BF16)
    br = jnp.concatenate([b_rg, jnp.zeros((8 - N_GROUPS,), F32), b_re]).reshape(ROUTER_ROWS, 1)
    tri = jnp.asarray(np.triu(np.ones((tq, tq), np.float32), 1), BF16)
    wc_b = w_conv_proj.astype(BF16)
    wo_b = w_out.astype(BF16)
    cbias = conv_b.reshape(1, CONV_DIM)
    g2 = norm2_g.reshape(1, d)

    x1, h2, route_i, route_f, counts = pl.pallas_call(
        functools.partial(_mix_kernel, tq=tq, n_seq_blocks=nb_seq),
        grid=(bsz, nj),
        in_specs=[pl.BlockSpec(memory_space=pltpu.SMEM),
                  rowq(d), rowq(ATTN_DIM), prev_blk, rowq(KV_DIM), next_blk,
                  prev_blk, rowq(KV_DIM), next_blk,
                  rowq(CONV_DIM), uprev, rowq(CONV_DIM), unext, rowq(D_MODEL), rowq(D_MODEL),
                  full2(conv_w), full2(cbias), full2(wa_p), full2(wc_b), full2(wo_b), full2(g2),
                  full2(wr_hi), full2(wr_lo), full2(br), full2(tri)],
        out_specs=[rowq(d), rowq(d), colq(8), colq(8),
                   pl.BlockSpec((N_EXPERTS, LANES), lambda b, j: (0, 0))],
        out_shape=[jax.ShapeDtypeStruct((t, d), F32), jax.ShapeDtypeStruct((t, d), F32),
                   jax.ShapeDtypeStruct((8, t), I32), jax.ShapeDtypeStruct((8, t), F32),
                   jax.ShapeDtypeStruct((N_EXPERTS, LANES), F32)],
        scratch_shapes=[pltpu.VMEM((tq + 16, CONV_DIM), F32), pltpu.VMEM((N_EXPERTS, LANES), F32)],
        compiler_params=pltpu.CompilerParams(dimension_semantics=("arbitrary", "arbitrary"),
                                             vmem_limit_bytes=VMEM_LIMIT),
        name="mix",
    )(attn_sink, xf, q, k, k, k, v, v, v, cb, u, u, u, ga, gc,
      conv_w, cbias, wa_p, wc_b, wo_b, g2, wr_hi, wr_lo, br, tri)

    n_rows = 2 * t + N_EXPERTS * MOE_ROWS
    nblk = n_rows // MOE_ROWS
    cnt = counts[:, 0].astype(I32)
    padded = ((cnt + MOE_ROWS - 1) // MOE_ROWS) * MOE_ROWS
    pad_ends = jnp.cumsum(padded)
    pad_starts = pad_ends - padded
    eids = jnp.arange(N_EXPERTS, dtype=I32)[:, None, None]
    dest = jnp.sum(jnp.where(route_i[None, 0:2] == eids, pad_starts[:, None, None], 0), axis=0) \
        + route_i[2:4]
    blk_start = jnp.arange(nblk, dtype=I32) * MOE_ROWS
    block_e = jnp.minimum(jnp.sum((pad_ends[None, :] <= blk_start[:, None]).astype(I32), axis=1),
                          N_EXPERTS - 1)
    nused = (pad_ends[-1:] // MOE_ROWS).astype(I32)
    tok = jnp.tile(jnp.arange(t, dtype=I32), 2)
    src_tok = jnp.zeros((n_rows,), I32).at[dest.reshape(-1)].set(tok).reshape(nblk, 1, MOE_ROWS)

    idx_spec = lambda off: pl.BlockSpec(
        (1, 1, MOE_ROWS), lambda i, be, nu: (jnp.minimum(i + off, nblk - 1), 0, 0),
        memory_space=pltpu.SMEM)
    wspec = lambda a, b: pl.BlockSpec((None, a, b), lambda i, be, nu: (be[i], 0, 0))
    y_pad = pl.pallas_call(
        _moe_kernel,
        grid_spec=pltpu.PrefetchScalarGridSpec(
            num_scalar_prefetch=2,
            grid=(nblk,),
            in_specs=[idx_spec(0), idx_spec(1), pl.BlockSpec(memory_space=pl.ANY),
                      wspec(d, D_FF), wspec(d, D_FF), wspec(D_FF, d)],
            out_specs=pl.BlockSpec((MOE_ROWS, d), lambda i, be, nu: (i, 0)),
            scratch_shapes=[pltpu.VMEM((2, MOE_ROWS, d), F32), pltpu.SemaphoreType.DMA((2,))]),
        out_shape=jax.ShapeDtypeStruct((n_rows, d), F32),
        compiler_params=pltpu.CompilerParams(dimension_semantics=("arbitrary",),
                                             vmem_limit_bytes=VMEM_LIMIT),
        name="moe",
    )(block_e, nused, src_tok, src_tok, h2, w_gate.astype(BF16), w_up.astype(BF16),
      w_down.astype(BF16))

    tc = min(256, t)
    nt = t // tc
    dest_tiles = dest.reshape(2, nt, tc).transpose(1, 0, 2).reshape(nt, 1, 2 * tc)
    gates_t = route_f[0:2].T
    cidx = lambda off: pl.BlockSpec((1, 1, 2 * tc), lambda i: (jnp.minimum(i + off, nt - 1), 0, 0),
                                    memory_space=pltpu.SMEM)
    out = pl.pallas_call(
        functools.partial(_combine_kernel, tm=tc),
        grid=(nt,),
        in_specs=[cidx(0), cidx(1), pl.BlockSpec((tc, d), lambda i: (i, 0)),
                  pl.BlockSpec((tc, 2), lambda i: (i, 0)), pl.BlockSpec(memory_space=pl.ANY)],
        out_specs=pl.BlockSpec((tc, d), lambda i: (i, 0)),
        out_shape=jax.ShapeDtypeStruct((t, d), F32),
        scratch_shapes=[pltpu.VMEM((2, 2 * tc, d), F32), pltpu.SemaphoreType.DMA((2,))],
        compiler_params=pltpu.CompilerParams(dimension_semantics=("arbitrary",),
                                             vmem_limit_bytes=VMEM_LIMIT),
        name="combine",
    )(dest_tiles, dest_tiles, x1, gates_t, y_pad)
    return out.reshape(bsz, seq, d)


def kernel(x, norm1_g, w_in, q_norm_g, k_norm_g, attn_sink, conv_w, conv_b, w_attn_proj, w_conv_proj, w_out, norm2_g, w_router_group, b_router_group, w_router_expert, b_router_expert, w_gate_e, w_up_e, w_down_e):
    for l in range(norm1_g.shape[0]):
        x = _layer(x, norm1_g[l], w_in[l], q_norm_g[l], k_norm_g[l], attn_sink[l], conv_w[l],
                   conv_b[l], w_attn_proj[l], w_conv_proj[l], w_out[l], norm2_g[l],
                   w_router_group[l], b_router_group[l], w_router_expert[l], b_router_expert[l],
                   w_gate_e[l], w_up_e[l], w_down_e[l])
    return x
```

```python
import functools

import jax
import jax.numpy as jnp
import numpy as np
from jax import lax
from jax.experimental import pallas as pl
from jax.experimental.pallas import tpu as pltpu
from jax.experimental.pallas import tpu_sc as plsc

F32 = jnp.float32
BF16 = jnp.bfloat16
I32 = jnp.int32

D_MODEL = 1024
N_HEADS = 8
N_KV_HEADS = 2
HEAD_DIM = 64
ATTN_DIM = N_HEADS * HEAD_DIM
KV_DIM = N_KV_HEADS * HEAD_DIM
WINDOW = 128
ROT_DIM = HEAD_DIM // 4
ROPE_THETA = 500000.0
CONV_DIM = D_MODEL // 2
N_GROUPS = 4
EXPERTS_PER_GROUP = 8
N_EXPERTS = N_GROUPS * EXPERTS_PER_GROUP
D_FF = D_MODEL // 4
NORM_EPS = 1e-6
MASK_VALUE = -1e30
NEG_BIG = -3.0e38

LANES = 128
TOK_SUBLANES = D_MODEL // LANES
SC_WINDOW = 32
ROUTER_ROWS = 8 + N_EXPERTS
MOE_ROWS = 256
VMEM_LIMIT = 56 * 1024 * 1024

_Q0, _K0, _V0, _CB0, _CC0, _CX0, _GA0, _GC0, _END = (
    0, 512, 640, 768, 1280, 1792, 2304, 3328, 4352)


def _rope_chunks(t, c, s1, s2):
    outs = []
    for j in range(t.shape[1] // LANES):
        tj = t[:, j * LANES:(j + 1) * LANES]
        outs.append(tj * c + pltpu.roll(tj, LANES - ROT_DIM // 2, 1) * s1
                    + pltpu.roll(tj, ROT_DIM // 2, 1) * s2)
    return outs[0] if len(outs) == 1 else jnp.concatenate(outs, axis=1)


def _proj_kernel(x_ref, g1_ref, w_ref, bd_ref, qg_ref, kg_ref, c_ref, s1_ref, s2_ref,
                 q_ref, k_ref, v_ref, cb_ref, u_ref, ga_ref, gc_ref):
    x = x_ref[...]
    ms = jnp.mean(x * x, axis=-1, keepdims=True)
    h = (x * lax.rsqrt(ms + NORM_EPS) * g1_ref[...]).astype(BF16)

    def proj(a, b):
        return jnp.dot(h, w_ref[:, a:b], preferred_element_type=F32)

    c, s1, s2 = c_ref[...], s1_ref[...], s2_ref[...]

    q = proj(_Q0, _K0)
    qms = jnp.dot((q * q).astype(BF16), bd_ref[...], preferred_element_type=F32)
    qg = jnp.concatenate([qg_ref[...]] * (ATTN_DIM // LANES), axis=1)
    qn = q * lax.rsqrt(qms + NORM_EPS) * qg
    q_ref[...] = (_rope_chunks(qn, c, s1, s2) * (HEAD_DIM ** -0.5)).astype(BF16)

    k = proj(_K0, _V0)
    kms = jnp.dot((k * k).astype(BF16), bd_ref[0:KV_DIM, 0:KV_DIM], preferred_element_type=F32)
    kn = k * lax.rsqrt(kms + NORM_EPS) * kg_ref[...]
    k_ref[...] = _rope_chunks(kn, c, s1, s2).astype(BF16)

    v_ref[...] = proj(_V0, _CB0).astype(BF16)
    cb_ref[...] = proj(_CB0, _CC0).astype(BF16)
    u_ref[...] = (proj(_CC0, _CX0) * proj(_CX0, _GA0)).astype(BF16)
    ga_ref[...] = jax.nn.sigmoid(proj(_GA0, _GC0)).astype(BF16)
    gc_ref[...] = jax.nn.sigmoid(proj(_GC0, _END)).astype(BF16)


def _mix_kernel(sink_ref, x_ref, q_ref, kp_ref, kc_ref, kn_ref, vp_ref, vc_ref, vn_ref,
                cb_ref, up_ref, uc_ref, un_ref, ga_ref, gc_ref,
                cw_ref, cbias_ref, wa_ref, wc_ref, wo_ref, g2_ref,
                wrh_ref, wrl_ref, br_ref, tri_ref,
                x1_ref, h2_ref, ri_ref, rf_ref, cnt_ref,
                ubuf, base_ref, *, tq, n_seq_blocks):
    b = pl.program_id(0)
    j = pl.program_id(1)
    nj = pl.num_programs(1)
    nsub = tq // WINDOW

    @pl.when((b == 0) & (j == 0))
    def _():
        base_ref[...] = jnp.zeros_like(base_ref)

    kbuf = jnp.concatenate([kp_ref[...], kc_ref[...], kn_ref[...]], axis=0)
    vbuf = jnp.concatenate([vp_ref[...], vc_ref[...], vn_ref[...]], axis=0)
    lane = lax.broadcasted_iota(I32, (WINDOW, LANES), 1)
    low_half = lane < HEAD_DIM
    qrow = lax.broadcasted_iota(I32, (WINDOW, 3 * WINDOW), 0)
    kcol = lax.broadcasted_iota(I32, (WINDOW, 3 * WINDOW), 1)
    dd = kcol - qrow
    band = (dd >= 0) & (dd <= 2 * WINDOW)
    attn_rows = []
    for n in range(nsub):
        lo_lim = jnp.where(j > 0, 0, WINDOW) if n == 0 else 0
        hi_lim = jnp.where(j < nj - 1, 3 * WINDOW, 2 * WINDOW) if n == nsub - 1 else 3 * WINDOW
        valid = band & (kcol >= lo_lim) & (kcol < hi_lim)
        qs = q_ref[n * WINDOW:(n + 1) * WINDOW, :]
        zero = jnp.zeros((WINDOW, LANES), BF16)
        q_stack = []
        for hd in range(N_HEADS):
            col = qs[:, (hd % 4) * LANES:(hd % 4 + 1) * LANES]
            keep = low_half if hd < 4 else jnp.logical_not(low_half)
            q_stack.append(jnp.where(keep, col, zero))
        q_stack = jnp.concatenate(q_stack, axis=0)
        kwin = kbuf[n * WINDOW:(n + 3) * WINDOW, :]
        vwin = vbuf[n * WINDOW:(n + 3) * WINDOW, :]
        s = lax.dot_general(q_stack, kwin, (((1,), (1,)), ((), ())),
                            preferred_element_type=F32)
        p_list, inv_list = [], []
        for hd in range(N_HEADS):
            sink = sink_ref[hd]
            sh = jnp.where(valid, s[hd * WINDOW:(hd + 1) * WINDOW, :], MASK_VALUE)
            m = jnp.maximum(jnp.max(sh, axis=-1, keepdims=True), sink)
            e = jnp.exp(sh - m)
            den = jnp.sum(e, axis=-1, keepdims=True) + jnp.exp(sink - m)
            p_list.append(e.astype(BF16))
            inv_list.append(1.0 / den)
        p = jnp.concatenate(p_list, axis=0)
        o = jnp.dot(p, vwin, preferred_element_type=F32)
        cols = []
        for c4 in range(4):
            oa = o[c4 * WINDOW:(c4 + 1) * WINDOW, :] * inv_list[c4]
            ob = o[(c4 + 4) * WINDOW:(c4 + 5) * WINDOW, :] * inv_list[c4 + 4]
            cols.append(jnp.where(low_half, oa, ob))
        attn_rows.append(jnp.concatenate(cols, axis=1).astype(BF16))
    attn = attn_rows[0] if nsub == 1 else jnp.concatenate(attn_rows, axis=0)

    ubuf[8:8 + tq, :] = uc_ref[...].astype(F32)
    prev_rows = up_ref[...].astype(F32)
    next_rows = un_ref[...].astype(F32)
    ubuf[7:8, :] = prev_rows[15:16, :] * (j > 0).astype(F32)
    ubuf[8 + tq:9 + tq, :] = next_rows[0:1, :] * (j < nj - 1).astype(F32)
    cw = cw_ref[...]
    conv = (ubuf[7:7 + tq, :] * cw[0:1, :] + ubuf[8:8 + tq, :] * cw[1:2, :]
            + ubuf[9:9 + tq, :] * cw[2:3, :] + cbias_ref[...])
    cgate = (cb_ref[...].astype(F32) * conv).astype(BF16)

    merged = (ga_ref[...].astype(F32) * jnp.dot(attn, wa_ref[...], preferred_element_type=F32)
              + gc_ref[...].astype(F32) * jnp.dot(cgate, wc_ref[...], preferred_element_type=F32))
    x1 = x_ref[...] + jnp.dot(merged.astype(BF16), wo_ref[...], preferred_element_type=F32)
    x1_ref[...] = x1

    ms = jnp.mean(x1 * x1, axis=-1, keepdims=True)
    h2 = x1 * lax.rsqrt(ms + NORM_EPS) * g2_ref[...]
    for c in range(TOK_SUBLANES):
        h2_ref[pl.ds(c, tq, stride=TOK_SUBLANES), :] = h2[:, c * LANES:(c + 1) * LANES]
    h_hi = h2.astype(BF16)
    h_lo = (h2 - h_hi.astype(F32)).astype(BF16)
    nt = (((1,), (1,)), ((), ()))
    logits = (lax.dot_general(wrh_ref[...], h_hi, nt, preferred_element_type=F32)
              + lax.dot_general(wrh_ref[...], h_lo, nt, preferred_element_type=F32)
              + lax.dot_general(wrl_ref[...], h_hi, nt, preferred_element_type=F32)
              + br_ref[...])
    grow = lax.broadcasted_iota(I32, (8, tq), 0)
    gl = jnp.where(grow < N_GROUPS, logits[0:8, :], NEG_BIG)
    gmax = jnp.max(gl, axis=0, keepdims=True)
    grp = jnp.min(jnp.where(gl == gmax, grow, 8), axis=0, keepdims=True)
    p_grp = 1.0 / jnp.sum(jnp.exp(gl - gmax), axis=0, keepdims=True)
    el = logits[8:ROUTER_ROWS, :]
    erow = lax.broadcasted_iota(I32, (N_EXPERTS, tq), 0)
    in_grp = (erow // EXPERTS_PER_GROUP) == grp
    l_a = jnp.where(in_grp, el, NEG_BIG)
    l1 = jnp.max(l_a, axis=0, keepdims=True)
    i1 = jnp.min(jnp.where(l_a == l1, erow, N_EXPERTS), axis=0, keepdims=True)
    l_b = jnp.where(erow == i1, NEG_BIG, l_a)
    l2 = jnp.max(l_b, axis=0, keepdims=True)
    i2 = jnp.min(jnp.where(l_b == l2, erow, N_EXPERTS), axis=0, keepdims=True)
    r21 = jnp.exp(l2 - l1)
    t1 = 1.0 / (1.0 + r21)
    gate1 = p_grp * t1
    gate2 = p_grp * (r21 * t1)
    oh1 = (erow == i1).astype(F32)
    oh2 = (erow == i2).astype(F32)
    oh = oh1 + oh2
    before = jnp.dot(oh.astype(BF16), tri_ref[...], preferred_element_type=F32) + base_ref[...][:, 0:1]
    rank1 = jnp.sum(oh1 * before, axis=0, keepdims=True)
    rank2 = jnp.sum(oh2 * before, axis=0, keepdims=True)
    new_base = base_ref[...] + jnp.sum(oh, axis=1, keepdims=True)
    base_ref[...] = new_base
    cnt_ref[...] = new_base
    zi = jnp.zeros((4, tq), I32)
    ri_ref[...] = jnp.concatenate([i1, i2, rank1.astype(I32), rank2.astype(I32), zi], axis=0)
    rf_ref[...] = jnp.concatenate([gate1, gate2, jnp.zeros((6, tq), F32)], axis=0)


def _moe_kernel(be_ref, nused_ref, x_ref, wg_ref, wu_ref, wd_ref, y_ref):
    i = pl.program_id(0)

    @pl.when(i < nused_ref[0])
    def _():
        xb = jnp.concatenate(
            [x_ref[pl.ds(c, MOE_ROWS, stride=TOK_SUBLANES), :].astype(BF16) for c in range(TOK_SUBLANES)],
            axis=1)
        g = jnp.dot(xb, wg_ref[...], preferred_element_type=F32)
        u = jnp.dot(xb, wu_ref[...], preferred_element_type=F32)
        hid = (g * jax.nn.sigmoid(g) * u).astype(BF16)
        y = jnp.dot(hid, wd_ref[...], preferred_element_type=F32)
        for c in range(TOK_SUBLANES):
            y_ref[pl.ds(c, MOE_ROWS, stride=TOK_SUBLANES), :] = y[:, c * LANES:(c + 1) * LANES]

    @pl.when(i >= nused_ref[0])
    def _():
        y_ref[...] = jnp.zeros_like(y_ref)


def _combine_kernel(x1_ref, gate_ref, y0_ref, y1_ref, o_ref, *, tm):
    g = gate_ref[...]
    g0, g1 = g[:, 0:1], g[:, 1:2]
    for c in range(TOK_SUBLANES):
        cols = slice(c * LANES, (c + 1) * LANES)
        o_ref[:, cols] = (x1_ref[:, cols] + g0 * y0_ref[pl.ds(c, tm, stride=TOK_SUBLANES), :]
                          + g1 * y1_ref[pl.ds(c, tm, stride=TOK_SUBLANES), :])


def _sc_worker_span(n_tokens):
    info = plsc.get_sparse_core_info()
    n_workers = info.num_cores * info.num_subcores
    assert n_tokens % (n_workers * SC_WINDOW) == 0
    return info.num_cores, n_tokens // n_workers


def _sc_dispatch(h3, d0, d1, n_rows):
    t = h3.shape[0]
    n_cores, per_w = _sc_worker_span(t)
    mesh = plsc.VectorSubcoreMesh(core_axis_name="c", subcore_axis_name="s")

    @functools.partial(
        pl.kernel, mesh=mesh,
        out_type=jax.ShapeDtypeStruct((n_rows,) + h3.shape[1:], h3.dtype),
        scratch_types=[pltpu.VMEM((SC_WINDOW,), I32), pltpu.VMEM((SC_WINDOW,), I32),
                       pltpu.VMEM((SC_WINDOW,) + h3.shape[1:], h3.dtype)],
        name="sc_dispatch")
    def run(h_hbm, d0_hbm, d1_hbm, o_hbm, i0_v, i1_v, rows_v):
        base = (lax.axis_index("s") * n_cores + lax.axis_index("c")) * per_w

        @pl.loop(0, per_w // SC_WINDOW)
        def _(w):
            off = base + w * SC_WINDOW
            pltpu.sync_copy(d0_hbm.at[pl.ds(off, SC_WINDOW)], i0_v)
            pltpu.sync_copy(d1_hbm.at[pl.ds(off, SC_WINDOW)], i1_v)
            pltpu.sync_copy(h_hbm.at[pl.ds(off, SC_WINDOW)], rows_v)
            pltpu.sync_copy(rows_v, o_hbm.at[i0_v])
            pltpu.sync_copy(rows_v, o_hbm.at[i1_v])

    return run(h3, d0, d1)


def _sc_gather(y3, d0, d1):
    t = d0.shape[0]
    n_cores, per_w = _sc_worker_span(t)
    mesh = plsc.VectorSubcoreMesh(core_axis_name="c", subcore_axis_name="s")

    @functools.partial(
        pl.kernel, mesh=mesh,
        out_type=jax.ShapeDtypeStruct((2, t) + y3.shape[1:], y3.dtype),
        scratch_types=[pltpu.VMEM((SC_WINDOW,), I32), pltpu.VMEM((SC_WINDOW,), I32),
                       pltpu.VMEM((SC_WINDOW,) + y3.shape[1:], y3.dtype),
                       pltpu.VMEM((SC_WINDOW,) + y3.shape[1:], y3.dtype)],
        name="sc_gather")
    def run(y_hbm, d0_hbm, d1_hbm, o_hbm, i0_v, i1_v, r0_v, r1_v):
        base = (lax.axis_index("s") * n_cores + lax.axis_index("c")) * per_w

        @pl.loop(0, per_w // SC_WINDOW)
        def _(w):
            off = base + w * SC_WINDOW
            pltpu.sync_copy(d0_hbm.at[pl.ds(off, SC_WINDOW)], i0_v)
            pltpu.sync_copy(d1_hbm.at[pl.ds(off, SC_WINDOW)], i1_v)
            pltpu.sync_copy(y_hbm.at[i0_v], r0_v)
            pltpu.sync_copy(y_hbm.at[i1_v], r1_v)
            pltpu.sync_copy(r0_v, o_hbm.at[0, pl.ds(off, SC_WINDOW)])
            pltpu.sync_copy(r1_v, o_hbm.at[1, pl.ds(off, SC_WINDOW)])

    return run(y3, d0, d1)


def _rope_lane_tables(seq_len):
    half = ROT_DIM // 2
    inv_freq = ROPE_THETA ** (-jnp.arange(0, ROT_DIM, 2, dtype=F32) / ROT_DIM)
    ang = jnp.arange(seq_len, dtype=F32)[:, None] * inv_freq[None, :]
    cos, sin = jnp.cos(ang), jnp.sin(ang)
    m = np.arange(LANES) % HEAD_DIM
    f = m % half
    c = jnp.where(m[None, :] < ROT_DIM, cos[:, f], 1.0)
    s1 = jnp.where(m[None, :] < half, -sin[:, f], 0.0)
    s2 = jnp.where((m[None, :] >= half) & (m[None, :] < ROT_DIM), sin[:, f], 0.0)
    return c.astype(F32), s1.astype(F32), s2.astype(F32)


def _layer(x, norm1_g, w_in, q_norm_g, k_norm_g, attn_sink, conv_w, conv_b, w_attn_proj,
           w_conv_proj, w_out, norm2_g, w_rg, b_rg, w_re, b_re, w_gate, w_up, w_down):
    bsz, seq, d = x.shape
    t = bsz * seq
    tm = min(512, t)
    tq = min(256, seq)
    assert d == D_MODEL and seq % tq == 0 and t % tm == 0 and seq % tm == 0 and tq % WINDOW == 0
    xf = x.reshape(t, d)

    hpg = N_HEADS // N_KV_HEADS
    wq_p = (w_in[:, :ATTN_DIM].reshape(d, N_KV_HEADS, hpg, HEAD_DIM).transpose(0, 2, 1, 3)
            .reshape(d, ATTN_DIM))
    w_in_p = jnp.concatenate([wq_p, w_in[:, ATTN_DIM:]], axis=1).astype(BF16)
    wa_p = (w_attn_proj.reshape(N_KV_HEADS, hpg, HEAD_DIM, d).transpose(1, 0, 2, 3)
            .reshape(ATTN_DIM, d).astype(BF16))
    bd = jnp.asarray(np.kron(np.eye(N_HEADS), np.full((HEAD_DIM, HEAD_DIM), 1.0 / HEAD_DIM)), BF16)
    qg = jnp.tile(q_norm_g, LANES // HEAD_DIM).reshape(1, LANES)
    kg = jnp.tile(k_norm_g, LANES // HEAD_DIM).reshape(1, LANES)
    ctab, s1tab, s2tab = _rope_lane_tables(seq)
    nseq_t = seq // tm

    row = lambda w: pl.BlockSpec((tm, w), lambda i: (i, 0))
    full = lambda a: pl.BlockSpec(a.shape, lambda i: (0,) * a.ndim)
    tab = pl.BlockSpec((tm, LANES), lambda i: (i % nseq_t, 0))
    q, k, v, cb, u, ga, gc = pl.pallas_call(
        _proj_kernel,
        grid=(t // tm,),
        in_specs=[row(d), full(norm1_g.reshape(1, d)), full(w_in_p), full(bd), full(qg), full(kg),
                  tab, tab, tab],
        out_specs=[row(ATTN_DIM), row(KV_DIM), row(KV_DIM), row(CONV_DIM), row(CONV_DIM),
                   row(D_MODEL), row(D_MODEL)],
        out_shape=[jax.ShapeDtypeStruct((t, w), BF16)
                   for w in (ATTN_DIM, KV_DIM, KV_DIM, CONV_DIM, CONV_DIM, D_MODEL, D_MODEL)],
        compiler_params=pltpu.CompilerParams(dimension_semantics=("arbitrary",),
                                             vmem_limit_bytes=VMEM_LIMIT),
        name="proj",
    )(xf, norm1_g.reshape(1, d), w_in_p, bd, qg, kg, ctab, s1tab, s2tab)

    nj = seq // tq
    wpb = tq // WINDOW
    nb_seq = seq // WINDOW
    hpb = tq // 16
    rowq = lambda w: pl.BlockSpec((tq, w), lambda b, j: (b * nj + j, 0))
    prev_blk = pl.BlockSpec((WINDOW, KV_DIM),
                            lambda b, j: (b * nb_seq + jnp.maximum(j * wpb - 1, 0), 0))
    next_blk = pl.BlockSpec((WINDOW, KV_DIM),
                            lambda b, j: (b * nb_seq + jnp.minimum((j + 1) * wpb, nb_seq - 1), 0))
    uprev = pl.BlockSpec((16, CONV_DIM),
                         lambda b, j: (jnp.maximum((b * nj + j) * hpb - 1, 0), 0))
    unext = pl.BlockSpec((16, CONV_DIM),
                         lambda b, j: (jnp.minimum((b * nj + j + 1) * hpb, t // 16 - 1), 0))
    full2 = lambda a: pl.BlockSpec(a.shape, lambda b, j: (0,) * a.ndim)
    colq = lambda r: pl.BlockSpec((r, tq), lambda b, j: (0, b * nj + j))

    wr = jnp.concatenate([w_rg.T, jnp.zeros((8 - N_GROUPS, d), F32), w_re.T], axis=0)
    wr_hi = wr.astype(BF16)
    wr_lo = (wr - wr_hi.astype(F32)).astype(BF16)
    br = jnp.concatenate([b_rg, jnp.zeros((8 - N_GROUPS,), F32), b_re]).reshape(ROUTER_ROWS, 1)
    tri = jnp.asarray(np.triu(np.ones((tq, tq), np.float32), 1), BF16)
    wc_b = w_conv_proj.astype(BF16)
    wo_b = w_out.astype(BF16)
    cbias = conv_b.reshape(1, CONV_DIM)
    g2 = norm2_g.reshape(1, d)

    x1, h2, route_i, route_f, counts = pl.pallas_call(
        functools.partial(_mix_kernel, tq=tq, n_seq_blocks=nb_seq),
        grid=(bsz, nj),
        in_specs=[pl.BlockSpec(memory_space=pltpu.SMEM),
                  rowq(d), rowq(ATTN_DIM), prev_blk, rowq(KV_DIM), next_blk,
                  prev_blk, rowq(KV_DIM), next_blk,
                  rowq(CONV_DIM), uprev, rowq(CONV_DIM), unext, rowq(D_MODEL), rowq(D_MODEL),
                  full2(conv_w), full2(cbias), full2(wa_p), full2(wc_b), full2(wo_b), full2(g2),
                  full2(wr_hi), full2(wr_lo), full2(br), full2(tri)],
        out_specs=[rowq(d), pl.BlockSpec((tq * TOK_SUBLANES, LANES), lambda b, j: (b * nj + j, 0)),
                   colq(8), colq(8),
                   pl.BlockSpec((N_EXPERTS, LANES), lambda b, j: (0, 0))],
        out_shape=[jax.ShapeDtypeStruct((t, d), F32),
                   jax.ShapeDtypeStruct((t * TOK_SUBLANES, LANES), F32),
                   jax.ShapeDtypeStruct((8, t), I32), jax.ShapeDtypeStruct((8, t), F32),
                   jax.ShapeDtypeStruct((N_EXPERTS, LANES), F32)],
        scratch_shapes=[pltpu.VMEM((tq + 16, CONV_DIM), F32), pltpu.VMEM((N_EXPERTS, LANES), F32)],
        compiler_params=pltpu.CompilerParams(dimension_semantics=("arbitrary", "arbitrary"),
                                             vmem_limit_bytes=VMEM_LIMIT),
        name="mix",
    )(attn_sink, xf, q, k, k, k, v, v, v, cb, u, u, u, ga, gc,
      conv_w, cbias, wa_p, wc_b, wo_b, g2, wr_hi, wr_lo, br, tri)

    n_rows = 2 * t + N_EXPERTS * MOE_ROWS
    nblk = n_rows // MOE_ROWS
    cnt = counts[:, 0].astype(I32)
    padded = ((cnt + MOE_ROWS - 1) // MOE_ROWS) * MOE_ROWS
    pad_ends = jnp.cumsum(padded)
    pad_starts = pad_ends - padded
    eids = jnp.arange(N_EXPERTS, dtype=I32)[:, None, None]
    dest = jnp.sum(jnp.where(route_i[None, 0:2] == eids, pad_starts[:, None, None], 0), axis=0) \
        + route_i[2:4]
    blk_start = jnp.arange(nblk, dtype=I32) * MOE_ROWS
    block_e = jnp.minimum(jnp.sum((pad_ends[None, :] <= blk_start[:, None]).astype(I32), axis=1),
                          N_EXPERTS - 1)
    nused = (pad_ends[-1:] // MOE_ROWS).astype(I32)

    x_pad = _sc_dispatch(h2.reshape(t, TOK_SUBLANES, LANES), dest[0], dest[1], n_rows)
    x_pad = x_pad.reshape(n_rows * TOK_SUBLANES, LANES)

    blk_rows = MOE_ROWS * TOK_SUBLANES
    wspec = lambda a, b: pl.BlockSpec((None, a, b), lambda i, be, nu: (be[i], 0, 0))
    y_pad = pl.pallas_call(
        _moe_kernel,
        grid_spec=pltpu.PrefetchScalarGridSpec(
            num_scalar_prefetch=2,
            grid=(nblk,),
            in_specs=[pl.BlockSpec((blk_rows, LANES), lambda i, be, nu: (i, 0)),
                      wspec(d, D_FF), wspec(d, D_FF), wspec(D_FF, d)],
            out_specs=pl.BlockSpec((blk_rows, LANES), lambda i, be, nu: (i, 0))),
        out_shape=jax.ShapeDtypeStruct((n_rows * TOK_SUBLANES, LANES), F32),
        compiler_params=pltpu.CompilerParams(dimension_semantics=("arbitrary",),
                                             vmem_limit_bytes=VMEM_LIMIT),
        name="moe",
    )(block_e, nused, x_pad, w_gate.astype(BF16), w_up.astype(BF16), w_down.astype(BF16))

    yk = _sc_gather(y_pad.reshape(n_rows, TOK_SUBLANES, LANES), dest[0], dest[1])
    yk = yk.reshape(2 * t * TOK_SUBLANES, LANES)
    tc = min(256, t)
    nt = t // tc
    gates_t = route_f[0:2].T
    out = pl.pallas_call(
        functools.partial(_combine_kernel, tm=tc),
        grid=(nt,),
        in_specs=[pl.BlockSpec((tc, d), lambda i: (i, 0)),
                  pl.BlockSpec((tc, 2), lambda i: (i, 0)),
                  pl.BlockSpec((tc * TOK_SUBLANES, LANES), lambda i: (i, 0)),
                  pl.BlockSpec((tc * TOK_SUBLANES, LANES), lambda i: (nt + i, 0))],
        out_specs=pl.BlockSpec((tc, d), lambda i: (i, 0)),
        out_shape=jax.ShapeDtypeStruct((t, d), F32),
        compiler_params=pltpu.CompilerParams(dimension_semantics=("arbitrary",),
                                             vmem_limit_bytes=VMEM_LIMIT),
        name="combine",
    )(x1, gates_t, yk, yk)
    return out.reshape(bsz, seq, d)


def kernel(x, norm1_g, w_in, q_norm_g, k_norm_g, attn_sink, conv_w, conv_b, w_attn_proj, w_conv_proj, w_out, norm2_g, w_router_group, b_router_group, w_router_expert, b_router_expert, w_gate_e, w_up_e, w_down_e):
    for l in range(norm1_g.shape[0]):
        x = _layer(x, norm1_g[l], w_in[l], q_norm_g[l], k_norm_g[l], attn_sink[l], conv_w[l],
                   conv_b[l], w_attn_proj[l], w_conv_proj[l], w_out[l], norm2_g[l],
                   w_router_group[l], b_router_group[l], w_router_expert[l], b_router_expert[l],
                   w_gate_e[l], w_up_e[l], w_down_e[l])
    return x
```

```python
import functools

import jax
import jax.numpy as jnp
import numpy as np
from jax import lax
from jax.experimental import pallas as pl
from jax.experimental.pallas import tpu as pltpu
from jax.experimental.pallas import tpu_sc as plsc

F32 = jnp.float32
BF16 = jnp.bfloat16
I32 = jnp.int32

D_MODEL = 1024
N_HEADS = 8
N_KV_HEADS = 2
HEAD_DIM = 64
ATTN_DIM = N_HEADS * HEAD_DIM
KV_DIM = N_KV_HEADS * HEAD_DIM
WINDOW = 128
ROT_DIM = HEAD_DIM // 4
ROPE_THETA = 500000.0
CONV_DIM = D_MODEL // 2
N_GROUPS = 4
EXPERTS_PER_GROUP = 8
N_EXPERTS = N_GROUPS * EXPERTS_PER_GROUP
D_FF = D_MODEL // 4
NORM_EPS = 1e-6
MASK_VALUE = -1e30
NEG_BIG = -3.0e38

LANES = 128
TOK_SUBLANES = D_MODEL // LANES
SC_WINDOW = 32
ROUTER_ROWS = 8 + N_EXPERTS
MOE_ROWS = 512
VMEM_LIMIT = 56 * 1024 * 1024

_Q0, _K0, _V0, _CB0, _CC0, _CX0, _GA0, _GC0, _END = (
    0, 512, 640, 768, 1280, 1792, 2304, 3328, 4352)


def _rope_chunks(t, c, s1, s2):
    outs = []
    for j in range(t.shape[1] // LANES):
        tj = t[:, j * LANES:(j + 1) * LANES]
        outs.append(tj * c + pltpu.roll(tj, LANES - ROT_DIM // 2, 1) * s1
                    + pltpu.roll(tj, ROT_DIM // 2, 1) * s2)
    return outs[0] if len(outs) == 1 else jnp.concatenate(outs, axis=1)


def _proj_kernel(x_ref, g1_ref, w_ref, bd_ref, qg_ref, kg_ref, c_ref, s1_ref, s2_ref,
                 q_ref, k_ref, v_ref, cb_ref, u_ref, ga_ref, gc_ref):
    x = x_ref[...]
    ms = jnp.mean(x * x, axis=-1, keepdims=True)
    h = (x * lax.rsqrt(ms + NORM_EPS) * g1_ref[...]).astype(BF16)

    def proj(a, b):
        return jnp.dot(h, w_ref[:, a:b], preferred_element_type=F32)

    c, s1, s2 = c_ref[...], s1_ref[...], s2_ref[...]

    q = proj(_Q0, _K0)
    qms = jnp.dot((q * q).astype(BF16), bd_ref[...], preferred_element_type=F32)
    qg = jnp.concatenate([qg_ref[...]] * (ATTN_DIM // LANES), axis=1)
    qn = q * lax.rsqrt(qms + NORM_EPS) * qg
    q_ref[...] = (_rope_chunks(qn, c, s1, s2) * (HEAD_DIM ** -0.5)).astype(BF16)

    k = proj(_K0, _V0)
    kms = jnp.dot((k * k).astype(BF16), bd_ref[0:KV_DIM, 0:KV_DIM], preferred_element_type=F32)
    kn = k * lax.rsqrt(kms + NORM_EPS) * kg_ref[...]
    k_ref[...] = _rope_chunks(kn, c, s1, s2).astype(BF16)

    v_ref[...] = proj(_V0, _CB0).astype(BF16)
    cb_ref[...] = proj(_CB0, _CC0).astype(BF16)
    u_ref[...] = (proj(_CC0, _CX0) * proj(_CX0, _GA0)).astype(BF16)
    ga_ref[...] = jax.nn.sigmoid(proj(_GA0, _GC0)).astype(BF16)
    gc_ref[...] = jax.nn.sigmoid(proj(_GC0, _END)).astype(BF16)


def _mix_kernel(sink_ref, x_ref, q_ref, kp_ref, kc_ref, kn_ref, vp_ref, vc_ref, vn_ref,
                cb_ref, up_ref, uc_ref, un_ref, ga_ref, gc_ref,
                cw_ref, cbias_ref, wa_ref, wc_ref, wo_ref, g2_ref,
                wrh_ref, wrl_ref, br_ref, tri_ref,
                x1_ref, h2_ref, ri_ref, rf_ref, cnt_ref,
                ubuf, base_ref, *, tq, n_seq_blocks):
    b = pl.program_id(0)
    j = pl.program_id(1)
    nj = pl.num_programs(1)
    nsub = tq // WINDOW

    @pl.when((b == 0) & (j == 0))
    def _():
        base_ref[...] = jnp.zeros_like(base_ref)

    kbuf = jnp.concatenate([kp_ref[...], kc_ref[...], kn_ref[...]], axis=0)
    vbuf = jnp.concatenate([vp_ref[...], vc_ref[...], vn_ref[...]], axis=0)
    lane = lax.broadcasted_iota(I32, (WINDOW, LANES), 1)
    low_half = lane < HEAD_DIM
    qrow = lax.broadcasted_iota(I32, (WINDOW, 3 * WINDOW), 0)
    kcol = lax.broadcasted_iota(I32, (WINDOW, 3 * WINDOW), 1)
    dd = kcol - qrow
    band = (dd >= 0) & (dd <= 2 * WINDOW)
    attn_rows = []
    for n in range(nsub):
        lo_lim = jnp.where(j > 0, 0, WINDOW) if n == 0 else 0
        hi_lim = jnp.where(j < nj - 1, 3 * WINDOW, 2 * WINDOW) if n == nsub - 1 else 3 * WINDOW
        valid = band & (kcol >= lo_lim) & (kcol < hi_lim)
        qs = q_ref[n * WINDOW:(n + 1) * WINDOW, :]
        zero = jnp.zeros((WINDOW, LANES), BF16)
        q_stack = []
        for hd in range(N_HEADS):
            col = qs[:, (hd % 4) * LANES:(hd % 4 + 1) * LANES]
            keep = low_half if hd < 4 else jnp.logical_not(low_half)
            q_stack.append(jnp.where(keep, col, zero))
        q_stack = jnp.concatenate(q_stack, axis=0)
        kwin = kbuf[n * WINDOW:(n + 3) * WINDOW, :]
        vwin = vbuf[n * WINDOW:(n + 3) * WINDOW, :]
        s = lax.dot_general(q_stack, kwin, (((1,), (1,)), ((), ())),
                            preferred_element_type=F32)
        p_list, inv_list = [], []
        for hd in range(N_HEADS):
            sink = sink_ref[hd]
            sh = jnp.where(valid, s[hd * WINDOW:(hd + 1) * WINDOW, :], MASK_VALUE)
            m = jnp.maximum(jnp.max(sh, axis=-1, keepdims=True), sink)
            e = jnp.exp(sh - m)
            den = jnp.sum(e, axis=-1, keepdims=True) + jnp.exp(sink - m)
            p_list.append(e.astype(BF16))
            inv_list.append(1.0 / den)
        p = jnp.concatenate(p_list, axis=0)
        o = jnp.dot(p, vwin, preferred_element_type=F32)
        cols = []
        for c4 in range(4):
            oa = o[c4 * WINDOW:(c4 + 1) * WINDOW, :] * inv_list[c4]
            ob = o[(c4 + 4) * WINDOW:(c4 + 5) * WINDOW, :] * inv_list[c4 + 4]
            cols.append(jnp.where(low_half, oa, ob))
        attn_rows.append(jnp.concatenate(cols, axis=1).astype(BF16))
    attn = attn_rows[0] if nsub == 1 else jnp.concatenate(attn_rows, axis=0)

    ubuf[8:8 + tq, :] = uc_ref[...].astype(F32)
    prev_rows = up_ref[...].astype(F32)
    next_rows = un_ref[...].astype(F32)
    ubuf[7:8, :] = prev_rows[15:16, :] * (j > 0).astype(F32)
    ubuf[8 + tq:9 + tq, :] = next_rows[0:1, :] * (j < nj - 1).astype(F32)
    cw = cw_ref[...]
    conv = (ubuf[7:7 + tq, :] * cw[0:1, :] + ubuf[8:8 + tq, :] * cw[1:2, :]
            + ubuf[9:9 + tq, :] * cw[2:3, :] + cbias_ref[...])
    cgate = (cb_ref[...].astype(F32) * conv).astype(BF16)

    merged = (ga_ref[...].astype(F32) * jnp.dot(attn, wa_ref[...], preferred_element_type=F32)
              + gc_ref[...].astype(F32) * jnp.dot(cgate, wc_ref[...], preferred_element_type=F32))
    x1 = x_ref[...] + jnp.dot(merged.astype(BF16), wo_ref[...], preferred_element_type=F32)
    x1_ref[...] = x1

    ms = jnp.mean(x1 * x1, axis=-1, keepdims=True)
    h2 = x1 * lax.rsqrt(ms + NORM_EPS) * g2_ref[...]
    for c in range(TOK_SUBLANES):
        h2_ref[pl.ds(c, tq, stride=TOK_SUBLANES), :] = h2[:, c * LANES:(c + 1) * LANES]
    h_hi = h2.astype(BF16)
    h_lo = (h2 - h_hi.astype(F32)).astype(BF16)
    nt = (((1,), (1,)), ((), ()))
    logits = (lax.dot_general(wrh_ref[...], h_hi, nt, preferred_element_type=F32)
              + lax.dot_general(wrh_ref[...], h_lo, nt, preferred_element_type=F32)
              + lax.dot_general(wrl_ref[...], h_hi, nt, preferred_element_type=F32)
              + br_ref[...])
    grow = lax.broadcasted_iota(I32, (8, tq), 0)
    gl = jnp.where(grow < N_GROUPS, logits[0:8, :], NEG_BIG)
    gmax = jnp.max(gl, axis=0, keepdims=True)
    grp = jnp.min(jnp.where(gl == gmax, grow, 8), axis=0, keepdims=True)
    p_grp = 1.0 / jnp.sum(jnp.exp(gl - gmax), axis=0, keepdims=True)
    el = logits[8:ROUTER_ROWS, :]
    erow = lax.broadcasted_iota(I32, (N_EXPERTS, tq), 0)
    in_grp = (erow // EXPERTS_PER_GROUP) == grp
    l_a = jnp.where(in_grp, el, NEG_BIG)
    l1 = jnp.max(l_a, axis=0, keepdims=True)
    i1 = jnp.min(jnp.where(l_a == l1, erow, N_EXPERTS), axis=0, keepdims=True)
    l_b = jnp.where(erow == i1, NEG_BIG, l_a)
    l2 = jnp.max(l_b, axis=0, keepdims=True)
    i2 = jnp.min(jnp.where(l_b == l2, erow, N_EXPERTS), axis=0, keepdims=True)
    r21 = jnp.exp(l2 - l1)
    t1 = 1.0 / (1.0 + r21)
    gate1 = p_grp * t1
    gate2 = p_grp * (r21 * t1)
    oh1 = (erow == i1).astype(F32)
    oh2 = (erow == i2).astype(F32)
    oh = oh1 + oh2
    before = jnp.dot(oh.astype(BF16), tri_ref[...], preferred_element_type=F32) + base_ref[...][:, 0:1]
    rank1 = jnp.sum(oh1 * before, axis=0, keepdims=True)
    rank2 = jnp.sum(oh2 * before, axis=0, keepdims=True)
    new_base = base_ref[...] + jnp.sum(oh, axis=1, keepdims=True)
    base_ref[...] = new_base
    cnt_ref[...] = new_base
    zi = jnp.zeros((4, tq), I32)
    ri_ref[...] = jnp.concatenate([i1, i2, rank1.astype(I32), rank2.astype(I32), zi], axis=0)
    rf_ref[...] = jnp.concatenate([gate1, gate2, jnp.zeros((6, tq), F32)], axis=0)


def _moe_kernel(be_ref, nused_ref, x_ref, wg_ref, wu_ref, wd_ref, y_ref, wg_s, wu_s, wd_s):
    i = pl.program_id(0)

    @pl.when((i == 0) | (be_ref[i] != be_ref[jnp.maximum(i - 1, 0)]))
    def _():
        wg_s[...] = wg_ref[...].astype(BF16)
        wu_s[...] = wu_ref[...].astype(BF16)
        wd_s[...] = wd_ref[...].astype(BF16)

    @pl.when(i < nused_ref[0])
    def _():
        xb = jnp.concatenate(
            [x_ref[pl.ds(c, MOE_ROWS, stride=TOK_SUBLANES), :].astype(BF16) for c in range(TOK_SUBLANES)],
            axis=1)
        g = jnp.dot(xb, wg_s[...], preferred_element_type=F32)
        u = jnp.dot(xb, wu_s[...], preferred_element_type=F32)
        hid = (g * jax.nn.sigmoid(g) * u).astype(BF16)
        y = jnp.dot(hid, wd_s[...], preferred_element_type=F32)
        for c in range(TOK_SUBLANES):
            y_ref[pl.ds(c, MOE_ROWS, stride=TOK_SUBLANES), :] = y[:, c * LANES:(c + 1) * LANES]

    @pl.when(i >= nused_ref[0])
    def _():
        y_ref[...] = jnp.zeros_like(y_ref)


def _combine_kernel(x1_ref, gate_ref, y0_ref, y1_ref, o_ref, *, tm):
    g = gate_ref[...]
    g0, g1 = g[:, 0:1], g[:, 1:2]
    for c in range(TOK_SUBLANES):
        cols = slice(c * LANES, (c + 1) * LANES)
        o_ref[:, cols] = (x1_ref[:, cols] + g0 * y0_ref[pl.ds(c, tm, stride=TOK_SUBLANES), :]
                          + g1 * y1_ref[pl.ds(c, tm, stride=TOK_SUBLANES), :])


def _sc_worker_span(n_tokens):
    info = plsc.get_sparse_core_info()
    n_workers = info.num_cores * info.num_subcores
    assert n_tokens % (n_workers * SC_WINDOW) == 0
    return info.num_cores, n_tokens // n_workers


def _sc_dispatch(h3, d0, d1, n_rows):
    t = h3.shape[0]
    n_cores, per_w = _sc_worker_span(t)
    mesh = plsc.VectorSubcoreMesh(core_axis_name="c", subcore_axis_name="s")

    @functools.partial(
        pl.kernel, mesh=mesh,
        out_type=jax.ShapeDtypeStruct((n_rows,) + h3.shape[1:], h3.dtype),
        scratch_types=[pltpu.VMEM((SC_WINDOW,), I32), pltpu.VMEM((SC_WINDOW,), I32),
                       pltpu.VMEM((SC_WINDOW,) + h3.shape[1:], h3.dtype)],
        name="sc_dispatch")
    def run(h_hbm, d0_hbm, d1_hbm, o_hbm, i0_v, i1_v, rows_v):
        base = (lax.axis_index("s") * n_cores + lax.axis_index("c")) * per_w

        @pl.loop(0, per_w // SC_WINDOW)
        def _(w):
            off = base + w * SC_WINDOW
            pltpu.sync_copy(d0_hbm.at[pl.ds(off, SC_WINDOW)], i0_v)
            pltpu.sync_copy(d1_hbm.at[pl.ds(off, SC_WINDOW)], i1_v)
            pltpu.sync_copy(h_hbm.at[pl.ds(off, SC_WINDOW)], rows_v)
            pltpu.sync_copy(rows_v, o_hbm.at[i0_v])
            pltpu.sync_copy(rows_v, o_hbm.at[i1_v])

    return run(h3, d0, d1)


def _sc_gather(y3, d0, d1):
    t = d0.shape[0]
    n_cores, per_w = _sc_worker_span(t)
    mesh = plsc.VectorSubcoreMesh(core_axis_name="c", subcore_axis_name="s")

    @functools.partial(
        pl.kernel, mesh=mesh,
        out_type=jax.ShapeDtypeStruct((2, t) + y3.shape[1:], y3.dtype),
        scratch_types=[pltpu.VMEM((SC_WINDOW,), I32), pltpu.VMEM((SC_WINDOW,), I32),
                       pltpu.VMEM((SC_WINDOW,) + y3.shape[1:], y3.dtype),
                       pltpu.VMEM((SC_WINDOW,) + y3.shape[1:], y3.dtype)],
        name="sc_gather")
    def run(y_hbm, d0_hbm, d1_hbm, o_hbm, i0_v, i1_v, r0_v, r1_v):
        base = (lax.axis_index("s") * n_cores + lax.axis_index("c")) * per_w

        @pl.loop(0, per_w // SC_WINDOW)
        def _(w):
            off = base + w * SC_WINDOW
            pltpu.sync_copy(d0_hbm.at[pl.ds(off, SC_WINDOW)], i0_v)
            pltpu.sync_copy(d1_hbm.at[pl.ds(off, SC_WINDOW)], i1_v)
            pltpu.sync_copy(y_hbm.at[i0_v], r0_v)
            pltpu.sync_copy(y_hbm.at[i1_v], r1_v)
            pltpu.sync_copy(r0_v, o_hbm.at[0, pl.ds(off, SC_WINDOW)])
            pltpu.sync_copy(r1_v, o_hbm.at[1, pl.ds(off, SC_WINDOW)])

    return run(y3, d0, d1)


def _rope_lane_tables(seq_len):
    half = ROT_DIM // 2
    inv_freq = ROPE_THETA ** (-jnp.arange(0, ROT_DIM, 2, dtype=F32) / ROT_DIM)
    ang = jnp.arange(seq_len, dtype=F32)[:, None] * inv_freq[None, :]
    cos, sin = jnp.cos(ang), jnp.sin(ang)
    m = np.arange(LANES) % HEAD_DIM
    f = m % half
    c = jnp.where(m[None, :] < ROT_DIM, cos[:, f], 1.0)
    s1 = jnp.where(m[None, :] < half, -sin[:, f], 0.0)
    s2 = jnp.where((m[None, :] >= half) & (m[None, :] < ROT_DIM), sin[:, f], 0.0)
    return c.astype(F32), s1.astype(F32), s2.astype(F32)


def _layer(x, norm1_g, w_in, q_norm_g, k_norm_g, attn_sink, conv_w, conv_b, w_attn_proj,
           w_conv_proj, w_out, norm2_g, w_rg, b_rg, w_re, b_re, w_gate, w_up, w_down):
    bsz, seq, d = x.shape
    t = bsz * seq
    tm = min(512, t)
    tq = min(256, seq)
    assert d == D_MODEL and seq % tq == 0 and t % tm == 0 and seq % tm == 0 and tq % WINDOW == 0
    xf = x.reshape(t, d)

    hpg = N_HEADS // N_KV_HEADS
    wq_p = (w_in[:, :ATTN_DIM].reshape(d, N_KV_HEADS, hpg, HEAD_DIM).transpose(0, 2, 1, 3)
            .reshape(d, ATTN_DIM))
    w_in_p = jnp.concatenate([wq_p, w_in[:, ATTN_DIM:]], axis=1).astype(BF16)
    wa_p = (w_attn_proj.reshape(N_KV_HEADS, hpg, HEAD_DIM, d).transpose(1, 0, 2, 3)
            .reshape(ATTN_DIM, d).astype(BF16))
    bd = jnp.asarray(np.kron(np.eye(N_HEADS), np.full((HEAD_DIM, HEAD_DIM), 1.0 / HEAD_DIM)), BF16)
    qg = jnp.tile(q_norm_g, LANES // HEAD_DIM).reshape(1, LANES)
    kg = jnp.tile(k_norm_g, LANES // HEAD_DIM).reshape(1, LANES)
    ctab, s1tab, s2tab = _rope_lane_tables(seq)
    nseq_t = seq // tm

    row = lambda w: pl.BlockSpec((tm, w), lambda i: (i, 0))
    full = lambda a: pl.BlockSpec(a.shape, lambda i: (0,) * a.ndim)
    tab = pl.BlockSpec((tm, LANES), lambda i: (i % nseq_t, 0))
    q, k, v, cb, u, ga, gc = pl.pallas_call(
        _proj_kernel,
        grid=(t // tm,),
        in_specs=[row(d), full(norm1_g.reshape(1, d)), full(w_in_p), full(bd), full(qg), full(kg),
                  tab, tab, tab],
        out_specs=[row(ATTN_DIM), row(KV_DIM), row(KV_DIM), row(CONV_DIM), row(CONV_DIM),
                   row(D_MODEL), row(D_MODEL)],
        out_shape=[jax.ShapeDtypeStruct((t, w), BF16)
                   for w in (ATTN_DIM, KV_DIM, KV_DIM, CONV_DIM, CONV_DIM, D_MODEL, D_MODEL)],
        compiler_params=pltpu.CompilerParams(dimension_semantics=("arbitrary",),
                                             vmem_limit_bytes=VMEM_LIMIT),
        name="proj",
    )(xf, norm1_g.reshape(1, d), w_in_p, bd, qg, kg, ctab, s1tab, s2tab)

    nj = seq // tq
    wpb = tq // WINDOW
    nb_seq = seq // WINDOW
    hpb = tq // 16
    rowq = lambda w: pl.BlockSpec((tq, w), lambda b, j: (b * nj + j, 0))
    prev_blk = pl.BlockSpec((WINDOW, KV_DIM),
                            lambda b, j: (b * nb_seq + jnp.maximum(j * wpb - 1, 0), 0))
    next_blk = pl.BlockSpec((WINDOW, KV_DIM),
                            lambda b, j: (b * nb_seq + jnp.minimum((j + 1) * wpb, nb_seq - 1), 0))
    uprev = pl.BlockSpec((16, CONV_DIM),
                         lambda b, j: (jnp.maximum((b * nj + j) * hpb - 1, 0), 0))
    unext = pl.BlockSpec((16, CONV_DIM),
                         lambda b, j: (jnp.minimum((b * nj + j + 1) * hpb, t // 16 - 1), 0))
    full2 = lambda a: pl.BlockSpec(a.shape, lambda b, j: (0,) * a.ndim)
    colq = lambda r: pl.BlockSpec((r, tq), lambda b, j: (0, b * nj + j))

    wr = jnp.concatenate([w_rg.T, jnp.zeros((8 - N_GROUPS, d), F32), w_re.T], axis=0)
    wr_hi = wr.astype(BF16)
    wr_lo = (wr - wr_hi.astype(F32)).astype(BF16)
    br = jnp.concatenate([b_rg, jnp.zeros((8 - N_GROUPS,), F32), b_re]).reshape(ROUTER_ROWS, 1)
    tri = jnp.asarray(np.triu(np.ones((tq, tq), np.float32), 1), BF16)
    wc_b = w_conv_proj.astype(BF16)
    wo_b = w_out.astype(BF16)
    cbias = conv_b.reshape(1, CONV_DIM)
    g2 = norm2_g.reshape(1, d)

    x1, h2, route_i, route_f, counts = pl.pallas_call(
        functools.partial(_mix_kernel, tq=tq, n_seq_blocks=nb_seq),
        grid=(bsz, nj),
        in_specs=[pl.BlockSpec(memory_space=pltpu.SMEM),
                  rowq(d), rowq(ATTN_DIM), prev_blk, rowq(KV_DIM), next_blk,
                  prev_blk, rowq(KV_DIM), next_blk,
                  rowq(CONV_DIM), uprev, rowq(CONV_DIM), unext, rowq(D_MODEL), rowq(D_MODEL),
                  full2(conv_w), full2(cbias), full2(wa_p), full2(wc_b), full2(wo_b), full2(g2),
                  full2(wr_hi), full2(wr_lo), full2(br), full2(tri)],
        out_specs=[rowq(d), pl.BlockSpec((tq * TOK_SUBLANES, LANES), lambda b, j: (b * nj + j, 0)),
                   colq(8), colq(8),
                   pl.BlockSpec((N_EXPERTS, LANES), lambda b, j: (0, 0))],
        out_shape=[jax.ShapeDtypeStruct((t, d), F32),
                   jax.ShapeDtypeStruct((t * TOK_SUBLANES, LANES), F32),
                   jax.ShapeDtypeStruct((8, t), I32), jax.ShapeDtypeStruct((8, t), F32),
                   jax.ShapeDtypeStruct((N_EXPERTS, LANES), F32)],
        scratch_shapes=[pltpu.VMEM((tq + 16, CONV_DIM), F32), pltpu.VMEM((N_EXPERTS, LANES), F32)],
        compiler_params=pltpu.CompilerParams(dimension_semantics=("arbitrary", "arbitrary"),
                                             vmem_limit_bytes=VMEM_LIMIT),
        name="mix",
    )(attn_sink, xf, q, k, k, k, v, v, v, cb, u, u, u, ga, gc,
      conv_w, cbias, wa_p, wc_b, wo_b, g2, wr_hi, wr_lo, br, tri)

    n_rows = 2 * t + N_EXPERTS * MOE_ROWS
    nblk = n_rows // MOE_ROWS
    cnt = counts[:, 0].astype(I32)
    padded = ((cnt + MOE_ROWS - 1) // MOE_ROWS) * MOE_ROWS
    pad_ends = jnp.cumsum(padded)
    pad_starts = pad_ends - padded
    eids = jnp.arange(N_EXPERTS, dtype=I32)[:, None, None]
    dest = jnp.sum(jnp.where(route_i[None, 0:2] == eids, pad_starts[:, None, None], 0), axis=0) \
        + route_i[2:4]
    blk_start = jnp.arange(nblk, dtype=I32) * MOE_ROWS
    block_e = jnp.minimum(jnp.sum((pad_ends[None, :] <= blk_start[:, None]).astype(I32), axis=1),
                          N_EXPERTS - 1)
    nused = (pad_ends[-1:] // MOE_ROWS).astype(I32)

    x_pad = _sc_dispatch(h2.reshape(t, TOK_SUBLANES, LANES), dest[0], dest[1], n_rows)
    x_pad = x_pad.reshape(n_rows * TOK_SUBLANES, LANES)

    blk_rows = MOE_ROWS * TOK_SUBLANES
    wspec = lambda a, b: pl.BlockSpec((None, a, b), lambda i, be, nu: (be[i], 0, 0))
    y_pad = pl.pallas_call(
        _moe_kernel,
        grid_spec=pltpu.PrefetchScalarGridSpec(
            num_scalar_prefetch=2,
            grid=(nblk,),
            in_specs=[pl.BlockSpec((blk_rows, LANES), lambda i, be, nu: (i, 0)),
                      wspec(d, D_FF), wspec(d, D_FF), wspec(D_FF, d)],
            out_specs=pl.BlockSpec((blk_rows, LANES), lambda i, be, nu: (i, 0)),
            scratch_shapes=[pltpu.VMEM((d, D_FF), BF16), pltpu.VMEM((d, D_FF), BF16),
                            pltpu.VMEM((D_FF, d), BF16)]),
        out_shape=jax.ShapeDtypeStruct((n_rows * TOK_SUBLANES, LANES), F32),
        compiler_params=pltpu.CompilerParams(dimension_semantics=("arbitrary",),
                                             vmem_limit_bytes=VMEM_LIMIT),
        name="moe",
    )(block_e, nused, x_pad, w_gate, w_up, w_down)

    yk = _sc_gather(y_pad.reshape(n_rows, TOK_SUBLANES, LANES), dest[0], dest[1])
    yk = yk.reshape(2 * t * TOK_SUBLANES, LANES)
    tc = min(512, t)
    nt = t // tc
    gates_t = route_f[0:2].T
    out = pl.pallas_call(
        functools.partial(_combine_kernel, tm=tc),
        grid=(nt,),
        in_specs=[pl.BlockSpec((tc, d), lambda i: (i, 0)),
                  pl.BlockSpec((tc, 2), lambda i: (i, 0)),
                  pl.BlockSpec((tc * TOK_SUBLANES, LANES), lambda i: (i, 0)),
                  pl.BlockSpec((tc * TOK_SUBLANES, LANES), lambda i: (nt + i, 0))],
        out_specs=pl.BlockSpec((tc, d), lambda i: (i, 0)),
        out_shape=jax.ShapeDtypeStruct((t, d), F32),
        compiler_params=pltpu.CompilerParams(dimension_semantics=("arbitrary",),
                                             vmem_limit_bytes=VMEM_LIMIT),
        name="combine",
    )(x1, gates_t, yk, yk)
    return out.reshape(bsz, seq, d)


def kernel(x, norm1_g, w_in, q_norm_g, k_norm_g, attn_sink, conv_w, conv_b, w_attn_proj, w_conv_proj, w_out, norm2_g, w_router_group, b_router_group, w_router_expert, b_router_expert, w_gate_e, w_up_e, w_down_e):
    for l in range(norm1_g.shape[0]):
        x = _layer(x, norm1_g[l], w_in[l], q_norm_g[l], k_norm_g[l], attn_sink[l], conv_w[l],
                   conv_b[l], w_attn_proj[l], w_conv_proj[l], w_out[l], norm2_g[l],
                   w_router_group[l], b_router_group[l], w_router_expert[l], b_router_expert[l],
                   w_gate_e[l], w_up_e[l], w_down_e[l])
    return x
```

```python
import functools

import jax
import jax.numpy as jnp
import numpy as np
from jax import lax
from jax.experimental import pallas as pl
from jax.experimental.pallas import tpu as pltpu
from jax.experimental.pallas import tpu_sc as plsc

F32 = jnp.float32
BF16 = jnp.bfloat16
I32 = jnp.int32
U32 = jnp.uint32
HI_MASK = np.uint32(0xFFFF0000)

D_MODEL = 1024
N_HEADS = 8
N_KV_HEADS = 2
HEAD_DIM = 64
ATTN_DIM = N_HEADS * HEAD_DIM
KV_DIM = N_KV_HEADS * HEAD_DIM
WINDOW = 128
ROT_DIM = HEAD_DIM // 4
ROPE_THETA = 500000.0
CONV_DIM = D_MODEL // 2
N_GROUPS = 4
EXPERTS_PER_GROUP = 8
N_EXPERTS = N_GROUPS * EXPERTS_PER_GROUP
D_FF = D_MODEL // 4
NORM_EPS = 1e-6
MASK_VALUE = -1e30
NEG_BIG = -3.0e38

LANES = 128
TOK_SUBLANES = D_MODEL // (2 * LANES)
HALF_D = D_MODEL // 2
SC_WINDOW = 32
ROUTER_ROWS = 8 + N_EXPERTS
MOE_ROWS = 512
VMEM_LIMIT = 56 * 1024 * 1024

_Q0, _K0, _V0, _CB0, _CC0, _CX0, _GA0, _GC0, _END = (
    0, 512, 640, 768, 1280, 1792, 2304, 3328, 4352)


def _store_packed_rows(ref, val, n):
    for c in range(TOK_SUBLANES):
        lo = val[:, c * LANES:(c + 1) * LANES].astype(BF16).astype(F32)
        hi = val[:, HALF_D + c * LANES:HALF_D + (c + 1) * LANES].astype(BF16).astype(F32)
        word = (lax.bitcast_convert_type(lo, U32) >> 16) | (lax.bitcast_convert_type(hi, U32) & HI_MASK)
        ref[pl.ds(c, n, stride=TOK_SUBLANES), :] = word


def _load_packed_rows(ref, n):
    words = [ref[pl.ds(c, n, stride=TOK_SUBLANES), :] for c in range(TOK_SUBLANES)]
    lo = [lax.bitcast_convert_type(w << 16, F32) for w in words]
    hi = [lax.bitcast_convert_type(w & HI_MASK, F32) for w in words]
    return lo + hi


def _rope_chunks(t, c, s1, s2):
    outs = []
    for j in range(t.shape[1] // LANES):
        tj = t[:, j * LANES:(j + 1) * LANES]
        outs.append(tj * c + pltpu.roll(tj, LANES - ROT_DIM // 2, 1) * s1
                    + pltpu.roll(tj, ROT_DIM // 2, 1) * s2)
    return outs[0] if len(outs) == 1 else jnp.concatenate(outs, axis=1)


def _proj_kernel(x_ref, g1_ref, w_ref, bd_ref, qg_ref, kg_ref, c_ref, s1_ref, s2_ref,
                 q_ref, k_ref, v_ref, cb_ref, u_ref, ga_ref, gc_ref):
    x = x_ref[...]
    ms = jnp.mean(x * x, axis=-1, keepdims=True)
    h = (x * lax.rsqrt(ms + NORM_EPS) * g1_ref[...]).astype(BF16)

    def proj(a, b):
        return jnp.dot(h, w_ref[:, a:b], preferred_element_type=F32)

    c, s1, s2 = c_ref[...], s1_ref[...], s2_ref[...]

    q = proj(_Q0, _K0)
    qms = jnp.dot((q * q).astype(BF16), bd_ref[...], preferred_element_type=F32)
    qg = jnp.concatenate([qg_ref[...]] * (ATTN_DIM // LANES), axis=1)
    qn = q * lax.rsqrt(qms + NORM_EPS) * qg
    q_ref[...] = (_rope_chunks(qn, c, s1, s2) * (HEAD_DIM ** -0.5)).astype(BF16)

    k = proj(_K0, _V0)
    kms = jnp.dot((k * k).astype(BF16), bd_ref[0:KV_DIM, 0:KV_DIM], preferred_element_type=F32)
    kn = k * lax.rsqrt(kms + NORM_EPS) * kg_ref[...]
    k_ref[...] = _rope_chunks(kn, c, s1, s2).astype(BF16)

    v_ref[...] = proj(_V0, _CB0).astype(BF16)
    cb_ref[...] = proj(_CB0, _CC0).astype(BF16)
    u_ref[...] = (proj(_CC0, _CX0) * proj(_CX0, _GA0)).astype(BF16)
    ga_ref[...] = jax.nn.sigmoid(proj(_GA0, _GC0)).astype(BF16)
    gc_ref[...] = jax.nn.sigmoid(proj(_GC0, _END)).astype(BF16)


def _mix_kernel(sink_ref, x_ref, q_ref, kp_ref, kc_ref, kn_ref, vp_ref, vc_ref, vn_ref,
                cb_ref, up_ref, uc_ref, un_ref, ga_ref, gc_ref,
                cw_ref, cbias_ref, wa_ref, wc_ref, wo_ref, g2_ref,
                wrh_ref, wrl_ref, br_ref, tri_ref,
                x1_ref, h2_ref, ri_ref, rf_ref, cnt_ref,
                ubuf, base_ref, *, tq, n_seq_blocks):
    b = pl.program_id(0)
    j = pl.program_id(1)
    nj = pl.num_programs(1)
    nsub = tq // WINDOW

    @pl.when((b == 0) & (j == 0))
    def _():
        base_ref[...] = jnp.zeros_like(base_ref)

    kbuf = jnp.concatenate([kp_ref[...], kc_ref[...], kn_ref[...]], axis=0)
    vbuf = jnp.concatenate([vp_ref[...], vc_ref[...], vn_ref[...]], axis=0)
    lane = lax.broadcasted_iota(I32, (WINDOW, LANES), 1)
    low_half = lane < HEAD_DIM
    qrow = lax.broadcasted_iota(I32, (WINDOW, 3 * WINDOW), 0)
    kcol = lax.broadcasted_iota(I32, (WINDOW, 3 * WINDOW), 1)
    dd = kcol - qrow
    band = (dd >= 0) & (dd <= 2 * WINDOW)
    attn_rows = []
    for n in range(nsub):
        lo_lim = jnp.where(j > 0, 0, WINDOW) if n == 0 else 0
        hi_lim = jnp.where(j < nj - 1, 3 * WINDOW, 2 * WINDOW) if n == nsub - 1 else 3 * WINDOW
        valid = band & (kcol >= lo_lim) & (kcol < hi_lim)
        qs = q_ref[n * WINDOW:(n + 1) * WINDOW, :]
        zero = jnp.zeros((WINDOW, LANES), BF16)
        q_stack = []
        for hd in range(N_HEADS):
            col = qs[:, (hd % 4) * LANES:(hd % 4 + 1) * LANES]
            keep = low_half if hd < 4 else jnp.logical_not(low_half)
            q_stack.append(jnp.where(keep, col, zero))
        q_stack = jnp.concatenate(q_stack, axis=0)
        kwin = kbuf[n * WINDOW:(n + 3) * WINDOW, :]
        vwin = vbuf[n * WINDOW:(n + 3) * WINDOW, :]
        s = lax.dot_general(q_stack, kwin, (((1,), (1,)), ((), ())),
                            preferred_element_type=F32)
        p_list, inv_list = [], []
        for hd in range(N_HEADS):
            sink = sink_ref[hd]
            sh = jnp.where(valid, s[hd * WINDOW:(hd + 1) * WINDOW, :], MASK_VALUE)
            m = jnp.maximum(jnp.max(sh, axis=-1, keepdims=True), sink)
            e = jnp.exp(sh - m)
            den = jnp.sum(e, axis=-1, keepdims=True) + jnp.exp(sink - m)
            p_list.append(e.astype(BF16))
            inv_list.append(1.0 / den)
        p = jnp.concatenate(p_list, axis=0)
        o = jnp.dot(p, vwin, preferred_element_type=F32)
        cols = []
        for c4 in range(4):
            oa = o[c4 * WINDOW:(c4 + 1) * WINDOW, :] * inv_list[c4]
            ob = o[(c4 + 4) * WINDOW:(c4 + 5) * WINDOW, :] * inv_list[c4 + 4]
            cols.append(jnp.where(low_half, oa, ob))
        attn_rows.append(jnp.concatenate(cols, axis=1).astype(BF16))
    attn = attn_rows[0] if nsub == 1 else jnp.concatenate(attn_rows, axis=0)

    ubuf[8:8 + tq, :] = uc_ref[...].astype(F32)
    prev_rows = up_ref[...].astype(F32)
    next_rows = un_ref[...].astype(F32)
    ubuf[7:8, :] = prev_rows[15:16, :] * (j > 0).astype(F32)
    ubuf[8 + tq:9 + tq, :] = next_rows[0:1, :] * (j < nj - 1).astype(F32)
    cw = cw_ref[...]
    conv = (ubuf[7:7 + tq, :] * cw[0:1, :] + ubuf[8:8 + tq, :] * cw[1:2, :]
            + ubuf[9:9 + tq, :] * cw[2:3, :] + cbias_ref[...])
    cgate = (cb_ref[...].astype(F32) * conv).astype(BF16)

    merged = (ga_ref[...].astype(F32) * jnp.dot(attn, wa_ref[...], preferred_element_type=F32)
              + gc_ref[...].astype(F32) * jnp.dot(cgate, wc_ref[...], preferred_element_type=F32))
    x1 = x_ref[...] + jnp.dot(merged.astype(BF16), wo_ref[...], preferred_element_type=F32)
    x1_ref[...] = x1

    ms = jnp.mean(x1 * x1, axis=-1, keepdims=True)
    h2 = x1 * lax.rsqrt(ms + NORM_EPS) * g2_ref[...]
    _store_packed_rows(h2_ref, h2, tq)
    h_hi = h2.astype(BF16)
    h_lo = (h2 - h_hi.astype(F32)).astype(BF16)
    nt = (((1,), (1,)), ((), ()))
    logits = (lax.dot_general(wrh_ref[...], h_hi, nt, preferred_element_type=F32)
              + lax.dot_general(wrh_ref[...], h_lo, nt, preferred_element_type=F32)
              + lax.dot_general(wrl_ref[...], h_hi, nt, preferred_element_type=F32)
              + br_ref[...])
    grow = lax.broadcasted_iota(I32, (8, tq), 0)
    gl = jnp.where(grow < N_GROUPS, logits[0:8, :], NEG_BIG)
    gmax = jnp.max(gl, axis=0, keepdims=True)
    grp = jnp.min(jnp.where(gl == gmax, grow, 8), axis=0, keepdims=True)
    p_grp = 1.0 / jnp.sum(jnp.exp(gl - gmax), axis=0, keepdims=True)
    el = logits[8:ROUTER_ROWS, :]
    erow = lax.broadcasted_iota(I32, (N_EXPERTS, tq), 0)
    in_grp = (erow // EXPERTS_PER_GROUP) == grp
    l_a = jnp.where(in_grp, el, NEG_BIG)
    l1 = jnp.max(l_a, axis=0, keepdims=True)
    i1 = jnp.min(jnp.where(l_a == l1, erow, N_EXPERTS), axis=0, keepdims=True)
    l_b = jnp.where(erow == i1, NEG_BIG, l_a)
    l2 = jnp.max(l_b, axis=0, keepdims=True)
    i2 = jnp.min(jnp.where(l_b == l2, erow, N_EXPERTS), axis=0, keepdims=True)
    r21 = jnp.exp(l2 - l1)
    t1 = 1.0 / (1.0 + r21)
    gate1 = p_grp * t1
    gate2 = p_grp * (r21 * t1)
    oh1 = (erow == i1).astype(F32)
    oh2 = (erow == i2).astype(F32)
    oh = oh1 + oh2
    before = jnp.dot(oh.astype(BF16), tri_ref[...], preferred_element_type=F32) + base_ref[...][:, 0:1]
    rank1 = jnp.sum(oh1 * before, axis=0, keepdims=True)
    rank2 = jnp.sum(oh2 * before, axis=0, keepdims=True)
    new_base = base_ref[...] + jnp.sum(oh, axis=1, keepdims=True)
    base_ref[...] = new_base
    cnt_ref[...] = new_base
    zi = jnp.zeros((4, tq), I32)
    ri_ref[...] = jnp.concatenate([i1, i2, rank1.astype(I32), rank2.astype(I32), zi], axis=0)
    rf_ref[...] = jnp.concatenate([gate1, gate2, jnp.zeros((6, tq), F32)], axis=0)


def _moe_kernel(be_ref, nused_ref, x_ref, wg_ref, wu_ref, wd_ref, y_ref, wg_s, wu_s, wd_s):
    i = pl.program_id(0)

    @pl.when((i == 0) | (be_ref[i] != be_ref[jnp.maximum(i - 1, 0)]))
    def _():
        wg_s[...] = wg_ref[...].astype(BF16)
        wu_s[...] = wu_ref[...].astype(BF16)
        wd_s[...] = wd_ref[...].astype(BF16)

    @pl.when(i < nused_ref[0])
    def _():
        xb = jnp.concatenate([c.astype(BF16) for c in _load_packed_rows(x_ref, MOE_ROWS)], axis=1)
        g = jnp.dot(xb, wg_s[...], preferred_element_type=F32)
        u = jnp.dot(xb, wu_s[...], preferred_element_type=F32)
        hid = (g * jax.nn.sigmoid(g) * u).astype(BF16)
        _store_packed_rows(y_ref, jnp.dot(hid, wd_s[...], preferred_element_type=F32), MOE_ROWS)

    @pl.when(i >= nused_ref[0])
    def _():
        y_ref[...] = jnp.zeros_like(y_ref)


def _combine_kernel(x1_ref, gate_ref, y0_ref, y1_ref, o_ref, *, tm):
    g = gate_ref[...]
    g0, g1 = g[:, 0:1], g[:, 1:2]
    y0 = _load_packed_rows(y0_ref, tm)
    y1 = _load_packed_rows(y1_ref, tm)
    for c in range(2 * TOK_SUBLANES):
        cols = slice(c * LANES, (c + 1) * LANES)
        o_ref[:, cols] = x1_ref[:, cols] + g0 * y0[c] + g1 * y1[c]


def _sc_windows(n_tokens):
    info = plsc.get_sparse_core_info()
    n_workers = info.num_cores * info.num_subcores
    assert n_tokens % (n_workers * SC_WINDOW * 2) == 0
    return info.num_cores, n_tokens // (n_workers * SC_WINDOW)


def _sc_dispatch(h3, d0, d1, n_rows):
    t = h3.shape[0]
    n_cores, nwin = _sc_windows(t)
    mesh = plsc.VectorSubcoreMesh(core_axis_name="c", subcore_axis_name="s")
    rows = pltpu.VMEM((SC_WINDOW,) + h3.shape[1:], h3.dtype)
    idx = pltpu.VMEM((nwin, SC_WINDOW), I32)

    @functools.partial(
        pl.kernel, mesh=mesh,
        out_type=jax.ShapeDtypeStruct((n_rows,) + h3.shape[1:], h3.dtype),
        scratch_types=[idx, idx, idx, rows, rows] + [pltpu.SemaphoreType.DMA] * 3,
        name="sc_dispatch")
    def run(h_hbm, d0_hbm, d1_hbm, it_hbm, o_hbm, i0, i1, it, buf_a, buf_b, sem_a, sem_b, sem_s):
        wbase = (lax.axis_index("s") * n_cores + lax.axis_index("c")) * nwin
        pltpu.sync_copy(d0_hbm.at[pl.ds(wbase, nwin)], i0)
        pltpu.sync_copy(d1_hbm.at[pl.ds(wbase, nwin)], i1)
        pltpu.sync_copy(it_hbm.at[pl.ds(wbase, nwin)], it)

        @pl.loop(0, nwin, step=2)
        def _(w):
            ga = pltpu.async_copy(h_hbm.at[it.at[w]], buf_a, sem_a)
            gb = pltpu.async_copy(h_hbm.at[it.at[w + 1]], buf_b, sem_b)
            ga.wait()
            s0 = pltpu.async_copy(buf_a, o_hbm.at[i0.at[w]], sem_s)
            s1 = pltpu.async_copy(buf_a, o_hbm.at[i1.at[w]], sem_s)
            gb.wait()
            s2 = pltpu.async_copy(buf_b, o_hbm.at[i0.at[w + 1]], sem_s)
            s3 = pltpu.async_copy(buf_b, o_hbm.at[i1.at[w + 1]], sem_s)
            s0.wait()
            s1.wait()
            s2.wait()
            s3.wait()

    win = lambda a: a.reshape(-1, SC_WINDOW)
    return run(h3, win(d0), win(d1), win(jnp.arange(t, dtype=I32)))


def _sc_gather(y3, d0, d1):
    t = d0.shape[0]
    n_cores, nwin = _sc_windows(t)
    mesh = plsc.VectorSubcoreMesh(core_axis_name="c", subcore_axis_name="s")
    rows = pltpu.VMEM((SC_WINDOW,) + y3.shape[1:], y3.dtype)
    idx = pltpu.VMEM((nwin, SC_WINDOW), I32)

    @functools.partial(
        pl.kernel, mesh=mesh,
        out_type=jax.ShapeDtypeStruct((2 * t,) + y3.shape[1:], y3.dtype),
        scratch_types=[idx] * 4 + [rows] * 4 + [pltpu.SemaphoreType.DMA] * 5,
        name="sc_gather")
    def run(y_hbm, d0_hbm, d1_hbm, ita_hbm, itb_hbm, o_hbm, i0, i1, ita, itb,
            b0, b1, b2, b3, m0, m1, m2, m3, ms):
        wbase = (lax.axis_index("s") * n_cores + lax.axis_index("c")) * nwin
        pltpu.sync_copy(d0_hbm.at[pl.ds(wbase, nwin)], i0)
        pltpu.sync_copy(d1_hbm.at[pl.ds(wbase, nwin)], i1)
        pltpu.sync_copy(ita_hbm.at[pl.ds(wbase, nwin)], ita)
        pltpu.sync_copy(itb_hbm.at[pl.ds(wbase, nwin)], itb)

        @pl.loop(0, nwin, step=2)
        def _(w):
            g0 = pltpu.async_copy(y_hbm.at[i0.at[w]], b0, m0)
            g1 = pltpu.async_copy(y_hbm.at[i1.at[w]], b1, m1)
            g2 = pltpu.async_copy(y_hbm.at[i0.at[w + 1]], b2, m2)
            g3 = pltpu.async_copy(y_hbm.at[i1.at[w + 1]], b3, m3)
            g0.wait()
            s0 = pltpu.async_copy(b0, o_hbm.at[ita.at[w]], ms)
            g1.wait()
            s1 = pltpu.async_copy(b1, o_hbm.at[itb.at[w]], ms)
            g2.wait()
            s2 = pltpu.async_copy(b2, o_hbm.at[ita.at[w + 1]], ms)
            g3.wait()
            s3 = pltpu.async_copy(b3, o_hbm.at[itb.at[w + 1]], ms)
            s0.wait()
            s1.wait()
            s2.wait()
            s3.wait()

    win = lambda a: a.reshape(-1, SC_WINDOW)
    iota = jnp.arange(2 * t, dtype=I32)
    return run(y3, win(d0), win(d1), win(iota[:t]), win(iota[t:]))


def _rope_lane_tables(seq_len):
    half = ROT_DIM // 2
    inv_freq = ROPE_THETA ** (-jnp.arange(0, ROT_DIM, 2, dtype=F32) / ROT_DIM)
    ang = jnp.arange(seq_len, dtype=F32)[:, None] * inv_freq[None, :]
    cos, sin = jnp.cos(ang), jnp.sin(ang)
    m = np.arange(LANES) % HEAD_DIM
    f = m % half
    c = jnp.where(m[None, :] < ROT_DIM, cos[:, f], 1.0)
    s1 = jnp.where(m[None, :] < half, -sin[:, f], 0.0)
    s2 = jnp.where((m[None, :] >= half) & (m[None, :] < ROT_DIM), sin[:, f], 0.0)
    return c.astype(F32), s1.astype(F32), s2.astype(F32)


def _layer(x, norm1_g, w_in, q_norm_g, k_norm_g, attn_sink, conv_w, conv_b, w_attn_proj,
           w_conv_proj, w_out, norm2_g, w_rg, b_rg, w_re, b_re, w_gate, w_up, w_down):
    bsz, seq, d = x.shape
    t = bsz * seq
    tm = min(512, t)
    tq = min(256, seq)
    assert d == D_MODEL and seq % tq == 0 and t % tm == 0 and seq % tm == 0 and tq % WINDOW == 0
    xf = x.reshape(t, d)

    hpg = N_HEADS // N_KV_HEADS
    wq_p = (w_in[:, :ATTN_DIM].reshape(d, N_KV_HEADS, hpg, HEAD_DIM).transpose(0, 2, 1, 3)
            .reshape(d, ATTN_DIM))
    w_in_p = jnp.concatenate([wq_p, w_in[:, ATTN_DIM:]], axis=1).astype(BF16)
    wa_p = (w_attn_proj.reshape(N_KV_HEADS, hpg, HEAD_DIM, d).transpose(1, 0, 2, 3)
            .reshape(ATTN_DIM, d).astype(BF16))
    bd = jnp.asarray(np.kron(np.eye(N_HEADS), np.full((HEAD_DIM, HEAD_DIM), 1.0 / HEAD_DIM)), BF16)
    qg = jnp.tile(q_norm_g, LANES // HEAD_DIM).reshape(1, LANES)
    kg = jnp.tile(k_norm_g, LANES // HEAD_DIM).reshape(1, LANES)
    ctab, s1tab, s2tab = _rope_lane_tables(seq)
    nseq_t = seq // tm

    row = lambda w: pl.BlockSpec((tm, w), lambda i: (i, 0))
    full = lambda a: pl.BlockSpec(a.shape, lambda i: (0,) * a.ndim)
    tab = pl.BlockSpec((tm, LANES), lambda i: (i % nseq_t, 0))
    q, k, v, cb, u, ga, gc = pl.pallas_call(
        _proj_kernel,
        grid=(t // tm,),
        in_specs=[row(d), full(norm1_g.reshape(1, d)), full(w_in_p), full(bd), full(qg), full(kg),
                  tab, tab, tab],
        out_specs=[row(ATTN_DIM), row(KV_DIM), row(KV_DIM), row(CONV_DIM), row(CONV_DIM),
                   row(D_MODEL), row(D_MODEL)],
        out_shape=[jax.ShapeDtypeStruct((t, w), BF16)
                   for w in (ATTN_DIM, KV_DIM, KV_DIM, CONV_DIM, CONV_DIM, D_MODEL, D_MODEL)],
        compiler_params=pltpu.CompilerParams(dimension_semantics=("arbitrary",),
                                             vmem_limit_bytes=VMEM_LIMIT),
        name="proj",
    )(xf, norm1_g.reshape(1, d), w_in_p, bd, qg, kg, ctab, s1tab, s2tab)

    nj = seq // tq
    wpb = tq // WINDOW
    nb_seq = seq // WINDOW
    hpb = tq // 16
    rowq = lambda w: pl.BlockSpec((tq, w), lambda b, j: (b * nj + j, 0))
    prev_blk = pl.BlockSpec((WINDOW, KV_DIM),
                            lambda b, j: (b * nb_seq + jnp.maximum(j * wpb - 1, 0), 0))
    next_blk = pl.BlockSpec((WINDOW, KV_DIM),
                            lambda b, j: (b * nb_seq + jnp.minimum((j + 1) * wpb, nb_seq - 1), 0))
    uprev = pl.BlockSpec((16, CONV_DIM),
                         lambda b, j: (jnp.maximum((b * nj + j) * hpb - 1, 0), 0))
    unext = pl.BlockSpec((16, CONV_DIM),
                         lambda b, j: (jnp.minimum((b * nj + j + 1) * hpb, t // 16 - 1), 0))
    full2 = lambda a: pl.BlockSpec(a.shape, lambda b, j: (0,) * a.ndim)
    colq = lambda r: pl.BlockSpec((r, tq), lambda b, j: (0, b * nj + j))

    wr = jnp.concatenate([w_rg.T, jnp.zeros((8 - N_GROUPS, d), F32), w_re.T], axis=0)
    wr_hi = wr.astype(BF16)
    wr_lo = (wr - wr_hi.astype(F32)).astype(BF16)
    br = jnp.concatenate([b_rg, jnp.zeros((8 - N_GROUPS,), F32), b_re]).reshape(ROUTER_ROWS, 1)
    tri = jnp.asarray(np.triu(np.ones((tq, tq), np.float32), 1), BF16)
    wc_b = w_conv_proj.astype(BF16)
    wo_b = w_out.astype(BF16)
    cbias = conv_b.reshape(1, CONV_DIM)
    g2 = norm2_g.reshape(1, d)

    x1, h2, route_i, route_f, counts = pl.pallas_call(
        functools.partial(_mix_kernel, tq=tq, n_seq_blocks=nb_seq),
        grid=(bsz, nj),
        in_specs=[pl.BlockSpec(memory_space=pltpu.SMEM),
                  rowq(d), rowq(ATTN_DIM), prev_blk, rowq(KV_DIM), next_blk,
                  prev_blk, rowq(KV_DIM), next_blk,
                  rowq(CONV_DIM), uprev, rowq(CONV_DIM), unext, rowq(D_MODEL), rowq(D_MODEL),
                  full2(conv_w), full2(cbias), full2(wa_p), full2(wc_b), full2(wo_b), full2(g2),
                  full2(wr_hi), full2(wr_lo), full2(br), full2(tri)],
        out_specs=[rowq(d), pl.BlockSpec((tq * TOK_SUBLANES, LANES), lambda b, j: (b * nj + j, 0)),
                   colq(8), colq(8),
                   pl.BlockSpec((N_EXPERTS, LANES), lambda b, j: (0, 0))],
        out_shape=[jax.ShapeDtypeStruct((t, d), F32),
                   jax.ShapeDtypeStruct((t * TOK_SUBLANES, LANES), U32),
                   jax.ShapeDtypeStruct((8, t), I32), jax.ShapeDtypeStruct((8, t), F32),
                   jax.ShapeDtypeStruct((N_EXPERTS, LANES), F32)],
        scratch_shapes=[pltpu.VMEM((tq + 16, CONV_DIM), F32), pltpu.VMEM((N_EXPERTS, LANES), F32)],
        compiler_params=pltpu.CompilerParams(dimension_semantics=("arbitrary", "arbitrary"),
                                             vmem_limit_bytes=VMEM_LIMIT),
        name="mix",
    )(attn_sink, xf, q, k, k, k, v, v, v, cb, u, u, u, ga, gc,
      conv_w, cbias, wa_p, wc_b, wo_b, g2, wr_hi, wr_lo, br, tri)

    n_rows = 2 * t + N_EXPERTS * MOE_ROWS
    nblk = n_rows // MOE_ROWS
    cnt = counts[:, 0].astype(I32)
    padded = ((cnt + MOE_ROWS - 1) // MOE_ROWS) * MOE_ROWS
    pad_ends = jnp.cumsum(padded)
    pad_starts = pad_ends - padded
    eids = jnp.arange(N_EXPERTS, dtype=I32)[:, None, None]
    dest = jnp.sum(jnp.where(route_i[None, 0:2] == eids, pad_starts[:, None, None], 0), axis=0) \
        + route_i[2:4]
    blk_start = jnp.arange(nblk, dtype=I32) * MOE_ROWS
    block_e = jnp.minimum(jnp.sum((pad_ends[None, :] <= blk_start[:, None]).astype(I32), axis=1),
                          N_EXPERTS - 1)
    nused = (pad_ends[-1:] // MOE_ROWS).astype(I32)

    x_pad = _sc_dispatch(h2.reshape(t, TOK_SUBLANES, LANES), dest[0], dest[1], n_rows)
    x_pad = x_pad.reshape(n_rows * TOK_SUBLANES, LANES)

    blk_rows = MOE_ROWS * TOK_SUBLANES
    wspec = lambda a, b: pl.BlockSpec((None, a, b), lambda i, be, nu: (be[i], 0, 0))
    y_pad = pl.pallas_call(
        _moe_kernel,
        grid_spec=pltpu.PrefetchScalarGridSpec(
            num_scalar_prefetch=2,
            grid=(nblk,),
            in_specs=[pl.BlockSpec((blk_rows, LANES), lambda i, be, nu: (i, 0)),
                      wspec(d, D_FF), wspec(d, D_FF), wspec(D_FF, d)],
            out_specs=pl.BlockSpec((blk_rows, LANES), lambda i, be, nu: (i, 0)),
            scratch_shapes=[pltpu.VMEM((d, D_FF), BF16), pltpu.VMEM((d, D_FF), BF16),
                            pltpu.VMEM((D_FF, d), BF16)]),
        out_shape=jax.ShapeDtypeStruct((n_rows * TOK_SUBLANES, LANES), U32),
        compiler_params=pltpu.CompilerParams(dimension_semantics=("arbitrary",),
                                             vmem_limit_bytes=VMEM_LIMIT),
        name="moe",
    )(block_e, nused, x_pad, w_gate, w_up, w_down)

    yk = _sc_gather(y_pad.reshape(n_rows, TOK_SUBLANES, LANES), dest[0], dest[1])
    yk = yk.reshape(2 * t * TOK_SUBLANES, LANES)
    tc = min(512, t)
    nt = t // tc
    gates_t = route_f[0:2].T
    out = pl.pallas_call(
        functools.partial(_combine_kernel, tm=tc),
        grid=(nt,),
        in_specs=[pl.BlockSpec((tc, d), lambda i: (i, 0)),
                  pl.BlockSpec((tc, 2), lambda i: (i, 0)),
                  pl.BlockSpec((tc * TOK_SUBLANES, LANES), lambda i: (i, 0)),
                  pl.BlockSpec((tc * TOK_SUBLANES, LANES), lambda i: (nt + i, 0))],
        out_specs=pl.BlockSpec((tc, d), lambda i: (i, 0)),
        out_shape=jax.ShapeDtypeStruct((t, d), F32),
        compiler_params=pltpu.CompilerParams(dimension_semantics=("arbitrary",),
                                             vmem_limit_bytes=VMEM_LIMIT),
        name="combine",
    )(x1, gates_t, yk, yk)
    return out.reshape(bsz, seq, d)


def kernel(x, norm1_g, w_in, q_norm_g, k_norm_g, attn_sink, conv_w, conv_b, w_attn_proj, w_conv_proj, w_out, norm2_g, w_router_group, b_router_group, w_router_expert, b_router_expert, w_gate_e, w_up_e, w_down_e):
    for l in range(norm1_g.shape[0]):
        x = _layer(x, norm1_g[l], w_in[l], q_norm_g[l], k_norm_g[l], attn_sink[l], conv_w[l],
                   conv_b[l], w_attn_proj[l], w_conv_proj[l], w_out[l], norm2_g[l],
                   w_router_group[l], b_router_group[l], w_router_expert[l], b_router_expert[l],
                   w_gate_e[l], w_up_e[l], w_down_e[l])
    return x
```

```python
import functools

import jax
import jax.numpy as jnp
import numpy as np
from jax import lax
from jax.experimental import pallas as pl
from jax.experimental.pallas import tpu as pltpu
from jax.experimental.pallas import tpu_sc as plsc

F32 = jnp.float32
BF16 = jnp.bfloat16
I32 = jnp.int32
U32 = jnp.uint32
HI_MASK = np.uint32(0xFFFF0000)

D_MODEL = 1024
N_HEADS = 8
N_KV_HEADS = 2
HEAD_DIM = 64
ATTN_DIM = N_HEADS * HEAD_DIM
KV_DIM = N_KV_HEADS * HEAD_DIM
WINDOW = 128
ROT_DIM = HEAD_DIM // 4
ROPE_THETA = 500000.0
CONV_DIM = D_MODEL // 2
N_GROUPS = 4
EXPERTS_PER_GROUP = 8
N_EXPERTS = N_GROUPS * EXPERTS_PER_GROUP
D_FF = D_MODEL // 4
NORM_EPS = 1e-6
MASK_VALUE = -1e30
NEG_BIG = -3.0e38
LOG2E = 1.4426950408889634

LANES = 128
TOK_SUBLANES = D_MODEL // (2 * LANES)
HALF_D = D_MODEL // 2
SC_WINDOW = 32
ROUTER_ROWS = 8 + N_EXPERTS
ROUTE_TILE = 2048
RANK_CHUNK = 256
MOE_ROWS = 512
VMEM_LIMIT = 56 * 1024 * 1024

_Q0, _K0, _V0, _CB0, _CC0, _CX0, _GA0, _GC0, _END = (
    0, 512, 640, 768, 1280, 1792, 2304, 3328, 4352)


def _store_packed_rows(ref, val, n):
    for c in range(TOK_SUBLANES):
        lo = val[:, c * LANES:(c + 1) * LANES].astype(BF16).astype(F32)
        hi = val[:, HALF_D + c * LANES:HALF_D + (c + 1) * LANES].astype(BF16).astype(F32)
        word = (lax.bitcast_convert_type(lo, U32) >> 16) | (lax.bitcast_convert_type(hi, U32) & HI_MASK)
        ref[pl.ds(c, n, stride=TOK_SUBLANES), :] = word


def _load_packed_rows(ref, n):
    words = [ref[pl.ds(c, n, stride=TOK_SUBLANES), :] for c in range(TOK_SUBLANES)]
    lo = [lax.bitcast_convert_type(w << 16, F32) for w in words]
    hi = [lax.bitcast_convert_type(w & HI_MASK, F32) for w in words]
    return lo + hi


def _rope_chunks(t, c, s1, s2):
    outs = []
    for j in range(t.shape[1] // LANES):
        tj = t[:, j * LANES:(j + 1) * LANES]
        outs.append(tj * c + pltpu.roll(tj, LANES - ROT_DIM // 2, 1) * s1
                    + pltpu.roll(tj, ROT_DIM // 2, 1) * s2)
    return outs[0] if len(outs) == 1 else jnp.concatenate(outs, axis=1)


def _proj_kernel(x_ref, g1_ref, w_ref, bd_ref, qg_ref, kg_ref, c_ref, s1_ref, s2_ref,
                 q_ref, k_ref, v_ref, cb_ref, u_ref, ga_ref, gc_ref):
    x = x_ref[...]
    ms = jnp.mean(x * x, axis=-1, keepdims=True)
    h = (x * lax.rsqrt(ms + NORM_EPS) * g1_ref[...]).astype(BF16)

    def proj(a, b):
        return jnp.dot(h, w_ref[:, a:b], preferred_element_type=F32)

    c, s1, s2 = c_ref[...], s1_ref[...], s2_ref[...]

    q = proj(_Q0, _K0)
    qms = jnp.dot((q * q).astype(BF16), bd_ref[...], preferred_element_type=F32)
    qg = jnp.concatenate([qg_ref[...]] * (ATTN_DIM // LANES), axis=1)
    qn = q * lax.rsqrt(qms + NORM_EPS) * qg
    q_ref[...] = (_rope_chunks(qn, c, s1, s2) * (HEAD_DIM ** -0.5 * LOG2E)).astype(BF16)

    k = proj(_K0, _V0)
    kms = jnp.dot((k * k).astype(BF16), bd_ref[0:KV_DIM, 0:KV_DIM], preferred_element_type=F32)
    kn = k * lax.rsqrt(kms + NORM_EPS) * kg_ref[...]
    k_ref[...] = _rope_chunks(kn, c, s1, s2).astype(BF16)

    v_ref[...] = proj(_V0, _CB0).astype(BF16)
    cb_ref[...] = proj(_CB0, _CC0).astype(BF16)
    u_ref[...] = (proj(_CC0, _CX0) * proj(_CX0, _GA0)).astype(BF16)
    ga_ref[...] = jax.nn.sigmoid(proj(_GA0, _GC0)).astype(BF16)
    gc_ref[...] = jax.nn.sigmoid(proj(_GC0, _END)).astype(BF16)


def _mix_kernel(sink_ref, x_ref, q_ref, kp_ref, kc_ref, kn_ref, vp_ref, vc_ref, vn_ref,
                cb_ref, up_ref, uc_ref, un_ref, ga_ref, gc_ref,
                cw_ref, cbias_ref, wa_ref, wc_ref, wo_ref, g2_ref,
                wr_ref, br_ref,
                x1_ref, h2_ref, lg_ref,
                ubuf, attn_scr, *, tq):
    j = pl.program_id(1)
    nj = pl.num_programs(1)
    nsub = tq // WINDOW

    kbuf = jnp.concatenate([kp_ref[...], kc_ref[...], kn_ref[...]], axis=0)
    vbuf = jnp.concatenate([vp_ref[...], vc_ref[...], vn_ref[...]], axis=0)
    low_half = lax.broadcasted_iota(I32, (WINDOW, LANES), 1) < HEAD_DIM
    low_half3 = lax.broadcasted_iota(I32, (3 * WINDOW, LANES), 1) < HEAD_DIM
    qrow = lax.broadcasted_iota(I32, (WINDOW, WINDOW), 0)
    kcol = lax.broadcasted_iota(I32, (WINDOW, WINDOW), 1)
    bias_lo = jnp.where(kcol >= qrow, 0.0, MASK_VALUE)
    bias_hi = jnp.where(kcol <= qrow, 0.0, MASK_VALUE)
    one = jnp.ones((3 * WINDOW, LANES), BF16)
    for n in range(nsub):
        blo = jnp.where(j > 0, bias_lo, MASK_VALUE) if n == 0 else bias_lo
        bhi = jnp.where(j < nj - 1, bias_hi, MASK_VALUE) if n == nsub - 1 else bias_hi
        qs = q_ref[n * WINDOW:(n + 1) * WINDOW, :]
        zero = jnp.zeros((WINDOW, LANES), BF16)
        q_stack = []
        for hd in range(N_HEADS):
            col = qs[:, (hd % 4) * LANES:(hd % 4 + 1) * LANES]
            keep = low_half if hd < 4 else jnp.logical_not(low_half)
            q_stack.append(jnp.where(keep, col, zero))
        q_stack = jnp.concatenate(q_stack, axis=0)
        kwin = kbuf[n * WINDOW:(n + 3) * WINDOW, :]
        vwin = vbuf[n * WINDOW:(n + 3) * WINDOW, :]
        s = lax.dot_general(q_stack, kwin, (((1,), (1,)), ((), ())),
                            preferred_element_type=F32)
        p_list, es_list = [], []
        for hd in range(N_HEADS):
            sink = sink_ref[hd] * LOG2E
            rows = slice(hd * WINDOW, (hd + 1) * WINDOW)
            s0 = s[rows, 0:WINDOW] + blo
            s1 = s[rows, WINDOW:2 * WINDOW]
            s2 = s[rows, 2 * WINDOW:3 * WINDOW] + bhi
            m = jnp.max(jnp.maximum(jnp.maximum(s0, s1), s2), axis=-1, keepdims=True)
            m = jnp.maximum(m, sink)
            p_list.append(jnp.concatenate(
                [jnp.exp2(s0 - m).astype(BF16), jnp.exp2(s1 - m).astype(BF16),
                 jnp.exp2(s2 - m).astype(BF16)], axis=1))
            es_list.append(jnp.exp2(sink - m))
        half = N_HEADS // 2
        o_lo = jnp.dot(jnp.concatenate(p_list[:half], axis=0), jnp.where(low_half3, vwin, one),
                       preferred_element_type=F32)
        o_hi = jnp.dot(jnp.concatenate(p_list[half:], axis=0), jnp.where(low_half3, one, vwin),
                       preferred_element_type=F32)
        cols = []
        for c4 in range(half):
            rows = slice(c4 * WINDOW, (c4 + 1) * WINDOW)
            oa, ob = o_lo[rows, :], o_hi[rows, :]
            da = pltpu.roll(oa, HEAD_DIM, 1) + es_list[c4]
            db = pltpu.roll(ob, HEAD_DIM, 1) + es_list[c4 + half]
            cols.append(jnp.where(low_half, oa / da, ob / db))
        attn_scr[n * WINDOW:(n + 1) * WINDOW, :] = jnp.concatenate(cols, axis=1).astype(BF16)
    attn = attn_scr[...]

    ubuf[8:8 + tq, :] = uc_ref[...].astype(F32)
    prev_rows = up_ref[...].astype(F32)
    next_rows = un_ref[...].astype(F32)
    ubuf[7:8, :] = prev_rows[15:16, :] * (j > 0).astype(F32)
    ubuf[8 + tq:9 + tq, :] = next_rows[0:1, :] * (j < nj - 1).astype(F32)
    cw = cw_ref[...]
    conv = (ubuf[7:7 + tq, :] * cw[0:1, :] + ubuf[8:8 + tq, :] * cw[1:2, :]
            + ubuf[9:9 + tq, :] * cw[2:3, :] + cbias_ref[...])
    cgate = (cb_ref[...].astype(F32) * conv).astype(BF16)

    merged = (ga_ref[...].astype(F32) * jnp.dot(attn, wa_ref[...], preferred_element_type=F32)
              + gc_ref[...].astype(F32) * jnp.dot(cgate, wc_ref[...], preferred_element_type=F32))
    x1 = x_ref[...] + jnp.dot(merged.astype(BF16), wo_ref[...], preferred_element_type=F32)
    x1_ref[...] = x1

    ms = jnp.mean(x1 * x1, axis=-1, keepdims=True)
    h2 = x1 * lax.rsqrt(ms + NORM_EPS) * g2_ref[...]
    _store_packed_rows(h2_ref, h2, tq)
    h_hi = h2.astype(BF16)
    h_lo = (h2 - h_hi.astype(F32)).astype(BF16)
    acc = jnp.dot(h_hi, wr_ref[...], preferred_element_type=F32)
    logits = (acc[:, 0:LANES] + acc[:, LANES:2 * LANES]
              + jnp.dot(h_lo, wr_ref[:, 0:LANES], preferred_element_type=F32) + br_ref[...])
    lg_ref[...] = logits


def _route_kernel(lg_ref, tri_ref, ri_ref, rf_ref, cnt_ref, base_ref, *, tr):
    @pl.when(pl.program_id(0) == 0)
    def _():
        base_ref[...] = jnp.zeros_like(base_ref)

    logits = lg_ref[...].T[0:ROUTER_ROWS, :]
    grow = lax.broadcasted_iota(I32, (8, tr), 0)
    gl = jnp.where(grow < N_GROUPS, logits[0:8, :], NEG_BIG)
    gmax = jnp.max(gl, axis=0, keepdims=True)
    grp = jnp.min(jnp.where(gl == gmax, grow, 8), axis=0, keepdims=True)
    p_grp = 1.0 / jnp.sum(jnp.exp(gl - gmax), axis=0, keepdims=True)
    el = logits[8:ROUTER_ROWS, :]
    erow = lax.broadcasted_iota(I32, (N_EXPERTS, tr), 0)
    in_grp = (erow // EXPERTS_PER_GROUP) == grp
    l_a = jnp.where(in_grp, el, NEG_BIG)
    l1 = jnp.max(l_a, axis=0, keepdims=True)
    i1 = jnp.min(jnp.where(l_a == l1, erow, N_EXPERTS), axis=0, keepdims=True)
    l_b = jnp.where(erow == i1, NEG_BIG, l_a)
    l2 = jnp.max(l_b, axis=0, keepdims=True)
    i2 = jnp.min(jnp.where(l_b == l2, erow, N_EXPERTS), axis=0, keepdims=True)
    r21 = jnp.exp(l2 - l1)
    t1 = 1.0 / (1.0 + r21)
    gate1 = p_grp * t1
    gate2 = p_grp * (r21 * t1)
    oh1 = (erow == i1).astype(F32)
    oh2 = (erow == i2).astype(F32)
    oh = oh1 + oh2
    base = base_ref[...][:, 0:1]
    r1, r2 = [], []
    for c in range(tr // RANK_CHUNK):
        cols = slice(c * RANK_CHUNK, (c + 1) * RANK_CHUNK)
        ohc = oh[:, cols]
        before = jnp.dot(ohc.astype(BF16), tri_ref[...], preferred_element_type=F32) + base
        r1.append(jnp.sum(oh1[:, cols] * before, axis=0, keepdims=True))
        r2.append(jnp.sum(oh2[:, cols] * before, axis=0, keepdims=True))
        base = base + jnp.sum(ohc, axis=1, keepdims=True)
    rank1 = jnp.concatenate(r1, axis=1)
    rank2 = jnp.concatenate(r2, axis=1)
    new_base = jnp.broadcast_to(base, base_ref.shape)
    base_ref[...] = new_base
    cnt_ref[...] = new_base
    zi = jnp.zeros((4, tr), I32)
    ri_ref[...] = jnp.concatenate([i1, i2, rank1.astype(I32), rank2.astype(I32), zi], axis=0)
    rf_ref[...] = jnp.concatenate([gate1, gate2, jnp.zeros((6, tr), F32)], axis=0)


def _moe_kernel(be_ref, nused_ref, x_ref, wg_ref, wu_ref, wd_ref, y_ref, wg_s, wu_s, wd_s):
    i = pl.program_id(0)

    @pl.when((i == 0) | (be_ref[i] != be_ref[jnp.maximum(i - 1, 0)]))
    def _():
        wg_s[...] = wg_ref[...].astype(BF16)
        wu_s[...] = wu_ref[...].astype(BF16)
        wd_s[...] = wd_ref[...].astype(BF16)

    @pl.when(i < nused_ref[0])
    def _():
        xb = jnp.concatenate([c.astype(BF16) for c in _load_packed_rows(x_ref, MOE_ROWS)], axis=1)
        g = jnp.dot(xb, wg_s[...], preferred_element_type=F32)
        u = jnp.dot(xb, wu_s[...], preferred_element_type=F32)
        hid = (g * jax.nn.sigmoid(g) * u).astype(BF16)
        _store_packed_rows(y_ref, jnp.dot(hid, wd_s[...], preferred_element_type=F32), MOE_ROWS)

    @pl.when(i >= nused_ref[0])
    def _():
        y_ref[...] = jnp.zeros_like(y_ref)


def _combine_kernel(x1_ref, gate_ref, y0_ref, y1_ref, o_ref, *, tm):
    g = gate_ref[...]
    g0, g1 = g[:, 0:1], g[:, 1:2]
    y0 = _load_packed_rows(y0_ref, tm)
    y1 = _load_packed_rows(y1_ref, tm)
    for c in range(2 * TOK_SUBLANES):
        cols = slice(c * LANES, (c + 1) * LANES)
        o_ref[:, cols] = x1_ref[:, cols] + g0 * y0[c] + g1 * y1[c]


def _sc_windows(n_tokens):
    info = plsc.get_sparse_core_info()
    n_workers = info.num_cores * info.num_subcores
    assert n_tokens % (n_workers * SC_WINDOW * 2) == 0
    return info.num_cores, n_tokens // (n_workers * SC_WINDOW)


def _sc_dispatch(h3, d0, d1, n_rows):
    t = h3.shape[0]
    n_cores, nwin = _sc_windows(t)
    mesh = plsc.VectorSubcoreMesh(core_axis_name="c", subcore_axis_name="s")
    rows = pltpu.VMEM((SC_WINDOW,) + h3.shape[1:], h3.dtype)
    idx = pltpu.VMEM((nwin, SC_WINDOW), I32)

    @functools.partial(
        pl.kernel, mesh=mesh,
        out_type=jax.ShapeDtypeStruct((n_rows,) + h3.shape[1:], h3.dtype),
        scratch_types=[idx, idx, idx, rows, rows] + [pltpu.SemaphoreType.DMA] * 3,
        name="sc_dispatch")
    def run(h_hbm, d0_hbm, d1_hbm, it_hbm, o_hbm, i0, i1, it, buf_a, buf_b, sem_a, sem_b, sem_s):
        wbase = (lax.axis_index("s") * n_cores + lax.axis_index("c")) * nwin
        pltpu.sync_copy(d0_hbm.at[pl.ds(wbase, nwin)], i0)
        pltpu.sync_copy(d1_hbm.at[pl.ds(wbase, nwin)], i1)
        pltpu.sync_copy(it_hbm.at[pl.ds(wbase, nwin)], it)

        @pl.loop(0, nwin, step=2)
        def _(w):
            ga = pltpu.async_copy(h_hbm.at[it.at[w]], buf_a, sem_a)
            gb = pltpu.async_copy(h_hbm.at[it.at[w + 1]], buf_b, sem_b)
            ga.wait()
            s0 = pltpu.async_copy(buf_a, o_hbm.at[i0.at[w]], sem_s)
            s1 = pltpu.async_copy(buf_a, o_hbm.at[i1.at[w]], sem_s)
            gb.wait()
            s2 = pltpu.async_copy(buf_b, o_hbm.at[i0.at[w + 1]], sem_s)
            s3 = pltpu.async_copy(buf_b, o_hbm.at[i1.at[w + 1]], sem_s)
            s0.wait()
            s1.wait()
            s2.wait()
            s3.wait()

    win = lambda a: a.reshape(-1, SC_WINDOW)
    return run(h3, win(d0), win(d1), win(jnp.arange(t, dtype=I32)))


def _sc_gather(y3, d0, d1):
    t = d0.shape[0]
    n_cores, nwin = _sc_windows(t)
    mesh = plsc.VectorSubcoreMesh(core_axis_name="c", subcore_axis_name="s")
    rows = pltpu.VMEM((SC_WINDOW,) + y3.shape[1:], y3.dtype)
    idx = pltpu.VMEM((nwin, SC_WINDOW), I32)

    @functools.partial(
        pl.kernel, mesh=mesh,
        out_type=jax.ShapeDtypeStruct((2 * t,) + y3.shape[1:], y3.dtype),
        scratch_types=[idx] * 4 + [rows] * 4 + [pltpu.SemaphoreType.DMA] * 5,
        name="sc_gather")
    def run(y_hbm, d0_hbm, d1_hbm, ita_hbm, itb_hbm, o_hbm, i0, i1, ita, itb,
            b0, b1, b2, b3, m0, m1, m2, m3, ms):
        wbase = (lax.axis_index("s") * n_cores + lax.axis_index("c")) * nwin
        pltpu.sync_copy(d0_hbm.at[pl.ds(wbase, nwin)], i0)
        pltpu.sync_copy(d1_hbm.at[pl.ds(wbase, nwin)], i1)
        pltpu.sync_copy(ita_hbm.at[pl.ds(wbase, nwin)], ita)
        pltpu.sync_copy(itb_hbm.at[pl.ds(wbase, nwin)], itb)

        @pl.loop(0, nwin, step=2)
        def _(w):
            g0 = pltpu.async_copy(y_hbm.at[i0.at[w]], b0, m0)
            g1 = pltpu.async_copy(y_hbm.at[i1.at[w]], b1, m1)
            g2 = pltpu.async_copy(y_hbm.at[i0.at[w + 1]], b2, m2)
            g3 = pltpu.async_copy(y_hbm.at[i1.at[w + 1]], b3, m3)
            g0.wait()
            s0 = pltpu.async_copy(b0, o_hbm.at[ita.at[w]], ms)
            g1.wait()
            s1 = pltpu.async_copy(b1, o_hbm.at[itb.at[w]], ms)
            g2.wait()
            s2 = pltpu.async_copy(b2, o_hbm.at[ita.at[w + 1]], ms)
            g3.wait()
            s3 = pltpu.async_copy(b3, o_hbm.at[itb.at[w + 1]], ms)
            s0.wait()
            s1.wait()
            s2.wait()
            s3.wait()

    win = lambda a: a.reshape(-1, SC_WINDOW)
    iota = jnp.arange(2 * t, dtype=I32)
    return run(y3, win(d0), win(d1), win(iota[:t]), win(iota[t:]))


def _rope_lane_tables(seq_len):
    half = ROT_DIM // 2
    inv_freq = ROPE_THETA ** (-jnp.arange(0, ROT_DIM, 2, dtype=F32) / ROT_DIM)
    ang = jnp.arange(seq_len, dtype=F32)[:, None] * inv_freq[None, :]
    cos, sin = jnp.cos(ang), jnp.sin(ang)
    m = np.arange(LANES) % HEAD_DIM
    f = m % half
    c = jnp.where(m[None, :] < ROT_DIM, cos[:, f], 1.0)
    s1 = jnp.where(m[None, :] < half, -sin[:, f], 0.0)
    s2 = jnp.where((m[None, :] >= half) & (m[None, :] < ROT_DIM), sin[:, f], 0.0)
    return c.astype(F32), s1.astype(F32), s2.astype(F32)


def _layer(x, norm1_g, w_in, q_norm_g, k_norm_g, attn_sink, conv_w, conv_b, w_attn_proj,
           w_conv_proj, w_out, norm2_g, w_rg, b_rg, w_re, b_re, w_gate, w_up, w_down):
    bsz, seq, d = x.shape
    t = bsz * seq
    tm = min(512, t)
    tq = min(256, seq)
    assert d == D_MODEL and seq % tq == 0 and t % tm == 0 and seq % tm == 0 and tq % WINDOW == 0
    xf = x.reshape(t, d)

    hpg = N_HEADS // N_KV_HEADS
    wq_p = (w_in[:, :ATTN_DIM].reshape(d, N_KV_HEADS, hpg, HEAD_DIM).transpose(0, 2, 1, 3)
            .reshape(d, ATTN_DIM))
    w_in_p = jnp.concatenate([wq_p, w_in[:, ATTN_DIM:]], axis=1).astype(BF16)
    wa_p = (w_attn_proj.reshape(N_KV_HEADS, hpg, HEAD_DIM, d).transpose(1, 0, 2, 3)
            .reshape(ATTN_DIM, d).astype(BF16))
    bd = jnp.asarray(np.kron(np.eye(N_HEADS), np.full((HEAD_DIM, HEAD_DIM), 1.0 / HEAD_DIM)), BF16)
    qg = jnp.tile(q_norm_g, LANES // HEAD_DIM).reshape(1, LANES)
    kg = jnp.tile(k_norm_g, LANES // HEAD_DIM).reshape(1, LANES)
    ctab, s1tab, s2tab = _rope_lane_tables(seq)
    nseq_t = seq // tm

    row = lambda w: pl.BlockSpec((tm, w), lambda i: (i, 0))
    full = lambda a: pl.BlockSpec(a.shape, lambda i: (0,) * a.ndim)
    tab = pl.BlockSpec((tm, LANES), lambda i: (i % nseq_t, 0))
    q, k, v, cb, u, ga, gc = pl.pallas_call(
        _proj_kernel,
        grid=(t // tm,),
        in_specs=[row(d), full(norm1_g.reshape(1, d)), full(w_in_p), full(bd), full(qg), full(kg),
                  tab, tab, tab],
        out_specs=[row(ATTN_DIM), row(KV_DIM), row(KV_DIM), row(CONV_DIM), row(CONV_DIM),
                   row(D_MODEL), row(D_MODEL)],
        out_shape=[jax.ShapeDtypeStruct((t, w), BF16)
                   for w in (ATTN_DIM, KV_DIM, KV_DIM, CONV_DIM, CONV_DIM, D_MODEL, D_MODEL)],
        compiler_params=pltpu.CompilerParams(dimension_semantics=("arbitrary",),
                                             vmem_limit_bytes=VMEM_LIMIT),
        name="proj",
    )(xf, norm1_g.reshape(1, d), w_in_p, bd, qg, kg, ctab, s1tab, s2tab)

    nj = seq // tq
    wpb = tq // WINDOW
    nb_seq = seq // WINDOW
    hpb = tq // 16
    rowq = lambda w: pl.BlockSpec((tq, w), lambda b, j: (b * nj + j, 0))
    prev_blk = pl.BlockSpec((WINDOW, KV_DIM),
                            lambda b, j: (b * nb_seq + jnp.maximum(j * wpb - 1, 0), 0))
    next_blk = pl.BlockSpec((WINDOW, KV_DIM),
                            lambda b, j: (b * nb_seq + jnp.minimum((j + 1) * wpb, nb_seq - 1), 0))
    uprev = pl.BlockSpec((16, CONV_DIM),
                         lambda b, j: (jnp.maximum((b * nj + j) * hpb - 1, 0), 0))
    unext = pl.BlockSpec((16, CONV_DIM),
                         lambda b, j: (jnp.minimum((b * nj + j + 1) * hpb, t // 16 - 1), 0))
    full2 = lambda a: pl.BlockSpec(a.shape, lambda b, j: (0,) * a.ndim)

    wr = jnp.concatenate([w_rg, jnp.zeros((d, 8 - N_GROUPS), F32), w_re,
                          jnp.zeros((d, LANES - ROUTER_ROWS), F32)], axis=1)
    wr_hi = wr.astype(BF16)
    wr2 = jnp.concatenate([wr_hi, (wr - wr_hi.astype(F32)).astype(BF16)], axis=1)
    br = jnp.concatenate([b_rg, jnp.zeros((8 - N_GROUPS,), F32), b_re,
                          jnp.zeros((LANES - ROUTER_ROWS,), F32)]).reshape(1, LANES)
    wc_b = w_conv_proj.astype(BF16)
    wo_b = w_out.astype(BF16)
    cbias = conv_b.reshape(1, CONV_DIM)
    g2 = norm2_g.reshape(1, d)

    x1, h2, logits = pl.pallas_call(
        functools.partial(_mix_kernel, tq=tq),
        grid=(bsz, nj),
        in_specs=[pl.BlockSpec(memory_space=pltpu.SMEM),
                  rowq(d), rowq(ATTN_DIM), prev_blk, rowq(KV_DIM), next_blk,
                  prev_blk, rowq(KV_DIM), next_blk,
                  rowq(CONV_DIM), uprev, rowq(CONV_DIM), unext, rowq(D_MODEL), rowq(D_MODEL),
                  full2(conv_w), full2(cbias), full2(wa_p), full2(wc_b), full2(wo_b), full2(g2),
                  full2(wr2), full2(br)],
        out_specs=[rowq(d), pl.BlockSpec((tq * TOK_SUBLANES, LANES), lambda b, j: (b * nj + j, 0)),
                   rowq(LANES)],
        out_shape=[jax.ShapeDtypeStruct((t, d), F32),
                   jax.ShapeDtypeStruct((t * TOK_SUBLANES, LANES), U32),
                   jax.ShapeDtypeStruct((t, LANES), F32)],
        scratch_shapes=[pltpu.VMEM((tq + 16, CONV_DIM), F32), pltpu.VMEM((tq, ATTN_DIM), BF16)],
        compiler_params=pltpu.CompilerParams(dimension_semantics=("arbitrary", "arbitrary"),
                                             vmem_limit_bytes=VMEM_LIMIT),
        name="mix",
    )(attn_sink, xf, q, k, k, k, v, v, v, cb, u, u, u, ga, gc,
      conv_w, cbias, wa_p, wc_b, wo_b, g2, wr2, br)

    tr = min(ROUTE_TILE, t)
    assert t % tr == 0 and tr % RANK_CHUNK == 0
    tri = jnp.asarray(np.triu(np.ones((RANK_CHUNK, RANK_CHUNK), np.float32), 1), BF16)
    colr = lambda r: pl.BlockSpec((r, tr), lambda i: (0, i))
    route_i, route_f, counts = pl.pallas_call(
        functools.partial(_route_kernel, tr=tr),
        grid=(t // tr,),
        in_specs=[pl.BlockSpec((tr, LANES), lambda i: (i, 0)), pl.BlockSpec(tri.shape, lambda i: (0, 0))],
        out_specs=[colr(8), colr(8), pl.BlockSpec((N_EXPERTS, LANES), lambda i: (0, 0))],
        out_shape=[jax.ShapeDtypeStruct((8, t), I32), jax.ShapeDtypeStruct((8, t), F32),
                   jax.ShapeDtypeStruct((N_EXPERTS, LANES), F32)],
        scratch_shapes=[pltpu.VMEM((N_EXPERTS, LANES), F32)],
        compiler_params=pltpu.CompilerParams(dimension_semantics=("arbitrary",),
                                             vmem_limit_bytes=VMEM_LIMIT),
        name="route",
    )(logits, tri)

    n_rows = 2 * t + N_EXPERTS * MOE_ROWS
    nblk = n_rows // MOE_ROWS
    cnt = counts[:, 0].astype(I32)
    padded = ((cnt + MOE_ROWS - 1) // MOE_ROWS) * MOE_ROWS
    pad_ends = jnp.cumsum(padded)
    pad_starts = pad_ends - padded
    eids = jnp.arange(N_EXPERTS, dtype=I32)[:, None, None]
    dest = jnp.sum(jnp.where(route_i[None, 0:2] == eids, pad_starts[:, None, None], 0), axis=0) \
        + route_i[2:4]
    blk_start = jnp.arange(nblk, dtype=I32) * MOE_ROWS
    block_e = jnp.minimum(jnp.sum((pad_ends[None, :] <= blk_start[:, None]).astype(I32), axis=1),
                          N_EXPERTS - 1)
    nused = (pad_ends[-1:] // MOE_ROWS).astype(I32)

    x_pad = _sc_dispatch(h2.reshape(t, TOK_SUBLANES, LANES), dest[0], dest[1], n_rows)
    x_pad = x_pad.reshape(n_rows * TOK_SUBLANES, LANES)

    blk_rows = MOE_ROWS * TOK_SUBLANES
    wspec = lambda a, b: pl.BlockSpec((None, a, b), lambda i, be, nu: (be[i], 0, 0))
    y_pad = pl.pallas_call(
        _moe_kernel,
        grid_spec=pltpu.PrefetchScalarGridSpec(
            num_scalar_prefetch=2,
            grid=(nblk,),
            in_specs=[pl.BlockSpec((blk_rows, LANES), lambda i, be, nu: (i, 0)),
                      wspec(d, D_FF), wspec(d, D_FF), wspec(D_FF, d)],
            out_specs=pl.BlockSpec((blk_rows, LANES), lambda i, be, nu: (i, 0)),
            scratch_shapes=[pltpu.VMEM((d, D_FF), BF16), pltpu.VMEM((d, D_FF), BF16),
                            pltpu.VMEM((D_FF, d), BF16)]),
        out_shape=jax.ShapeDtypeStruct((n_rows * TOK_SUBLANES, LANES), U32),
        compiler_params=pltpu.CompilerParams(dimension_semantics=("arbitrary",),
                                             vmem_limit_bytes=VMEM_LIMIT),
        name="moe",
    )(block_e, nused, x_pad, w_gate, w_up, w_down)

    yk = _sc_gather(y_pad.reshape(n_rows, TOK_SUBLANES, LANES), dest[0], dest[1])
    yk = yk.reshape(2 * t * TOK_SUBLANES, LANES)
    tc = min(512, t)
    nt = t // tc
    gates_t = route_f[0:2].T
    out = pl.pallas_call(
        functools.partial(_combine_kernel, tm=tc),
        grid=(nt,),
        in_specs=[pl.BlockSpec((tc, d), lambda i: (i, 0)),
                  pl.BlockSpec((tc, 2), lambda i: (i, 0)),
                  pl.BlockSpec((tc * TOK_SUBLANES, LANES), lambda i: (i, 0)),
                  pl.BlockSpec((tc * TOK_SUBLANES, LANES), lambda i: (nt + i, 0))],
        out_specs=pl.BlockSpec((tc, d), lambda i: (i, 0)),
        out_shape=jax.ShapeDtypeStruct((t, d), F32),
        compiler_params=pltpu.CompilerParams(dimension_semantics=("arbitrary",),
                                             vmem_limit_bytes=VMEM_LIMIT),
        name="combine",
    )(x1, gates_t, yk, yk)
    return out.reshape(bsz, seq, d)


def kernel(x, norm1_g, w_in, q_norm_g, k_norm_g, attn_sink, conv_w, conv_b, w_attn_proj, w_conv_proj, w_out, norm2_g, w_router_group, b_router_group, w_router_expert, b_router_expert, w_gate_e, w_up_e, w_down_e):
    for l in range(norm1_g.shape[0]):
        x = _layer(x, norm1_g[l], w_in[l], q_norm_g[l], k_norm_g[l], attn_sink[l], conv_w[l],
                   conv_b[l], w_attn_proj[l], w_conv_proj[l], w_out[l], norm2_g[l],
                   w_router_group[l], b_router_group[l], w_router_expert[l], b_router_expert[l],
                   w_gate_e[l], w_up_e[l], w_down_e[l])
    return x
```

```python
import functools

import jax
import jax.numpy as jnp
import numpy as np
from jax import lax
from jax.experimental import pallas as pl
from jax.experimental.pallas import tpu as pltpu
from jax.experimental.pallas import tpu_sc as plsc

F32 = jnp.float32
BF16 = jnp.bfloat16
I32 = jnp.int32
U32 = jnp.uint32
HI_MASK = np.uint32(0xFFFF0000)

D_MODEL = 1024
N_HEADS = 8
N_KV_HEADS = 2
HEAD_DIM = 64
ATTN_DIM = N_HEADS * HEAD_DIM
KV_DIM = N_KV_HEADS * HEAD_DIM
WINDOW = 128
ROT_DIM = HEAD_DIM // 4
ROPE_THETA = 500000.0
CONV_DIM = D_MODEL // 2
N_GROUPS = 4
EXPERTS_PER_GROUP = 8
N_EXPERTS = N_GROUPS * EXPERTS_PER_GROUP
D_FF = D_MODEL // 4
NORM_EPS = 1e-6
MASK_VALUE = -1e30
NEG_BIG = -3.0e38
LOG2E = 1.4426950408889634

LANES = 128
TOK_SUBLANES = D_MODEL // (2 * LANES)
HALF_D = D_MODEL // 2
SC_WINDOW = 32
ROUTER_ROWS = 8 + N_EXPERTS
ROUTE_TILE = 2048
RANK_CHUNK = 256
MOE_ROWS = 512
TOKEN_CHUNKS = 2
VMEM_LIMIT = 56 * 1024 * 1024

_Q0, _K0, _V0, _CB0, _CC0, _CX0, _GA0, _GC0, _END = (
    0, 512, 640, 768, 1280, 1792, 2304, 3328, 4352)


def _store_packed_rows(ref, val, n):
    for c in range(TOK_SUBLANES):
        lo = val[:, c * LANES:(c + 1) * LANES].astype(BF16).astype(F32)
        hi = val[:, HALF_D + c * LANES:HALF_D + (c + 1) * LANES].astype(BF16).astype(F32)
        word = (lax.bitcast_convert_type(lo, U32) >> 16) | (lax.bitcast_convert_type(hi, U32) & HI_MASK)
        ref[pl.ds(c, n, stride=TOK_SUBLANES), :] = word


def _load_packed_rows(ref, n):
    words = [ref[pl.ds(c, n, stride=TOK_SUBLANES), :] for c in range(TOK_SUBLANES)]
    lo = [lax.bitcast_convert_type(w << 16, F32) for w in words]
    hi = [lax.bitcast_convert_type(w & HI_MASK, F32) for w in words]
    return lo + hi


def _rope_chunks(t, c, s1, s2):
    outs = []
    for j in range(t.shape[1] // LANES):
        tj = t[:, j * LANES:(j + 1) * LANES]
        outs.append(tj * c + pltpu.roll(tj, LANES - ROT_DIM // 2, 1) * s1
                    + pltpu.roll(tj, ROT_DIM // 2, 1) * s2)
    return outs[0] if len(outs) == 1 else jnp.concatenate(outs, axis=1)


def _proj_kernel(x_ref, g1_ref, w_ref, bd_ref, qg_ref, kg_ref, c_ref, s1_ref, s2_ref,
                 q_ref, k_ref, v_ref, cb_ref, u_ref, ga_ref, gc_ref):
    x = x_ref[...]
    ms = jnp.mean(x * x, axis=-1, keepdims=True)
    h = (x * lax.rsqrt(ms + NORM_EPS) * g1_ref[...]).astype(BF16)

    def proj(a, b):
        return jnp.dot(h, w_ref[:, a:b], preferred_element_type=F32)

    c, s1, s2 = c_ref[...], s1_ref[...], s2_ref[...]

    q = proj(_Q0, _K0)
    qms = jnp.dot((q * q).astype(BF16), bd_ref[...], preferred_element_type=F32)
    qg = jnp.concatenate([qg_ref[...]] * (ATTN_DIM // LANES), axis=1)
    qn = q * lax.rsqrt(qms + NORM_EPS) * qg
    q_ref[...] = (_rope_chunks(qn, c, s1, s2) * (HEAD_DIM ** -0.5 * LOG2E)).astype(BF16)

    k = proj(_K0, _V0)
    kms = jnp.dot((k * k).astype(BF16), bd_ref[0:KV_DIM, 0:KV_DIM], preferred_element_type=F32)
    kn = k * lax.rsqrt(kms + NORM_EPS) * kg_ref[...]
    k_ref[...] = _rope_chunks(kn, c, s1, s2).astype(BF16)

    v_ref[...] = proj(_V0, _CB0).astype(BF16)
    cb_ref[...] = proj(_CB0, _CC0).astype(BF16)
    u_ref[...] = (proj(_CC0, _CX0) * proj(_CX0, _GA0)).astype(BF16)
    ga_ref[...] = jax.nn.sigmoid(proj(_GA0, _GC0)).astype(BF16)
    gc_ref[...] = jax.nn.sigmoid(proj(_GC0, _END)).astype(BF16)


def _mix_kernel(sink_ref, x_ref, q_ref, kp_ref, kc_ref, kn_ref, vp_ref, vc_ref, vn_ref,
                cb_ref, up_ref, uc_ref, un_ref, ga_ref, gc_ref,
                cw_ref, cbias_ref, wa_ref, wc_ref, wo_ref, g2_ref,
                wr_ref, br_ref,
                x1_ref, h2_ref, lg_ref,
                ubuf, attn_scr, *, tq):
    j = pl.program_id(1)
    nj = pl.num_programs(1)
    nsub = tq // WINDOW

    kbuf = jnp.concatenate([kp_ref[...], kc_ref[...], kn_ref[...]], axis=0)
    vbuf = jnp.concatenate([vp_ref[...], vc_ref[...], vn_ref[...]], axis=0)
    low_half = lax.broadcasted_iota(I32, (WINDOW, LANES), 1) < HEAD_DIM
    low_half3 = lax.broadcasted_iota(I32, (3 * WINDOW, LANES), 1) < HEAD_DIM
    qrow = lax.broadcasted_iota(I32, (WINDOW, WINDOW), 0)
    kcol = lax.broadcasted_iota(I32, (WINDOW, WINDOW), 1)
    bias_lo = jnp.where(kcol >= qrow, 0.0, MASK_VALUE)
    bias_hi = jnp.where(kcol <= qrow, 0.0, MASK_VALUE)
    one = jnp.ones((3 * WINDOW, LANES), BF16)
    for n in range(nsub):
        blo = jnp.where(j > 0, bias_lo, MASK_VALUE) if n == 0 else bias_lo
        bhi = jnp.where(j < nj - 1, bias_hi, MASK_VALUE) if n == nsub - 1 else bias_hi
        qs = q_ref[n * WINDOW:(n + 1) * WINDOW, :]
        zero = jnp.zeros((WINDOW, LANES), BF16)
        q_stack = []
        for hd in range(N_HEADS):
            col = qs[:, (hd % 4) * LANES:(hd % 4 + 1) * LANES]
            keep = low_half if hd < 4 else jnp.logical_not(low_half)
            q_stack.append(jnp.where(keep, col, zero))
        q_stack = jnp.concatenate(q_stack, axis=0)
        kwin = kbuf[n * WINDOW:(n + 3) * WINDOW, :]
        vwin = vbuf[n * WINDOW:(n + 3) * WINDOW, :]
        s = lax.dot_general(q_stack, kwin, (((1,), (1,)), ((), ())),
                            preferred_element_type=F32)
        p_list, es_list = [], []
        for hd in range(N_HEADS):
            sink = sink_ref[hd] * LOG2E
            rows = slice(hd * WINDOW, (hd + 1) * WINDOW)
            s0 = s[rows, 0:WINDOW] + blo
            s1 = s[rows, WINDOW:2 * WINDOW]
            s2 = s[rows, 2 * WINDOW:3 * WINDOW] + bhi
            m = jnp.max(jnp.maximum(jnp.maximum(s0, s1), s2), axis=-1, keepdims=True)
            m = jnp.maximum(m, sink)
            p_list.append(jnp.concatenate(
                [jnp.exp2(s0 - m).astype(BF16), jnp.exp2(s1 - m).astype(BF16),
                 jnp.exp2(s2 - m).astype(BF16)], axis=1))
            es_list.append(jnp.exp2(sink - m))
        half = N_HEADS // 2
        o_lo = jnp.dot(jnp.concatenate(p_list[:half], axis=0), jnp.where(low_half3, vwin, one),
                       preferred_element_type=F32)
        o_hi = jnp.dot(jnp.concatenate(p_list[half:], axis=0), jnp.where(low_half3, one, vwin),
                       preferred_element_type=F32)
        cols = []
        for c4 in range(half):
            rows = slice(c4 * WINDOW, (c4 + 1) * WINDOW)
            oa, ob = o_lo[rows, :], o_hi[rows, :]
            da = pltpu.roll(oa, HEAD_DIM, 1) + es_list[c4]
            db = pltpu.roll(ob, HEAD_DIM, 1) + es_list[c4 + half]
            cols.append(jnp.where(low_half, oa / da, ob / db))
        attn_scr[n * WINDOW:(n + 1) * WINDOW, :] = jnp.concatenate(cols, axis=1).astype(BF16)
    attn = attn_scr[...]

    ubuf[8:8 + tq, :] = uc_ref[...].astype(F32)
    prev_rows = up_ref[...].astype(F32)
    next_rows = un_ref[...].astype(F32)
    ubuf[7:8, :] = prev_rows[15:16, :] * (j > 0).astype(F32)
    ubuf[8 + tq:9 + tq, :] = next_rows[0:1, :] * (j < nj - 1).astype(F32)
    cw = cw_ref[...]
    conv = (ubuf[7:7 + tq, :] * cw[0:1, :] + ubuf[8:8 + tq, :] * cw[1:2, :]
            + ubuf[9:9 + tq, :] * cw[2:3, :] + cbias_ref[...])
    cgate = (cb_ref[...].astype(F32) * conv).astype(BF16)

    merged = (ga_ref[...].astype(F32) * jnp.dot(attn, wa_ref[...], preferred_element_type=F32)
              + gc_ref[...].astype(F32) * jnp.dot(cgate, wc_ref[...], preferred_element_type=F32))
    x1 = x_ref[...] + jnp.dot(merged.astype(BF16), wo_ref[...], preferred_element_type=F32)
    x1_ref[...] = x1

    ms = jnp.mean(x1 * x1, axis=-1, keepdims=True)
    h2 = x1 * lax.rsqrt(ms + NORM_EPS) * g2_ref[...]
    _store_packed_rows(h2_ref, h2, tq)
    h_hi = h2.astype(BF16)
    h_lo = (h2 - h_hi.astype(F32)).astype(BF16)
    acc = jnp.dot(h_hi, wr_ref[...], preferred_element_type=F32)
    logits = (acc[:, 0:LANES] + acc[:, LANES:2 * LANES]
              + jnp.dot(h_lo, wr_ref[:, 0:LANES], preferred_element_type=F32) + br_ref[...])
    lg_ref[...] = logits


def _route_kernel(lg_ref, tri_ref, ri_ref, rf_ref, cnt_ref, base_ref, *, tr):
    @pl.when(pl.program_id(0) == 0)
    def _():
        base_ref[...] = jnp.zeros_like(base_ref)

    logits = lg_ref[...].T[0:ROUTER_ROWS, :]
    grow = lax.broadcasted_iota(I32, (8, tr), 0)
    gl = jnp.where(grow < N_GROUPS, logits[0:8, :], NEG_BIG)
    gmax = jnp.max(gl, axis=0, keepdims=True)
    grp = jnp.min(jnp.where(gl == gmax, grow, 8), axis=0, keepdims=True)
    p_grp = 1.0 / jnp.sum(jnp.exp(gl - gmax), axis=0, keepdims=True)
    el = logits[8:ROUTER_ROWS, :]
    erow = lax.broadcasted_iota(I32, (N_EXPERTS, tr), 0)
    in_grp = (erow // EXPERTS_PER_GROUP) == grp
    l_a = jnp.where(in_grp, el, NEG_BIG)
    l1 = jnp.max(l_a, axis=0, keepdims=True)
    i1 = jnp.min(jnp.where(l_a == l1, erow, N_EXPERTS), axis=0, keepdims=True)
    l_b = jnp.where(erow == i1, NEG_BIG, l_a)
    l2 = jnp.max(l_b, axis=0, keepdims=True)
    i2 = jnp.min(jnp.where(l_b == l2, erow, N_EXPERTS), axis=0, keepdims=True)
    r21 = jnp.exp(l2 - l1)
    t1 = 1.0 / (1.0 + r21)
    gate1 = p_grp * t1
    gate2 = p_grp * (r21 * t1)
    oh1 = (erow == i1).astype(F32)
    oh2 = (erow == i2).astype(F32)
    oh = oh1 + oh2
    base = base_ref[...][:, 0:1]
    r1, r2 = [], []
    for c in range(tr // RANK_CHUNK):
        cols = slice(c * RANK_CHUNK, (c + 1) * RANK_CHUNK)
        ohc = oh[:, cols]
        before = jnp.dot(ohc.astype(BF16), tri_ref[...], preferred_element_type=F32) + base
        r1.append(jnp.sum(oh1[:, cols] * before, axis=0, keepdims=True))
        r2.append(jnp.sum(oh2[:, cols] * before, axis=0, keepdims=True))
        base = base + jnp.sum(ohc, axis=1, keepdims=True)
    rank1 = jnp.concatenate(r1, axis=1)
    rank2 = jnp.concatenate(r2, axis=1)
    new_base = jnp.broadcast_to(base, base_ref.shape)
    base_ref[...] = new_base
    cnt_ref[...] = new_base
    zi = jnp.zeros((4, tr), I32)
    ri_ref[...] = jnp.concatenate([i1, i2, rank1.astype(I32), rank2.astype(I32), zi], axis=0)
    rf_ref[...] = jnp.concatenate([gate1, gate2, jnp.zeros((6, tr), F32)], axis=0)


def _moe_kernel(be_ref, nused_ref, x_ref, wg_ref, wu_ref, wd_ref, y_ref, wg_s, wu_s, wd_s):
    i = pl.program_id(0)

    @pl.when((i == 0) | (be_ref[i] != be_ref[jnp.maximum(i - 1, 0)]))
    def _():
        wg_s[...] = wg_ref[...].astype(BF16)
        wu_s[...] = wu_ref[...].astype(BF16)
        wd_s[...] = wd_ref[...].astype(BF16)

    @pl.when(i < nused_ref[0])
    def _():
        xb = jnp.concatenate([c.astype(BF16) for c in _load_packed_rows(x_ref, MOE_ROWS)], axis=1)
        g = jnp.dot(xb, wg_s[...], preferred_element_type=F32)
        u = jnp.dot(xb, wu_s[...], preferred_element_type=F32)
        hid = (g * jax.nn.sigmoid(g) * u).astype(BF16)
        _store_packed_rows(y_ref, jnp.dot(hid, wd_s[...], preferred_element_type=F32), MOE_ROWS)

    @pl.when(i >= nused_ref[0])
    def _():
        y_ref[...] = jnp.zeros_like(y_ref)


def _combine_kernel(x1_ref, gate_ref, y0_ref, y1_ref, *rest, tm):
    o_ref = rest[-1]
    g = gate_ref[...]
    g0, g1 = g[:, 0:1], g[:, 1:2]
    y0 = _load_packed_rows(y0_ref, tm)
    y1 = _load_packed_rows(y1_ref, tm)
    for c in range(2 * TOK_SUBLANES):
        cols = slice(c * LANES, (c + 1) * LANES)
        o_ref[:, cols] = x1_ref[:, cols] + g0 * y0[c] + g1 * y1[c]


def _sc_windows(n_tokens):
    info = plsc.get_sparse_core_info()
    n_workers = info.num_cores * info.num_subcores
    assert n_tokens % (n_workers * SC_WINDOW * 2) == 0
    return info.num_cores, n_tokens // (n_workers * SC_WINDOW)


def _sc_dispatch(h3, d0, d1, n_rows):
    t = h3.shape[0]
    n_cores, nwin = _sc_windows(t)
    mesh = plsc.VectorSubcoreMesh(core_axis_name="c", subcore_axis_name="s")
    rows = pltpu.VMEM((SC_WINDOW,) + h3.shape[1:], h3.dtype)
    idx = pltpu.VMEM((nwin, SC_WINDOW), I32)

    @functools.partial(
        pl.kernel, mesh=mesh,
        out_type=jax.ShapeDtypeStruct((n_rows,) + h3.shape[1:], h3.dtype),
        scratch_types=[idx, idx, idx, rows, rows] + [pltpu.SemaphoreType.DMA] * 3,
        name="sc_dispatch")
    def run(h_hbm, d0_hbm, d1_hbm, it_hbm, o_hbm, i0, i1, it, buf_a, buf_b, sem_a, sem_b, sem_s):
        wbase = (lax.axis_index("s") * n_cores + lax.axis_index("c")) * nwin
        pltpu.sync_copy(d0_hbm.at[pl.ds(wbase, nwin)], i0)
        pltpu.sync_copy(d1_hbm.at[pl.ds(wbase, nwin)], i1)
        pltpu.sync_copy(it_hbm.at[pl.ds(wbase, nwin)], it)

        @pl.loop(0, nwin, step=2)
        def _(w):
            ga = pltpu.async_copy(h_hbm.at[it.at[w]], buf_a, sem_a)
            gb = pltpu.async_copy(h_hbm.at[it.at[w + 1]], buf_b, sem_b)
            ga.wait()
            s0 = pltpu.async_copy(buf_a, o_hbm.at[i0.at[w]], sem_s)
            s1 = pltpu.async_copy(buf_a, o_hbm.at[i1.at[w]], sem_s)
            gb.wait()
            s2 = pltpu.async_copy(buf_b, o_hbm.at[i0.at[w + 1]], sem_s)
            s3 = pltpu.async_copy(buf_b, o_hbm.at[i1.at[w + 1]], sem_s)
            s0.wait()
            s1.wait()
            s2.wait()
            s3.wait()

    win = lambda a: a.reshape(-1, SC_WINDOW)
    return run(h3, win(d0), win(d1), win(jnp.arange(t, dtype=I32)))


def _sc_gather(y3, d0, d1):
    t = d0.shape[0]
    n_cores, nwin = _sc_windows(t)
    mesh = plsc.VectorSubcoreMesh(core_axis_name="c", subcore_axis_name="s")
    rows = pltpu.VMEM((SC_WINDOW,) + y3.shape[1:], y3.dtype)
    idx = pltpu.VMEM((nwin, SC_WINDOW), I32)

    @functools.partial(
        pl.kernel, mesh=mesh,
        out_type=jax.ShapeDtypeStruct((2 * t,) + y3.shape[1:], y3.dtype),
        scratch_types=[idx] * 4 + [rows] * 4 + [pltpu.SemaphoreType.DMA] * 5,
        name="sc_gather")
    def run(y_hbm, d0_hbm, d1_hbm, ita_hbm, itb_hbm, o_hbm, i0, i1, ita, itb,
            b0, b1, b2, b3, m0, m1, m2, m3, ms):
        wbase = (lax.axis_index("s") * n_cores + lax.axis_index("c")) * nwin
        pltpu.sync_copy(d0_hbm.at[pl.ds(wbase, nwin)], i0)
        pltpu.sync_copy(d1_hbm.at[pl.ds(wbase, nwin)], i1)
        pltpu.sync_copy(ita_hbm.at[pl.ds(wbase, nwin)], ita)
        pltpu.sync_copy(itb_hbm.at[pl.ds(wbase, nwin)], itb)

        @pl.loop(0, nwin, step=2)
        def _(w):
            g0 = pltpu.async_copy(y_hbm.at[i0.at[w]], b0, m0)
            g1 = pltpu.async_copy(y_hbm.at[i1.at[w]], b1, m1)
            g2 = pltpu.async_copy(y_hbm.at[i0.at[w + 1]], b2, m2)
            g3 = pltpu.async_copy(y_hbm.at[i1.at[w + 1]], b3, m3)
            g0.wait()
            s0 = pltpu.async_copy(b0, o_hbm.at[ita.at[w]], ms)
            g1.wait()
            s1 = pltpu.async_copy(b1, o_hbm.at[itb.at[w]], ms)
            g2.wait()
            s2 = pltpu.async_copy(b2, o_hbm.at[ita.at[w + 1]], ms)
            g3.wait()
            s3 = pltpu.async_copy(b3, o_hbm.at[itb.at[w + 1]], ms)
            s0.wait()
            s1.wait()
            s2.wait()
            s3.wait()

    win = lambda a: a.reshape(-1, SC_WINDOW)
    iota = jnp.arange(2 * t, dtype=I32)
    return run(y3, win(d0), win(d1), win(iota[:t]), win(iota[t:]))


def _rope_lane_tables(seq_len):
    half = ROT_DIM // 2
    inv_freq = ROPE_THETA ** (-jnp.arange(0, ROT_DIM, 2, dtype=F32) / ROT_DIM)
    ang = jnp.arange(seq_len, dtype=F32)[:, None] * inv_freq[None, :]
    cos, sin = jnp.cos(ang), jnp.sin(ang)
    m = np.arange(LANES) % HEAD_DIM
    f = m % half
    c = jnp.where(m[None, :] < ROT_DIM, cos[:, f], 1.0)
    s1 = jnp.where(m[None, :] < half, -sin[:, f], 0.0)
    s2 = jnp.where((m[None, :] >= half) & (m[None, :] < ROT_DIM), sin[:, f], 0.0)
    return c.astype(F32), s1.astype(F32), s2.astype(F32)


def _layer(x, norm1_g, w_in, q_norm_g, k_norm_g, attn_sink, conv_w, conv_b, w_attn_proj,
           w_conv_proj, w_out, norm2_g, w_rg, b_rg, w_re, b_re, w_gate, w_up, w_down):
    bsz, seq, d = x.shape
    t = bsz * seq
    tm = min(512, t)
    tq = min(256, seq)
    assert d == D_MODEL and seq % tq == 0 and t % tm == 0 and seq % tm == 0 and tq % WINDOW == 0
    xf = x.reshape(t, d)

    hpg = N_HEADS // N_KV_HEADS
    wq_p = (w_in[:, :ATTN_DIM].reshape(d, N_KV_HEADS, hpg, HEAD_DIM).transpose(0, 2, 1, 3)
            .reshape(d, ATTN_DIM))
    w_in_p = jnp.concatenate([wq_p, w_in[:, ATTN_DIM:]], axis=1).astype(BF16)
    wa_p = (w_attn_proj.reshape(N_KV_HEADS, hpg, HEAD_DIM, d).transpose(1, 0, 2, 3)
            .reshape(ATTN_DIM, d).astype(BF16))
    bd = jnp.asarray(np.kron(np.eye(N_HEADS), np.full((HEAD_DIM, HEAD_DIM), 1.0 / HEAD_DIM)), BF16)
    qg = jnp.tile(q_norm_g, LANES // HEAD_DIM).reshape(1, LANES)
    kg = jnp.tile(k_norm_g, LANES // HEAD_DIM).reshape(1, LANES)
    ctab, s1tab, s2tab = _rope_lane_tables(seq)
    nseq_t = seq // tm

    row = lambda w: pl.BlockSpec((tm, w), lambda i: (i, 0))
    full = lambda a: pl.BlockSpec(a.shape, lambda i: (0,) * a.ndim)
    tab = pl.BlockSpec((tm, LANES), lambda i: (i % nseq_t, 0))
    q, k, v, cb, u, ga, gc = pl.pallas_call(
        _proj_kernel,
        grid=(t // tm,),
        in_specs=[row(d), full(norm1_g.reshape(1, d)), full(w_in_p), full(bd), full(qg), full(kg),
                  tab, tab, tab],
        out_specs=[row(ATTN_DIM), row(KV_DIM), row(KV_DIM), row(CONV_DIM), row(CONV_DIM),
                   row(D_MODEL), row(D_MODEL)],
        out_shape=[jax.ShapeDtypeStruct((t, w), BF16)
                   for w in (ATTN_DIM, KV_DIM, KV_DIM, CONV_DIM, CONV_DIM, D_MODEL, D_MODEL)],
        compiler_params=pltpu.CompilerParams(dimension_semantics=("arbitrary",),
                                             vmem_limit_bytes=VMEM_LIMIT),
        name="proj",
    )(xf, norm1_g.reshape(1, d), w_in_p, bd, qg, kg, ctab, s1tab, s2tab)

    wr = jnp.concatenate([w_rg, jnp.zeros((d, 8 - N_GROUPS), F32), w_re,
                          jnp.zeros((d, LANES - ROUTER_ROWS), F32)], axis=1)
    wr_hi = wr.astype(BF16)
    wr2 = jnp.concatenate([wr_hi, (wr - wr_hi.astype(F32)).astype(BF16)], axis=1)
    br = jnp.concatenate([b_rg, jnp.zeros((8 - N_GROUPS,), F32), b_re,
                          jnp.zeros((LANES - ROUTER_ROWS,), F32)]).reshape(1, LANES)
    mix_params = (conv_w, conv_b.reshape(1, CONV_DIM), wa_p, w_conv_proj.astype(BF16), w_out.astype(BF16),
                  norm2_g.reshape(1, d), wr2, br)

    n_chunks = TOKEN_CHUNKS if bsz % TOKEN_CHUNKS == 0 else 1
    out = None
    for c in range(n_chunks):
        out = _chunk_tail(c, n_chunks, out, xf, (q, k, v, cb, u, ga, gc), attn_sink, mix_params,
                          (w_gate, w_up, w_down), bsz, seq, tq)
    return out.reshape(bsz, seq, d)


def _chunk_tail(c, n_chunks, out_prev, xf, acts, attn_sink, mix_params, expert_w, bsz, seq, tq):
    q, k, v, cb, u, ga, gc = acts
    d = D_MODEL
    t_all = bsz * seq
    bc = bsz // n_chunks
    b0 = c * bc
    t = bc * seq
    nj = seq // tq
    wpb = tq // WINDOW
    nb_seq = seq // WINDOW
    hpb = tq // 16
    row_in = lambda w: pl.BlockSpec((tq, w), lambda b, j: ((b0 + b) * nj + j, 0))
    row_out = lambda w: pl.BlockSpec((tq, w), lambda b, j: (b * nj + j, 0))
    prev_blk = pl.BlockSpec((WINDOW, KV_DIM),
                            lambda b, j: ((b0 + b) * nb_seq + jnp.maximum(j * wpb - 1, 0), 0))
    next_blk = pl.BlockSpec((WINDOW, KV_DIM),
                            lambda b, j: ((b0 + b) * nb_seq + jnp.minimum((j + 1) * wpb, nb_seq - 1), 0))
    uprev = pl.BlockSpec((16, CONV_DIM),
                         lambda b, j: (jnp.maximum(((b0 + b) * nj + j) * hpb - 1, 0), 0))
    unext = pl.BlockSpec((16, CONV_DIM),
                         lambda b, j: (jnp.minimum(((b0 + b) * nj + j + 1) * hpb, t_all // 16 - 1), 0))
    full2 = lambda a: pl.BlockSpec(a.shape, lambda b, j: (0,) * a.ndim)

    x1, h2, logits = pl.pallas_call(
        functools.partial(_mix_kernel, tq=tq),
        grid=(bc, nj),
        in_specs=[pl.BlockSpec(memory_space=pltpu.SMEM),
                  row_in(d), row_in(ATTN_DIM), prev_blk, row_in(KV_DIM), next_blk,
                  prev_blk, row_in(KV_DIM), next_blk,
                  row_in(CONV_DIM), uprev, row_in(CONV_DIM), unext, row_in(D_MODEL), row_in(D_MODEL)]
                 + [full2(p) for p in mix_params],
        out_specs=[row_out(d), pl.BlockSpec((tq * TOK_SUBLANES, LANES), lambda b, j: (b * nj + j, 0)),
                   row_out(LANES)],
        out_shape=[jax.ShapeDtypeStruct((t, d), F32),
                   jax.ShapeDtypeStruct((t * TOK_SUBLANES, LANES), U32),
                   jax.ShapeDtypeStruct((t, LANES), F32)],
        scratch_shapes=[pltpu.VMEM((tq + 16, CONV_DIM), F32), pltpu.VMEM((tq, ATTN_DIM), BF16)],
        compiler_params=pltpu.CompilerParams(dimension_semantics=("arbitrary", "arbitrary"),
                                             vmem_limit_bytes=VMEM_LIMIT),
        name="mix",
    )(attn_sink, xf, q, k, k, k, v, v, v, cb, u, u, u, ga, gc, *mix_params)

    tr = min(ROUTE_TILE, t)
    assert t % tr == 0 and tr % RANK_CHUNK == 0
    tri = jnp.asarray(np.triu(np.ones((RANK_CHUNK, RANK_CHUNK), np.float32), 1), BF16)
    colr = lambda r: pl.BlockSpec((r, tr), lambda i: (0, i))
    route_i, route_f, counts = pl.pallas_call(
        functools.partial(_route_kernel, tr=tr),
        grid=(t // tr,),
        in_specs=[pl.BlockSpec((tr, LANES), lambda i: (i, 0)), pl.BlockSpec(tri.shape, lambda i: (0, 0))],
        out_specs=[colr(8), colr(8), pl.BlockSpec((N_EXPERTS, LANES), lambda i: (0, 0))],
        out_shape=[jax.ShapeDtypeStruct((8, t), I32), jax.ShapeDtypeStruct((8, t), F32),
                   jax.ShapeDtypeStruct((N_EXPERTS, LANES), F32)],
        scratch_shapes=[pltpu.VMEM((N_EXPERTS, LANES), F32)],
        compiler_params=pltpu.CompilerParams(dimension_semantics=("arbitrary",),
                                             vmem_limit_bytes=VMEM_LIMIT),
        name="route",
    )(logits, tri)

    n_rows = 2 * t + N_EXPERTS * MOE_ROWS
    nblk = n_rows // MOE_ROWS
    cnt = counts[:, 0].astype(I32)
    padded = ((cnt + MOE_ROWS - 1) // MOE_ROWS) * MOE_ROWS
    pad_ends = jnp.cumsum(padded)
    pad_starts = pad_ends - padded
    eids = jnp.arange(N_EXPERTS, dtype=I32)[:, None, None]
    dest = jnp.sum(jnp.where(route_i[None, 0:2] == eids, pad_starts[:, None, None], 0), axis=0) \
        + route_i[2:4]
    blk_start = jnp.arange(nblk, dtype=I32) * MOE_ROWS
    block_e = jnp.minimum(jnp.sum((pad_ends[None, :] <= blk_start[:, None]).astype(I32), axis=1),
                          N_EXPERTS - 1)
    nused = (pad_ends[-1:] // MOE_ROWS).astype(I32)

    x_pad = _sc_dispatch(h2.reshape(t, TOK_SUBLANES, LANES), dest[0], dest[1], n_rows)
    x_pad = x_pad.reshape(n_rows * TOK_SUBLANES, LANES)

    w_gate, w_up, w_down = expert_w
    blk_rows = MOE_ROWS * TOK_SUBLANES
    wspec = lambda a, b: pl.BlockSpec((None, a, b), lambda i, be, nu: (be[i], 0, 0))
    y_pad = pl.pallas_call(
        _moe_kernel,
        grid_spec=pltpu.PrefetchScalarGridSpec(
            num_scalar_prefetch=2,
            grid=(nblk,),
            in_specs=[pl.BlockSpec((blk_rows, LANES), lambda i, be, nu: (i, 0)),
                      wspec(d, D_FF), wspec(d, D_FF), wspec(D_FF, d)],
            out_specs=pl.BlockSpec((blk_rows, LANES), lambda i, be, nu: (i, 0)),
            scratch_shapes=[pltpu.VMEM((d, D_FF), BF16), pltpu.VMEM((d, D_FF), BF16),
                            pltpu.VMEM((D_FF, d), BF16)]),
        out_shape=jax.ShapeDtypeStruct((n_rows * TOK_SUBLANES, LANES), U32),
        compiler_params=pltpu.CompilerParams(dimension_semantics=("arbitrary",),
                                             vmem_limit_bytes=VMEM_LIMIT),
        name="moe",
    )(block_e, nused, x_pad, w_gate, w_up, w_down)

    yk = _sc_gather(y_pad.reshape(n_rows, TOK_SUBLANES, LANES), dest[0], dest[1])
    yk = yk.reshape(2 * t * TOK_SUBLANES, LANES)
    tc = min(512, t)
    nt = t // tc
    gates_t = route_f[0:2].T
    in_specs = [pl.BlockSpec((tc, d), lambda i: (i, 0)),
                pl.BlockSpec((tc, 2), lambda i: (i, 0)),
                pl.BlockSpec((tc * TOK_SUBLANES, LANES), lambda i: (i, 0)),
                pl.BlockSpec((tc * TOK_SUBLANES, LANES), lambda i: (nt + i, 0))]
    args = [x1, gates_t, yk, yk]
    aliases = {}
    if out_prev is not None:
        in_specs.append(pl.BlockSpec(memory_space=pl.ANY))
        args.append(out_prev)
        aliases = {len(args) - 1: 0}
    return pl.pallas_call(
        functools.partial(_combine_kernel, tm=tc),
        grid=(nt,),
        in_specs=in_specs,
        out_specs=pl.BlockSpec((tc, d), lambda i: (c * nt + i, 0)),
        out_shape=jax.ShapeDtypeStruct((t_all, d), F32),
        input_output_aliases=aliases,
        compiler_params=pltpu.CompilerParams(dimension_semantics=("arbitrary",),
                                             vmem_limit_bytes=VMEM_LIMIT),
        name="combine",
    )(*args)


def kernel(x, norm1_g, w_in, q_norm_g, k_norm_g, attn_sink, conv_w, conv_b, w_attn_proj, w_conv_proj, w_out, norm2_g, w_router_group, b_router_group, w_router_expert, b_router_expert, w_gate_e, w_up_e, w_down_e):
    for l in range(norm1_g.shape[0]):
        x = _layer(x, norm1_g[l], w_in[l], q_norm_g[l], k_norm_g[l], attn_sink[l], conv_w[l],
                   conv_b[l], w_attn_proj[l], w_conv_proj[l], w_out[l], norm2_g[l],
                   w_router_group[l], b_router_group[l], w_router_expert[l], b_router_expert[l],
                   w_gate_e[l], w_up_e[l], w_down_e[l])
    return x
```

```python
import functools

import jax
import jax.numpy as jnp
import numpy as np
from jax import lax
from jax.experimental import pallas as pl
from jax.experimental.pallas import tpu as pltpu
from jax.experimental.pallas import tpu_sc as plsc

F32 = jnp.float32
BF16 = jnp.bfloat16
I32 = jnp.int32
U32 = jnp.uint32
HI_MASK = np.uint32(0xFFFF0000)

D_MODEL = 1024
N_HEADS = 8
N_KV_HEADS = 2
HEAD_DIM = 64
ATTN_DIM = N_HEADS * HEAD_DIM
KV_DIM = N_KV_HEADS * HEAD_DIM
WINDOW = 128
ROT_DIM = HEAD_DIM // 4
ROPE_THETA = 500000.0
CONV_DIM = D_MODEL // 2
N_GROUPS = 4
EXPERTS_PER_GROUP = 8
N_EXPERTS = N_GROUPS * EXPERTS_PER_GROUP
D_FF = D_MODEL // 4
NORM_EPS = 1e-6
MASK_VALUE = -1e30
NEG_BIG = -3.0e38
LOG2E = 1.4426950408889634

LANES = 128
TOK_SUBLANES = D_MODEL // (2 * LANES)
HALF_D = D_MODEL // 2
SC_WINDOW = 32
ROUTER_ROWS = 8 + N_EXPERTS
ROUTE_TILE = 2048
RANK_CHUNK = 256
MOE_ROWS = 512
TOKEN_CHUNKS = 1
VMEM_LIMIT = 56 * 1024 * 1024

_Q0, _K0, _V0, _CB0, _CC0, _CX0, _GA0, _GC0, _END = (
    0, 512, 640, 768, 1280, 1792, 2304, 3328, 4352)


def _store_packed_rows(ref, val, n):
    for c in range(TOK_SUBLANES):
        lo = val[:, c * LANES:(c + 1) * LANES].astype(BF16).astype(F32)
        hi = val[:, HALF_D + c * LANES:HALF_D + (c + 1) * LANES].astype(BF16).astype(F32)
        word = (lax.bitcast_convert_type(lo, U32) >> 16) | (lax.bitcast_convert_type(hi, U32) & HI_MASK)
        ref[pl.ds(c, n, stride=TOK_SUBLANES), :] = word


def _load_packed_rows(ref, n):
    words = [ref[pl.ds(c, n, stride=TOK_SUBLANES), :] for c in range(TOK_SUBLANES)]
    lo = [lax.bitcast_convert_type(w << 16, F32) for w in words]
    hi = [lax.bitcast_convert_type(w & HI_MASK, F32) for w in words]
    return lo + hi


def _rope_chunks(t, c, s1, s2):
    outs = []
    for j in range(t.shape[1] // LANES):
        tj = t[:, j * LANES:(j + 1) * LANES]
        outs.append(tj * c + pltpu.roll(tj, LANES - ROT_DIM // 2, 1) * s1
                    + pltpu.roll(tj, ROT_DIM // 2, 1) * s2)
    return outs[0] if len(outs) == 1 else jnp.concatenate(outs, axis=1)


def _proj_kernel(x_ref, g1_ref, w_ref, bd_ref, qg_ref, kg_ref, c_ref, s1_ref, s2_ref,
                 q_ref, k_ref, v_ref, cb_ref, u_ref, ga_ref, gc_ref):
    x = x_ref[...]
    ms = jnp.mean(x * x, axis=-1, keepdims=True)
    h = (x * lax.rsqrt(ms + NORM_EPS) * g1_ref[...]).astype(BF16)

    def proj(a, b):
        return jnp.dot(h, w_ref[:, a:b], preferred_element_type=F32)

    c, s1, s2 = c_ref[...], s1_ref[...], s2_ref[...]

    q = proj(_Q0, _K0)
    qms = jnp.dot((q * q).astype(BF16), bd_ref[...], preferred_element_type=F32)
    qg = jnp.concatenate([qg_ref[...]] * (ATTN_DIM // LANES), axis=1)
    qn = q * lax.rsqrt(qms + NORM_EPS) * qg
    q_ref[...] = (_rope_chunks(qn, c, s1, s2) * (HEAD_DIM ** -0.5 * LOG2E)).astype(BF16)

    k = proj(_K0, _V0)
    kms = jnp.dot((k * k).astype(BF16), bd_ref[0:KV_DIM, 0:KV_DIM], preferred_element_type=F32)
    kn = k * lax.rsqrt(kms + NORM_EPS) * kg_ref[...]
    k_ref[...] = _rope_chunks(kn, c, s1, s2).astype(BF16)

    v_ref[...] = proj(_V0, _CB0).astype(BF16)
    cb_ref[...] = proj(_CB0, _CC0).astype(BF16)
    u_ref[...] = (proj(_CC0, _CX0) * proj(_CX0, _GA0)).astype(BF16)
    ga_ref[...] = jax.nn.sigmoid(proj(_GA0, _GC0)).astype(BF16)
    gc_ref[...] = jax.nn.sigmoid(proj(_GC0, _END)).astype(BF16)


def _mix_kernel(sink_ref, x_ref, q_ref, kp_ref, kc_ref, kn_ref, vp_ref, vc_ref, vn_ref,
                cb_ref, up_ref, uc_ref, un_ref, ga_ref, gc_ref,
                cw_ref, cbias_ref, wa_ref, wc_ref, wo_ref, g2_ref,
                wr_ref, br_ref,
                x1_ref, h2_ref, lg_ref,
                ubuf, attn_scr, *, tq):
    j = pl.program_id(1)
    nj = pl.num_programs(1)
    nsub = tq // WINDOW

    kbuf = jnp.concatenate([kp_ref[...], kc_ref[...], kn_ref[...]], axis=0)
    vbuf = jnp.concatenate([vp_ref[...], vc_ref[...], vn_ref[...]], axis=0)
    low_half = lax.broadcasted_iota(I32, (WINDOW, LANES), 1) < HEAD_DIM
    low_half3 = lax.broadcasted_iota(I32, (3 * WINDOW, LANES), 1) < HEAD_DIM
    qrow = lax.broadcasted_iota(I32, (WINDOW, WINDOW), 0)
    kcol = lax.broadcasted_iota(I32, (WINDOW, WINDOW), 1)
    bias_lo = jnp.where(kcol >= qrow, 0.0, MASK_VALUE)
    bias_hi = jnp.where(kcol <= qrow, 0.0, MASK_VALUE)
    one = jnp.ones((3 * WINDOW, LANES), BF16)
    for n in range(nsub):
        blo = jnp.where(j > 0, bias_lo, MASK_VALUE) if n == 0 else bias_lo
        bhi = jnp.where(j < nj - 1, bias_hi, MASK_VALUE) if n == nsub - 1 else bias_hi
        qs = q_ref[n * WINDOW:(n + 1) * WINDOW, :]
        zero = jnp.zeros((WINDOW, LANES), BF16)
        q_stack = []
        for hd in range(N_HEADS):
            col = qs[:, (hd % 4) * LANES:(hd % 4 + 1) * LANES]
            keep = low_half if hd < 4 else jnp.logical_not(low_half)
            q_stack.append(jnp.where(keep, col, zero))
        q_stack = jnp.concatenate(q_stack, axis=0)
        kwin = kbuf[n * WINDOW:(n + 3) * WINDOW, :]
        vwin = vbuf[n * WINDOW:(n + 3) * WINDOW, :]
        s = lax.dot_general(q_stack, kwin, (((1,), (1,)), ((), ())),
                            preferred_element_type=F32)
        p_list, es_list = [], []
        for hd in range(N_HEADS):
            sink = sink_ref[hd] * LOG2E
            rows = slice(hd * WINDOW, (hd + 1) * WINDOW)
            s0 = s[rows, 0:WINDOW] + blo
            s1 = s[rows, WINDOW:2 * WINDOW]
            s2 = s[rows, 2 * WINDOW:3 * WINDOW] + bhi
            m = jnp.max(jnp.maximum(jnp.maximum(s0, s1), s2), axis=-1, keepdims=True)
            m = jnp.maximum(m, sink)
            p_list.append(jnp.concatenate(
                [jnp.exp2(s0 - m).astype(BF16), jnp.exp2(s1 - m).astype(BF16),
                 jnp.exp2(s2 - m).astype(BF16)], axis=1))
            es_list.append(jnp.exp2(sink - m))
        half = N_HEADS // 2
        o_lo = jnp.dot(jnp.concatenate(p_list[:half], axis=0), jnp.where(low_half3, vwin, one),
                       preferred_element_type=F32)
        o_hi = jnp.dot(jnp.concatenate(p_list[half:], axis=0), jnp.where(low_half3, one, vwin),
                       preferred_element_type=F32)
        cols = []
        for c4 in range(half):
            rows = slice(c4 * WINDOW, (c4 + 1) * WINDOW)
            oa, ob = o_lo[rows, :], o_hi[rows, :]
            da = pltpu.roll(oa, HEAD_DIM, 1) + es_list[c4]
            db = pltpu.roll(ob, HEAD_DIM, 1) + es_list[c4 + half]
            cols.append(jnp.where(low_half, oa / da, ob / db))
        attn_scr[n * WINDOW:(n + 1) * WINDOW, :] = jnp.concatenate(cols, axis=1).astype(BF16)
    attn = attn_scr[...]

    ubuf[8:8 + tq, :] = uc_ref[...].astype(F32)
    prev_rows = up_ref[...].astype(F32)
    next_rows = un_ref[...].astype(F32)
    ubuf[7:8, :] = prev_rows[15:16, :] * (j > 0).astype(F32)
    ubuf[8 + tq:9 + tq, :] = next_rows[0:1, :] * (j < nj - 1).astype(F32)
    cw = cw_ref[...]
    conv = (ubuf[7:7 + tq, :] * cw[0:1, :] + ubuf[8:8 + tq, :] * cw[1:2, :]
            + ubuf[9:9 + tq, :] * cw[2:3, :] + cbias_ref[...])
    cgate = (cb_ref[...].astype(F32) * conv).astype(BF16)

    merged = (ga_ref[...].astype(F32) * jnp.dot(attn, wa_ref[...], preferred_element_type=F32)
              + gc_ref[...].astype(F32) * jnp.dot(cgate, wc_ref[...], preferred_element_type=F32))
    x1 = x_ref[...] + jnp.dot(merged.astype(BF16), wo_ref[...], preferred_element_type=F32)
    x1_ref[...] = x1

    ms = jnp.mean(x1 * x1, axis=-1, keepdims=True)
    h2 = x1 * lax.rsqrt(ms + NORM_EPS) * g2_ref[...]
    _store_packed_rows(h2_ref, h2, tq)
    h_hi = h2.astype(BF16)
    h_lo = (h2 - h_hi.astype(F32)).astype(BF16)
    acc = jnp.dot(h_hi, wr_ref[...], preferred_element_type=F32)
    logits = (acc[:, 0:LANES] + acc[:, LANES:2 * LANES]
              + jnp.dot(h_lo, wr_ref[:, 0:LANES], preferred_element_type=F32) + br_ref[...])
    lg_ref[...] = logits


def _route_kernel(lg_ref, tri_ref, ri_ref, rf_ref, cnt_ref, base_ref, *, tr):
    @pl.when(pl.program_id(0) == 0)
    def _():
        base_ref[...] = jnp.zeros_like(base_ref)

    logits = lg_ref[...].T[0:ROUTER_ROWS, :]
    grow = lax.broadcasted_iota(I32, (8, tr), 0)
    gl = jnp.where(grow < N_GROUPS, logits[0:8, :], NEG_BIG)
    gmax = jnp.max(gl, axis=0, keepdims=True)
    grp = jnp.min(jnp.where(gl == gmax, grow, 8), axis=0, keepdims=True)
    p_grp = 1.0 / jnp.sum(jnp.exp(gl - gmax), axis=0, keepdims=True)
    el = logits[8:ROUTER_ROWS, :]
    erow = lax.broadcasted_iota(I32, (N_EXPERTS, tr), 0)
    in_grp = (erow // EXPERTS_PER_GROUP) == grp
    l_a = jnp.where(in_grp, el, NEG_BIG)
    l1 = jnp.max(l_a, axis=0, keepdims=True)
    i1 = jnp.min(jnp.where(l_a == l1, erow, N_EXPERTS), axis=0, keepdims=True)
    l_b = jnp.where(erow == i1, NEG_BIG, l_a)
    l2 = jnp.max(l_b, axis=0, keepdims=True)
    i2 = jnp.min(jnp.where(l_b == l2, erow, N_EXPERTS), axis=0, keepdims=True)
    r21 = jnp.exp(l2 - l1)
    t1 = 1.0 / (1.0 + r21)
    gate1 = p_grp * t1
    gate2 = p_grp * (r21 * t1)
    oh1 = (erow == i1).astype(F32)
    oh2 = (erow == i2).astype(F32)
    oh = oh1 + oh2
    base = base_ref[...][:, 0:1]
    r1, r2 = [], []
    for c in range(tr // RANK_CHUNK):
        cols = slice(c * RANK_CHUNK, (c + 1) * RANK_CHUNK)
        ohc = oh[:, cols]
        before = jnp.dot(ohc.astype(BF16), tri_ref[...], preferred_element_type=F32) + base
        r1.append(jnp.sum(oh1[:, cols] * before, axis=0, keepdims=True))
        r2.append(jnp.sum(oh2[:, cols] * before, axis=0, keepdims=True))
        base = base + jnp.sum(ohc, axis=1, keepdims=True)
    rank1 = jnp.concatenate(r1, axis=1)
    rank2 = jnp.concatenate(r2, axis=1)
    new_base = jnp.broadcast_to(base, base_ref.shape)
    base_ref[...] = new_base
    cnt_ref[...] = new_base
    zi = jnp.zeros((4, tr), I32)
    ri_ref[...] = jnp.concatenate([i1, i2, rank1.astype(I32), rank2.astype(I32), zi], axis=0)
    rf_ref[...] = jnp.concatenate([gate1, gate2, jnp.zeros((6, tr), F32)], axis=0)


def _moe_kernel(be_ref, nused_ref, x_ref, wg_ref, wu_ref, wd_ref, y_ref, wg_s, wu_s, wd_s):
    i = pl.program_id(0)

    changed = (i == 0) | (be_ref[i] != be_ref[jnp.maximum(i - 1, 0)])

    def narrow(_, carry):
        wg_s[...] = wg_ref[...].astype(BF16)
        wu_s[...] = wu_ref[...].astype(BF16)
        wd_s[...] = wd_ref[...].astype(BF16)
        return carry

    lax.fori_loop(0, changed.astype(I32), narrow, 0)

    @pl.when(i < nused_ref[0])
    def _():
        xb = jnp.concatenate([c.astype(BF16) for c in _load_packed_rows(x_ref, MOE_ROWS)], axis=1)
        g = jnp.dot(xb, wg_s[...], preferred_element_type=F32)
        u = jnp.dot(xb, wu_s[...], preferred_element_type=F32)
        hid = (g * jax.nn.sigmoid(g) * u).astype(BF16)
        _store_packed_rows(y_ref, jnp.dot(hid, wd_s[...], preferred_element_type=F32), MOE_ROWS)

    @pl.when(i >= nused_ref[0])
    def _():
        y_ref[...] = jnp.zeros_like(y_ref)


def _combine_kernel(x1_ref, gate_ref, y0_ref, y1_ref, *rest, tm):
    o_ref = rest[-1]
    g = gate_ref[...]
    g0, g1 = g[:, 0:1], g[:, 1:2]
    y0 = _load_packed_rows(y0_ref, tm)
    y1 = _load_packed_rows(y1_ref, tm)
    for c in range(2 * TOK_SUBLANES):
        cols = slice(c * LANES, (c + 1) * LANES)
        o_ref[:, cols] = x1_ref[:, cols] + g0 * y0[c] + g1 * y1[c]


def _sc_windows(n_tokens):
    info = plsc.get_sparse_core_info()
    n_workers = info.num_cores * info.num_subcores
    assert n_tokens % (n_workers * SC_WINDOW * 2) == 0
    return info.num_cores, n_tokens // (n_workers * SC_WINDOW)


def _sc_dispatch(h3, d0, d1, n_rows):
    t = h3.shape[0]
    n_cores, nwin = _sc_windows(t)
    mesh = plsc.VectorSubcoreMesh(core_axis_name="c", subcore_axis_name="s")
    rows = pltpu.VMEM((SC_WINDOW,) + h3.shape[1:], h3.dtype)
    idx = pltpu.VMEM((nwin, SC_WINDOW), I32)

    @functools.partial(
        pl.kernel, mesh=mesh,
        out_type=jax.ShapeDtypeStruct((n_rows,) + h3.shape[1:], h3.dtype),
        scratch_types=[idx, idx, idx, rows, rows] + [pltpu.SemaphoreType.DMA] * 3,
        name="sc_dispatch")
    def run(h_hbm, d0_hbm, d1_hbm, it_hbm, o_hbm, i0, i1, it, buf_a, buf_b, sem_a, sem_b, sem_s):
        wbase = (lax.axis_index("s") * n_cores + lax.axis_index("c")) * nwin
        pltpu.sync_copy(d0_hbm.at[pl.ds(wbase, nwin)], i0)
        pltpu.sync_copy(d1_hbm.at[pl.ds(wbase, nwin)], i1)
        pltpu.sync_copy(it_hbm.at[pl.ds(wbase, nwin)], it)

        @pl.loop(0, nwin, step=2)
        def _(w):
            ga = pltpu.async_copy(h_hbm.at[it.at[w]], buf_a, sem_a)
            gb = pltpu.async_copy(h_hbm.at[it.at[w + 1]], buf_b, sem_b)
            ga.wait()
            s0 = pltpu.async_copy(buf_a, o_hbm.at[i0.at[w]], sem_s)
            s1 = pltpu.async_copy(buf_a, o_hbm.at[i1.at[w]], sem_s)
            gb.wait()
            s2 = pltpu.async_copy(buf_b, o_hbm.at[i0.at[w + 1]], sem_s)
            s3 = pltpu.async_copy(buf_b, o_hbm.at[i1.at[w + 1]], sem_s)
            s0.wait()
            s1.wait()
            s2.wait()
            s3.wait()

    win = lambda a: a.reshape(-1, SC_WINDOW)
    return run(h3, win(d0), win(d1), win(jnp.arange(t, dtype=I32)))


def _sc_gather(y3, d0, d1):
    t = d0.shape[0]
    n_cores, nwin = _sc_windows(t)
    mesh = plsc.VectorSubcoreMesh(core_axis_name="c", subcore_axis_name="s")
    rows = pltpu.VMEM((SC_WINDOW,) + y3.shape[1:], y3.dtype)
    idx = pltpu.VMEM((nwin, SC_WINDOW), I32)

    @functools.partial(
        pl.kernel, mesh=mesh,
        out_type=jax.ShapeDtypeStruct((2 * t,) + y3.shape[1:], y3.dtype),
        scratch_types=[idx] * 4 + [rows] * 4 + [pltpu.SemaphoreType.DMA] * 5,
        name="sc_gather")
    def run(y_hbm, d0_hbm, d1_hbm, ita_hbm, itb_hbm, o_hbm, i0, i1, ita, itb,
            b0, b1, b2, b3, m0, m1, m2, m3, ms):
        wbase = (lax.axis_index("s") * n_cores + lax.axis_index("c")) * nwin
        pltpu.sync_copy(d0_hbm.at[pl.ds(wbase, nwin)], i0)
        pltpu.sync_copy(d1_hbm.at[pl.ds(wbase, nwin)], i1)
        pltpu.sync_copy(ita_hbm.at[pl.ds(wbase, nwin)], ita)
        pltpu.sync_copy(itb_hbm.at[pl.ds(wbase, nwin)], itb)

        @pl.loop(0, nwin, step=2)
        def _(w):
            g0 = pltpu.async_copy(y_hbm.at[i0.at[w]], b0, m0)
            g1 = pltpu.async_copy(y_hbm.at[i1.at[w]], b1, m1)
            g2 = pltpu.async_copy(y_hbm.at[i0.at[w + 1]], b2, m2)
            g3 = pltpu.async_copy(y_hbm.at[i1.at[w + 1]], b3, m3)
            g0.wait()
            s0 = pltpu.async_copy(b0, o_hbm.at[ita.at[w]], ms)
            g1.wait()
            s1 = pltpu.async_copy(b1, o_hbm.at[itb.at[w]], ms)
            g2.wait()
            s2 = pltpu.async_copy(b2, o_hbm.at[ita.at[w + 1]], ms)
            g3.wait()
            s3 = pltpu.async_copy(b3, o_hbm.at[itb.at[w + 1]], ms)
            s0.wait()
            s1.wait()
            s2.wait()
            s3.wait()

    win = lambda a: a.reshape(-1, SC_WINDOW)
    iota = jnp.arange(2 * t, dtype=I32)
    return run(y3, win(d0), win(d1), win(iota[:t]), win(iota[t:]))


def _rope_lane_tables(seq_len):
    half = ROT_DIM // 2
    inv_freq = ROPE_THETA ** (-jnp.arange(0, ROT_DIM, 2, dtype=F32) / ROT_DIM)
    ang = jnp.arange(seq_len, dtype=F32)[:, None] * inv_freq[None, :]
    cos, sin = jnp.cos(ang), jnp.sin(ang)
    m = np.arange(LANES) % HEAD_DIM
    f = m % half
    c = jnp.where(m[None, :] < ROT_DIM, cos[:, f], 1.0)
    s1 = jnp.where(m[None, :] < half, -sin[:, f], 0.0)
    s2 = jnp.where((m[None, :] >= half) & (m[None, :] < ROT_DIM), sin[:, f], 0.0)
    return c.astype(F32), s1.astype(F32), s2.astype(F32)


def _layer(x, norm1_g, w_in, q_norm_g, k_norm_g, attn_sink, conv_w, conv_b, w_attn_proj,
           w_conv_proj, w_out, norm2_g, w_rg, b_rg, w_re, b_re, w_gate, w_up, w_down):
    bsz, seq, d = x.shape
    t = bsz * seq
    tm = min(512, t)
    tq = min(256, seq)
    assert d == D_MODEL and seq % tq == 0 and t % tm == 0 and seq % tm == 0 and tq % WINDOW == 0
    xf = x.reshape(t, d)

    hpg = N_HEADS // N_KV_HEADS
    wq_p = (w_in[:, :ATTN_DIM].reshape(d, N_KV_HEADS, hpg, HEAD_DIM).transpose(0, 2, 1, 3)
            .reshape(d, ATTN_DIM))
    w_in_p = jnp.concatenate([wq_p, w_in[:, ATTN_DIM:]], axis=1).astype(BF16)
    wa_p = (w_attn_proj.reshape(N_KV_HEADS, hpg, HEAD_DIM, d).transpose(1, 0, 2, 3)
            .reshape(ATTN_DIM, d).astype(BF16))
    bd = jnp.asarray(np.kron(np.eye(N_HEADS), np.full((HEAD_DIM, HEAD_DIM), 1.0 / HEAD_DIM)), BF16)
    qg = jnp.tile(q_norm_g, LANES // HEAD_DIM).reshape(1, LANES)
    kg = jnp.tile(k_norm_g, LANES // HEAD_DIM).reshape(1, LANES)
    ctab, s1tab, s2tab = _rope_lane_tables(seq)
    nseq_t = seq // tm

    row = lambda w: pl.BlockSpec((tm, w), lambda i: (i, 0))
    full = lambda a: pl.BlockSpec(a.shape, lambda i: (0,) * a.ndim)
    tab = pl.BlockSpec((tm, LANES), lambda i: (i % nseq_t, 0))
    q, k, v, cb, u, ga, gc = pl.pallas_call(
        _proj_kernel,
        grid=(t // tm,),
        in_specs=[row(d), full(norm1_g.reshape(1, d)), full(w_in_p), full(bd), full(qg), full(kg),
                  tab, tab, tab],
        out_specs=[row(ATTN_DIM), row(KV_DIM), row(KV_DIM), row(CONV_DIM), row(CONV_DIM),
                   row(D_MODEL), row(D_MODEL)],
        out_shape=[jax.ShapeDtypeStruct((t, w), BF16)
                   for w in (ATTN_DIM, KV_DIM, KV_DIM, CONV_DIM, CONV_DIM, D_MODEL, D_MODEL)],
        compiler_params=pltpu.CompilerParams(dimension_semantics=("arbitrary",),
                                             vmem_limit_bytes=VMEM_LIMIT),
        name="proj",
    )(xf, norm1_g.reshape(1, d), w_in_p, bd, qg, kg, ctab, s1tab, s2tab)

    wr = jnp.concatenate([w_rg, jnp.zeros((d, 8 - N_GROUPS), F32), w_re,
                          jnp.zeros((d, LANES - ROUTER_ROWS), F32)], axis=1)
    wr_hi = wr.astype(BF16)
    wr2 = jnp.concatenate([wr_hi, (wr - wr_hi.astype(F32)).astype(BF16)], axis=1)
    br = jnp.concatenate([b_rg, jnp.zeros((8 - N_GROUPS,), F32), b_re,
                          jnp.zeros((LANES - ROUTER_ROWS,), F32)]).reshape(1, LANES)
    mix_params = (conv_w, conv_b.reshape(1, CONV_DIM), wa_p, w_conv_proj.astype(BF16), w_out.astype(BF16),
                  norm2_g.reshape(1, d), wr2, br)

    n_chunks = TOKEN_CHUNKS if bsz % TOKEN_CHUNKS == 0 else 1
    out = None
    for c in range(n_chunks):
        out = _chunk_tail(c, n_chunks, out, xf, (q, k, v, cb, u, ga, gc), attn_sink, mix_params,
                          (w_gate, w_up, w_down), bsz, seq, tq)
    return out.reshape(bsz, seq, d)


def _chunk_tail(c, n_chunks, out_prev, xf, acts, attn_sink, mix_params, expert_w, bsz, seq, tq):
    q, k, v, cb, u, ga, gc = acts
    d = D_MODEL
    t_all = bsz * seq
    bc = bsz // n_chunks
    b0 = c * bc
    t = bc * seq
    nj = seq // tq
    wpb = tq // WINDOW
    nb_seq = seq // WINDOW
    hpb = tq // 16
    row_in = lambda w: pl.BlockSpec((tq, w), lambda b, j: ((b0 + b) * nj + j, 0))
    row_out = lambda w: pl.BlockSpec((tq, w), lambda b, j: (b * nj + j, 0))
    prev_blk = pl.BlockSpec((WINDOW, KV_DIM),
                            lambda b, j: ((b0 + b) * nb_seq + jnp.maximum(j * wpb - 1, 0), 0))
    next_blk = pl.BlockSpec((WINDOW, KV_DIM),
                            lambda b, j: ((b0 + b) * nb_seq + jnp.minimum((j + 1) * wpb, nb_seq - 1), 0))
    uprev = pl.BlockSpec((16, CONV_DIM),
                         lambda b, j: (jnp.maximum(((b0 + b) * nj + j) * hpb - 1, 0), 0))
    unext = pl.BlockSpec((16, CONV_DIM),
                         lambda b, j: (jnp.minimum(((b0 + b) * nj + j + 1) * hpb, t_all // 16 - 1), 0))
    full2 = lambda a: pl.BlockSpec(a.shape, lambda b, j: (0,) * a.ndim)

    x1, h2, logits = pl.pallas_call(
        functools.partial(_mix_kernel, tq=tq),
        grid=(bc, nj),
        in_specs=[pl.BlockSpec(memory_space=pltpu.SMEM),
                  row_in(d), row_in(ATTN_DIM), prev_blk, row_in(KV_DIM), next_blk,
                  prev_blk, row_in(KV_DIM), next_blk,
                  row_in(CONV_DIM), uprev, row_in(CONV_DIM), unext, row_in(D_MODEL), row_in(D_MODEL)]
                 + [full2(p) for p in mix_params],
        out_specs=[row_out(d), pl.BlockSpec((tq * TOK_SUBLANES, LANES), lambda b, j: (b * nj + j, 0)),
                   row_out(LANES)],
        out_shape=[jax.ShapeDtypeStruct((t, d), F32),
                   jax.ShapeDtypeStruct((t * TOK_SUBLANES, LANES), U32),
                   jax.ShapeDtypeStruct((t, LANES), F32)],
        scratch_shapes=[pltpu.VMEM((tq + 16, CONV_DIM), F32), pltpu.VMEM((tq, ATTN_DIM), BF16)],
        compiler_params=pltpu.CompilerParams(dimension_semantics=("arbitrary", "arbitrary"),
                                             vmem_limit_bytes=VMEM_LIMIT),
        name="mix",
    )(attn_sink, xf, q, k, k, k, v, v, v, cb, u, u, u, ga, gc, *mix_params)

    tr = min(ROUTE_TILE, t)
    assert t % tr == 0 and tr % RANK_CHUNK == 0
    tri = jnp.asarray(np.triu(np.ones((RANK_CHUNK, RANK_CHUNK), np.float32), 1), BF16)
    colr = lambda r: pl.BlockSpec((r, tr), lambda i: (0, i))
    route_i, route_f, counts = pl.pallas_call(
        functools.partial(_route_kernel, tr=tr),
        grid=(t // tr,),
        in_specs=[pl.BlockSpec((tr, LANES), lambda i: (i, 0)), pl.BlockSpec(tri.shape, lambda i: (0, 0))],
        out_specs=[colr(8), colr(8), pl.BlockSpec((N_EXPERTS, LANES), lambda i: (0, 0))],
        out_shape=[jax.ShapeDtypeStruct((8, t), I32), jax.ShapeDtypeStruct((8, t), F32),
                   jax.ShapeDtypeStruct((N_EXPERTS, LANES), F32)],
        scratch_shapes=[pltpu.VMEM((N_EXPERTS, LANES), F32)],
        compiler_params=pltpu.CompilerParams(dimension_semantics=("arbitrary",),
                                             vmem_limit_bytes=VMEM_LIMIT),
        name="route",
    )(logits, tri)

    n_rows = 2 * t + N_EXPERTS * MOE_ROWS
    nblk = n_rows // MOE_ROWS
    cnt = counts[:, 0].astype(I32)
    padded = ((cnt + MOE_ROWS - 1) // MOE_ROWS) * MOE_ROWS
    pad_ends = jnp.cumsum(padded)
    pad_starts = pad_ends - padded
    eids = jnp.arange(N_EXPERTS, dtype=I32)[:, None, None]
    dest = jnp.sum(jnp.where(route_i[None, 0:2] == eids, pad_starts[:, None, None], 0), axis=0) \
        + route_i[2:4]
    blk_start = jnp.arange(nblk, dtype=I32) * MOE_ROWS
    block_e = jnp.minimum(jnp.sum((pad_ends[None, :] <= blk_start[:, None]).astype(I32), axis=1),
                          N_EXPERTS - 1)
    nused = (pad_ends[-1:] // MOE_ROWS).astype(I32)

    x_pad = _sc_dispatch(h2.reshape(t, TOK_SUBLANES, LANES), dest[0], dest[1], n_rows)
    x_pad = x_pad.reshape(n_rows * TOK_SUBLANES, LANES)

    w_gate, w_up, w_down = expert_w
    blk_rows = MOE_ROWS * TOK_SUBLANES
    wspec = lambda a, b: pl.BlockSpec((None, a, b), lambda i, be, nu: (be[i], 0, 0))
    y_pad = pl.pallas_call(
        _moe_kernel,
        grid_spec=pltpu.PrefetchScalarGridSpec(
            num_scalar_prefetch=2,
            grid=(nblk,),
            in_specs=[pl.BlockSpec((blk_rows, LANES), lambda i, be, nu: (i, 0)),
                      wspec(d, D_FF), wspec(d, D_FF), wspec(D_FF, d)],
            out_specs=pl.BlockSpec((blk_rows, LANES), lambda i, be, nu: (i, 0)),
            scratch_shapes=[pltpu.VMEM((d, D_FF), BF16), pltpu.VMEM((d, D_FF), BF16),
                            pltpu.VMEM((D_FF, d), BF16)]),
        out_shape=jax.ShapeDtypeStruct((n_rows * TOK_SUBLANES, LANES), U32),
        compiler_params=pltpu.CompilerParams(dimension_semantics=("arbitrary",),
                                             vmem_limit_bytes=VMEM_LIMIT),
        name="moe",
    )(block_e, nused, x_pad, w_gate, w_up, w_down)

    yk = _sc_gather(y_pad.reshape(n_rows, TOK_SUBLANES, LANES), dest[0], dest[1])
    yk = yk.reshape(2 * t * TOK_SUBLANES, LANES)
    tc = min(512, t)
    nt = t // tc
    gates_t = route_f[0:2].T
    in_specs = [pl.BlockSpec((tc, d), lambda i: (i, 0)),
                pl.BlockSpec((tc, 2), lambda i: (i, 0)),
                pl.BlockSpec((tc * TOK_SUBLANES, LANES), lambda i: (i, 0)),
                pl.BlockSpec((tc * TOK_SUBLANES, LANES), lambda i: (nt + i, 0))]
    args = [x1, gates_t, yk, yk]
    aliases = {}
    if out_prev is not None:
        in_specs.append(pl.BlockSpec(memory_space=pl.ANY))
        args.append(out_prev)
        aliases = {len(args) - 1: 0}
    return pl.pallas_call(
        functools.partial(_combine_kernel, tm=tc),
        grid=(nt,),
        in_specs=in_specs,
        out_specs=pl.BlockSpec((tc, d), lambda i: (c * nt + i, 0)),
        out_shape=jax.ShapeDtypeStruct((t_all, d), F32),
        input_output_aliases=aliases,
        compiler_params=pltpu.CompilerParams(dimension_semantics=("arbitrary",),
                                             vmem_limit_bytes=VMEM_LIMIT),
        name="combine",
    )(*args)


def kernel(x, norm1_g, w_in, q_norm_g, k_norm_g, attn_sink, conv_w, conv_b, w_attn_proj, w_conv_proj, w_out, norm2_g, w_router_group, b_router_group, w_router_expert, b_router_expert, w_gate_e, w_up_e, w_down_e):
    for l in range(norm1_g.shape[0]):
        x = _layer(x, norm1_g[l], w_in[l], q_norm_g[l], k_norm_g[l], attn_sink[l], conv_w[l],
                   conv_b[l], w_attn_proj[l], w_conv_proj[l], w_out[l], norm2_g[l],
                   w_router_group[l], b_router_group[l], w_router_expert[l], b_router_expert[l],
                   w_gate_e[l], w_up_e[l], w_down_e[l])
    return x
```

```python
import functools

import jax
import jax.numpy as jnp
import numpy as np
from jax import lax
from jax.experimental import pallas as pl
from jax.experimental.pallas import tpu as pltpu
from jax.experimental.pallas import tpu_sc as plsc

F32 = jnp.float32
BF16 = jnp.bfloat16
I32 = jnp.int32
U32 = jnp.uint32
HI_MASK = np.uint32(0xFFFF0000)

D_MODEL = 1024
N_HEADS = 8
N_KV_HEADS = 2
HEAD_DIM = 64
ATTN_DIM = N_HEADS * HEAD_DIM
KV_DIM = N_KV_HEADS * HEAD_DIM
WINDOW = 128
ROT_DIM = HEAD_DIM // 4
ROPE_THETA = 500000.0
CONV_DIM = D_MODEL // 2
N_GROUPS = 4
EXPERTS_PER_GROUP = 8
N_EXPERTS = N_GROUPS * EXPERTS_PER_GROUP
D_FF = D_MODEL // 4
NORM_EPS = 1e-6
MASK_VALUE = -1e30
NEG_BIG = -3.0e38
LOG2E = 1.4426950408889634

LANES = 128
TOK_SUBLANES = D_MODEL // (2 * LANES)
HALF_D = D_MODEL // 2
SC_WINDOW = 32
ROUTER_ROWS = 8 + N_EXPERTS
ROUTE_TILE = 2048
RANK_CHUNK = 256
MOE_ROWS = 512
TOKEN_CHUNKS = 1
VMEM_LIMIT = 56 * 1024 * 1024

_Q0, _K0, _V0, _CB0, _CC0, _CX0, _GA0, _GC0, _END = (
    0, 512, 640, 768, 1280, 1792, 2304, 3328, 4352)


def _store_packed_rows(ref, val, n):
    for c in range(TOK_SUBLANES):
        lo = val[:, c * LANES:(c + 1) * LANES].astype(BF16).astype(F32)
        hi = val[:, HALF_D + c * LANES:HALF_D + (c + 1) * LANES].astype(BF16).astype(F32)
        word = (lax.bitcast_convert_type(lo, U32) >> 16) | (lax.bitcast_convert_type(hi, U32) & HI_MASK)
        ref[pl.ds(c, n, stride=TOK_SUBLANES), :] = word


def _load_packed_rows(ref, n):
    words = [ref[pl.ds(c, n, stride=TOK_SUBLANES), :] for c in range(TOK_SUBLANES)]
    lo = [lax.bitcast_convert_type(w << 16, F32) for w in words]
    hi = [lax.bitcast_convert_type(w & HI_MASK, F32) for w in words]
    return lo + hi


def _rope_chunks(t, c, s1, s2):
    outs = []
    for j in range(t.shape[1] // LANES):
        tj = t[:, j * LANES:(j + 1) * LANES]
        outs.append(tj * c + pltpu.roll(tj, LANES - ROT_DIM // 2, 1) * s1
                    + pltpu.roll(tj, ROT_DIM // 2, 1) * s2)
    return outs[0] if len(outs) == 1 else jnp.concatenate(outs, axis=1)


def _proj_kernel(x_ref, g1_ref, w_ref, bd_ref, qg_ref, kg_ref, c_ref, s1_ref, s2_ref,
                 q_ref, k_ref, v_ref, cb_ref, u_ref, ga_ref, gc_ref):
    x = x_ref[...]
    ms = jnp.mean(x * x, axis=-1, keepdims=True)
    h = (x * lax.rsqrt(ms + NORM_EPS) * g1_ref[...]).astype(BF16)

    def proj(a, b):
        return jnp.dot(h, w_ref[:, a:b], preferred_element_type=F32)

    c, s1, s2 = c_ref[...], s1_ref[...], s2_ref[...]

    q = proj(_Q0, _K0)
    qms = jnp.dot((q * q).astype(BF16), bd_ref[...], preferred_element_type=F32)
    qg = jnp.concatenate([qg_ref[...]] * (ATTN_DIM // LANES), axis=1)
    qn = q * lax.rsqrt(qms + NORM_EPS) * qg
    q_ref[...] = (_rope_chunks(qn, c, s1, s2) * (HEAD_DIM ** -0.5 * LOG2E)).astype(BF16)

    k = proj(_K0, _V0)
    kms = jnp.dot((k * k).astype(BF16), bd_ref[0:KV_DIM, 0:KV_DIM], preferred_element_type=F32)
    kn = k * lax.rsqrt(kms + NORM_EPS) * kg_ref[...]
    k_ref[...] = _rope_chunks(kn, c, s1, s2).astype(BF16)

    v_ref[...] = proj(_V0, _CB0).astype(BF16)
    cb_ref[...] = proj(_CB0, _CC0).astype(BF16)
    u_ref[...] = (proj(_CC0, _CX0) * proj(_CX0, _GA0)).astype(BF16)
    ga_ref[...] = jax.nn.sigmoid(proj(_GA0, _GC0)).astype(BF16)
    gc_ref[...] = jax.nn.sigmoid(proj(_GC0, _END)).astype(BF16)


def _mix_kernel(sink_ref, x_ref, q_ref, kp_ref, kc_ref, kn_ref, vp_ref, vc_ref, vn_ref,
                cb_ref, up_ref, uc_ref, un_ref, ga_ref, gc_ref,
                cw_ref, cbias_ref, wa_ref, wc_ref, wo_ref, g2_ref,
                wr_ref, br_ref,
                x1_ref, h2_ref, lg_ref,
                ubuf, attn_scr, *, tq):
    j = pl.program_id(1)
    nj = pl.num_programs(1)
    nsub = tq // WINDOW

    kbuf = jnp.concatenate([kp_ref[...], kc_ref[...], kn_ref[...]], axis=0)
    vbuf = jnp.concatenate([vp_ref[...], vc_ref[...], vn_ref[...]], axis=0)
    low_half = lax.broadcasted_iota(I32, (WINDOW, LANES), 1) < HEAD_DIM
    low_half3 = lax.broadcasted_iota(I32, (3 * WINDOW, LANES), 1) < HEAD_DIM
    qrow = lax.broadcasted_iota(I32, (WINDOW, WINDOW), 0)
    kcol = lax.broadcasted_iota(I32, (WINDOW, WINDOW), 1)
    bias_lo = jnp.where(kcol >= qrow, 0.0, MASK_VALUE)
    bias_hi = jnp.where(kcol <= qrow, 0.0, MASK_VALUE)
    one = jnp.ones((3 * WINDOW, LANES), BF16)
    for n in range(nsub):
        blo = jnp.where(j > 0, bias_lo, MASK_VALUE) if n == 0 else bias_lo
        bhi = jnp.where(j < nj - 1, bias_hi, MASK_VALUE) if n == nsub - 1 else bias_hi
        qs = q_ref[n * WINDOW:(n + 1) * WINDOW, :]
        zero = jnp.zeros((WINDOW, LANES), BF16)
        q_stack = []
        for hd in range(N_HEADS):
            col = qs[:, (hd % 4) * LANES:(hd % 4 + 1) * LANES]
            keep = low_half if hd < 4 else jnp.logical_not(low_half)
            q_stack.append(jnp.where(keep, col, zero))
        q_stack = jnp.concatenate(q_stack, axis=0)
        kwin = kbuf[n * WINDOW:(n + 3) * WINDOW, :]
        vwin = vbuf[n * WINDOW:(n + 3) * WINDOW, :]
        s = lax.dot_general(q_stack, kwin, (((1,), (1,)), ((), ())),
                            preferred_element_type=F32)
        p_list, es_list = [], []
        for hd in range(N_HEADS):
            sink = sink_ref[hd] * LOG2E
            rows = slice(hd * WINDOW, (hd + 1) * WINDOW)
            s0 = s[rows, 0:WINDOW] + blo
            s1 = s[rows, WINDOW:2 * WINDOW]
            s2 = s[rows, 2 * WINDOW:3 * WINDOW] + bhi
            m = jnp.max(jnp.maximum(jnp.maximum(s0, s1), s2), axis=-1, keepdims=True)
            m = jnp.maximum(m, sink)
            p_list.append(jnp.concatenate(
                [jnp.exp2(s0 - m).astype(BF16), jnp.exp2(s1 - m).astype(BF16),
                 jnp.exp2(s2 - m).astype(BF16)], axis=1))
            es_list.append(jnp.exp2(sink - m))
        half = N_HEADS // 2
        o_lo = jnp.dot(jnp.concatenate(p_list[:half], axis=0), jnp.where(low_half3, vwin, one),
                       preferred_element_type=F32)
        o_hi = jnp.dot(jnp.concatenate(p_list[half:], axis=0), jnp.where(low_half3, one, vwin),
                       preferred_element_type=F32)
        cols = []
        for c4 in range(half):
            rows = slice(c4 * WINDOW, (c4 + 1) * WINDOW)
            oa, ob = o_lo[rows, :], o_hi[rows, :]
            da = pltpu.roll(oa, HEAD_DIM, 1) + es_list[c4]
            db = pltpu.roll(ob, HEAD_DIM, 1) + es_list[c4 + half]
            cols.append(jnp.where(low_half, oa / da, ob / db))
        attn_scr[n * WINDOW:(n + 1) * WINDOW, :] = jnp.concatenate(cols, axis=1).astype(BF16)
    attn = attn_scr[...]

    ubuf[8:8 + tq, :] = uc_ref[...].astype(F32)
    prev_rows = up_ref[...].astype(F32)
    next_rows = un_ref[...].astype(F32)
    ubuf[7:8, :] = prev_rows[15:16, :] * (j > 0).astype(F32)
    ubuf[8 + tq:9 + tq, :] = next_rows[0:1, :] * (j < nj - 1).astype(F32)
    cw = cw_ref[...]
    conv = (ubuf[7:7 + tq, :] * cw[0:1, :] + ubuf[8:8 + tq, :] * cw[1:2, :]
            + ubuf[9:9 + tq, :] * cw[2:3, :] + cbias_ref[...])
    cgate = (cb_ref[...].astype(F32) * conv).astype(BF16)

    merged = (ga_ref[...].astype(F32) * jnp.dot(attn, wa_ref[...], preferred_element_type=F32)
              + gc_ref[...].astype(F32) * jnp.dot(cgate, wc_ref[...], preferred_element_type=F32))
    x1 = x_ref[...] + jnp.dot(merged.astype(BF16), wo_ref[...], preferred_element_type=F32)
    x1_ref[...] = x1

    ms = jnp.mean(x1 * x1, axis=-1, keepdims=True)
    h2 = x1 * lax.rsqrt(ms + NORM_EPS) * g2_ref[...]
    _store_packed_rows(h2_ref, h2, tq)
    h_hi = h2.astype(BF16)
    h_lo = (h2 - h_hi.astype(F32)).astype(BF16)
    acc = jnp.dot(h_hi, wr_ref[...], preferred_element_type=F32)
    logits = (acc[:, 0:LANES] + acc[:, LANES:2 * LANES]
              + jnp.dot(h_lo, wr_ref[:, 0:LANES], preferred_element_type=F32) + br_ref[...])
    lg_ref[...] = logits


def _route_kernel(lg_ref, tri_ref, ri_ref, rf_ref, cnt_ref, base_ref, *, tr):
    @pl.when(pl.program_id(0) == 0)
    def _():
        base_ref[...] = jnp.zeros_like(base_ref)

    logits = lg_ref[...].T[0:ROUTER_ROWS, :]
    grow = lax.broadcasted_iota(I32, (8, tr), 0)
    gl = jnp.where(grow < N_GROUPS, logits[0:8, :], NEG_BIG)
    gmax = jnp.max(gl, axis=0, keepdims=True)
    grp = jnp.min(jnp.where(gl == gmax, grow, 8), axis=0, keepdims=True)
    p_grp = 1.0 / jnp.sum(jnp.exp(gl - gmax), axis=0, keepdims=True)
    el = logits[8:ROUTER_ROWS, :]
    erow = lax.broadcasted_iota(I32, (N_EXPERTS, tr), 0)
    in_grp = (erow // EXPERTS_PER_GROUP) == grp
    l_a = jnp.where(in_grp, el, NEG_BIG)
    l1 = jnp.max(l_a, axis=0, keepdims=True)
    i1 = jnp.min(jnp.where(l_a == l1, erow, N_EXPERTS), axis=0, keepdims=True)
    l_b = jnp.where(erow == i1, NEG_BIG, l_a)
    l2 = jnp.max(l_b, axis=0, keepdims=True)
    i2 = jnp.min(jnp.where(l_b == l2, erow, N_EXPERTS), axis=0, keepdims=True)
    r21 = jnp.exp(l2 - l1)
    t1 = 1.0 / (1.0 + r21)
    gate1 = p_grp * t1
    gate2 = p_grp * (r21 * t1)
    oh1 = (erow == i1).astype(F32)
    oh2 = (erow == i2).astype(F32)
    oh = oh1 + oh2
    base = base_ref[...][:, 0:1]
    r1, r2 = [], []
    for c in range(tr // RANK_CHUNK):
        cols = slice(c * RANK_CHUNK, (c + 1) * RANK_CHUNK)
        ohc = oh[:, cols]
        before = jnp.dot(ohc.astype(BF16), tri_ref[...], preferred_element_type=F32) + base
        r1.append(jnp.sum(oh1[:, cols] * before, axis=0, keepdims=True))
        r2.append(jnp.sum(oh2[:, cols] * before, axis=0, keepdims=True))
        base = base + jnp.sum(ohc, axis=1, keepdims=True)
    rank1 = jnp.concatenate(r1, axis=1)
    rank2 = jnp.concatenate(r2, axis=1)
    new_base = jnp.broadcast_to(base, base_ref.shape)
    base_ref[...] = new_base
    cnt_ref[...] = new_base
    zi = jnp.zeros((4, tr), I32)
    ri_ref[...] = jnp.concatenate([i1, i2, rank1.astype(I32), rank2.astype(I32), zi], axis=0)
    rf_ref[...] = jnp.concatenate([gate1, gate2, jnp.zeros((6, tr), F32)], axis=0)


def _moe_kernel(be_ref, nused_ref, x_ref, wg_ref, wu_ref, wd_ref, y_ref, wg_s, wu_s, wd_s):
    i = pl.program_id(0)

    changed = (i == 0) | (be_ref[i] != be_ref[jnp.maximum(i - 1, 0)])

    def narrow(_, carry):
        wg_s[...] = wg_ref[...].astype(BF16)
        wu_s[...] = wu_ref[...].astype(BF16)
        wd_s[...] = wd_ref[...].astype(BF16)
        return carry

    lax.fori_loop(0, changed.astype(I32), narrow, 0)

    @pl.when(i < nused_ref[0])
    def _():
        xb = jnp.concatenate([c.astype(BF16) for c in _load_packed_rows(x_ref, MOE_ROWS)], axis=1)
        g = jnp.dot(xb, wg_s[...], preferred_element_type=F32)
        u = jnp.dot(xb, wu_s[...], preferred_element_type=F32)
        hid = (g * jax.nn.sigmoid(g) * u).astype(BF16)
        _store_packed_rows(y_ref, jnp.dot(hid, wd_s[...], preferred_element_type=F32), MOE_ROWS)

    @pl.when(i >= nused_ref[0])
    def _():
        y_ref[...] = jnp.zeros_like(y_ref)


def _combine_kernel(x1_ref, gate_ref, y0_ref, y1_ref, *rest, tm):
    o_ref = rest[-1]
    g = gate_ref[...]
    g0, g1 = g[:, 0:1], g[:, 1:2]
    y0 = _load_packed_rows(y0_ref, tm)
    y1 = _load_packed_rows(y1_ref, tm)
    for c in range(2 * TOK_SUBLANES):
        cols = slice(c * LANES, (c + 1) * LANES)
        o_ref[:, cols] = x1_ref[:, cols] + g0 * y0[c] + g1 * y1[c]


def _sc_windows(n_tokens):
    info = plsc.get_sparse_core_info()
    n_workers = info.num_cores * info.num_subcores
    assert n_tokens % (n_workers * SC_WINDOW * 2) == 0
    return info.num_cores, n_tokens // (n_workers * SC_WINDOW)


def _sc_dispatch(h3, d0, d1, n_rows):
    t = h3.shape[0]
    n_cores, nwin = _sc_windows(t)
    mesh = plsc.VectorSubcoreMesh(core_axis_name="c", subcore_axis_name="s")
    rows = pltpu.VMEM((SC_WINDOW,) + h3.shape[1:], h3.dtype)
    idx = pltpu.VMEM((nwin, SC_WINDOW), I32)

    @functools.partial(
        pl.kernel, mesh=mesh,
        out_type=jax.ShapeDtypeStruct((n_rows,) + h3.shape[1:], h3.dtype),
        scratch_types=[idx, idx, idx, rows, rows] + [pltpu.SemaphoreType.DMA] * 3,
        name="sc_dispatch")
    def run(h_hbm, d0_hbm, d1_hbm, it_hbm, o_hbm, i0, i1, it, buf_a, buf_b, sem_a, sem_b, sem_s):
        wbase = (lax.axis_index("s") * n_cores + lax.axis_index("c")) * nwin
        pltpu.sync_copy(d0_hbm.at[pl.ds(wbase, nwin)], i0)
        pltpu.sync_copy(d1_hbm.at[pl.ds(wbase, nwin)], i1)
        pltpu.sync_copy(it_hbm.at[pl.ds(wbase, nwin)], it)

        @pl.loop(0, nwin, step=2)
        def _(w):
            ga = pltpu.async_copy(h_hbm.at[it.at[w]], buf_a, sem_a)
            gb = pltpu.async_copy(h_hbm.at[it.at[w + 1]], buf_b, sem_b)
            ga.wait()
            s0 = pltpu.async_copy(buf_a, o_hbm.at[i0.at[w]], sem_s)
            s1 = pltpu.async_copy(buf_a, o_hbm.at[i1.at[w]], sem_s)
            gb.wait()
            s2 = pltpu.async_copy(buf_b, o_hbm.at[i0.at[w + 1]], sem_s)
            s3 = pltpu.async_copy(buf_b, o_hbm.at[i1.at[w + 1]], sem_s)
            s0.wait()
            s1.wait()
            s2.wait()
            s3.wait()

    win = lambda a: a.reshape(-1, SC_WINDOW)
    return run(h3, win(d0), win(d1), win(jnp.arange(t, dtype=I32)))


def _sc_gather(y3, d0, d1):
    t = d0.shape[0]
    n_cores, nwin = _sc_windows(t)
    mesh = plsc.VectorSubcoreMesh(core_axis_name="c", subcore_axis_name="s")
    rows = pltpu.VMEM((SC_WINDOW,) + y3.shape[1:], y3.dtype)
    idx = pltpu.VMEM((nwin, SC_WINDOW), I32)

    @functools.partial(
        pl.kernel, mesh=mesh,
        out_type=jax.ShapeDtypeStruct((2 * t,) + y3.shape[1:], y3.dtype),
        scratch_types=[idx] * 4 + [rows] * 4 + [pltpu.SemaphoreType.DMA] * 5,
        name="sc_gather")
    def run(y_hbm, d0_hbm, d1_hbm, ita_hbm, itb_hbm, o_hbm, i0, i1, ita, itb,
            b0, b1, b2, b3, m0, m1, m2, m3, ms):
        wbase = (lax.axis_index("s") * n_cores + lax.axis_index("c")) * nwin
        pltpu.sync_copy(d0_hbm.at[pl.ds(wbase, nwin)], i0)
        pltpu.sync_copy(d1_hbm.at[pl.ds(wbase, nwin)], i1)
        pltpu.sync_copy(ita_hbm.at[pl.ds(wbase, nwin)], ita)
        pltpu.sync_copy(itb_hbm.at[pl.ds(wbase, nwin)], itb)

        @pl.loop(0, nwin, step=2)
        def _(w):
            g0 = pltpu.async_copy(y_hbm.at[i0.at[w]], b0, m0)
            g1 = pltpu.async_copy(y_hbm.at[i1.at[w]], b1, m1)
            g2 = pltpu.async_copy(y_hbm.at[i0.at[w + 1]], b2, m2)
            g3 = pltpu.async_copy(y_hbm.at[i1.at[w + 1]], b3, m3)
            g0.wait()
            s0 = pltpu.async_copy(b0, o_hbm.at[ita.at[w]], ms)
            g1.wait()
            s1 = pltpu.async_copy(b1, o_hbm.at[itb.at[w]], ms)
            g2.wait()
            s2 = pltpu.async_copy(b2, o_hbm.at[ita.at[w + 1]], ms)
            g3.wait()
            s3 = pltpu.async_copy(b3, o_hbm.at[itb.at[w + 1]], ms)
            s0.wait()
            s1.wait()
            s2.wait()
            s3.wait()

    win = lambda a: a.reshape(-1, SC_WINDOW)
    iota = jnp.arange(2 * t, dtype=I32)
    return run(y3, win(d0), win(d1), win(iota[:t]), win(iota[t:]))


def _rope_lane_tables(seq_len):
    half = ROT_DIM // 2
    inv_freq = ROPE_THETA ** (-jnp.arange(0, ROT_DIM, 2, dtype=F32) / ROT_DIM)
    ang = jnp.arange(seq_len, dtype=F32)[:, None] * inv_freq[None, :]
    cos, sin = jnp.cos(ang), jnp.sin(ang)
    m = np.arange(LANES) % HEAD_DIM
    f = m % half
    c = jnp.where(m[None, :] < ROT_DIM, cos[:, f], 1.0)
    s1 = jnp.where(m[None, :] < half, -sin[:, f], 0.0)
    s2 = jnp.where((m[None, :] >= half) & (m[None, :] < ROT_DIM), sin[:, f], 0.0)
    return c.astype(F32), s1.astype(F32), s2.astype(F32)


def _layer(x, norm1_g, w_in, q_norm_g, k_norm_g, attn_sink, conv_w, conv_b, w_attn_proj,
           w_conv_proj, w_out, norm2_g, w_rg, b_rg, w_re, b_re, w_gate, w_up, w_down):
    bsz, seq, d = x.shape
    t = bsz * seq
    tm = min(512, t)
    tq = min(512, seq)
    assert d == D_MODEL and seq % tq == 0 and t % tm == 0 and seq % tm == 0 and tq % WINDOW == 0
    xf = x.reshape(t, d)

    hpg = N_HEADS // N_KV_HEADS
    wq_p = (w_in[:, :ATTN_DIM].reshape(d, N_KV_HEADS, hpg, HEAD_DIM).transpose(0, 2, 1, 3)
            .reshape(d, ATTN_DIM))
    w_in_p = jnp.concatenate([wq_p, w_in[:, ATTN_DIM:]], axis=1).astype(BF16)
    wa_p = (w_attn_proj.reshape(N_KV_HEADS, hpg, HEAD_DIM, d).transpose(1, 0, 2, 3)
            .reshape(ATTN_DIM, d).astype(BF16))
    bd = jnp.asarray(np.kron(np.eye(N_HEADS), np.full((HEAD_DIM, HEAD_DIM), 1.0 / HEAD_DIM)), BF16)
    qg = jnp.tile(q_norm_g, LANES // HEAD_DIM).reshape(1, LANES)
    kg = jnp.tile(k_norm_g, LANES // HEAD_DIM).reshape(1, LANES)
    ctab, s1tab, s2tab = _rope_lane_tables(seq)
    nseq_t = seq // tm

    row = lambda w: pl.BlockSpec((tm, w), lambda i: (i, 0))
    full = lambda a: pl.BlockSpec(a.shape, lambda i: (0,) * a.ndim)
    tab = pl.BlockSpec((tm, LANES), lambda i: (i % nseq_t, 0))
    q, k, v, cb, u, ga, gc = pl.pallas_call(
        _proj_kernel,
        grid=(t // tm,),
        in_specs=[row(d), full(norm1_g.reshape(1, d)), full(w_in_p), full(bd), full(qg), full(kg),
                  tab, tab, tab],
        out_specs=[row(ATTN_DIM), row(KV_DIM), row(KV_DIM), row(CONV_DIM), row(CONV_DIM),
                   row(D_MODEL), row(D_MODEL)],
        out_shape=[jax.ShapeDtypeStruct((t, w), BF16)
                   for w in (ATTN_DIM, KV_DIM, KV_DIM, CONV_DIM, CONV_DIM, D_MODEL, D_MODEL)],
        compiler_params=pltpu.CompilerParams(dimension_semantics=("arbitrary",),
                                             vmem_limit_bytes=VMEM_LIMIT),
        name="proj",
    )(xf, norm1_g.reshape(1, d), w_in_p, bd, qg, kg, ctab, s1tab, s2tab)

    wr = jnp.concatenate([w_rg, jnp.zeros((d, 8 - N_GROUPS), F32), w_re,
                          jnp.zeros((d, LANES - ROUTER_ROWS), F32)], axis=1)
    wr_hi = wr.astype(BF16)
    wr2 = jnp.concatenate([wr_hi, (wr - wr_hi.astype(F32)).astype(BF16)], axis=1)
    br = jnp.concatenate([b_rg, jnp.zeros((8 - N_GROUPS,), F32), b_re,
                          jnp.zeros((LANES - ROUTER_ROWS,), F32)]).reshape(1, LANES)
    mix_params = (conv_w, conv_b.reshape(1, CONV_DIM), wa_p, w_conv_proj.astype(BF16), w_out.astype(BF16),
                  norm2_g.reshape(1, d), wr2, br)

    n_chunks = TOKEN_CHUNKS if bsz % TOKEN_CHUNKS == 0 else 1
    out = None
    for c in range(n_chunks):
        out = _chunk_tail(c, n_chunks, out, xf, (q, k, v, cb, u, ga, gc), attn_sink, mix_params,
                          (w_gate, w_up, w_down), bsz, seq, tq)
    return out.reshape(bsz, seq, d)


def _chunk_tail(c, n_chunks, out_prev, xf, acts, attn_sink, mix_params, expert_w, bsz, seq, tq):
    q, k, v, cb, u, ga, gc = acts
    d = D_MODEL
    t_all = bsz * seq
    bc = bsz // n_chunks
    b0 = c * bc
    t = bc * seq
    nj = seq // tq
    wpb = tq // WINDOW
    nb_seq = seq // WINDOW
    hpb = tq // 16
    row_in = lambda w: pl.BlockSpec((tq, w), lambda b, j: ((b0 + b) * nj + j, 0))
    row_out = lambda w: pl.BlockSpec((tq, w), lambda b, j: (b * nj + j, 0))
    prev_blk = pl.BlockSpec((WINDOW, KV_DIM),
                            lambda b, j: ((b0 + b) * nb_seq + jnp.maximum(j * wpb - 1, 0), 0))
    next_blk = pl.BlockSpec((WINDOW, KV_DIM),
                            lambda b, j: ((b0 + b) * nb_seq + jnp.minimum((j + 1) * wpb, nb_seq - 1), 0))
    uprev = pl.BlockSpec((16, CONV_DIM),
                         lambda b, j: (jnp.maximum(((b0 + b) * nj + j) * hpb - 1, 0), 0))
    unext = pl.BlockSpec((16, CONV_DIM),
                         lambda b, j: (jnp.minimum(((b0 + b) * nj + j + 1) * hpb, t_all // 16 - 1), 0))
    full2 = lambda a: pl.BlockSpec(a.shape, lambda b, j: (0,) * a.ndim)

    x1, h2, logits = pl.pallas_call(
        functools.partial(_mix_kernel, tq=tq),
        grid=(bc, nj),
        in_specs=[pl.BlockSpec(memory_space=pltpu.SMEM),
                  row_in(d), row_in(ATTN_DIM), prev_blk, row_in(KV_DIM), next_blk,
                  prev_blk, row_in(KV_DIM), next_blk,
                  row_in(CONV_DIM), uprev, row_in(CONV_DIM), unext, row_in(D_MODEL), row_in(D_MODEL)]
                 + [full2(p) for p in mix_params],
        out_specs=[row_out(d), pl.BlockSpec((tq * TOK_SUBLANES, LANES), lambda b, j: (b * nj + j, 0)),
                   row_out(LANES)],
        out_shape=[jax.ShapeDtypeStruct((t, d), F32),
                   jax.ShapeDtypeStruct((t * TOK_SUBLANES, LANES), U32),
                   jax.ShapeDtypeStruct((t, LANES), F32)],
        scratch_shapes=[pltpu.VMEM((tq + 16, CONV_DIM), F32), pltpu.VMEM((tq, ATTN_DIM), BF16)],
        compiler_params=pltpu.CompilerParams(dimension_semantics=("arbitrary", "arbitrary"),
                                             vmem_limit_bytes=VMEM_LIMIT),
        name="mix",
    )(attn_sink, xf, q, k, k, k, v, v, v, cb, u, u, u, ga, gc, *mix_params)

    tr = min(ROUTE_TILE, t)
    assert t % tr == 0 and tr % RANK_CHUNK == 0
    tri = jnp.asarray(np.triu(np.ones((RANK_CHUNK, RANK_CHUNK), np.float32), 1), BF16)
    colr = lambda r: pl.BlockSpec((r, tr), lambda i: (0, i))
    route_i, route_f, counts = pl.pallas_call(
        functools.partial(_route_kernel, tr=tr),
        grid=(t // tr,),
        in_specs=[pl.BlockSpec((tr, LANES), lambda i: (i, 0)), pl.BlockSpec(tri.shape, lambda i: (0, 0))],
        out_specs=[colr(8), colr(8), pl.BlockSpec((N_EXPERTS, LANES), lambda i: (0, 0))],
        out_shape=[jax.ShapeDtypeStruct((8, t), I32), jax.ShapeDtypeStruct((8, t), F32),
                   jax.ShapeDtypeStruct((N_EXPERTS, LANES), F32)],
        scratch_shapes=[pltpu.VMEM((N_EXPERTS, LANES), F32)],
        compiler_params=pltpu.CompilerParams(dimension_semantics=("arbitrary",),
                                             vmem_limit_bytes=VMEM_LIMIT),
        name="route",
    )(logits, tri)

    n_rows = 2 * t + N_EXPERTS * MOE_ROWS
    nblk = n_rows // MOE_ROWS
    cnt = counts[:, 0].astype(I32)
    padded = ((cnt + MOE_ROWS - 1) // MOE_ROWS) * MOE_ROWS
    pad_ends = jnp.cumsum(padded)
    pad_starts = pad_ends - padded
    eids = jnp.arange(N_EXPERTS, dtype=I32)[:, None, None]
    dest = jnp.sum(jnp.where(route_i[None, 0:2] == eids, pad_starts[:, None, None], 0), axis=0) \
        + route_i[2:4]
    blk_start = jnp.arange(nblk, dtype=I32) * MOE_ROWS
    block_e = jnp.minimum(jnp.sum((pad_ends[None, :] <= blk_start[:, None]).astype(I32), axis=1),
                          N_EXPERTS - 1)
    nused = (pad_ends[-1:] // MOE_ROWS).astype(I32)

    x_pad = _sc_dispatch(h2.reshape(t, TOK_SUBLANES, LANES), dest[0], dest[1], n_rows)
    x_pad = x_pad.reshape(n_rows * TOK_SUBLANES, LANES)

    w_gate, w_up, w_down = expert_w
    blk_rows = MOE_ROWS * TOK_SUBLANES
    wspec = lambda a, b: pl.BlockSpec((None, a, b), lambda i, be, nu: (be[i], 0, 0))
    y_pad = pl.pallas_call(
        _moe_kernel,
        grid_spec=pltpu.PrefetchScalarGridSpec(
            num_scalar_prefetch=2,
            grid=(nblk,),
            in_specs=[pl.BlockSpec((blk_rows, LANES), lambda i, be, nu: (i, 0)),
                      wspec(d, D_FF), wspec(d, D_FF), wspec(D_FF, d)],
            out_specs=pl.BlockSpec((blk_rows, LANES), lambda i, be, nu: (i, 0)),
            scratch_shapes=[pltpu.VMEM((d, D_FF), BF16), pltpu.VMEM((d, D_FF), BF16),
                            pltpu.VMEM((D_FF, d), BF16)]),
        out_shape=jax.ShapeDtypeStruct((n_rows * TOK_SUBLANES, LANES), U32),
        compiler_params=pltpu.CompilerParams(dimension_semantics=("arbitrary",),
                                             vmem_limit_bytes=VMEM_LIMIT),
        name="moe",
    )(block_e, nused, x_pad, w_gate, w_up, w_down)

    yk = _sc_gather(y_pad.reshape(n_rows, TOK_SUBLANES, LANES), dest[0], dest[1])
    yk = yk.reshape(2 * t * TOK_SUBLANES, LANES)
    tc = min(512, t)
    nt = t // tc
    gates_t = route_f[0:2].T
    in_specs = [pl.BlockSpec((tc, d), lambda i: (i, 0)),
                pl.BlockSpec((tc, 2), lambda i: (i, 0)),
                pl.BlockSpec((tc * TOK_SUBLANES, LANES), lambda i: (i, 0)),
                pl.BlockSpec((tc * TOK_SUBLANES, LANES), lambda i: (nt + i, 0))]
    args = [x1, gates_t, yk, yk]
    aliases = {}
    if out_prev is not None:
        in_specs.append(pl.BlockSpec(memory_space=pl.ANY))
        args.append(out_prev)
        aliases = {len(args) - 1: 0}
    return pl.pallas_call(
        functools.partial(_combine_kernel, tm=tc),
        grid=(nt,),
        in_specs=in_specs,
        out_specs=pl.BlockSpec((tc, d), lambda i: (c * nt + i, 0)),
        out_shape=jax.ShapeDtypeStruct((t_all, d), F32),
        input_output_aliases=aliases,
        compiler_params=pltpu.CompilerParams(dimension_semantics=("arbitrary",),
                                             vmem_limit_bytes=VMEM_LIMIT),
        name="combine",
    )(*args)


def kernel(x, norm1_g, w_in, q_norm_g, k_norm_g, attn_sink, conv_w, conv_b, w_attn_proj, w_conv_proj, w_out, norm2_g, w_router_group, b_router_group, w_router_expert, b_router_expert, w_gate_e, w_up_e, w_down_e):
    for l in range(norm1_g.shape[0]):
        x = _layer(x, norm1_g[l], w_in[l], q_norm_g[l], k_norm_g[l], attn_sink[l], conv_w[l],
                   conv_b[l], w_attn_proj[l], w_conv_proj[l], w_out[l], norm2_g[l],
                   w_router_group[l], b_router_group[l], w_router_expert[l], b_router_expert[l],
                   w_gate_e[l], w_up_e[l], w_down_e[l])
    return x
```

```python
import functools

import jax
import jax.numpy as jnp
import numpy as np
from jax import lax
from jax.experimental import pallas as pl
from jax.experimental.pallas import tpu as pltpu
from jax.experimental.pallas import tpu_sc as plsc

F32 = jnp.float32
BF16 = jnp.bfloat16
I32 = jnp.int32
U32 = jnp.uint32
HI_MASK = np.uint32(0xFFFF0000)

D_MODEL = 1024
N_HEADS = 8
N_KV_HEADS = 2
HEAD_DIM = 64
ATTN_DIM = N_HEADS * HEAD_DIM
KV_DIM = N_KV_HEADS * HEAD_DIM
WINDOW = 128
ROT_DIM = HEAD_DIM // 4
ROPE_THETA = 500000.0
CONV_DIM = D_MODEL // 2
N_GROUPS = 4
EXPERTS_PER_GROUP = 8
N_EXPERTS = N_GROUPS * EXPERTS_PER_GROUP
D_FF = D_MODEL // 4
NORM_EPS = 1e-6
MASK_VALUE = -1e30
NEG_BIG = -3.0e38
LOG2E = 1.4426950408889634

LANES = 128
TOK_SUBLANES = D_MODEL // (2 * LANES)
HALF_D = D_MODEL // 2
SC_WINDOW = 32
ROUTER_ROWS = 8 + N_EXPERTS
ROUTE_TILE = 2048
RANK_CHUNK = 256
MOE_ROWS = 512
TOKEN_CHUNKS = 1
VMEM_LIMIT = 56 * 1024 * 1024

_Q0, _K0, _V0, _CB0, _CC0, _CX0, _GA0, _GC0, _END = (
    0, 512, 640, 768, 1280, 1792, 2304, 3328, 4352)


def _store_packed_rows(ref, val, n):
    for c in range(TOK_SUBLANES):
        lo = val[:, c * LANES:(c + 1) * LANES].astype(BF16).astype(F32)
        hi = val[:, HALF_D + c * LANES:HALF_D + (c + 1) * LANES].astype(BF16).astype(F32)
        word = (lax.bitcast_convert_type(lo, U32) >> 16) | (lax.bitcast_convert_type(hi, U32) & HI_MASK)
        ref[pl.ds(c, n, stride=TOK_SUBLANES), :] = word


def _load_packed_rows(ref, n):
    words = [ref[pl.ds(c, n, stride=TOK_SUBLANES), :] for c in range(TOK_SUBLANES)]
    lo = [lax.bitcast_convert_type(w << 16, F32) for w in words]
    hi = [lax.bitcast_convert_type(w & HI_MASK, F32) for w in words]
    return lo + hi


def _rope_chunks(t, c, s1, s2):
    outs = []
    for j in range(t.shape[1] // LANES):
        tj = t[:, j * LANES:(j + 1) * LANES]
        outs.append(tj * c + pltpu.roll(tj, LANES - ROT_DIM // 2, 1) * s1
                    + pltpu.roll(tj, ROT_DIM // 2, 1) * s2)
    return outs[0] if len(outs) == 1 else jnp.concatenate(outs, axis=1)


def _proj_kernel(x_ref, g1_ref, w_ref, bd_ref, qg_ref, kg_ref, c_ref, s1_ref, s2_ref,
                 q_ref, k_ref, v_ref, cb_ref, u_ref, ga_ref, gc_ref):
    x = x_ref[...]
    ms = jnp.mean(x * x, axis=-1, keepdims=True)
    h = (x * lax.rsqrt(ms + NORM_EPS) * g1_ref[...]).astype(BF16)

    def proj(a, b):
        return jnp.dot(h, w_ref[:, a:b], preferred_element_type=F32)

    c, s1, s2 = c_ref[...], s1_ref[...], s2_ref[...]

    q = proj(_Q0, _K0)
    qms = jnp.dot((q * q).astype(BF16), bd_ref[...], preferred_element_type=F32)
    qg = jnp.concatenate([qg_ref[...]] * (ATTN_DIM // LANES), axis=1)
    qn = q * lax.rsqrt(qms + NORM_EPS) * qg
    q_ref[...] = (_rope_chunks(qn, c, s1, s2) * (HEAD_DIM ** -0.5 * LOG2E)).astype(BF16)

    k = proj(_K0, _V0)
    kms = jnp.dot((k * k).astype(BF16), bd_ref[0:KV_DIM, 0:KV_DIM], preferred_element_type=F32)
    kn = k * lax.rsqrt(kms + NORM_EPS) * kg_ref[...]
    k_ref[...] = _rope_chunks(kn, c, s1, s2).astype(BF16)

    v_ref[...] = proj(_V0, _CB0).astype(BF16)
    cb_ref[...] = proj(_CB0, _CC0).astype(BF16)
    u_ref[...] = (proj(_CC0, _CX0) * proj(_CX0, _GA0)).astype(BF16)
    ga_ref[...] = jax.nn.sigmoid(proj(_GA0, _GC0)).astype(BF16)
    gc_ref[...] = jax.nn.sigmoid(proj(_GC0, _END)).astype(BF16)


def _mix_kernel(sink_ref, x_ref, q_ref, kp_ref, kc_ref, kn_ref, vp_ref, vc_ref, vn_ref,
                cb_ref, up_ref, uc_ref, un_ref, ga_ref, gc_ref,
                cw_ref, cbias_ref, wa_ref, wc_ref, wo_ref, g2_ref,
                wr_ref, br_ref,
                x1_ref, h2_ref, lg_ref,
                ubuf, attn_scr, *, tq):
    j = pl.program_id(1)
    nj = pl.num_programs(1)
    nsub = tq // WINDOW

    kbuf = jnp.concatenate([kp_ref[...], kc_ref[...], kn_ref[...]], axis=0)
    vbuf = jnp.concatenate([vp_ref[...], vc_ref[...], vn_ref[...]], axis=0)
    low_half = lax.broadcasted_iota(I32, (WINDOW, LANES), 1) < HEAD_DIM
    low_half3 = lax.broadcasted_iota(I32, (3 * WINDOW, LANES), 1) < HEAD_DIM
    qrow = lax.broadcasted_iota(I32, (WINDOW, WINDOW), 0)
    kcol = lax.broadcasted_iota(I32, (WINDOW, WINDOW), 1)
    bias_lo = jnp.where(kcol >= qrow, 0.0, MASK_VALUE)
    bias_hi = jnp.where(kcol <= qrow, 0.0, MASK_VALUE)
    one = jnp.ones((3 * WINDOW, LANES), BF16)
    for n in range(nsub):
        blo = jnp.where(j > 0, bias_lo, MASK_VALUE) if n == 0 else bias_lo
        bhi = jnp.where(j < nj - 1, bias_hi, MASK_VALUE) if n == nsub - 1 else bias_hi
        qs = q_ref[n * WINDOW:(n + 1) * WINDOW, :]
        zero = jnp.zeros((WINDOW, LANES), BF16)
        q_stack = []
        for hd in range(N_HEADS):
            col = qs[:, (hd % 4) * LANES:(hd % 4 + 1) * LANES]
            keep = low_half if hd < 4 else jnp.logical_not(low_half)
            q_stack.append(jnp.where(keep, col, zero))
        q_stack = jnp.concatenate(q_stack, axis=0)
        kwin = kbuf[n * WINDOW:(n + 3) * WINDOW, :]
        vwin = vbuf[n * WINDOW:(n + 3) * WINDOW, :]
        s = lax.dot_general(q_stack, kwin, (((1,), (1,)), ((), ())),
                            preferred_element_type=F32)
        p_list, es_list = [], []
        for hd in range(N_HEADS):
            sink = sink_ref[hd] * LOG2E
            rows = slice(hd * WINDOW, (hd + 1) * WINDOW)
            s0 = s[rows, 0:WINDOW] + blo
            s1 = s[rows, WINDOW:2 * WINDOW]
            s2 = s[rows, 2 * WINDOW:3 * WINDOW] + bhi
            m = jnp.max(jnp.maximum(jnp.maximum(s0, s1), s2), axis=-1, keepdims=True)
            m = jnp.maximum(m, sink)
            p_list.append(jnp.concatenate(
                [jnp.exp2(s0 - m).astype(BF16), jnp.exp2(s1 - m).astype(BF16),
                 jnp.exp2(s2 - m).astype(BF16)], axis=1))
            es_list.append(jnp.exp2(sink - m))
        half = N_HEADS // 2
        o_lo = jnp.dot(jnp.concatenate(p_list[:half], axis=0), jnp.where(low_half3, vwin, one),
                       preferred_element_type=F32)
        o_hi = jnp.dot(jnp.concatenate(p_list[half:], axis=0), jnp.where(low_half3, one, vwin),
                       preferred_element_type=F32)
        cols = []
        for c4 in range(half):
            rows = slice(c4 * WINDOW, (c4 + 1) * WINDOW)
            oa, ob = o_lo[rows, :], o_hi[rows, :]
            da = pltpu.roll(oa, HEAD_DIM, 1) + es_list[c4]
            db = pltpu.roll(ob, HEAD_DIM, 1) + es_list[c4 + half]
            cols.append(jnp.where(low_half, oa / da, ob / db))
        attn_scr[n * WINDOW:(n + 1) * WINDOW, :] = jnp.concatenate(cols, axis=1).astype(BF16)
    attn = attn_scr[...]

    ubuf[8:8 + tq, :] = uc_ref[...].astype(F32)
    prev_rows = up_ref[...].astype(F32)
    next_rows = un_ref[...].astype(F32)
    ubuf[7:8, :] = prev_rows[15:16, :] * (j > 0).astype(F32)
    ubuf[8 + tq:9 + tq, :] = next_rows[0:1, :] * (j < nj - 1).astype(F32)
    cw = cw_ref[...]
    conv = (ubuf[7:7 + tq, :] * cw[0:1, :] + ubuf[8:8 + tq, :] * cw[1:2, :]
            + ubuf[9:9 + tq, :] * cw[2:3, :] + cbias_ref[...])
    cgate = (cb_ref[...].astype(F32) * conv).astype(BF16)

    merged = (ga_ref[...].astype(F32) * jnp.dot(attn, wa_ref[...], preferred_element_type=F32)
              + gc_ref[...].astype(F32) * jnp.dot(cgate, wc_ref[...], preferred_element_type=F32))
    x1 = x_ref[...] + jnp.dot(merged.astype(BF16), wo_ref[...], preferred_element_type=F32)
    x1_ref[...] = x1

    ms = jnp.mean(x1 * x1, axis=-1, keepdims=True)
    h2 = x1 * lax.rsqrt(ms + NORM_EPS) * g2_ref[...]
    _store_packed_rows(h2_ref, h2, tq)
    h_hi = h2.astype(BF16)
    h_lo = (h2 - h_hi.astype(F32)).astype(BF16)
    acc = jnp.dot(h_hi, wr_ref[...], preferred_element_type=F32)
    logits = (acc[:, 0:LANES] + acc[:, LANES:2 * LANES]
              + jnp.dot(h_lo, wr_ref[:, 0:LANES], preferred_element_type=F32) + br_ref[...])
    lg_ref[...] = logits


def _route_kernel(lg_ref, tri_ref, ri_ref, rf_ref, cnt_ref, base_ref, *, tr):
    @pl.when(pl.program_id(0) == 0)
    def _():
        base_ref[...] = jnp.zeros_like(base_ref)

    logits = lg_ref[...].T[0:ROUTER_ROWS, :]
    grow = lax.broadcasted_iota(I32, (8, tr), 0)
    gl = jnp.where(grow < N_GROUPS, logits[0:8, :], NEG_BIG)
    gmax = jnp.max(gl, axis=0, keepdims=True)
    grp = jnp.min(jnp.where(gl == gmax, grow, 8), axis=0, keepdims=True)
    p_grp = 1.0 / jnp.sum(jnp.exp(gl - gmax), axis=0, keepdims=True)
    el = logits[8:ROUTER_ROWS, :]
    erow = lax.broadcasted_iota(I32, (N_EXPERTS, tr), 0)
    in_grp = (erow // EXPERTS_PER_GROUP) == grp
    l_a = jnp.where(in_grp, el, NEG_BIG)
    l1 = jnp.max(l_a, axis=0, keepdims=True)
    i1 = jnp.min(jnp.where(l_a == l1, erow, N_EXPERTS), axis=0, keepdims=True)
    l_b = jnp.where(erow == i1, NEG_BIG, l_a)
    l2 = jnp.max(l_b, axis=0, keepdims=True)
    i2 = jnp.min(jnp.where(l_b == l2, erow, N_EXPERTS), axis=0, keepdims=True)
    r21 = jnp.exp(l2 - l1)
    t1 = 1.0 / (1.0 + r21)
    gate1 = p_grp * t1
    gate2 = p_grp * (r21 * t1)
    oh1 = (erow == i1).astype(F32)
    oh2 = (erow == i2).astype(F32)
    oh = oh1 + oh2
    base = base_ref[...][:, 0:1]
    r1, r2 = [], []
    for c in range(tr // RANK_CHUNK):
        cols = slice(c * RANK_CHUNK, (c + 1) * RANK_CHUNK)
        ohc = oh[:, cols]
        before = jnp.dot(ohc.astype(BF16), tri_ref[...], preferred_element_type=F32) + base
        r1.append(jnp.sum(oh1[:, cols] * before, axis=0, keepdims=True))
        r2.append(jnp.sum(oh2[:, cols] * before, axis=0, keepdims=True))
        base = base + jnp.sum(ohc, axis=1, keepdims=True)
    rank1 = jnp.concatenate(r1, axis=1)
    rank2 = jnp.concatenate(r2, axis=1)
    new_base = jnp.broadcast_to(base, base_ref.shape)
    base_ref[...] = new_base
    cnt_ref[...] = new_base
    zi = jnp.zeros((4, tr), I32)
    ri_ref[...] = jnp.concatenate([i1, i2, rank1.astype(I32), rank2.astype(I32), zi], axis=0)
    rf_ref[...] = jnp.concatenate([gate1, gate2, jnp.zeros((6, tr), F32)], axis=0)


def _moe_kernel(be_ref, nused_ref, x_ref, wg_ref, wu_ref, wd_ref, y_ref, wg_s, wu_s, wd_s):
    i = pl.program_id(0)

    changed = (i == 0) | (be_ref[i] != be_ref[jnp.maximum(i - 1, 0)])

    def narrow(_, carry):
        wg_s[...] = wg_ref[...].astype(BF16)
        wu_s[...] = wu_ref[...].astype(BF16)
        wd_s[...] = wd_ref[...].astype(BF16)
        return carry

    lax.fori_loop(0, changed.astype(I32), narrow, 0)

    @pl.when(i < nused_ref[0])
    def _():
        xb = jnp.concatenate([c.astype(BF16) for c in _load_packed_rows(x_ref, MOE_ROWS)], axis=1)
        g = jnp.dot(xb, wg_s[...], preferred_element_type=F32)
        u = jnp.dot(xb, wu_s[...], preferred_element_type=F32)
        hid = (g * jax.nn.sigmoid(g) * u).astype(BF16)
        _store_packed_rows(y_ref, jnp.dot(hid, wd_s[...], preferred_element_type=F32), MOE_ROWS)

    @pl.when(i >= nused_ref[0])
    def _():
        y_ref[...] = jnp.zeros_like(y_ref)


def _combine_kernel(x1_ref, gate_ref, y0_ref, y1_ref, *rest, tm):
    o_ref = rest[-1]
    g = gate_ref[...]
    g0, g1 = g[:, 0:1], g[:, 1:2]
    y0 = _load_packed_rows(y0_ref, tm)
    y1 = _load_packed_rows(y1_ref, tm)
    for c in range(2 * TOK_SUBLANES):
        cols = slice(c * LANES, (c + 1) * LANES)
        o_ref[:, cols] = x1_ref[:, cols] + g0 * y0[c] + g1 * y1[c]


def _sc_windows(n_tokens):
    info = plsc.get_sparse_core_info()
    n_workers = info.num_cores * info.num_subcores
    assert n_tokens % (n_workers * SC_WINDOW * 2) == 0
    return info.num_cores, n_tokens // (n_workers * SC_WINDOW)


def _sc_dispatch(h3, d0, d1, n_rows):
    t = h3.shape[0]
    n_cores, nwin = _sc_windows(t)
    mesh = plsc.VectorSubcoreMesh(core_axis_name="c", subcore_axis_name="s")
    rows = pltpu.VMEM((SC_WINDOW,) + h3.shape[1:], h3.dtype)
    idx = pltpu.VMEM((nwin, SC_WINDOW), I32)

    @functools.partial(
        pl.kernel, mesh=mesh,
        out_type=jax.ShapeDtypeStruct((n_rows,) + h3.shape[1:], h3.dtype),
        scratch_types=[idx, idx, idx, rows, rows] + [pltpu.SemaphoreType.DMA] * 3,
        name="sc_dispatch")
    def run(h_hbm, d0_hbm, d1_hbm, it_hbm, o_hbm, i0, i1, it, buf_a, buf_b, sem_a, sem_b, sem_s):
        wbase = (lax.axis_index("s") * n_cores + lax.axis_index("c")) * nwin
        pltpu.sync_copy(d0_hbm.at[pl.ds(wbase, nwin)], i0)
        pltpu.sync_copy(d1_hbm.at[pl.ds(wbase, nwin)], i1)
        pltpu.sync_copy(it_hbm.at[pl.ds(wbase, nwin)], it)

        @pl.loop(0, nwin, step=2)
        def _(w):
            ga = pltpu.async_copy(h_hbm.at[it.at[w]], buf_a, sem_a)
            gb = pltpu.async_copy(h_hbm.at[it.at[w + 1]], buf_b, sem_b)
            ga.wait()
            s0 = pltpu.async_copy(buf_a, o_hbm.at[i0.at[w]], sem_s)
            s1 = pltpu.async_copy(buf_a, o_hbm.at[i1.at[w]], sem_s)
            gb.wait()
            s2 = pltpu.async_copy(buf_b, o_hbm.at[i0.at[w + 1]], sem_s)
            s3 = pltpu.async_copy(buf_b, o_hbm.at[i1.at[w + 1]], sem_s)
            s0.wait()
            s1.wait()
            s2.wait()
            s3.wait()

    win = lambda a: a.reshape(-1, SC_WINDOW)
    return run(h3, win(d0), win(d1), win(jnp.arange(t, dtype=I32)))


def _sc_gather(y3, d0, d1):
    t = d0.shape[0]
    n_cores, nwin = _sc_windows(t)
    mesh = plsc.VectorSubcoreMesh(core_axis_name="c", subcore_axis_name="s")
    rows = pltpu.VMEM((SC_WINDOW,) + y3.shape[1:], y3.dtype)
    idx = pltpu.VMEM((nwin, SC_WINDOW), I32)

    @functools.partial(
        pl.kernel, mesh=mesh,
        out_type=jax.ShapeDtypeStruct((2 * t,) + y3.shape[1:], y3.dtype),
        scratch_types=[idx] * 4 + [rows] * 4 + [pltpu.SemaphoreType.DMA] * 5,
        name="sc_gather")
    def run(y_hbm, d0_hbm, d1_hbm, ita_hbm, itb_hbm, o_hbm, i0, i1, ita, itb,
            b0, b1, b2, b3, m0, m1, m2, m3, ms):
        wbase = (lax.axis_index("s") * n_cores + lax.axis_index("c")) * nwin
        pltpu.sync_copy(d0_hbm.at[pl.ds(wbase, nwin)], i0)
        pltpu.sync_copy(d1_hbm.at[pl.ds(wbase, nwin)], i1)
        pltpu.sync_copy(ita_hbm.at[pl.ds(wbase, nwin)], ita)
        pltpu.sync_copy(itb_hbm.at[pl.ds(wbase, nwin)], itb)

        @pl.loop(0, nwin, step=2)
        def _(w):
            g0 = pltpu.async_copy(y_hbm.at[i0.at[w]], b0, m0)
            g1 = pltpu.async_copy(y_hbm.at[i1.at[w]], b1, m1)
            g2 = pltpu.async_copy(y_hbm.at[i0.at[w + 1]], b2, m2)
            g3 = pltpu.async_copy(y_hbm.at[i1.at[w + 1]], b3, m3)
            g0.wait()
            s0 = pltpu.async_copy(b0, o_hbm.at[ita.at[w]], ms)
            g1.wait()
            s1 = pltpu.async_copy(b1, o_hbm.at[itb.at[w]], ms)
            g2.wait()
            s2 = pltpu.async_copy(b2, o_hbm.at[ita.at[w + 1]], ms)
            g3.wait()
            s3 = pltpu.async_copy(b3, o_hbm.at[itb.at[w + 1]], ms)
            s0.wait()
            s1.wait()
            s2.wait()
            s3.wait()

    win = lambda a: a.reshape(-1, SC_WINDOW)
    iota = jnp.arange(2 * t, dtype=I32)
    return run(y3, win(d0), win(d1), win(iota[:t]), win(iota[t:]))


def _rope_lane_tables(seq_len):
    half = ROT_DIM // 2
    inv_freq = ROPE_THETA ** (-np.arange(0, ROT_DIM, 2, dtype=np.float64) / ROT_DIM)
    ang = np.arange(seq_len, dtype=np.float64)[:, None] * inv_freq[None, :]
    cos, sin = np.cos(ang), np.sin(ang)
    m = np.arange(LANES) % HEAD_DIM
    f = m % half
    c = np.where(m[None, :] < ROT_DIM, cos[:, f], 1.0)
    s1 = np.where(m[None, :] < half, -sin[:, f], 0.0)
    s2 = np.where((m[None, :] >= half) & (m[None, :] < ROT_DIM), sin[:, f], 0.0)
    return tuple(jnp.asarray(a, F32) for a in (c, s1, s2))


def _layer(x, norm1_g, w_in, q_norm_g, k_norm_g, attn_sink, conv_w, conv_b, w_attn_proj,
           w_conv_proj, w_out, norm2_g, w_rg, b_rg, w_re, b_re, w_gate, w_up, w_down):
    bsz, seq, d = x.shape
    t = bsz * seq
    tm = min(512, t)
    tq = min(512, seq)
    assert d == D_MODEL and seq % tq == 0 and t % tm == 0 and seq % tm == 0 and tq % WINDOW == 0
    xf = x.reshape(t, d)

    hpg = N_HEADS // N_KV_HEADS
    wq_p = (w_in[:, :ATTN_DIM].reshape(d, N_KV_HEADS, hpg, HEAD_DIM).transpose(0, 2, 1, 3)
            .reshape(d, ATTN_DIM))
    w_in_p = jnp.concatenate([wq_p, w_in[:, ATTN_DIM:]], axis=1).astype(BF16)
    wa_p = (w_attn_proj.reshape(N_KV_HEADS, hpg, HEAD_DIM, d).transpose(1, 0, 2, 3)
            .reshape(ATTN_DIM, d).astype(BF16))
    bd = jnp.asarray(np.kron(np.eye(N_HEADS), np.full((HEAD_DIM, HEAD_DIM), 1.0 / HEAD_DIM)), BF16)
    qg = jnp.tile(q_norm_g, LANES // HEAD_DIM).reshape(1, LANES)
    kg = jnp.tile(k_norm_g, LANES // HEAD_DIM).reshape(1, LANES)
    ctab, s1tab, s2tab = _rope_lane_tables(seq)
    nseq_t = seq // tm

    row = lambda w: pl.BlockSpec((tm, w), lambda i: (i, 0))
    full = lambda a: pl.BlockSpec(a.shape, lambda i: (0,) * a.ndim)
    tab = pl.BlockSpec((tm, LANES), lambda i: (i % nseq_t, 0))
    q, k, v, cb, u, ga, gc = pl.pallas_call(
        _proj_kernel,
        grid=(t // tm,),
        in_specs=[row(d), full(norm1_g.reshape(1, d)), full(w_in_p), full(bd), full(qg), full(kg),
                  tab, tab, tab],
        out_specs=[row(ATTN_DIM), row(KV_DIM), row(KV_DIM), row(CONV_DIM), row(CONV_DIM),
                   row(D_MODEL), row(D_MODEL)],
        out_shape=[jax.ShapeDtypeStruct((t, w), BF16)
                   for w in (ATTN_DIM, KV_DIM, KV_DIM, CONV_DIM, CONV_DIM, D_MODEL, D_MODEL)],
        compiler_params=pltpu.CompilerParams(dimension_semantics=("arbitrary",),
                                             vmem_limit_bytes=VMEM_LIMIT),
        name="proj",
    )(xf, norm1_g.reshape(1, d), w_in_p, bd, qg, kg, ctab, s1tab, s2tab)

    wr = jnp.concatenate([w_rg, jnp.zeros((d, 8 - N_GROUPS), F32), w_re,
                          jnp.zeros((d, LANES - ROUTER_ROWS), F32)], axis=1)
    wr_hi = wr.astype(BF16)
    wr2 = jnp.concatenate([wr_hi, (wr - wr_hi.astype(F32)).astype(BF16)], axis=1)
    br = jnp.concatenate([b_rg, jnp.zeros((8 - N_GROUPS,), F32), b_re,
                          jnp.zeros((LANES - ROUTER_ROWS,), F32)]).reshape(1, LANES)
    mix_params = (conv_w, conv_b.reshape(1, CONV_DIM), wa_p, w_conv_proj.astype(BF16), w_out.astype(BF16),
                  norm2_g.reshape(1, d), wr2, br)

    n_chunks = TOKEN_CHUNKS if bsz % TOKEN_CHUNKS == 0 else 1
    out = None
    for c in range(n_chunks):
        out = _chunk_tail(c, n_chunks, out, xf, (q, k, v, cb, u, ga, gc), attn_sink, mix_params,
                          (w_gate, w_up, w_down), bsz, seq, tq)
    return out.reshape(bsz, seq, d)


def _chunk_tail(c, n_chunks, out_prev, xf, acts, attn_sink, mix_params, expert_w, bsz, seq, tq):
    q, k, v, cb, u, ga, gc = acts
    d = D_MODEL
    t_all = bsz * seq
    bc = bsz // n_chunks
    b0 = c * bc
    t = bc * seq
    nj = seq // tq
    wpb = tq // WINDOW
    nb_seq = seq // WINDOW
    hpb = tq // 16
    row_in = lambda w: pl.BlockSpec((tq, w), lambda b, j: ((b0 + b) * nj + j, 0))
    row_out = lambda w: pl.BlockSpec((tq, w), lambda b, j: (b * nj + j, 0))
    prev_blk = pl.BlockSpec((WINDOW, KV_DIM),
                            lambda b, j: ((b0 + b) * nb_seq + jnp.maximum(j * wpb - 1, 0), 0))
    next_blk = pl.BlockSpec((WINDOW, KV_DIM),
                            lambda b, j: ((b0 + b) * nb_seq + jnp.minimum((j + 1) * wpb, nb_seq - 1), 0))
    uprev = pl.BlockSpec((16, CONV_DIM),
                         lambda b, j: (jnp.maximum(((b0 + b) * nj + j) * hpb - 1, 0), 0))
    unext = pl.BlockSpec((16, CONV_DIM),
                         lambda b, j: (jnp.minimum(((b0 + b) * nj + j + 1) * hpb, t_all // 16 - 1), 0))
    full2 = lambda a: pl.BlockSpec(a.shape, lambda b, j: (0,) * a.ndim)

    x1, h2, logits = pl.pallas_call(
        functools.partial(_mix_kernel, tq=tq),
        grid=(bc, nj),
        in_specs=[pl.BlockSpec(memory_space=pltpu.SMEM),
                  row_in(d), row_in(ATTN_DIM), prev_blk, row_in(KV_DIM), next_blk,
                  prev_blk, row_in(KV_DIM), next_blk,
                  row_in(CONV_DIM), uprev, row_in(CONV_DIM), unext, row_in(D_MODEL), row_in(D_MODEL)]
                 + [full2(p) for p in mix_params],
        out_specs=[row_out(d), pl.BlockSpec((tq * TOK_SUBLANES, LANES), lambda b, j: (b * nj + j, 0)),
                   row_out(LANES)],
        out_shape=[jax.ShapeDtypeStruct((t, d), F32),
                   jax.ShapeDtypeStruct((t * TOK_SUBLANES, LANES), U32),
                   jax.ShapeDtypeStruct((t, LANES), F32)],
        scratch_shapes=[pltpu.VMEM((tq + 16, CONV_DIM), F32), pltpu.VMEM((tq, ATTN_DIM), BF16)],
        compiler_params=pltpu.CompilerParams(dimension_semantics=("arbitrary", "arbitrary"),
                                             vmem_limit_bytes=VMEM_LIMIT),
        name="mix",
    )(attn_sink, xf, q, k, k, k, v, v, v, cb, u, u, u, ga, gc, *mix_params)

    tr = min(ROUTE_TILE, t)
    assert t % tr == 0 and tr % RANK_CHUNK == 0
    tri = jnp.asarray(np.triu(np.ones((RANK_CHUNK, RANK_CHUNK), np.float32), 1), BF16)
    colr = lambda r: pl.BlockSpec((r, tr), lambda i: (0, i))
    route_i, route_f, counts = pl.pallas_call(
        functools.partial(_route_kernel, tr=tr),
        grid=(t // tr,),
        in_specs=[pl.BlockSpec((tr, LANES), lambda i: (i, 0)), pl.BlockSpec(tri.shape, lambda i: (0, 0))],
        out_specs=[colr(8), colr(8), pl.BlockSpec((N_EXPERTS, LANES), lambda i: (0, 0))],
        out_shape=[jax.ShapeDtypeStruct((8, t), I32), jax.ShapeDtypeStruct((8, t), F32),
                   jax.ShapeDtypeStruct((N_EXPERTS, LANES), F32)],
        scratch_shapes=[pltpu.VMEM((N_EXPERTS, LANES), F32)],
        compiler_params=pltpu.CompilerParams(dimension_semantics=("arbitrary",),
                                             vmem_limit_bytes=VMEM_LIMIT),
        name="route",
    )(logits, tri)

    n_rows = 2 * t + N_EXPERTS * MOE_ROWS
    nblk = n_rows // MOE_ROWS
    cnt = counts[:, 0].astype(I32)
    padded = ((cnt + MOE_ROWS - 1) // MOE_ROWS) * MOE_ROWS
    pad_ends = jnp.cumsum(padded)
    pad_starts = pad_ends - padded
    eids = jnp.arange(N_EXPERTS, dtype=I32)[:, None, None]
    dest = jnp.sum(jnp.where(route_i[None, 0:2] == eids, pad_starts[:, None, None], 0), axis=0) \
        + route_i[2:4]
    blk_start = jnp.arange(nblk, dtype=I32) * MOE_ROWS
    block_e = jnp.minimum(jnp.sum((pad_ends[None, :] <= blk_start[:, None]).astype(I32), axis=1),
                          N_EXPERTS - 1)
    nused = (pad_ends[-1:] // MOE_ROWS).astype(I32)

    x_pad = _sc_dispatch(h2.reshape(t, TOK_SUBLANES, LANES), dest[0], dest[1], n_rows)
    x_pad = x_pad.reshape(n_rows * TOK_SUBLANES, LANES)

    w_gate, w_up, w_down = expert_w
    blk_rows = MOE_ROWS * TOK_SUBLANES
    wspec = lambda a, b: pl.BlockSpec((None, a, b), lambda i, be, nu: (be[i], 0, 0))
    y_pad = pl.pallas_call(
        _moe_kernel,
        grid_spec=pltpu.PrefetchScalarGridSpec(
            num_scalar_prefetch=2,
            grid=(nblk,),
            in_specs=[pl.BlockSpec((blk_rows, LANES), lambda i, be, nu: (jnp.minimum(i, nu[0] - 1), 0)),
                      wspec(d, D_FF), wspec(d, D_FF), wspec(D_FF, d)],
            out_specs=pl.BlockSpec((blk_rows, LANES),
                                   lambda i, be, nu: (jnp.where(i < nu[0], i, nblk - 1), 0)),
            scratch_shapes=[pltpu.VMEM((d, D_FF), BF16), pltpu.VMEM((d, D_FF), BF16),
                            pltpu.VMEM((D_FF, d), BF16)]),
        out_shape=jax.ShapeDtypeStruct((n_rows * TOK_SUBLANES, LANES), U32),
        compiler_params=pltpu.CompilerParams(dimension_semantics=("arbitrary",),
                                             vmem_limit_bytes=VMEM_LIMIT),
        name="moe",
    )(block_e, nused, x_pad, w_gate, w_up, w_down)

    yk = _sc_gather(y_pad.reshape(n_rows, TOK_SUBLANES, LANES), dest[0], dest[1])
    yk = yk.reshape(2 * t * TOK_SUBLANES, LANES)
    tc = min(512, t)
    nt = t // tc
    gates_t = route_f[0:2].T
    in_specs = [pl.BlockSpec((tc, d), lambda i: (i, 0)),
                pl.BlockSpec((tc, 2), lambda i: (i, 0)),
                pl.BlockSpec((tc * TOK_SUBLANES, LANES), lambda i: (i, 0)),
                pl.BlockSpec((tc * TOK_SUBLANES, LANES), lambda i: (nt + i, 0))]
    args = [x1, gates_t, yk, yk]
    aliases = {}
    if out_prev is not None:
        in_specs.append(pl.BlockSpec(memory_space=pl.ANY))
        args.append(out_prev)
        aliases = {len(args) - 1: 0}
    return pl.pallas_call(
        functools.partial(_combine_kernel, tm=tc),
        grid=(nt,),
        in_specs=in_specs,
        out_specs=pl.BlockSpec((tc, d), lambda i: (c * nt + i, 0)),
        out_shape=jax.ShapeDtypeStruct((t_all, d), F32),
        input_output_aliases=aliases,
        compiler_params=pltpu.CompilerParams(dimension_semantics=("arbitrary",),
                                             vmem_limit_bytes=VMEM_LIMIT),
        name="combine",
    )(*args)


def kernel(x, norm1_g, w_in, q_norm_g, k_norm_g, attn_sink, conv_w, conv_b, w_attn_proj, w_conv_proj, w_out, norm2_g, w_router_group, b_router_group, w_router_expert, b_router_expert, w_gate_e, w_up_e, w_down_e):
    for l in range(norm1_g.shape[0]):
        x = _layer(x, norm1_g[l], w_in[l], q_norm_g[l], k_norm_g[l], attn_sink[l], conv_w[l],
                   conv_b[l], w_attn_proj[l], w_conv_proj[l], w_out[l], norm2_g[l],
                   w_router_group[l], b_router_group[l], w_router_expert[l], b_router_expert[l],
                   w_gate_e[l], w_up_e[l], w_down_e[l])
    return x
```

```python
import functools

import jax
import jax.numpy as jnp
import numpy as np
from jax import lax
from jax.experimental import pallas as pl
from jax.experimental.pallas import tpu as pltpu
from jax.experimental.pallas import tpu_sc as plsc

F32 = jnp.float32
BF16 = jnp.bfloat16
I32 = jnp.int32
U32 = jnp.uint32
HI_MASK = np.uint32(0xFFFF0000)

D_MODEL = 1024
N_HEADS = 8
N_KV_HEADS = 2
HEAD_DIM = 64
ATTN_DIM = N_HEADS * HEAD_DIM
KV_DIM = N_KV_HEADS * HEAD_DIM
WINDOW = 128
ROT_DIM = HEAD_DIM // 4
ROPE_THETA = 500000.0
CONV_DIM = D_MODEL // 2
N_GROUPS = 4
EXPERTS_PER_GROUP = 8
N_EXPERTS = N_GROUPS * EXPERTS_PER_GROUP
D_FF = D_MODEL // 4
NORM_EPS = 1e-6
MASK_VALUE = -1e30
NEG_BIG = -3.0e38
LOG2E = 1.4426950408889634

LANES = 128
TOK_SUBLANES = D_MODEL // (2 * LANES)
HALF_D = D_MODEL // 2
SC_WINDOW = 32
ROUTER_ROWS = 8 + N_EXPERTS
ROUTE_TILE = 2048
RANK_CHUNK = 256
MOE_ROWS = 512
TOKEN_CHUNKS = 1
VMEM_LIMIT = 56 * 1024 * 1024

_Q0, _K0, _V0, _CB0, _CC0, _CX0, _GA0, _GC0, _END = (
    0, 512, 640, 768, 1280, 1792, 2304, 3328, 4352)


def _store_packed_rows(ref, val, n):
    for c in range(TOK_SUBLANES):
        lo = val[:, c * LANES:(c + 1) * LANES].astype(BF16).astype(F32)
        hi = val[:, HALF_D + c * LANES:HALF_D + (c + 1) * LANES].astype(BF16).astype(F32)
        word = (lax.bitcast_convert_type(lo, U32) >> 16) | (lax.bitcast_convert_type(hi, U32) & HI_MASK)
        ref[pl.ds(c, n, stride=TOK_SUBLANES), :] = word


def _load_packed_rows(ref, n):
    words = [ref[pl.ds(c, n, stride=TOK_SUBLANES), :] for c in range(TOK_SUBLANES)]
    lo = [lax.bitcast_convert_type(w << 16, F32) for w in words]
    hi = [lax.bitcast_convert_type(w & HI_MASK, F32) for w in words]
    return lo + hi


def _rope_chunks(t, c, s1, s2):
    outs = []
    for j in range(t.shape[1] // LANES):
        tj = t[:, j * LANES:(j + 1) * LANES]
        outs.append(tj * c + pltpu.roll(tj, LANES - ROT_DIM // 2, 1) * s1
                    + pltpu.roll(tj, ROT_DIM // 2, 1) * s2)
    return outs[0] if len(outs) == 1 else jnp.concatenate(outs, axis=1)


def _proj_kernel(x_ref, g1_ref, w_ref, bd_ref, qg_ref, kg_ref, c_ref, s1_ref, s2_ref,
                 q_ref, k_ref, v_ref, cb_ref, u_ref, ga_ref, gc_ref):
    x = x_ref[...]
    ms = jnp.mean(x * x, axis=-1, keepdims=True)
    h = (x * lax.rsqrt(ms + NORM_EPS) * g1_ref[...]).astype(BF16)

    def proj(a, b):
        return jnp.dot(h, w_ref[:, a:b], preferred_element_type=F32)

    c, s1, s2 = c_ref[...], s1_ref[...], s2_ref[...]

    q = proj(_Q0, _K0)
    qms = jnp.dot((q * q).astype(BF16), bd_ref[...], preferred_element_type=F32)
    qg = jnp.concatenate([qg_ref[...]] * (ATTN_DIM // LANES), axis=1)
    qn = q * lax.rsqrt(qms + NORM_EPS) * qg
    q_ref[...] = (_rope_chunks(qn, c, s1, s2) * (HEAD_DIM ** -0.5 * LOG2E)).astype(BF16)

    k = proj(_K0, _V0)
    kms = jnp.dot((k * k).astype(BF16), bd_ref[0:KV_DIM, 0:KV_DIM], preferred_element_type=F32)
    kn = k * lax.rsqrt(kms + NORM_EPS) * kg_ref[...]
    k_ref[...] = _rope_chunks(kn, c, s1, s2).astype(BF16)

    v_ref[...] = proj(_V0, _CB0).astype(BF16)
    cb_ref[...] = proj(_CB0, _CC0).astype(BF16)
    u_ref[...] = (proj(_CC0, _CX0) * proj(_CX0, _GA0)).astype(BF16)
    ga_ref[...] = jax.nn.sigmoid(proj(_GA0, _GC0)).astype(BF16)
    gc_ref[...] = jax.nn.sigmoid(proj(_GC0, _END)).astype(BF16)


def _mix_kernel(sink_ref, x_ref, q_ref, kp_ref, kc_ref, kn_ref, vp_ref, vc_ref, vn_ref,
                cb_ref, up_ref, uc_ref, un_ref, ga_ref, gc_ref,
                cw_ref, cbias_ref, wa_ref, wc_ref, wo_ref, g2_ref,
                wr_ref, br_ref,
                x1_ref, h2_ref, lg_ref,
                ubuf, attn_scr, *, tq):
    j = pl.program_id(1)
    nj = pl.num_programs(1)
    nsub = tq // WINDOW

    kbuf = jnp.concatenate([kp_ref[...], kc_ref[...], kn_ref[...]], axis=0)
    vbuf = jnp.concatenate([vp_ref[...], vc_ref[...], vn_ref[...]], axis=0)
    low_half = lax.broadcasted_iota(I32, (WINDOW, LANES), 1) < HEAD_DIM
    low_half3 = lax.broadcasted_iota(I32, (3 * WINDOW, LANES), 1) < HEAD_DIM
    qrow = lax.broadcasted_iota(I32, (WINDOW, WINDOW), 0)
    kcol = lax.broadcasted_iota(I32, (WINDOW, WINDOW), 1)
    bias_lo = jnp.where(kcol >= qrow, 0.0, MASK_VALUE)
    bias_hi = jnp.where(kcol <= qrow, 0.0, MASK_VALUE)
    one = jnp.ones((3 * WINDOW, LANES), BF16)
    for n in range(nsub):
        blo = jnp.where(j > 0, bias_lo, MASK_VALUE) if n == 0 else bias_lo
        bhi = jnp.where(j < nj - 1, bias_hi, MASK_VALUE) if n == nsub - 1 else bias_hi
        qs = q_ref[n * WINDOW:(n + 1) * WINDOW, :]
        zero = jnp.zeros((WINDOW, LANES), BF16)
        q_stack = []
        for hd in range(N_HEADS):
            col = qs[:, (hd % 4) * LANES:(hd % 4 + 1) * LANES]
            keep = low_half if hd < 4 else jnp.logical_not(low_half)
            q_stack.append(jnp.where(keep, col, zero))
        q_stack = jnp.concatenate(q_stack, axis=0)
        kwin = kbuf[n * WINDOW:(n + 3) * WINDOW, :]
        vwin = vbuf[n * WINDOW:(n + 3) * WINDOW, :]
        s = lax.dot_general(q_stack, kwin, (((1,), (1,)), ((), ())),
                            preferred_element_type=F32)
        p_list, es_list = [], []
        for hd in range(N_HEADS):
            sink = sink_ref[hd] * LOG2E
            rows = slice(hd * WINDOW, (hd + 1) * WINDOW)
            s0 = s[rows, 0:WINDOW] + blo
            s1 = s[rows, WINDOW:2 * WINDOW]
            s2 = s[rows, 2 * WINDOW:3 * WINDOW] + bhi
            m = jnp.max(jnp.maximum(jnp.maximum(s0, s1), s2), axis=-1, keepdims=True)
            m = jnp.maximum(m, sink)
            p_list.append(jnp.concatenate(
                [jnp.exp2(s0 - m).astype(BF16), jnp.exp2(s1 - m).astype(BF16),
                 jnp.exp2(s2 - m).astype(BF16)], axis=1))
            es_list.append(jnp.exp2(sink - m))
        half = N_HEADS // 2
        o_lo = jnp.dot(jnp.concatenate(p_list[:half], axis=0), jnp.where(low_half3, vwin, one),
                       preferred_element_type=F32)
        o_hi = jnp.dot(jnp.concatenate(p_list[half:], axis=0), jnp.where(low_half3, one, vwin),
                       preferred_element_type=F32)
        cols = []
        for c4 in range(half):
            rows = slice(c4 * WINDOW, (c4 + 1) * WINDOW)
            oa, ob = o_lo[rows, :], o_hi[rows, :]
            da = pltpu.roll(oa, HEAD_DIM, 1) + es_list[c4]
            db = pltpu.roll(ob, HEAD_DIM, 1) + es_list[c4 + half]
            cols.append(jnp.where(low_half, oa / da, ob / db))
        attn_scr[n * WINDOW:(n + 1) * WINDOW, :] = jnp.concatenate(cols, axis=1).astype(BF16)
    attn = attn_scr[...]

    ubuf[8:8 + tq, :] = uc_ref[...].astype(F32)
    prev_rows = up_ref[...].astype(F32)
    next_rows = un_ref[...].astype(F32)
    ubuf[7:8, :] = prev_rows[15:16, :] * (j > 0).astype(F32)
    ubuf[8 + tq:9 + tq, :] = next_rows[0:1, :] * (j < nj - 1).astype(F32)
    cw = cw_ref[...]
    conv = (ubuf[7:7 + tq, :] * cw[0:1, :] + ubuf[8:8 + tq, :] * cw[1:2, :]
            + ubuf[9:9 + tq, :] * cw[2:3, :] + cbias_ref[...])
    cgate = (cb_ref[...].astype(F32) * conv).astype(BF16)

    merged = (ga_ref[...].astype(F32) * jnp.dot(attn, wa_ref[...], preferred_element_type=F32)
              + gc_ref[...].astype(F32) * jnp.dot(cgate, wc_ref[...], preferred_element_type=F32))
    x1 = x_ref[...] + jnp.dot(merged.astype(BF16), wo_ref[...], preferred_element_type=F32)
    x1_ref[...] = x1

    ms = jnp.mean(x1 * x1, axis=-1, keepdims=True)
    h2 = x1 * lax.rsqrt(ms + NORM_EPS) * g2_ref[...]
    _store_packed_rows(h2_ref, h2, tq)
    h_hi = h2.astype(BF16)
    h_lo = (h2 - h_hi.astype(F32)).astype(BF16)
    acc = jnp.dot(h_hi, wr_ref[...], preferred_element_type=F32)
    logits = (acc[:, 0:LANES] + acc[:, LANES:2 * LANES]
              + jnp.dot(h_lo, wr_ref[:, 0:LANES], preferred_element_type=F32) + br_ref[...])
    lg_ref[...] = logits


def _route_kernel(lg_ref, tri_ref, ri_ref, rf_ref, cnt_ref, base_ref, *, tr):
    @pl.when(pl.program_id(0) == 0)
    def _():
        base_ref[...] = jnp.zeros_like(base_ref)

    logits = lg_ref[...].T[0:ROUTER_ROWS, :]
    grow = lax.broadcasted_iota(I32, (8, tr), 0)
    gl = jnp.where(grow < N_GROUPS, logits[0:8, :], NEG_BIG)
    gmax = jnp.max(gl, axis=0, keepdims=True)
    grp = jnp.min(jnp.where(gl == gmax, grow, 8), axis=0, keepdims=True)
    p_grp = 1.0 / jnp.sum(jnp.exp(gl - gmax), axis=0, keepdims=True)
    el = logits[8:ROUTER_ROWS, :]
    erow = lax.broadcasted_iota(I32, (N_EXPERTS, tr), 0)
    in_grp = (erow // EXPERTS_PER_GROUP) == grp
    l_a = jnp.where(in_grp, el, NEG_BIG)
    l1 = jnp.max(l_a, axis=0, keepdims=True)
    i1 = jnp.min(jnp.where(l_a == l1, erow, N_EXPERTS), axis=0, keepdims=True)
    l_b = jnp.where(erow == i1, NEG_BIG, l_a)
    l2 = jnp.max(l_b, axis=0, keepdims=True)
    i2 = jnp.min(jnp.where(l_b == l2, erow, N_EXPERTS), axis=0, keepdims=True)
    r21 = jnp.exp(l2 - l1)
    t1 = 1.0 / (1.0 + r21)
    gate1 = p_grp * t1
    gate2 = p_grp * (r21 * t1)
    oh1 = (erow == i1).astype(F32)
    oh2 = (erow == i2).astype(F32)
    oh = oh1 + oh2
    base = base_ref[...][:, 0:1]
    r1, r2 = [], []
    for c in range(tr // RANK_CHUNK):
        cols = slice(c * RANK_CHUNK, (c + 1) * RANK_CHUNK)
        ohc = oh[:, cols]
        before = jnp.dot(ohc.astype(BF16), tri_ref[...], preferred_element_type=F32) + base
        r1.append(jnp.sum(oh1[:, cols] * before, axis=0, keepdims=True))
        r2.append(jnp.sum(oh2[:, cols] * before, axis=0, keepdims=True))
        base = base + jnp.sum(ohc, axis=1, keepdims=True)
    rank1 = jnp.concatenate(r1, axis=1)
    rank2 = jnp.concatenate(r2, axis=1)
    new_base = jnp.broadcast_to(base, base_ref.shape)
    base_ref[...] = new_base
    cnt_ref[...] = new_base
    zi = jnp.zeros((4, tr), I32)
    ri_ref[...] = jnp.concatenate([i1, i2, rank1.astype(I32), rank2.astype(I32), zi], axis=0)
    rf_ref[...] = jnp.concatenate([gate1, gate2, jnp.zeros((6, tr), F32)], axis=0)


def _moe_kernel(be_ref, nused_ref, x_ref, wg_ref, wu_ref, wd_ref, y_ref, wg_s, wu_s, wd_s):
    i = pl.program_id(0)

    changed = (i == 0) | (be_ref[i] != be_ref[jnp.maximum(i - 1, 0)])

    def narrow(_, carry):
        wg_s[...] = wg_ref[...].astype(BF16)
        wu_s[...] = wu_ref[...].astype(BF16)
        wd_s[...] = wd_ref[...].astype(BF16)
        return carry

    lax.fori_loop(0, changed.astype(I32), narrow, 0)

    @pl.when(i < nused_ref[0])
    def _():
        xb = jnp.concatenate([c.astype(BF16) for c in _load_packed_rows(x_ref, MOE_ROWS)], axis=1)
        g = jnp.dot(xb, wg_s[...], preferred_element_type=F32)
        u = jnp.dot(xb, wu_s[...], preferred_element_type=F32)
        hid = (g * jax.nn.sigmoid(g) * u).astype(BF16)
        _store_packed_rows(y_ref, jnp.dot(hid, wd_s[...], preferred_element_type=F32), MOE_ROWS)

    @pl.when(i >= nused_ref[0])
    def _():
        y_ref[...] = jnp.zeros_like(y_ref)


def _combine_kernel(x1_ref, gate_ref, y0_ref, y1_ref, *rest, tm):
    o_ref = rest[-1]
    g = gate_ref[...]
    g0, g1 = g[:, 0:1], g[:, 1:2]
    y0 = _load_packed_rows(y0_ref, tm)
    y1 = _load_packed_rows(y1_ref, tm)
    for c in range(2 * TOK_SUBLANES):
        cols = slice(c * LANES, (c + 1) * LANES)
        o_ref[:, cols] = x1_ref[:, cols] + g0 * y0[c] + g1 * y1[c]


def _sc_windows(n_tokens):
    info = plsc.get_sparse_core_info()
    n_workers = info.num_cores * info.num_subcores
    assert n_tokens % (n_workers * SC_WINDOW * 2) == 0
    return info.num_cores, n_tokens // (n_workers * SC_WINDOW)


def _sc_dispatch(h3, d0, d1, n_rows):
    t = h3.shape[0]
    n_cores, nwin = _sc_windows(t)
    mesh = plsc.VectorSubcoreMesh(core_axis_name="c", subcore_axis_name="s")
    rows = pltpu.VMEM((SC_WINDOW,) + h3.shape[1:], h3.dtype)
    idx = pltpu.VMEM((nwin, SC_WINDOW), I32)

    @functools.partial(
        pl.kernel, mesh=mesh,
        out_type=jax.ShapeDtypeStruct((n_rows,) + h3.shape[1:], h3.dtype),
        scratch_types=[idx, idx, idx, rows, rows] + [pltpu.SemaphoreType.DMA] * 3,
        name="sc_dispatch")
    def run(h_hbm, d0_hbm, d1_hbm, it_hbm, o_hbm, i0, i1, it, buf_a, buf_b, sem_a, sem_b, sem_s):
        wbase = (lax.axis_index("s") * n_cores + lax.axis_index("c")) * nwin
        pltpu.sync_copy(d0_hbm.at[pl.ds(wbase, nwin)], i0)
        pltpu.sync_copy(d1_hbm.at[pl.ds(wbase, nwin)], i1)
        pltpu.sync_copy(it_hbm.at[pl.ds(wbase, nwin)], it)

        @pl.loop(0, nwin, step=2)
        def _(w):
            ga = pltpu.async_copy(h_hbm.at[it.at[w]], buf_a, sem_a)
            gb = pltpu.async_copy(h_hbm.at[it.at[w + 1]], buf_b, sem_b)
            ga.wait()
            s0 = pltpu.async_copy(buf_a, o_hbm.at[i0.at[w]], sem_s)
            s1 = pltpu.async_copy(buf_a, o_hbm.at[i1.at[w]], sem_s)
            gb.wait()
            s2 = pltpu.async_copy(buf_b, o_hbm.at[i0.at[w + 1]], sem_s)
            s3 = pltpu.async_copy(buf_b, o_hbm.at[i1.at[w + 1]], sem_s)
            s0.wait()
            s1.wait()
            s2.wait()
            s3.wait()

    win = lambda a: a.reshape(-1, SC_WINDOW)
    return run(h3, win(d0), win(d1), win(jnp.arange(t, dtype=I32)))


def _sc_gather(y3, d0, d1):
    t = d0.shape[0]
    n_cores, nwin = _sc_windows(t)
    mesh = plsc.VectorSubcoreMesh(core_axis_name="c", subcore_axis_name="s")
    rows = pltpu.VMEM((SC_WINDOW,) + y3.shape[1:], y3.dtype)
    idx = pltpu.VMEM((nwin, SC_WINDOW), I32)

    @functools.partial(
        pl.kernel, mesh=mesh,
        out_type=jax.ShapeDtypeStruct((2 * t,) + y3.shape[1:], y3.dtype),
        scratch_types=[idx] * 4 + [rows] * 4 + [pltpu.SemaphoreType.DMA] * 5,
        name="sc_gather")
    def run(y_hbm, d0_hbm, d1_hbm, ita_hbm, itb_hbm, o_hbm, i0, i1, ita, itb,
            b0, b1, b2, b3, m0, m1, m2, m3, ms):
        wbase = (lax.axis_index("s") * n_cores + lax.axis_index("c")) * nwin
        pltpu.sync_copy(d0_hbm.at[pl.ds(wbase, nwin)], i0)
        pltpu.sync_copy(d1_hbm.at[pl.ds(wbase, nwin)], i1)
        pltpu.sync_copy(ita_hbm.at[pl.ds(wbase, nwin)], ita)
        pltpu.sync_copy(itb_hbm.at[pl.ds(wbase, nwin)], itb)

        @pl.loop(0, nwin, step=2)
        def _(w):
            g0 = pltpu.async_copy(y_hbm.at[i0.at[w]], b0, m0)
            g1 = pltpu.async_copy(y_hbm.at[i1.at[w]], b1, m1)
            g2 = pltpu.async_copy(y_hbm.at[i0.at[w + 1]], b2, m2)
            g3 = pltpu.async_copy(y_hbm.at[i1.at[w + 1]], b3, m3)
            g0.wait()
            s0 = pltpu.async_copy(b0, o_hbm.at[ita.at[w]], ms)
            g1.wait()
            s1 = pltpu.async_copy(b1, o_hbm.at[itb.at[w]], ms)
            g2.wait()
            s2 = pltpu.async_copy(b2, o_hbm.at[ita.at[w + 1]], ms)
            g3.wait()
            s3 = pltpu.async_copy(b3, o_hbm.at[itb.at[w + 1]], ms)
            s0.wait()
            s1.wait()
            s2.wait()
            s3.wait()

    win = lambda a: a.reshape(-1, SC_WINDOW)
    iota = jnp.arange(2 * t, dtype=I32)
    return run(y3, win(d0), win(d1), win(iota[:t]), win(iota[t:]))


def _rope_lane_tables(seq_len):
    half = ROT_DIM // 2
    inv_freq = ROPE_THETA ** (-np.arange(0, ROT_DIM, 2, dtype=np.float64) / ROT_DIM)
    ang = np.arange(seq_len, dtype=np.float64)[:, None] * inv_freq[None, :]
    cos, sin = np.cos(ang), np.sin(ang)
    m = np.arange(LANES) % HEAD_DIM
    f = m % half
    c = np.where(m[None, :] < ROT_DIM, cos[:, f], 1.0)
    s1 = np.where(m[None, :] < half, -sin[:, f], 0.0)
    s2 = np.where((m[None, :] >= half) & (m[None, :] < ROT_DIM), sin[:, f], 0.0)
    return tuple(jnp.asarray(a, F32) for a in (c, s1, s2))


def _layer(x, norm1_g, w_in, q_norm_g, k_norm_g, attn_sink, conv_w, conv_b, w_attn_proj,
           w_conv_proj, w_out, norm2_g, w_rg, b_rg, w_re, b_re, w_gate, w_up, w_down):
    bsz, seq, d = x.shape
    t = bsz * seq
    tm = min(1024, seq)
    tq = min(512, seq)
    assert d == D_MODEL and seq % tq == 0 and t % tm == 0 and seq % tm == 0 and tq % WINDOW == 0
    xf = x.reshape(t, d)

    hpg = N_HEADS // N_KV_HEADS
    wq_p = (w_in[:, :ATTN_DIM].reshape(d, N_KV_HEADS, hpg, HEAD_DIM).transpose(0, 2, 1, 3)
            .reshape(d, ATTN_DIM))
    w_in_p = jnp.concatenate([wq_p, w_in[:, ATTN_DIM:]], axis=1).astype(BF16)
    wa_p = (w_attn_proj.reshape(N_KV_HEADS, hpg, HEAD_DIM, d).transpose(1, 0, 2, 3)
            .reshape(ATTN_DIM, d).astype(BF16))
    bd = jnp.asarray(np.kron(np.eye(N_HEADS), np.full((HEAD_DIM, HEAD_DIM), 1.0 / HEAD_DIM)), BF16)
    qg = jnp.tile(q_norm_g, LANES // HEAD_DIM).reshape(1, LANES)
    kg = jnp.tile(k_norm_g, LANES // HEAD_DIM).reshape(1, LANES)
    ctab, s1tab, s2tab = _rope_lane_tables(seq)
    nseq_t = seq // tm

    row = lambda w: pl.BlockSpec((tm, w), lambda i: (i, 0))
    full = lambda a: pl.BlockSpec(a.shape, lambda i: (0,) * a.ndim)
    tab = pl.BlockSpec((tm, LANES), lambda i: (i % nseq_t, 0))
    q, k, v, cb, u, ga, gc = pl.pallas_call(
        _proj_kernel,
        grid=(t // tm,),
        in_specs=[row(d), full(norm1_g.reshape(1, d)), full(w_in_p), full(bd), full(qg), full(kg),
                  tab, tab, tab],
        out_specs=[row(ATTN_DIM), row(KV_DIM), row(KV_DIM), row(CONV_DIM), row(CONV_DIM),
                   row(D_MODEL), row(D_MODEL)],
        out_shape=[jax.ShapeDtypeStruct((t, w), BF16)
                   for w in (ATTN_DIM, KV_DIM, KV_DIM, CONV_DIM, CONV_DIM, D_MODEL, D_MODEL)],
        compiler_params=pltpu.CompilerParams(dimension_semantics=("arbitrary",),
                                             vmem_limit_bytes=VMEM_LIMIT),
        name="proj",
    )(xf, norm1_g.reshape(1, d), w_in_p, bd, qg, kg, ctab, s1tab, s2tab)

    wr = jnp.concatenate([w_rg, jnp.zeros((d, 8 - N_GROUPS), F32), w_re,
                          jnp.zeros((d, LANES - ROUTER_ROWS), F32)], axis=1)
    wr_hi = wr.astype(BF16)
    wr2 = jnp.concatenate([wr_hi, (wr - wr_hi.astype(F32)).astype(BF16)], axis=1)
    br = jnp.concatenate([b_rg, jnp.zeros((8 - N_GROUPS,), F32), b_re,
                          jnp.zeros((LANES - ROUTER_ROWS,), F32)]).reshape(1, LANES)
    mix_params = (conv_w, conv_b.reshape(1, CONV_DIM), wa_p, w_conv_proj.astype(BF16), w_out.astype(BF16),
                  norm2_g.reshape(1, d), wr2, br)

    n_chunks = TOKEN_CHUNKS if bsz % TOKEN_CHUNKS == 0 else 1
    out = None
    for c in range(n_chunks):
        out = _chunk_tail(c, n_chunks, out, xf, (q, k, v, cb, u, ga, gc), attn_sink, mix_params,
                          (w_gate, w_up, w_down), bsz, seq, tq)
    return out.reshape(bsz, seq, d)


def _chunk_tail(c, n_chunks, out_prev, xf, acts, attn_sink, mix_params, expert_w, bsz, seq, tq):
    q, k, v, cb, u, ga, gc = acts
    d = D_MODEL
    t_all = bsz * seq
    bc = bsz // n_chunks
    b0 = c * bc
    t = bc * seq
    nj = seq // tq
    wpb = tq // WINDOW
    nb_seq = seq // WINDOW
    hpb = tq // 16
    row_in = lambda w: pl.BlockSpec((tq, w), lambda b, j: ((b0 + b) * nj + j, 0))
    row_out = lambda w: pl.BlockSpec((tq, w), lambda b, j: (b * nj + j, 0))
    prev_blk = pl.BlockSpec((WINDOW, KV_DIM),
                            lambda b, j: ((b0 + b) * nb_seq + jnp.maximum(j * wpb - 1, 0), 0))
    next_blk = pl.BlockSpec((WINDOW, KV_DIM),
                            lambda b, j: ((b0 + b) * nb_seq + jnp.minimum((j + 1) * wpb, nb_seq - 1), 0))
    uprev = pl.BlockSpec((16, CONV_DIM),
                         lambda b, j: (jnp.maximum(((b0 + b) * nj + j) * hpb - 1, 0), 0))
    unext = pl.BlockSpec((16, CONV_DIM),
                         lambda b, j: (jnp.minimum(((b0 + b) * nj + j + 1) * hpb, t_all // 16 - 1), 0))
    full2 = lambda a: pl.BlockSpec(a.shape, lambda b, j: (0,) * a.ndim)

    x1, h2, logits = pl.pallas_call(
        functools.partial(_mix_kernel, tq=tq),
        grid=(bc, nj),
        in_specs=[pl.BlockSpec(memory_space=pltpu.SMEM),
                  row_in(d), row_in(ATTN_DIM), prev_blk, row_in(KV_DIM), next_blk,
                  prev_blk, row_in(KV_DIM), next_blk,
                  row_in(CONV_DIM), uprev, row_in(CONV_DIM), unext, row_in(D_MODEL), row_in(D_MODEL)]
                 + [full2(p) for p in mix_params],
        out_specs=[row_out(d), pl.BlockSpec((tq * TOK_SUBLANES, LANES), lambda b, j: (b * nj + j, 0)),
                   row_out(LANES)],
        out_shape=[jax.ShapeDtypeStruct((t, d), F32),
                   jax.ShapeDtypeStruct((t * TOK_SUBLANES, LANES), U32),
                   jax.ShapeDtypeStruct((t, LANES), F32)],
        scratch_shapes=[pltpu.VMEM((tq + 16, CONV_DIM), F32), pltpu.VMEM((tq, ATTN_DIM), BF16)],
        compiler_params=pltpu.CompilerParams(dimension_semantics=("arbitrary", "arbitrary"),
                                             vmem_limit_bytes=VMEM_LIMIT),
        name="mix",
    )(attn_sink, xf, q, k, k, k, v, v, v, cb, u, u, u, ga, gc, *mix_params)

    tr = min(ROUTE_TILE, t)
    assert t % tr == 0 and tr % RANK_CHUNK == 0
    tri = jnp.asarray(np.triu(np.ones((RANK_CHUNK, RANK_CHUNK), np.float32), 1), BF16)
    colr = lambda r: pl.BlockSpec((r, tr), lambda i: (0, i))
    route_i, route_f, counts = pl.pallas_call(
        functools.partial(_route_kernel, tr=tr),
        grid=(t // tr,),
        in_specs=[pl.BlockSpec((tr, LANES), lambda i: (i, 0)), pl.BlockSpec(tri.shape, lambda i: (0, 0))],
        out_specs=[colr(8), colr(8), pl.BlockSpec((N_EXPERTS, LANES), lambda i: (0, 0))],
        out_shape=[jax.ShapeDtypeStruct((8, t), I32), jax.ShapeDtypeStruct((8, t), F32),
                   jax.ShapeDtypeStruct((N_EXPERTS, LANES), F32)],
        scratch_shapes=[pltpu.VMEM((N_EXPERTS, LANES), F32)],
        compiler_params=pltpu.CompilerParams(dimension_semantics=("arbitrary",),
                                             vmem_limit_bytes=VMEM_LIMIT),
        name="route",
    )(logits, tri)

    n_rows = 2 * t + N_EXPERTS * MOE_ROWS
    nblk = n_rows // MOE_ROWS
    cnt = counts[:, 0].astype(I32)
    padded = ((cnt + MOE_ROWS - 1) // MOE_ROWS) * MOE_ROWS
    pad_ends = jnp.cumsum(padded)
    pad_starts = pad_ends - padded
    eids = jnp.arange(N_EXPERTS, dtype=I32)[:, None, None]
    dest = jnp.sum(jnp.where(route_i[None, 0:2] == eids, pad_starts[:, None, None], 0), axis=0) \
        + route_i[2:4]
    blk_start = jnp.arange(nblk, dtype=I32) * MOE_ROWS
    block_e = jnp.minimum(jnp.sum((pad_ends[None, :] <= blk_start[:, None]).astype(I32), axis=1),
                          N_EXPERTS - 1)
    nused = (pad_ends[-1:] // MOE_ROWS).astype(I32)

    x_pad = _sc_dispatch(h2.reshape(t, TOK_SUBLANES, LANES), dest[0], dest[1], n_rows)
    x_pad = x_pad.reshape(n_rows * TOK_SUBLANES, LANES)

    w_gate, w_up, w_down = expert_w
    blk_rows = MOE_ROWS * TOK_SUBLANES
    wspec = lambda a, b: pl.BlockSpec((None, a, b), lambda i, be, nu: (be[i], 0, 0))
    y_pad = pl.pallas_call(
        _moe_kernel,
        grid_spec=pltpu.PrefetchScalarGridSpec(
            num_scalar_prefetch=2,
            grid=(nblk,),
            in_specs=[pl.BlockSpec((blk_rows, LANES), lambda i, be, nu: (jnp.minimum(i, nu[0] - 1), 0)),
                      wspec(d, D_FF), wspec(d, D_FF), wspec(D_FF, d)],
            out_specs=pl.BlockSpec((blk_rows, LANES),
                                   lambda i, be, nu: (jnp.where(i < nu[0], i, nblk - 1), 0)),
            scratch_shapes=[pltpu.VMEM((d, D_FF), BF16), pltpu.VMEM((d, D_FF), BF16),
                            pltpu.VMEM((D_FF, d), BF16)]),
        out_shape=jax.ShapeDtypeStruct((n_rows * TOK_SUBLANES, LANES), U32),
        compiler_params=pltpu.CompilerParams(dimension_semantics=("arbitrary",),
                                             vmem_limit_bytes=VMEM_LIMIT),
        name="moe",
    )(block_e, nused, x_pad, w_gate, w_up, w_down)

    yk = _sc_gather(y_pad.reshape(n_rows, TOK_SUBLANES, LANES), dest[0], dest[1])
    yk = yk.reshape(2 * t * TOK_SUBLANES, LANES)
    tc = min(1024, t)
    nt = t // tc
    gates_t = route_f[0:2].T
    in_specs = [pl.BlockSpec((tc, d), lambda i: (i, 0)),
                pl.BlockSpec((tc, 2), lambda i: (i, 0)),
                pl.BlockSpec((tc * TOK_SUBLANES, LANES), lambda i: (i, 0)),
                pl.BlockSpec((tc * TOK_SUBLANES, LANES), lambda i: (nt + i, 0))]
    args = [x1, gates_t, yk, yk]
    aliases = {}
    if out_prev is not None:
        in_specs.append(pl.BlockSpec(memory_space=pl.ANY))
        args.append(out_prev)
        aliases = {len(args) - 1: 0}
    return pl.pallas_call(
        functools.partial(_combine_kernel, tm=tc),
        grid=(nt,),
        in_specs=in_specs,
        out_specs=pl.BlockSpec((tc, d), lambda i: (c * nt + i, 0)),
        out_shape=jax.ShapeDtypeStruct((t_all, d), F32),
        input_output_aliases=aliases,
        compiler_params=pltpu.CompilerParams(dimension_semantics=("arbitrary",),
                                             vmem_limit_bytes=VMEM_LIMIT),
        name="combine",
    )(*args)


def kernel(x, norm1_g, w_in, q_norm_g, k_norm_g, attn_sink, conv_w, conv_b, w_attn_proj, w_conv_proj, w_out, norm2_g, w_router_group, b_router_group, w_router_expert, b_router_expert, w_gate_e, w_up_e, w_down_e):
    for l in range(norm1_g.shape[0]):
        x = _layer(x, norm1_g[l], w_in[l], q_norm_g[l], k_norm_g[l], attn_sink[l], conv_w[l],
                   conv_b[l], w_attn_proj[l], w_conv_proj[l], w_out[l], norm2_g[l],
                   w_router_group[l], b_router_group[l], w_router_expert[l], b_router_expert[l],
                   w_gate_e[l], w_up_e[l], w_down_e[l])
    return x
```

```python
import functools

import jax
import jax.numpy as jnp
import numpy as np
from jax import lax
from jax.experimental import pallas as pl
from jax.experimental.pallas import tpu as pltpu
from jax.experimental.pallas import tpu_sc as plsc

F32 = jnp.float32
BF16 = jnp.bfloat16
I32 = jnp.int32
U32 = jnp.uint32
HI_MASK = np.uint32(0xFFFF0000)

D_MODEL = 1024
N_HEADS = 8
N_KV_HEADS = 2
HEAD_DIM = 64
ATTN_DIM = N_HEADS * HEAD_DIM
KV_DIM = N_KV_HEADS * HEAD_DIM
WINDOW = 128
ROT_DIM = HEAD_DIM // 4
ROPE_THETA = 500000.0
CONV_DIM = D_MODEL // 2
N_GROUPS = 4
EXPERTS_PER_GROUP = 8
N_EXPERTS = N_GROUPS * EXPERTS_PER_GROUP
D_FF = D_MODEL // 4
NORM_EPS = 1e-6
MASK_VALUE = -1e30
NEG_BIG = -3.0e38
LOG2E = 1.4426950408889634

LANES = 128
TOK_SUBLANES = D_MODEL // (2 * LANES)
HALF_D = D_MODEL // 2
SC_WINDOW = 32
ROUTER_ROWS = 8 + N_EXPERTS
ROUTE_TILE = 2048
RANK_CHUNK = 256
MOE_ROWS = 512
TOKEN_CHUNKS = 1
VMEM_LIMIT = 56 * 1024 * 1024

_Q0, _K0, _V0, _CB0, _CC0, _CX0, _GA0, _GC0, _END = (
    0, 512, 640, 768, 1280, 1792, 2304, 3328, 4352)


def _store_packed_rows(ref, val, n):
    for c in range(TOK_SUBLANES):
        lo = val[:, c * LANES:(c + 1) * LANES].astype(BF16).astype(F32)
        hi = val[:, HALF_D + c * LANES:HALF_D + (c + 1) * LANES].astype(BF16).astype(F32)
        word = (lax.bitcast_convert_type(lo, U32) >> 16) | (lax.bitcast_convert_type(hi, U32) & HI_MASK)
        ref[pl.ds(c, n, stride=TOK_SUBLANES), :] = word


def _load_packed_rows(ref, n):
    words = [ref[pl.ds(c, n, stride=TOK_SUBLANES), :] for c in range(TOK_SUBLANES)]
    lo = [lax.bitcast_convert_type(w << 16, F32) for w in words]
    hi = [lax.bitcast_convert_type(w & HI_MASK, F32) for w in words]
    return lo + hi


def _rope_chunks(t, c, s1, s2):
    outs = []
    for j in range(t.shape[1] // LANES):
        tj = t[:, j * LANES:(j + 1) * LANES]
        outs.append(tj * c + pltpu.roll(tj, LANES - ROT_DIM // 2, 1) * s1
                    + pltpu.roll(tj, ROT_DIM // 2, 1) * s2)
    return outs[0] if len(outs) == 1 else jnp.concatenate(outs, axis=1)


def _proj_kernel(x_ref, g1_ref, w_ref, wq_ref, bd_ref, qg_ref, kg_ref, c_ref, s1_ref, s2_ref,
                 q_ref, k_ref, v_ref, cb_ref, u_ref, ga_ref, gc_ref):
    x = x_ref[...]
    ms = jnp.mean(x * x, axis=-1, keepdims=True)
    h = (x * lax.rsqrt(ms + NORM_EPS) * g1_ref[...]).astype(BF16)

    def proj(a, b):
        return jnp.dot(h, w_ref[:, a:b], preferred_element_type=F32)

    c, s1, s2 = c_ref[...], s1_ref[...], s2_ref[...]

    q = jnp.dot(h, wq_ref[...], preferred_element_type=F32)
    qms = jnp.dot((q * q).astype(BF16), bd_ref[...], preferred_element_type=F32)
    qg = jnp.concatenate([qg_ref[...]] * (ATTN_DIM // LANES), axis=1)
    qn = q * lax.rsqrt(qms + NORM_EPS) * qg
    q_ref[...] = (_rope_chunks(qn, c, s1, s2) * (HEAD_DIM ** -0.5 * LOG2E)).astype(BF16)

    k = proj(_K0, _V0)
    kms = jnp.dot((k * k).astype(BF16), bd_ref[0:KV_DIM, 0:KV_DIM], preferred_element_type=F32)
    kn = k * lax.rsqrt(kms + NORM_EPS) * kg_ref[...]
    k_ref[...] = _rope_chunks(kn, c, s1, s2).astype(BF16)

    v_ref[...] = proj(_V0, _CB0).astype(BF16)
    cb_ref[...] = proj(_CB0, _CC0).astype(BF16)
    u_ref[...] = (proj(_CC0, _CX0) * proj(_CX0, _GA0)).astype(BF16)
    ga_ref[...] = jax.nn.sigmoid(proj(_GA0, _GC0)).astype(BF16)
    gc_ref[...] = jax.nn.sigmoid(proj(_GC0, _END)).astype(BF16)


def _mix_kernel(sink_ref, x_ref, q_ref, kp_ref, kc_ref, kn_ref, vp_ref, vc_ref, vn_ref,
                cb_ref, up_ref, uc_ref, un_ref, ga_ref, gc_ref,
                cw_ref, cbias_ref, wa_ref, wc_ref, wo_ref, g2_ref,
                wr_ref, br_ref,
                x1_ref, h2_ref, lg_ref,
                ubuf, attn_scr, *, tq):
    j = pl.program_id(1)
    nj = pl.num_programs(1)
    nsub = tq // WINDOW

    kbuf = jnp.concatenate([kp_ref[...], kc_ref[...], kn_ref[...]], axis=0)
    vbuf = jnp.concatenate([vp_ref[...], vc_ref[...], vn_ref[...]], axis=0)
    low_half = lax.broadcasted_iota(I32, (WINDOW, LANES), 1) < HEAD_DIM
    low_half3 = lax.broadcasted_iota(I32, (3 * WINDOW, LANES), 1) < HEAD_DIM
    qrow = lax.broadcasted_iota(I32, (WINDOW, WINDOW), 0)
    kcol = lax.broadcasted_iota(I32, (WINDOW, WINDOW), 1)
    bias_lo = jnp.where(kcol >= qrow, 0.0, MASK_VALUE)
    bias_hi = jnp.where(kcol <= qrow, 0.0, MASK_VALUE)
    one = jnp.ones((3 * WINDOW, LANES), BF16)
    for n in range(nsub):
        blo = jnp.where(j > 0, bias_lo, MASK_VALUE) if n == 0 else bias_lo
        bhi = jnp.where(j < nj - 1, bias_hi, MASK_VALUE) if n == nsub - 1 else bias_hi
        qs = q_ref[n * WINDOW:(n + 1) * WINDOW, :]
        zero = jnp.zeros((WINDOW, LANES), BF16)
        q_stack = []
        for hd in range(N_HEADS):
            col = qs[:, (hd % 4) * LANES:(hd % 4 + 1) * LANES]
            keep = low_half if hd < 4 else jnp.logical_not(low_half)
            q_stack.append(jnp.where(keep, col, zero))
        q_stack = jnp.concatenate(q_stack, axis=0)
        kwin = kbuf[n * WINDOW:(n + 3) * WINDOW, :]
        vwin = vbuf[n * WINDOW:(n + 3) * WINDOW, :]
        s = lax.dot_general(q_stack, kwin, (((1,), (1,)), ((), ())),
                            preferred_element_type=F32)
        p_list, es_list = [], []
        for hd in range(N_HEADS):
            sink = sink_ref[hd] * LOG2E
            rows = slice(hd * WINDOW, (hd + 1) * WINDOW)
            s0 = s[rows, 0:WINDOW] + blo
            s1 = s[rows, WINDOW:2 * WINDOW]
            s2 = s[rows, 2 * WINDOW:3 * WINDOW] + bhi
            m = jnp.max(jnp.maximum(jnp.maximum(s0, s1), s2), axis=-1, keepdims=True)
            m = jnp.maximum(m, sink)
            p_list.append(jnp.concatenate(
                [jnp.exp2(s0 - m).astype(BF16), jnp.exp2(s1 - m).astype(BF16),
                 jnp.exp2(s2 - m).astype(BF16)], axis=1))
            es_list.append(jnp.exp2(sink - m))
        half = N_HEADS // 2
        o_lo = jnp.dot(jnp.concatenate(p_list[:half], axis=0), jnp.where(low_half3, vwin, one),
                       preferred_element_type=F32)
        o_hi = jnp.dot(jnp.concatenate(p_list[half:], axis=0), jnp.where(low_half3, one, vwin),
                       preferred_element_type=F32)
        cols = []
        for c4 in range(half):
            rows = slice(c4 * WINDOW, (c4 + 1) * WINDOW)
            oa, ob = o_lo[rows, :], o_hi[rows, :]
            da = pltpu.roll(oa, HEAD_DIM, 1) + es_list[c4]
            db = pltpu.roll(ob, HEAD_DIM, 1) + es_list[c4 + half]
            cols.append(jnp.where(low_half, oa / da, ob / db))
        attn_scr[n * WINDOW:(n + 1) * WINDOW, :] = jnp.concatenate(cols, axis=1).astype(BF16)
    attn = attn_scr[...]

    ubuf[8:8 + tq, :] = uc_ref[...].astype(F32)
    prev_rows = up_ref[...].astype(F32)
    next_rows = un_ref[...].astype(F32)
    ubuf[7:8, :] = prev_rows[15:16, :] * (j > 0).astype(F32)
    ubuf[8 + tq:9 + tq, :] = next_rows[0:1, :] * (j < nj - 1).astype(F32)
    cw = cw_ref[...]
    conv = (ubuf[7:7 + tq, :] * cw[0:1, :] + ubuf[8:8 + tq, :] * cw[1:2, :]
            + ubuf[9:9 + tq, :] * cw[2:3, :] + cbias_ref[...])
    cgate = (cb_ref[...].astype(F32) * conv).astype(BF16)

    merged = (ga_ref[...].astype(F32) * jnp.dot(attn, wa_ref[...], preferred_element_type=F32)
              + gc_ref[...].astype(F32) * jnp.dot(cgate, wc_ref[...], preferred_element_type=F32))
    x1 = x_ref[...] + jnp.dot(merged.astype(BF16), wo_ref[...], preferred_element_type=F32)
    x1_ref[...] = x1

    ms = jnp.mean(x1 * x1, axis=-1, keepdims=True)
    h2 = x1 * lax.rsqrt(ms + NORM_EPS) * g2_ref[...]
    _store_packed_rows(h2_ref, h2, tq)
    h_hi = h2.astype(BF16)
    h_lo = (h2 - h_hi.astype(F32)).astype(BF16)
    acc = jnp.dot(h_hi, wr_ref[...], preferred_element_type=F32)
    logits = (acc[:, 0:LANES] + acc[:, LANES:2 * LANES]
              + jnp.dot(h_lo, wr_ref[:, 0:LANES], preferred_element_type=F32) + br_ref[...])
    lg_ref[...] = logits


def _route_kernel(lg_ref, tri_ref, ri_ref, rf_ref, cnt_ref, base_ref, *, tr):
    @pl.when(pl.program_id(0) == 0)
    def _():
        base_ref[...] = jnp.zeros_like(base_ref)

    logits = lg_ref[...].T[0:ROUTER_ROWS, :]
    grow = lax.broadcasted_iota(I32, (8, tr), 0)
    gl = jnp.where(grow < N_GROUPS, logits[0:8, :], NEG_BIG)
    gmax = jnp.max(gl, axis=0, keepdims=True)
    grp = jnp.min(jnp.where(gl == gmax, grow, 8), axis=0, keepdims=True)
    p_grp = 1.0 / jnp.sum(jnp.exp(gl - gmax), axis=0, keepdims=True)
    el = logits[8:ROUTER_ROWS, :]
    erow = lax.broadcasted_iota(I32, (N_EXPERTS, tr), 0)
    in_grp = (erow // EXPERTS_PER_GROUP) == grp
    l_a = jnp.where(in_grp, el, NEG_BIG)
    l1 = jnp.max(l_a, axis=0, keepdims=True)
    i1 = jnp.min(jnp.where(l_a == l1, erow, N_EXPERTS), axis=0, keepdims=True)
    l_b = jnp.where(erow == i1, NEG_BIG, l_a)
    l2 = jnp.max(l_b, axis=0, keepdims=True)
    i2 = jnp.min(jnp.where(l_b == l2, erow, N_EXPERTS), axis=0, keepdims=True)
    r21 = jnp.exp(l2 - l1)
    t1 = 1.0 / (1.0 + r21)
    gate1 = p_grp * t1
    gate2 = p_grp * (r21 * t1)
    oh1 = (erow == i1).astype(F32)
    oh2 = (erow == i2).astype(F32)
    oh = oh1 + oh2
    base = base_ref[...][:, 0:1]
    r1, r2 = [], []
    for c in range(tr // RANK_CHUNK):
        cols = slice(c * RANK_CHUNK, (c + 1) * RANK_CHUNK)
        ohc = oh[:, cols]
        before = jnp.dot(ohc.astype(BF16), tri_ref[...], preferred_element_type=F32) + base
        r1.append(jnp.sum(oh1[:, cols] * before, axis=0, keepdims=True))
        r2.append(jnp.sum(oh2[:, cols] * before, axis=0, keepdims=True))
        base = base + jnp.sum(ohc, axis=1, keepdims=True)
    rank1 = jnp.concatenate(r1, axis=1)
    rank2 = jnp.concatenate(r2, axis=1)
    new_base = jnp.broadcast_to(base, base_ref.shape)
    base_ref[...] = new_base
    cnt_ref[...] = new_base
    zi = jnp.zeros((4, tr), I32)
    ri_ref[...] = jnp.concatenate([i1, i2, rank1.astype(I32), rank2.astype(I32), zi], axis=0)
    rf_ref[...] = jnp.concatenate([gate1, gate2, jnp.zeros((LANES - 2, tr), F32)], axis=0).T


def _moe_kernel(be_ref, nused_ref, x_ref, wg_ref, wu_ref, wd_ref, y_ref, wg_s, wu_s, wd_s):
    i = pl.program_id(0)

    changed = (i == 0) | (be_ref[i] != be_ref[jnp.maximum(i - 1, 0)])

    def narrow(_, carry):
        wg_s[...] = wg_ref[...].astype(BF16)
        wu_s[...] = wu_ref[...].astype(BF16)
        wd_s[...] = wd_ref[...].astype(BF16)
        return carry

    lax.fori_loop(0, changed.astype(I32), narrow, 0)

    @pl.when(i < nused_ref[0])
    def _():
        xb = jnp.concatenate([c.astype(BF16) for c in _load_packed_rows(x_ref, MOE_ROWS)], axis=1)
        g = jnp.dot(xb, wg_s[...], preferred_element_type=F32)
        u = jnp.dot(xb, wu_s[...], preferred_element_type=F32)
        hid = (g * jax.nn.sigmoid(g) * u).astype(BF16)
        _store_packed_rows(y_ref, jnp.dot(hid, wd_s[...], preferred_element_type=F32), MOE_ROWS)

    @pl.when(i >= nused_ref[0])
    def _():
        y_ref[...] = jnp.zeros_like(y_ref)


def _combine_kernel(x1_ref, gate_ref, y0_ref, y1_ref, *rest, tm):
    o_ref = rest[-1]
    g = gate_ref[...]
    g0, g1 = g[:, 0:1], g[:, 1:2]
    y0 = _load_packed_rows(y0_ref, tm)
    y1 = _load_packed_rows(y1_ref, tm)
    for c in range(2 * TOK_SUBLANES):
        cols = slice(c * LANES, (c + 1) * LANES)
        o_ref[:, cols] = x1_ref[:, cols] + g0 * y0[c] + g1 * y1[c]


def _sc_windows(n_tokens):
    info = plsc.get_sparse_core_info()
    n_workers = info.num_cores * info.num_subcores
    assert n_tokens % (n_workers * SC_WINDOW * 2) == 0
    return info.num_cores, n_tokens // (n_workers * SC_WINDOW)


def _sc_dispatch(h3, d0, d1, n_rows):
    t = h3.shape[0]
    n_cores, nwin = _sc_windows(t)
    mesh = plsc.VectorSubcoreMesh(core_axis_name="c", subcore_axis_name="s")
    rows = pltpu.VMEM((SC_WINDOW,) + h3.shape[1:], h3.dtype)
    idx = pltpu.VMEM((nwin, SC_WINDOW), I32)

    @functools.partial(
        pl.kernel, mesh=mesh,
        out_type=jax.ShapeDtypeStruct((n_rows,) + h3.shape[1:], h3.dtype),
        scratch_types=[idx, idx, idx, rows, rows] + [pltpu.SemaphoreType.DMA] * 3,
        name="sc_dispatch")
    def run(h_hbm, d0_hbm, d1_hbm, it_hbm, o_hbm, i0, i1, it, buf_a, buf_b, sem_a, sem_b, sem_s):
        wbase = (lax.axis_index("s") * n_cores + lax.axis_index("c")) * nwin
        pltpu.sync_copy(d0_hbm.at[pl.ds(wbase, nwin)], i0)
        pltpu.sync_copy(d1_hbm.at[pl.ds(wbase, nwin)], i1)
        pltpu.sync_copy(it_hbm.at[pl.ds(wbase, nwin)], it)

        @pl.loop(0, nwin, step=2)
        def _(w):
            ga = pltpu.async_copy(h_hbm.at[it.at[w]], buf_a, sem_a)
            gb = pltpu.async_copy(h_hbm.at[it.at[w + 1]], buf_b, sem_b)
            ga.wait()
            s0 = pltpu.async_copy(buf_a, o_hbm.at[i0.at[w]], sem_s)
            s1 = pltpu.async_copy(buf_a, o_hbm.at[i1.at[w]], sem_s)
            gb.wait()
            s2 = pltpu.async_copy(buf_b, o_hbm.at[i0.at[w + 1]], sem_s)
            s3 = pltpu.async_copy(buf_b, o_hbm.at[i1.at[w + 1]], sem_s)
            s0.wait()
            s1.wait()
            s2.wait()
            s3.wait()

    win = lambda a: a.reshape(-1, SC_WINDOW)
    return run(h3, win(d0), win(d1), win(jnp.arange(t, dtype=I32)))


def _sc_gather(y3, d0, d1):
    t = d0.shape[0]
    n_cores, nwin = _sc_windows(t)
    mesh = plsc.VectorSubcoreMesh(core_axis_name="c", subcore_axis_name="s")
    rows = pltpu.VMEM((SC_WINDOW,) + y3.shape[1:], y3.dtype)
    idx = pltpu.VMEM((nwin, SC_WINDOW), I32)

    @functools.partial(
        pl.kernel, mesh=mesh,
        out_type=jax.ShapeDtypeStruct((2 * t,) + y3.shape[1:], y3.dtype),
        scratch_types=[idx] * 4 + [rows] * 4 + [pltpu.SemaphoreType.DMA] * 5,
        name="sc_gather")
    def run(y_hbm, d0_hbm, d1_hbm, ita_hbm, itb_hbm, o_hbm, i0, i1, ita, itb,
            b0, b1, b2, b3, m0, m1, m2, m3, ms):
        wbase = (lax.axis_index("s") * n_cores + lax.axis_index("c")) * nwin
        pltpu.sync_copy(d0_hbm.at[pl.ds(wbase, nwin)], i0)
        pltpu.sync_copy(d1_hbm.at[pl.ds(wbase, nwin)], i1)
        pltpu.sync_copy(ita_hbm.at[pl.ds(wbase, nwin)], ita)
        pltpu.sync_copy(itb_hbm.at[pl.ds(wbase, nwin)], itb)

        @pl.loop(0, nwin, step=2)
        def _(w):
            g0 = pltpu.async_copy(y_hbm.at[i0.at[w]], b0, m0)
            g1 = pltpu.async_copy(y_hbm.at[i1.at[w]], b1, m1)
            g2 = pltpu.async_copy(y_hbm.at[i0.at[w + 1]], b2, m2)
            g3 = pltpu.async_copy(y_hbm.at[i1.at[w + 1]], b3, m3)
            g0.wait()
            s0 = pltpu.async_copy(b0, o_hbm.at[ita.at[w]], ms)
            g1.wait()
            s1 = pltpu.async_copy(b1, o_hbm.at[itb.at[w]], ms)
            g2.wait()
            s2 = pltpu.async_copy(b2, o_hbm.at[ita.at[w + 1]], ms)
            g3.wait()
            s3 = pltpu.async_copy(b3, o_hbm.at[itb.at[w + 1]], ms)
            s0.wait()
            s1.wait()
            s2.wait()
            s3.wait()

    win = lambda a: a.reshape(-1, SC_WINDOW)
    iota = jnp.arange(2 * t, dtype=I32)
    return run(y3, win(d0), win(d1), win(iota[:t]), win(iota[t:]))


def _rope_lane_tables(seq_len):
    half = ROT_DIM // 2
    inv_freq = ROPE_THETA ** (-np.arange(0, ROT_DIM, 2, dtype=np.float64) / ROT_DIM)
    ang = np.arange(seq_len, dtype=np.float64)[:, None] * inv_freq[None, :]
    cos, sin = np.cos(ang), np.sin(ang)
    m = np.arange(LANES) % HEAD_DIM
    f = m % half
    c = np.where(m[None, :] < ROT_DIM, cos[:, f], 1.0)
    s1 = np.where(m[None, :] < half, -sin[:, f], 0.0)
    s2 = np.where((m[None, :] >= half) & (m[None, :] < ROT_DIM), sin[:, f], 0.0)
    return tuple(jnp.asarray(a, F32) for a in (c, s1, s2))


def _layer(x, norm1_g, w_in, q_norm_g, k_norm_g, attn_sink, conv_w, conv_b, w_attn_proj,
           w_conv_proj, w_out, norm2_g, w_rg, b_rg, w_re, b_re, w_gate, w_up, w_down):
    bsz, seq, d = x.shape
    t = bsz * seq
    tm = min(512, seq)
    tq = min(512, seq)
    assert d == D_MODEL and seq % tq == 0 and t % tm == 0 and seq % tm == 0 and tq % WINDOW == 0
    xf = x.reshape(t, d)

    hpg = N_HEADS // N_KV_HEADS
    wq_p = (w_in[:, :ATTN_DIM].reshape(d, N_KV_HEADS, hpg, HEAD_DIM).transpose(0, 2, 1, 3)
            .reshape(d, ATTN_DIM))
    wq_p = wq_p.astype(BF16)
    w_in_b = w_in.astype(BF16)
    wa_p = (w_attn_proj.reshape(N_KV_HEADS, hpg, HEAD_DIM, d).transpose(1, 0, 2, 3)
            .reshape(ATTN_DIM, d).astype(BF16))
    bd = jnp.asarray(np.kron(np.eye(N_HEADS), np.full((HEAD_DIM, HEAD_DIM), 1.0 / HEAD_DIM)), BF16)
    qg = jnp.tile(q_norm_g, LANES // HEAD_DIM).reshape(1, LANES)
    kg = jnp.tile(k_norm_g, LANES // HEAD_DIM).reshape(1, LANES)
    ctab, s1tab, s2tab = _rope_lane_tables(seq)
    nseq_t = seq // tm

    row = lambda w: pl.BlockSpec((tm, w), lambda i: (i, 0))
    full = lambda a: pl.BlockSpec(a.shape, lambda i: (0,) * a.ndim)
    tab = pl.BlockSpec((tm, LANES), lambda i: (i % nseq_t, 0))
    q, k, v, cb, u, ga, gc = pl.pallas_call(
        _proj_kernel,
        grid=(t // tm,),
        in_specs=[row(d), full(norm1_g.reshape(1, d)), full(w_in_b), full(wq_p), full(bd), full(qg), full(kg),
                  tab, tab, tab],
        out_specs=[row(ATTN_DIM), row(KV_DIM), row(KV_DIM), row(CONV_DIM), row(CONV_DIM),
                   row(D_MODEL), row(D_MODEL)],
        out_shape=[jax.ShapeDtypeStruct((t, w), BF16)
                   for w in (ATTN_DIM, KV_DIM, KV_DIM, CONV_DIM, CONV_DIM, D_MODEL, D_MODEL)],
        compiler_params=pltpu.CompilerParams(dimension_semantics=("arbitrary",),
                                             vmem_limit_bytes=VMEM_LIMIT),
        name="proj",
    )(xf, norm1_g.reshape(1, d), w_in_b, wq_p, bd, qg, kg, ctab, s1tab, s2tab)

    wr = jnp.concatenate([w_rg, jnp.zeros((d, 8 - N_GROUPS), F32), w_re,
                          jnp.zeros((d, LANES - ROUTER_ROWS), F32)], axis=1)
    wr_hi = wr.astype(BF16)
    wr2 = jnp.concatenate([wr_hi, (wr - wr_hi.astype(F32)).astype(BF16)], axis=1)
    br = jnp.concatenate([b_rg, jnp.zeros((8 - N_GROUPS,), F32), b_re,
                          jnp.zeros((LANES - ROUTER_ROWS,), F32)]).reshape(1, LANES)
    mix_params = (conv_w, conv_b.reshape(1, CONV_DIM), wa_p, w_conv_proj.astype(BF16), w_out.astype(BF16),
                  norm2_g.reshape(1, d), wr2, br)

    n_chunks = TOKEN_CHUNKS if bsz % TOKEN_CHUNKS == 0 else 1
    out = None
    for c in range(n_chunks):
        out = _chunk_tail(c, n_chunks, out, xf, (q, k, v, cb, u, ga, gc), attn_sink, mix_params,
                          (w_gate, w_up, w_down), bsz, seq, tq)
    return out.reshape(bsz, seq, d)


def _chunk_tail(c, n_chunks, out_prev, xf, acts, attn_sink, mix_params, expert_w, bsz, seq, tq):
    q, k, v, cb, u, ga, gc = acts
    d = D_MODEL
    t_all = bsz * seq
    bc = bsz // n_chunks
    b0 = c * bc
    t = bc * seq
    nj = seq // tq
    wpb = tq // WINDOW
    nb_seq = seq // WINDOW
    hpb = tq // 16
    row_in = lambda w: pl.BlockSpec((tq, w), lambda b, j: ((b0 + b) * nj + j, 0))
    row_out = lambda w: pl.BlockSpec((tq, w), lambda b, j: (b * nj + j, 0))
    prev_blk = pl.BlockSpec((WINDOW, KV_DIM),
                            lambda b, j: ((b0 + b) * nb_seq + jnp.maximum(j * wpb - 1, 0), 0))
    next_blk = pl.BlockSpec((WINDOW, KV_DIM),
                            lambda b, j: ((b0 + b) * nb_seq + jnp.minimum((j + 1) * wpb, nb_seq - 1), 0))
    uprev = pl.BlockSpec((16, CONV_DIM),
                         lambda b, j: (jnp.maximum(((b0 + b) * nj + j) * hpb - 1, 0), 0))
    unext = pl.BlockSpec((16, CONV_DIM),
                         lambda b, j: (jnp.minimum(((b0 + b) * nj + j + 1) * hpb, t_all // 16 - 1), 0))
    full2 = lambda a: pl.BlockSpec(a.shape, lambda b, j: (0,) * a.ndim)

    x1, h2, logits = pl.pallas_call(
        functools.partial(_mix_kernel, tq=tq),
        grid=(bc, nj),
        in_specs=[pl.BlockSpec(memory_space=pltpu.SMEM),
                  row_in(d), row_in(ATTN_DIM), prev_blk, row_in(KV_DIM), next_blk,
                  prev_blk, row_in(KV_DIM), next_blk,
                  row_in(CONV_DIM), uprev, row_in(CONV_DIM), unext, row_in(D_MODEL), row_in(D_MODEL)]
                 + [full2(p) for p in mix_params],
        out_specs=[row_out(d), pl.BlockSpec((tq * TOK_SUBLANES, LANES), lambda b, j: (b * nj + j, 0)),
                   row_out(LANES)],
        out_shape=[jax.ShapeDtypeStruct((t, d), F32),
                   jax.ShapeDtypeStruct((t * TOK_SUBLANES, LANES), U32),
                   jax.ShapeDtypeStruct((t, LANES), F32)],
        scratch_shapes=[pltpu.VMEM((tq + 16, CONV_DIM), F32), pltpu.VMEM((tq, ATTN_DIM), BF16)],
        compiler_params=pltpu.CompilerParams(dimension_semantics=("arbitrary", "arbitrary"),
                                             vmem_limit_bytes=VMEM_LIMIT),
        name="mix",
    )(attn_sink, xf, q, k, k, k, v, v, v, cb, u, u, u, ga, gc, *mix_params)

    tr = min(ROUTE_TILE, t)
    assert t % tr == 0 and tr % RANK_CHUNK == 0
    tri = jnp.asarray(np.triu(np.ones((RANK_CHUNK, RANK_CHUNK), np.float32), 1), BF16)
    colr = lambda r: pl.BlockSpec((r, tr), lambda i: (0, i))
    route_i, route_f, counts = pl.pallas_call(
        functools.partial(_route_kernel, tr=tr),
        grid=(t // tr,),
        in_specs=[pl.BlockSpec((tr, LANES), lambda i: (i, 0)), pl.BlockSpec(tri.shape, lambda i: (0, 0))],
        out_specs=[colr(8), pl.BlockSpec((tr, LANES), lambda i: (i, 0)),
                   pl.BlockSpec((N_EXPERTS, LANES), lambda i: (0, 0))],
        out_shape=[jax.ShapeDtypeStruct((8, t), I32), jax.ShapeDtypeStruct((t, LANES), F32),
                   jax.ShapeDtypeStruct((N_EXPERTS, LANES), F32)],
        scratch_shapes=[pltpu.VMEM((N_EXPERTS, LANES), F32)],
        compiler_params=pltpu.CompilerParams(dimension_semantics=("arbitrary",),
                                             vmem_limit_bytes=VMEM_LIMIT),
        name="route",
    )(logits, tri)

    n_rows = 2 * t + N_EXPERTS * MOE_ROWS
    nblk = n_rows // MOE_ROWS
    cnt = counts[:, 0].astype(I32)
    padded = ((cnt + MOE_ROWS - 1) // MOE_ROWS) * MOE_ROWS
    pad_ends = jnp.cumsum(padded)
    pad_starts = pad_ends - padded
    eids = jnp.arange(N_EXPERTS, dtype=I32)[:, None, None]
    dest = jnp.sum(jnp.where(route_i[None, 0:2] == eids, pad_starts[:, None, None], 0), axis=0) \
        + route_i[2:4]
    blk_start = jnp.arange(nblk, dtype=I32) * MOE_ROWS
    block_e = jnp.minimum(jnp.sum((pad_ends[None, :] <= blk_start[:, None]).astype(I32), axis=1),
                          N_EXPERTS - 1)
    nused = (pad_ends[-1:] // MOE_ROWS).astype(I32)

    x_pad = _sc_dispatch(h2.reshape(t, TOK_SUBLANES, LANES), dest[0], dest[1], n_rows)
    x_pad = x_pad.reshape(n_rows * TOK_SUBLANES, LANES)

    w_gate, w_up, w_down = expert_w
    blk_rows = MOE_ROWS * TOK_SUBLANES
    wspec = lambda a, b: pl.BlockSpec((None, a, b), lambda i, be, nu: (be[i], 0, 0))
    y_pad = pl.pallas_call(
        _moe_kernel,
        grid_spec=pltpu.PrefetchScalarGridSpec(
            num_scalar_prefetch=2,
            grid=(nblk,),
            in_specs=[pl.BlockSpec((blk_rows, LANES), lambda i, be, nu: (jnp.minimum(i, nu[0] - 1), 0)),
                      wspec(d, D_FF), wspec(d, D_FF), wspec(D_FF, d)],
            out_specs=pl.BlockSpec((blk_rows, LANES),
                                   lambda i, be, nu: (jnp.where(i < nu[0], i, nblk - 1), 0)),
            scratch_shapes=[pltpu.VMEM((d, D_FF), BF16), pltpu.VMEM((d, D_FF), BF16),
                            pltpu.VMEM((D_FF, d), BF16)]),
        out_shape=jax.ShapeDtypeStruct((n_rows * TOK_SUBLANES, LANES), U32),
        compiler_params=pltpu.CompilerParams(dimension_semantics=("arbitrary",),
                                             vmem_limit_bytes=VMEM_LIMIT),
        name="moe",
    )(block_e, nused, x_pad, w_gate, w_up, w_down)

    yk = _sc_gather(y_pad.reshape(n_rows, TOK_SUBLANES, LANES), dest[0], dest[1])
    yk = yk.reshape(2 * t * TOK_SUBLANES, LANES)
    tc = min(512, t)
    nt = t // tc
    in_specs = [pl.BlockSpec((tc, d), lambda i: (i, 0)),
                pl.BlockSpec((tc, LANES), lambda i: (i, 0)),
                pl.BlockSpec((tc * TOK_SUBLANES, LANES), lambda i: (i, 0)),
                pl.BlockSpec((tc * TOK_SUBLANES, LANES), lambda i: (nt + i, 0))]
    args = [x1, route_f, yk, yk]
    aliases = {}
    if out_prev is not None:
        in_specs.append(pl.BlockSpec(memory_space=pl.ANY))
        args.append(out_prev)
        aliases = {len(args) - 1: 0}
    return pl.pallas_call(
        functools.partial(_combine_kernel, tm=tc),
        grid=(nt,),
        in_specs=in_specs,
        out_specs=pl.BlockSpec((tc, d), lambda i: (c * nt + i, 0)),
        out_shape=jax.ShapeDtypeStruct((t_all, d), F32),
        input_output_aliases=aliases,
        compiler_params=pltpu.CompilerParams(dimension_semantics=("arbitrary",),
                                             vmem_limit_bytes=VMEM_LIMIT),
        name="combine",
    )(*args)


def kernel(x, norm1_g, w_in, q_norm_g, k_norm_g, attn_sink, conv_w, conv_b, w_attn_proj, w_conv_proj, w_out, norm2_g, w_router_group, b_router_group, w_router_expert, b_router_expert, w_gate_e, w_up_e, w_down_e):
    for l in range(norm1_g.shape[0]):
        x = _layer(x, norm1_g[l], w_in[l], q_norm_g[l], k_norm_g[l], attn_sink[l], conv_w[l],
                   conv_b[l], w_attn_proj[l], w_conv_proj[l], w_out[l], norm2_g[l],
                   w_router_group[l], b_router_group[l], w_router_expert[l], b_router_expert[l],
                   w_gate_e[l], w_up_e[l], w_down_e[l])
    return x
```

```python
import functools

import jax
import jax.numpy as jnp
import numpy as np
from jax import lax
from jax.experimental import pallas as pl
from jax.experimental.pallas import tpu as pltpu
from jax.experimental.pallas import tpu_sc as plsc

F32 = jnp.float32
BF16 = jnp.bfloat16
I32 = jnp.int32
U32 = jnp.uint32
HI_MASK = np.uint32(0xFFFF0000)

D_MODEL = 1024
N_HEADS = 8
N_KV_HEADS = 2
HEAD_DIM = 64
ATTN_DIM = N_HEADS * HEAD_DIM
KV_DIM = N_KV_HEADS * HEAD_DIM
WINDOW = 128
ROT_DIM = HEAD_DIM // 4
ROPE_THETA = 500000.0
CONV_DIM = D_MODEL // 2
N_GROUPS = 4
EXPERTS_PER_GROUP = 8
N_EXPERTS = N_GROUPS * EXPERTS_PER_GROUP
D_FF = D_MODEL // 4
NORM_EPS = 1e-6
MASK_VALUE = -1e30
NEG_BIG = -3.0e38
LOG2E = 1.4426950408889634

LANES = 128
TOK_SUBLANES = D_MODEL // (2 * LANES)
HALF_D = D_MODEL // 2
SC_WINDOW = 32
ROUTER_ROWS = 8 + N_EXPERTS
ROUTE_TILE = 2048
RANK_CHUNK = 256
MOE_ROWS = 512
TOKEN_CHUNKS = 1
VMEM_LIMIT = 56 * 1024 * 1024

_Q0, _K0, _V0, _CB0, _CC0, _CX0, _GA0, _GC0, _END = (
    0, 512, 640, 768, 1280, 1792, 2304, 3328, 4352)


def _store_packed_rows(ref, val, n):
    for c in range(TOK_SUBLANES):
        lo = val[:, c * LANES:(c + 1) * LANES].astype(BF16).astype(F32)
        hi = val[:, HALF_D + c * LANES:HALF_D + (c + 1) * LANES].astype(BF16).astype(F32)
        word = (lax.bitcast_convert_type(lo, U32) >> 16) | (lax.bitcast_convert_type(hi, U32) & HI_MASK)
        ref[pl.ds(c, n, stride=TOK_SUBLANES), :] = word


def _load_packed_rows(ref, n):
    words = [ref[pl.ds(c, n, stride=TOK_SUBLANES), :] for c in range(TOK_SUBLANES)]
    lo = [lax.bitcast_convert_type(w << 16, F32) for w in words]
    hi = [lax.bitcast_convert_type(w & HI_MASK, F32) for w in words]
    return lo + hi


def _rope_chunks(t, c, s1, s2):
    outs = []
    for j in range(t.shape[1] // LANES):
        tj = t[:, j * LANES:(j + 1) * LANES]
        outs.append(tj * c + pltpu.roll(tj, LANES - ROT_DIM // 2, 1) * s1
                    + pltpu.roll(tj, ROT_DIM // 2, 1) * s2)
    return outs[0] if len(outs) == 1 else jnp.concatenate(outs, axis=1)


def _proj_kernel(x_ref, g1_ref, w_ref, wq_ref, bd_ref, qg_ref, kg_ref, c_ref, s1_ref, s2_ref,
                 q_ref, k_ref, v_ref, cb_ref, u_ref, ga_ref, gc_ref):
    x = x_ref[...]
    ms = jnp.mean(x * x, axis=-1, keepdims=True)
    h = (x * lax.rsqrt(ms + NORM_EPS) * g1_ref[...]).astype(BF16)

    def proj(a, b):
        return jnp.dot(h, w_ref[:, a:b], preferred_element_type=F32)

    c, s1, s2 = c_ref[...], s1_ref[...], s2_ref[...]

    q = jnp.dot(h, wq_ref[...], preferred_element_type=F32)
    qms = jnp.dot((q * q).astype(BF16), bd_ref[...], preferred_element_type=F32)
    qg = jnp.concatenate([qg_ref[...]] * (ATTN_DIM // LANES), axis=1)
    qn = q * lax.rsqrt(qms + NORM_EPS) * qg
    q_ref[...] = (_rope_chunks(qn, c, s1, s2) * (HEAD_DIM ** -0.5 * LOG2E)).astype(BF16)

    k = proj(_K0, _V0)
    kms = jnp.dot((k * k).astype(BF16), bd_ref[0:KV_DIM, 0:KV_DIM], preferred_element_type=F32)
    kn = k * lax.rsqrt(kms + NORM_EPS) * kg_ref[...]
    k_ref[...] = _rope_chunks(kn, c, s1, s2).astype(BF16)

    v_ref[...] = proj(_V0, _CB0).astype(BF16)
    cb_ref[...] = proj(_CB0, _CC0).astype(BF16)
    u_ref[...] = (proj(_CC0, _CX0) * proj(_CX0, _GA0)).astype(BF16)
    ga_ref[...] = jax.nn.sigmoid(proj(_GA0, _GC0)).astype(BF16)
    gc_ref[...] = jax.nn.sigmoid(proj(_GC0, _END)).astype(BF16)


def _mix_kernel(sink_ref, x_ref, q_ref, kp_ref, kc_ref, kn_ref, vp_ref, vc_ref, vn_ref,
                cb_ref, up_ref, uc_ref, un_ref, ga_ref, gc_ref,
                cw_ref, cbias_ref, wa_ref, wc_ref, wo_ref, g2_ref,
                wr_ref, br_ref,
                x1_ref, h2_ref, lg_ref,
                ubuf, attn_scr, *, tq):
    j = pl.program_id(1)
    nj = pl.num_programs(1)
    nsub = tq // WINDOW

    kbuf = jnp.concatenate([kp_ref[...], kc_ref[...], kn_ref[...]], axis=0)
    vbuf = jnp.concatenate([vp_ref[...], vc_ref[...], vn_ref[...]], axis=0)
    low_half = lax.broadcasted_iota(I32, (WINDOW, LANES), 1) < HEAD_DIM
    low_half3 = lax.broadcasted_iota(I32, (3 * WINDOW, LANES), 1) < HEAD_DIM
    qrow = lax.broadcasted_iota(I32, (WINDOW, WINDOW), 0)
    kcol = lax.broadcasted_iota(I32, (WINDOW, WINDOW), 1)
    bias_lo = jnp.where(kcol >= qrow, 0.0, MASK_VALUE)
    bias_hi = jnp.where(kcol <= qrow, 0.0, MASK_VALUE)
    one = jnp.ones((3 * WINDOW, LANES), BF16)
    for n in range(nsub):
        blo = jnp.where(j > 0, bias_lo, MASK_VALUE) if n == 0 else bias_lo
        bhi = jnp.where(j < nj - 1, bias_hi, MASK_VALUE) if n == nsub - 1 else bias_hi
        qs = q_ref[n * WINDOW:(n + 1) * WINDOW, :]
        zero = jnp.zeros((WINDOW, LANES), BF16)
        q_stack = []
        for hd in range(N_HEADS):
            col = qs[:, (hd % 4) * LANES:(hd % 4 + 1) * LANES]
            keep = low_half if hd < 4 else jnp.logical_not(low_half)
            q_stack.append(jnp.where(keep, col, zero))
        q_stack = jnp.concatenate(q_stack, axis=0)
        kwin = kbuf[n * WINDOW:(n + 3) * WINDOW, :]
        vwin = vbuf[n * WINDOW:(n + 3) * WINDOW, :]
        s = lax.dot_general(q_stack, kwin, (((1,), (1,)), ((), ())),
                            preferred_element_type=F32)
        p_list, es_list = [], []
        for hd in range(N_HEADS):
            sink = sink_ref[hd] * LOG2E
            rows = slice(hd * WINDOW, (hd + 1) * WINDOW)
            s0 = s[rows, 0:WINDOW] + blo
            s1 = s[rows, WINDOW:2 * WINDOW]
            s2 = s[rows, 2 * WINDOW:3 * WINDOW] + bhi
            m = jnp.max(jnp.maximum(jnp.maximum(s0, s1), s2), axis=-1, keepdims=True)
            m = jnp.maximum(m, sink)
            p_list.append(jnp.concatenate(
                [jnp.exp2(s0 - m).astype(BF16), jnp.exp2(s1 - m).astype(BF16),
                 jnp.exp2(s2 - m).astype(BF16)], axis=1))
            es_list.append(jnp.exp2(sink - m))
        half = N_HEADS // 2
        o_lo = jnp.dot(jnp.concatenate(p_list[:half], axis=0), jnp.where(low_half3, vwin, one),
                       preferred_element_type=F32)
        o_hi = jnp.dot(jnp.concatenate(p_list[half:], axis=0), jnp.where(low_half3, one, vwin),
                       preferred_element_type=F32)
        cols = []
        for c4 in range(half):
            rows = slice(c4 * WINDOW, (c4 + 1) * WINDOW)
            oa, ob = o_lo[rows, :], o_hi[rows, :]
            da = pltpu.roll(oa, HEAD_DIM, 1) + es_list[c4]
            db = pltpu.roll(ob, HEAD_DIM, 1) + es_list[c4 + half]
            cols.append(jnp.where(low_half, oa / da, ob / db))
        attn_scr[n * WINDOW:(n + 1) * WINDOW, :] = jnp.concatenate(cols, axis=1).astype(BF16)
    attn = attn_scr[...]

    ubuf[8:8 + tq, :] = uc_ref[...].astype(F32)
    prev_rows = up_ref[...].astype(F32)
    next_rows = un_ref[...].astype(F32)
    ubuf[7:8, :] = prev_rows[15:16, :] * (j > 0).astype(F32)
    ubuf[8 + tq:9 + tq, :] = next_rows[0:1, :] * (j < nj - 1).astype(F32)
    cw = cw_ref[...]
    conv = (ubuf[7:7 + tq, :] * cw[0:1, :] + ubuf[8:8 + tq, :] * cw[1:2, :]
            + ubuf[9:9 + tq, :] * cw[2:3, :] + cbias_ref[...])
    cgate = (cb_ref[...].astype(F32) * conv).astype(BF16)

    merged = (ga_ref[...].astype(F32) * jnp.dot(attn, wa_ref[...], preferred_element_type=F32)
              + gc_ref[...].astype(F32) * jnp.dot(cgate, wc_ref[...], preferred_element_type=F32))
    x1 = x_ref[...] + jnp.dot(merged.astype(BF16), wo_ref[...], preferred_element_type=F32)
    x1_ref[...] = x1

    ms = jnp.mean(x1 * x1, axis=-1, keepdims=True)
    h2 = x1 * lax.rsqrt(ms + NORM_EPS) * g2_ref[...]
    _store_packed_rows(h2_ref, h2, tq)
    h_hi = h2.astype(BF16)
    h_lo = (h2 - h_hi.astype(F32)).astype(BF16)
    acc = jnp.dot(h_hi, wr_ref[...], preferred_element_type=F32)
    logits = (acc[:, 0:LANES] + acc[:, LANES:2 * LANES]
              + jnp.dot(h_lo, wr_ref[:, 0:LANES], preferred_element_type=F32) + br_ref[...])
    lg_ref[...] = logits


def _route_kernel(lg_ref, tri_ref, ri_ref, rf_ref, cnt_ref, base_ref, *, tr):
    @pl.when(pl.program_id(0) == 0)
    def _():
        base_ref[...] = jnp.zeros_like(base_ref)

    logits = lg_ref[...].T[0:ROUTER_ROWS, :]
    grow = lax.broadcasted_iota(I32, (8, tr), 0)
    gl = jnp.where(grow < N_GROUPS, logits[0:8, :], NEG_BIG)
    gmax = jnp.max(gl, axis=0, keepdims=True)
    grp = jnp.min(jnp.where(gl == gmax, grow, 8), axis=0, keepdims=True)
    p_grp = 1.0 / jnp.sum(jnp.exp(gl - gmax), axis=0, keepdims=True)
    el = logits[8:ROUTER_ROWS, :]
    erow = lax.broadcasted_iota(I32, (N_EXPERTS, tr), 0)
    in_grp = (erow // EXPERTS_PER_GROUP) == grp
    l_a = jnp.where(in_grp, el, NEG_BIG)
    l1 = jnp.max(l_a, axis=0, keepdims=True)
    i1 = jnp.min(jnp.where(l_a == l1, erow, N_EXPERTS), axis=0, keepdims=True)
    l_b = jnp.where(erow == i1, NEG_BIG, l_a)
    l2 = jnp.max(l_b, axis=0, keepdims=True)
    i2 = jnp.min(jnp.where(l_b == l2, erow, N_EXPERTS), axis=0, keepdims=True)
    r21 = jnp.exp(l2 - l1)
    t1 = 1.0 / (1.0 + r21)
    gate1 = p_grp * t1
    gate2 = p_grp * (r21 * t1)
    oh1 = (erow == i1).astype(F32)
    oh2 = (erow == i2).astype(F32)
    oh = oh1 + oh2
    base = base_ref[...][:, 0:1]
    r1, r2 = [], []
    for c in range(tr // RANK_CHUNK):
        cols = slice(c * RANK_CHUNK, (c + 1) * RANK_CHUNK)
        ohc = oh[:, cols]
        before = jnp.dot(ohc.astype(BF16), tri_ref[...], preferred_element_type=F32) + base
        r1.append(jnp.sum(oh1[:, cols] * before, axis=0, keepdims=True))
        r2.append(jnp.sum(oh2[:, cols] * before, axis=0, keepdims=True))
        base = base + jnp.sum(ohc, axis=1, keepdims=True)
    rank1 = jnp.concatenate(r1, axis=1)
    rank2 = jnp.concatenate(r2, axis=1)
    new_base = jnp.broadcast_to(base, base_ref.shape)
    base_ref[...] = new_base
    cnt_ref[...] = new_base
    zi = jnp.zeros((4, tr), I32)
    ri_ref[...] = jnp.concatenate([i1, i2, rank1.astype(I32), rank2.astype(I32), zi], axis=0)
    rf_ref[...] = jnp.concatenate([gate1, gate2, jnp.zeros((LANES - 2, tr), F32)], axis=0).T


def _moe_kernel(be_ref, nused_ref, x_ref, wg_ref, wu_ref, wd_ref, y_ref, wg_s, wu_s, wd_s):
    i = pl.program_id(0)

    changed = (i == 0) | (be_ref[i] != be_ref[jnp.maximum(i - 1, 0)])

    def narrow(_, carry):
        wg_s[...] = wg_ref[...].astype(BF16)
        wu_s[...] = wu_ref[...].astype(BF16)
        wd_s[...] = wd_ref[...].astype(BF16)
        return carry

    lax.fori_loop(0, changed.astype(I32), narrow, 0)

    @pl.when(i < nused_ref[0])
    def _():
        xb = jnp.concatenate([c.astype(BF16) for c in _load_packed_rows(x_ref, MOE_ROWS)], axis=1)
        g = jnp.dot(xb, wg_s[...], preferred_element_type=F32)
        u = jnp.dot(xb, wu_s[...], preferred_element_type=F32)
        hid = (g * jax.nn.sigmoid(g) * u).astype(BF16)
        _store_packed_rows(y_ref, jnp.dot(hid, wd_s[...], preferred_element_type=F32), MOE_ROWS)

    @pl.when(i >= nused_ref[0])
    def _():
        y_ref[...] = jnp.zeros_like(y_ref)


def _combine_kernel(x1_ref, gate_ref, y0_ref, y1_ref, *rest, tm):
    o_ref = rest[-1]
    g = gate_ref[...]
    g0, g1 = g[:, 0:1], g[:, 1:2]
    y0 = _load_packed_rows(y0_ref, tm)
    y1 = _load_packed_rows(y1_ref, tm)
    for c in range(2 * TOK_SUBLANES):
        cols = slice(c * LANES, (c + 1) * LANES)
        o_ref[:, cols] = x1_ref[:, cols] + g0 * y0[c] + g1 * y1[c]


def _sc_windows(n_tokens):
    info = plsc.get_sparse_core_info()
    n_workers = info.num_cores * info.num_subcores
    assert n_tokens % (n_workers * SC_WINDOW * 2) == 0
    return info.num_cores, n_tokens // (n_workers * SC_WINDOW)


def _sc_dispatch(h3, d0, d1, n_rows):
    t = h3.shape[0]
    n_cores, nwin = _sc_windows(t)
    mesh = plsc.VectorSubcoreMesh(core_axis_name="c", subcore_axis_name="s")
    rows = pltpu.VMEM((SC_WINDOW,) + h3.shape[1:], h3.dtype)
    idx = pltpu.VMEM((nwin, SC_WINDOW), I32)

    @functools.partial(
        pl.kernel, mesh=mesh,
        out_type=jax.ShapeDtypeStruct((n_rows,) + h3.shape[1:], h3.dtype),
        scratch_types=[idx, idx, idx, rows, rows] + [pltpu.SemaphoreType.DMA] * 3,
        name="sc_dispatch")
    def run(h_hbm, d0_hbm, d1_hbm, it_hbm, o_hbm, i0, i1, it, buf_a, buf_b, sem_a, sem_b, sem_s):
        wbase = (lax.axis_index("s") * n_cores + lax.axis_index("c")) * nwin
        pltpu.sync_copy(d0_hbm.at[pl.ds(wbase, nwin)], i0)
        pltpu.sync_copy(d1_hbm.at[pl.ds(wbase, nwin)], i1)
        pltpu.sync_copy(it_hbm.at[pl.ds(wbase, nwin)], it)

        @pl.loop(0, nwin, step=2)
        def _(w):
            ga = pltpu.async_copy(h_hbm.at[it.at[w]], buf_a, sem_a)
            gb = pltpu.async_copy(h_hbm.at[it.at[w + 1]], buf_b, sem_b)
            ga.wait()
            s0 = pltpu.async_copy(buf_a, o_hbm.at[i0.at[w]], sem_s)
            s1 = pltpu.async_copy(buf_a, o_hbm.at[i1.at[w]], sem_s)
            gb.wait()
            s2 = pltpu.async_copy(buf_b, o_hbm.at[i0.at[w + 1]], sem_s)
            s3 = pltpu.async_copy(buf_b, o_hbm.at[i1.at[w + 1]], sem_s)
            s0.wait()
            s1.wait()
            s2.wait()
            s3.wait()

    win = lambda a: a.reshape(-1, SC_WINDOW)
    return run(h3, win(d0), win(d1), win(jnp.arange(t, dtype=I32)))


def _sc_gather(y3, d0, d1):
    t = d0.shape[0]
    n_cores, nwin = _sc_windows(t)
    mesh = plsc.VectorSubcoreMesh(core_axis_name="c", subcore_axis_name="s")
    rows = pltpu.VMEM((SC_WINDOW,) + y3.shape[1:], y3.dtype)
    idx = pltpu.VMEM((nwin, SC_WINDOW), I32)

    @functools.partial(
        pl.kernel, mesh=mesh,
        out_type=jax.ShapeDtypeStruct((2 * t,) + y3.shape[1:], y3.dtype),
        scratch_types=[idx] * 4 + [rows] * 4 + [pltpu.SemaphoreType.DMA] * 5,
        name="sc_gather")
    def run(y_hbm, d0_hbm, d1_hbm, ita_hbm, itb_hbm, o_hbm, i0, i1, ita, itb,
            b0, b1, b2, b3, m0, m1, m2, m3, ms):
        wbase = (lax.axis_index("s") * n_cores + lax.axis_index("c")) * nwin
        pltpu.sync_copy(d0_hbm.at[pl.ds(wbase, nwin)], i0)
        pltpu.sync_copy(d1_hbm.at[pl.ds(wbase, nwin)], i1)
        pltpu.sync_copy(ita_hbm.at[pl.ds(wbase, nwin)], ita)
        pltpu.sync_copy(itb_hbm.at[pl.ds(wbase, nwin)], itb)

        @pl.loop(0, nwin, step=2)
        def _(w):
            g0 = pltpu.async_copy(y_hbm.at[i0.at[w]], b0, m0)
            g1 = pltpu.async_copy(y_hbm.at[i1.at[w]], b1, m1)
            g2 = pltpu.async_copy(y_hbm.at[i0.at[w + 1]], b2, m2)
            g3 = pltpu.async_copy(y_hbm.at[i1.at[w + 1]], b3, m3)
            g0.wait()
            s0 = pltpu.async_copy(b0, o_hbm.at[ita.at[w]], ms)
            g1.wait()
            s1 = pltpu.async_copy(b1, o_hbm.at[itb.at[w]], ms)
            g2.wait()
            s2 = pltpu.async_copy(b2, o_hbm.at[ita.at[w + 1]], ms)
            g3.wait()
            s3 = pltpu.async_copy(b3, o_hbm.at[itb.at[w + 1]], ms)
            s0.wait()
            s1.wait()
            s2.wait()
            s3.wait()

    win = lambda a: a.reshape(-1, SC_WINDOW)
    iota = jnp.arange(2 * t, dtype=I32)
    return run(y3, win(d0), win(d1), win(iota[:t]), win(iota[t:]))


def _rope_lane_tables(seq_len):
    half = ROT_DIM // 2
    inv_freq = ROPE_THETA ** (-np.arange(0, ROT_DIM, 2, dtype=np.float64) / ROT_DIM)
    ang = np.arange(seq_len, dtype=np.float64)[:, None] * inv_freq[None, :]
    cos, sin = np.cos(ang), np.sin(ang)
    m = np.arange(LANES) % HEAD_DIM
    f = m % half
    c = np.where(m[None, :] < ROT_DIM, cos[:, f], 1.0)
    s1 = np.where(m[None, :] < half, -sin[:, f], 0.0)
    s2 = np.where((m[None, :] >= half) & (m[None, :] < ROT_DIM), sin[:, f], 0.0)
    return tuple(jnp.asarray(a, F32) for a in (c, s1, s2))


def _layer(x, norm1_g, w_in, q_norm_g, k_norm_g, attn_sink, conv_w, conv_b, w_attn_proj,
           w_conv_proj, w_out, norm2_g, w_rg, b_rg, w_re, b_re, w_gate, w_up, w_down):
    bsz, seq, d = x.shape
    t = bsz * seq
    tm = min(512, seq)
    tq = min(1024, seq)
    assert d == D_MODEL and seq % tq == 0 and t % tm == 0 and seq % tm == 0 and tq % WINDOW == 0
    xf = x.reshape(t, d)

    hpg = N_HEADS // N_KV_HEADS
    wq_p = (w_in[:, :ATTN_DIM].reshape(d, N_KV_HEADS, hpg, HEAD_DIM).transpose(0, 2, 1, 3)
            .reshape(d, ATTN_DIM))
    wq_p = wq_p.astype(BF16)
    w_in_b = w_in.astype(BF16)
    wa_p = (w_attn_proj.reshape(N_KV_HEADS, hpg, HEAD_DIM, d).transpose(1, 0, 2, 3)
            .reshape(ATTN_DIM, d).astype(BF16))
    bd = jnp.asarray(np.kron(np.eye(N_HEADS), np.full((HEAD_DIM, HEAD_DIM), 1.0 / HEAD_DIM)), BF16)
    qg = jnp.tile(q_norm_g, LANES // HEAD_DIM).reshape(1, LANES)
    kg = jnp.tile(k_norm_g, LANES // HEAD_DIM).reshape(1, LANES)
    ctab, s1tab, s2tab = _rope_lane_tables(seq)
    nseq_t = seq // tm

    row = lambda w: pl.BlockSpec((tm, w), lambda i: (i, 0))
    full = lambda a: pl.BlockSpec(a.shape, lambda i: (0,) * a.ndim)
    tab = pl.BlockSpec((tm, LANES), lambda i: (i % nseq_t, 0))
    q, k, v, cb, u, ga, gc = pl.pallas_call(
        _proj_kernel,
        grid=(t // tm,),
        in_specs=[row(d), full(norm1_g.reshape(1, d)), full(w_in_b), full(wq_p), full(bd), full(qg), full(kg),
                  tab, tab, tab],
        out_specs=[row(ATTN_DIM), row(KV_DIM), row(KV_DIM), row(CONV_DIM), row(CONV_DIM),
                   row(D_MODEL), row(D_MODEL)],
        out_shape=[jax.ShapeDtypeStruct((t, w), BF16)
                   for w in (ATTN_DIM, KV_DIM, KV_DIM, CONV_DIM, CONV_DIM, D_MODEL, D_MODEL)],
        compiler_params=pltpu.CompilerParams(dimension_semantics=("arbitrary",),
                                             vmem_limit_bytes=VMEM_LIMIT),
        name="proj",
    )(xf, norm1_g.reshape(1, d), w_in_b, wq_p, bd, qg, kg, ctab, s1tab, s2tab)

    wr = jnp.concatenate([w_rg, jnp.zeros((d, 8 - N_GROUPS), F32), w_re,
                          jnp.zeros((d, LANES - ROUTER_ROWS), F32)], axis=1)
    wr_hi = wr.astype(BF16)
    wr2 = jnp.concatenate([wr_hi, (wr - wr_hi.astype(F32)).astype(BF16)], axis=1)
    br = jnp.concatenate([b_rg, jnp.zeros((8 - N_GROUPS,), F32), b_re,
                          jnp.zeros((LANES - ROUTER_ROWS,), F32)]).reshape(1, LANES)
    mix_params = (conv_w, conv_b.reshape(1, CONV_DIM), wa_p, w_conv_proj.astype(BF16), w_out.astype(BF16),
                  norm2_g.reshape(1, d), wr2, br)

    n_chunks = TOKEN_CHUNKS if bsz % TOKEN_CHUNKS == 0 else 1
    out = None
    for c in range(n_chunks):
        out = _chunk_tail(c, n_chunks, out, xf, (q, k, v, cb, u, ga, gc), attn_sink, mix_params,
                          (w_gate, w_up, w_down), bsz, seq, tq)
    return out.reshape(bsz, seq, d)


def _chunk_tail(c, n_chunks, out_prev, xf, acts, attn_sink, mix_params, expert_w, bsz, seq, tq):
    q, k, v, cb, u, ga, gc = acts
    d = D_MODEL
    t_all = bsz * seq
    bc = bsz // n_chunks
    b0 = c * bc
    t = bc * seq
    nj = seq // tq
    wpb = tq // WINDOW
    nb_seq = seq // WINDOW
    hpb = tq // 16
    row_in = lambda w: pl.BlockSpec((tq, w), lambda b, j: ((b0 + b) * nj + j, 0))
    row_out = lambda w: pl.BlockSpec((tq, w), lambda b, j: (b * nj + j, 0))
    prev_blk = pl.BlockSpec((WINDOW, KV_DIM),
                            lambda b, j: ((b0 + b) * nb_seq + jnp.maximum(j * wpb - 1, 0), 0))
    next_blk = pl.BlockSpec((WINDOW, KV_DIM),
                            lambda b, j: ((b0 + b) * nb_seq + jnp.minimum((j + 1) * wpb, nb_seq - 1), 0))
    uprev = pl.BlockSpec((16, CONV_DIM),
                         lambda b, j: (jnp.maximum(((b0 + b) * nj + j) * hpb - 1, 0), 0))
    unext = pl.BlockSpec((16, CONV_DIM),
                         lambda b, j: (jnp.minimum(((b0 + b) * nj + j + 1) * hpb, t_all // 16 - 1), 0))
    full2 = lambda a: pl.BlockSpec(a.shape, lambda b, j: (0,) * a.ndim)

    x1, h2, logits = pl.pallas_call(
        functools.partial(_mix_kernel, tq=tq),
        grid=(bc, nj),
        in_specs=[pl.BlockSpec(memory_space=pltpu.SMEM),
                  row_in(d), row_in(ATTN_DIM), prev_blk, row_in(KV_DIM), next_blk,
                  prev_blk, row_in(KV_DIM), next_blk,
                  row_in(CONV_DIM), uprev, row_in(CONV_DIM), unext, row_in(D_MODEL), row_in(D_MODEL)]
                 + [full2(p) for p in mix_params],
        out_specs=[row_out(d), pl.BlockSpec((tq * TOK_SUBLANES, LANES), lambda b, j: (b * nj + j, 0)),
                   row_out(LANES)],
        out_shape=[jax.ShapeDtypeStruct((t, d), F32),
                   jax.ShapeDtypeStruct((t * TOK_SUBLANES, LANES), U32),
                   jax.ShapeDtypeStruct((t, LANES), F32)],
        scratch_shapes=[pltpu.VMEM((tq + 16, CONV_DIM), F32), pltpu.VMEM((tq, ATTN_DIM), BF16)],
        compiler_params=pltpu.CompilerParams(dimension_semantics=("arbitrary", "arbitrary"),
                                             vmem_limit_bytes=VMEM_LIMIT),
        name="mix",
    )(attn_sink, xf, q, k, k, k, v, v, v, cb, u, u, u, ga, gc, *mix_params)

    tr = min(ROUTE_TILE, t)
    assert t % tr == 0 and tr % RANK_CHUNK == 0
    tri = jnp.asarray(np.triu(np.ones((RANK_CHUNK, RANK_CHUNK), np.float32), 1), BF16)
    colr = lambda r: pl.BlockSpec((r, tr), lambda i: (0, i))
    route_i, route_f, counts = pl.pallas_call(
        functools.partial(_route_kernel, tr=tr),
        grid=(t // tr,),
        in_specs=[pl.BlockSpec((tr, LANES), lambda i: (i, 0)), pl.BlockSpec(tri.shape, lambda i: (0, 0))],
        out_specs=[colr(8), pl.BlockSpec((tr, LANES), lambda i: (i, 0)),
                   pl.BlockSpec((N_EXPERTS, LANES), lambda i: (0, 0))],
        out_shape=[jax.ShapeDtypeStruct((8, t), I32), jax.ShapeDtypeStruct((t, LANES), F32),
                   jax.ShapeDtypeStruct((N_EXPERTS, LANES), F32)],
        scratch_shapes=[pltpu.VMEM((N_EXPERTS, LANES), F32)],
        compiler_params=pltpu.CompilerParams(dimension_semantics=("arbitrary",),
                                             vmem_limit_bytes=VMEM_LIMIT),
        name="route",
    )(logits, tri)

    n_rows = 2 * t + N_EXPERTS * MOE_ROWS
    nblk = n_rows // MOE_ROWS
    cnt = counts[:, 0].astype(I32)
    padded = ((cnt + MOE_ROWS - 1) // MOE_ROWS) * MOE_ROWS
    pad_ends = jnp.cumsum(padded)
    pad_starts = pad_ends - padded
    eids = jnp.arange(N_EXPERTS, dtype=I32)[:, None, None]
    dest = jnp.sum(jnp.where(route_i[None, 0:2] == eids, pad_starts[:, None, None], 0), axis=0) \
        + route_i[2:4]
    blk_start = jnp.arange(nblk, dtype=I32) * MOE_ROWS
    block_e = jnp.minimum(jnp.sum((pad_ends[None, :] <= blk_start[:, None]).astype(I32), axis=1),
                          N_EXPERTS - 1)
    nused = (pad_ends[-1:] // MOE_ROWS).astype(I32)

    x_pad = _sc_dispatch(h2.reshape(t, TOK_SUBLANES, LANES), dest[0], dest[1], n_rows)
    x_pad = x_pad.reshape(n_rows * TOK_SUBLANES, LANES)

    w_gate, w_up, w_down = expert_w
    blk_rows = MOE_ROWS * TOK_SUBLANES
    wspec = lambda a, b: pl.BlockSpec((None, a, b), lambda i, be, nu: (be[i], 0, 0))
    y_pad = pl.pallas_call(
        _moe_kernel,
        grid_spec=pltpu.PrefetchScalarGridSpec(
            num_scalar_prefetch=2,
            grid=(nblk,),
            in_specs=[pl.BlockSpec((blk_rows, LANES), lambda i, be, nu: (jnp.minimum(i, nu[0] - 1), 0)),
                      wspec(d, D_FF), wspec(d, D_FF), wspec(D_FF, d)],
            out_specs=pl.BlockSpec((blk_rows, LANES),
                                   lambda i, be, nu: (jnp.where(i < nu[0], i, nblk - 1), 0)),
            scratch_shapes=[pltpu.VMEM((d, D_FF), BF16), pltpu.VMEM((d, D_FF), BF16),
                            pltpu.VMEM((D_FF, d), BF16)]),
        out_shape=jax.ShapeDtypeStruct((n_rows * TOK_SUBLANES, LANES), U32),
        compiler_params=pltpu.CompilerParams(dimension_semantics=("arbitrary",),
                                             vmem_limit_bytes=VMEM_LIMIT),
        name="moe",
    )(block_e, nused, x_pad, w_gate, w_up, w_down)

    yk = _sc_gather(y_pad.reshape(n_rows, TOK_SUBLANES, LANES), dest[0], dest[1])
    yk = yk.reshape(2 * t * TOK_SUBLANES, LANES)
    tc = min(512, t)
    nt = t // tc
    in_specs = [pl.BlockSpec((tc, d), lambda i: (i, 0)),
                pl.BlockSpec((tc, LANES), lambda i: (i, 0)),
                pl.BlockSpec((tc * TOK_SUBLANES, LANES), lambda i: (i, 0)),
                pl.BlockSpec((tc * TOK_SUBLANES, LANES), lambda i: (nt + i, 0))]
    args = [x1, route_f, yk, yk]
    aliases = {}
    if out_prev is not None:
        in_specs.append(pl.BlockSpec(memory_space=pl.ANY))
        args.append(out_prev)
        aliases = {len(args) - 1: 0}
    return pl.pallas_call(
        functools.partial(_combine_kernel, tm=tc),
        grid=(nt,),
        in_specs=in_specs,
        out_specs=pl.BlockSpec((tc, d), lambda i: (c * nt + i, 0)),
        out_shape=jax.ShapeDtypeStruct((t_all, d), F32),
        input_output_aliases=aliases,
        compiler_params=pltpu.CompilerParams(dimension_semantics=("arbitrary",),
                                             vmem_limit_bytes=VMEM_LIMIT),
        name="combine",
    )(*args)


def kernel(x, norm1_g, w_in, q_norm_g, k_norm_g, attn_sink, conv_w, conv_b, w_attn_proj, w_conv_proj, w_out, norm2_g, w_router_group, b_router_group, w_router_expert, b_router_expert, w_gate_e, w_up_e, w_down_e):
    for l in range(norm1_g.shape[0]):
        x = _layer(x, norm1_g[l], w_in[l], q_norm_g[l], k_norm_g[l], attn_sink[l], conv_w[l],
                   conv_b[l], w_attn_proj[l], w_conv_proj[l], w_out[l], norm2_g[l],
                   w_router_group[l], b_router_group[l], w_router_expert[l], b_router_expert[l],
                   w_gate_e[l], w_up_e[l], w_down_e[l])
    return x
```

```python
import functools

import jax
import jax.numpy as jnp
import numpy as np
from jax import lax
from jax.experimental import pallas as pl
from jax.experimental.pallas import tpu as pltpu
from jax.experimental.pallas import tpu_sc as plsc

F32 = jnp.float32
BF16 = jnp.bfloat16
I32 = jnp.int32
U32 = jnp.uint32
HI_MASK = np.uint32(0xFFFF0000)

D_MODEL = 1024
N_HEADS = 8
N_KV_HEADS = 2
HEAD_DIM = 64
ATTN_DIM = N_HEADS * HEAD_DIM
KV_DIM = N_KV_HEADS * HEAD_DIM
WINDOW = 128
ROT_DIM = HEAD_DIM // 4
ROPE_THETA = 500000.0
CONV_DIM = D_MODEL // 2
N_GROUPS = 4
EXPERTS_PER_GROUP = 8
N_EXPERTS = N_GROUPS * EXPERTS_PER_GROUP
D_FF = D_MODEL // 4
NORM_EPS = 1e-6
MASK_VALUE = -1e30
NEG_BIG = -3.0e38
LOG2E = 1.4426950408889634

LANES = 128
TOK_SUBLANES = D_MODEL // (2 * LANES)
HALF_D = D_MODEL // 2
SC_WINDOW = 32
ROUTER_ROWS = 8 + N_EXPERTS
ROUTE_TILE = 4096
RANK_CHUNK = 256
MOE_ROWS = 512
TOKEN_CHUNKS = 1
COMBINE_PARTS = 2
VMEM_LIMIT = 56 * 1024 * 1024

_Q0, _K0, _V0, _CB0, _CC0, _CX0, _GA0, _GC0, _END = (
    0, 512, 640, 768, 1280, 1792, 2304, 3328, 4352)


def _store_packed_rows(ref, val, n):
    for c in range(TOK_SUBLANES):
        lo = val[:, c * LANES:(c + 1) * LANES].astype(BF16).astype(F32)
        hi = val[:, HALF_D + c * LANES:HALF_D + (c + 1) * LANES].astype(BF16).astype(F32)
        word = (lax.bitcast_convert_type(lo, U32) >> 16) | (lax.bitcast_convert_type(hi, U32) & HI_MASK)
        ref[pl.ds(c, n, stride=TOK_SUBLANES), :] = word


def _load_packed_rows(ref, n):
    words = [ref[pl.ds(c, n, stride=TOK_SUBLANES), :] for c in range(TOK_SUBLANES)]
    lo = [lax.bitcast_convert_type(w << 16, F32) for w in words]
    hi = [lax.bitcast_convert_type(w & HI_MASK, F32) for w in words]
    return lo + hi


def _rope_chunks(t, c, s1, s2):
    outs = []
    for j in range(t.shape[1] // LANES):
        tj = t[:, j * LANES:(j + 1) * LANES]
        outs.append(tj * c + pltpu.roll(tj, LANES - ROT_DIM // 2, 1) * s1
                    + pltpu.roll(tj, ROT_DIM // 2, 1) * s2)
    return outs[0] if len(outs) == 1 else jnp.concatenate(outs, axis=1)


def _proj_kernel(x_ref, g1_ref, w_ref, wq_ref, bd_ref, qg_ref, kg_ref, c_ref, s1_ref, s2_ref,
                 q_ref, k_ref, v_ref, cb_ref, u_ref, ga_ref, gc_ref):
    x = x_ref[...]
    ms = jnp.mean(x * x, axis=-1, keepdims=True)
    h = (x * lax.rsqrt(ms + NORM_EPS) * g1_ref[...]).astype(BF16)

    def proj(a, b):
        return jnp.dot(h, w_ref[:, a:b], preferred_element_type=F32)

    c, s1, s2 = c_ref[...], s1_ref[...], s2_ref[...]

    q = jnp.dot(h, wq_ref[...], preferred_element_type=F32)
    qms = jnp.dot((q * q).astype(BF16), bd_ref[...], preferred_element_type=F32)
    qg = jnp.concatenate([qg_ref[...]] * (ATTN_DIM // LANES), axis=1)
    qn = q * lax.rsqrt(qms + NORM_EPS) * qg
    q_ref[...] = (_rope_chunks(qn, c, s1, s2) * (HEAD_DIM ** -0.5 * LOG2E)).astype(BF16)

    k = proj(_K0, _V0)
    kms = jnp.dot((k * k).astype(BF16), bd_ref[0:KV_DIM, 0:KV_DIM], preferred_element_type=F32)
    kn = k * lax.rsqrt(kms + NORM_EPS) * kg_ref[...]
    k_ref[...] = _rope_chunks(kn, c, s1, s2).astype(BF16)

    v_ref[...] = proj(_V0, _CB0).astype(BF16)
    cb_ref[...] = proj(_CB0, _CC0).astype(BF16)
    u_ref[...] = (proj(_CC0, _CX0) * proj(_CX0, _GA0)).astype(BF16)
    ga_ref[...] = jax.nn.sigmoid(proj(_GA0, _GC0)).astype(BF16)
    gc_ref[...] = jax.nn.sigmoid(proj(_GC0, _END)).astype(BF16)


def _mix_kernel(sink_ref, x_ref, q_ref, kp_ref, kc_ref, kn_ref, vp_ref, vc_ref, vn_ref,
                cb_ref, up_ref, uc_ref, un_ref, ga_ref, gc_ref,
                cw_ref, cbias_ref, wa_ref, wc_ref, wo_ref, g2_ref,
                wr_ref, br_ref,
                x1_ref, h2_ref, lg_ref,
                ubuf, attn_scr, *, tq):
    j = pl.program_id(1)
    nj = pl.num_programs(1)
    nsub = tq // WINDOW

    kbuf = jnp.concatenate([kp_ref[...], kc_ref[...], kn_ref[...]], axis=0)
    vbuf = jnp.concatenate([vp_ref[...], vc_ref[...], vn_ref[...]], axis=0)
    low_half = lax.broadcasted_iota(I32, (WINDOW, LANES), 1) < HEAD_DIM
    low_half3 = lax.broadcasted_iota(I32, (3 * WINDOW, LANES), 1) < HEAD_DIM
    qrow = lax.broadcasted_iota(I32, (WINDOW, WINDOW), 0)
    kcol = lax.broadcasted_iota(I32, (WINDOW, WINDOW), 1)
    bias_lo = jnp.where(kcol >= qrow, 0.0, MASK_VALUE)
    bias_hi = jnp.where(kcol <= qrow, 0.0, MASK_VALUE)
    one = jnp.ones((3 * WINDOW, LANES), BF16)
    for n in range(nsub):
        blo = jnp.where(j > 0, bias_lo, MASK_VALUE) if n == 0 else bias_lo
        bhi = jnp.where(j < nj - 1, bias_hi, MASK_VALUE) if n == nsub - 1 else bias_hi
        qs = q_ref[n * WINDOW:(n + 1) * WINDOW, :]
        zero = jnp.zeros((WINDOW, LANES), BF16)
        q_stack = []
        for hd in range(N_HEADS):
            col = qs[:, (hd % 4) * LANES:(hd % 4 + 1) * LANES]
            keep = low_half if hd < 4 else jnp.logical_not(low_half)
            q_stack.append(jnp.where(keep, col, zero))
        q_stack = jnp.concatenate(q_stack, axis=0)
        kwin = kbuf[n * WINDOW:(n + 3) * WINDOW, :]
        vwin = vbuf[n * WINDOW:(n + 3) * WINDOW, :]
        s = lax.dot_general(q_stack, kwin, (((1,), (1,)), ((), ())),
                            preferred_element_type=F32)
        p_list, es_list = [], []
        for hd in range(N_HEADS):
            sink = sink_ref[hd] * LOG2E
            rows = slice(hd * WINDOW, (hd + 1) * WINDOW)
            s0 = s[rows, 0:WINDOW] + blo
            s1 = s[rows, WINDOW:2 * WINDOW]
            s2 = s[rows, 2 * WINDOW:3 * WINDOW] + bhi
            m = jnp.max(jnp.maximum(jnp.maximum(s0, s1), s2), axis=-1, keepdims=True)
            m = jnp.maximum(m, sink)
            p_list.append(jnp.concatenate(
                [jnp.exp2(s0 - m).astype(BF16), jnp.exp2(s1 - m).astype(BF16),
                 jnp.exp2(s2 - m).astype(BF16)], axis=1))
            es_list.append(jnp.exp2(sink - m))
        half = N_HEADS // 2
        o_lo = jnp.dot(jnp.concatenate(p_list[:half], axis=0), jnp.where(low_half3, vwin, one),
                       preferred_element_type=F32)
        o_hi = jnp.dot(jnp.concatenate(p_list[half:], axis=0), jnp.where(low_half3, one, vwin),
                       preferred_element_type=F32)
        cols = []
        for c4 in range(half):
            rows = slice(c4 * WINDOW, (c4 + 1) * WINDOW)
            oa, ob = o_lo[rows, :], o_hi[rows, :]
            da = pltpu.roll(oa, HEAD_DIM, 1) + es_list[c4]
            db = pltpu.roll(ob, HEAD_DIM, 1) + es_list[c4 + half]
            cols.append(jnp.where(low_half, oa / da, ob / db))
        attn_scr[n * WINDOW:(n + 1) * WINDOW, :] = jnp.concatenate(cols, axis=1).astype(BF16)
    attn = attn_scr[...]

    ubuf[8:8 + tq, :] = uc_ref[...].astype(F32)
    prev_rows = up_ref[...].astype(F32)
    next_rows = un_ref[...].astype(F32)
    ubuf[7:8, :] = prev_rows[15:16, :] * (j > 0).astype(F32)
    ubuf[8 + tq:9 + tq, :] = next_rows[0:1, :] * (j < nj - 1).astype(F32)
    cw = cw_ref[...]
    conv = (ubuf[7:7 + tq, :] * cw[0:1, :] + ubuf[8:8 + tq, :] * cw[1:2, :]
            + ubuf[9:9 + tq, :] * cw[2:3, :] + cbias_ref[...])
    cgate = (cb_ref[...].astype(F32) * conv).astype(BF16)

    merged = (ga_ref[...].astype(F32) * jnp.dot(attn, wa_ref[...], preferred_element_type=F32)
              + gc_ref[...].astype(F32) * jnp.dot(cgate, wc_ref[...], preferred_element_type=F32))
    x1 = x_ref[...] + jnp.dot(merged.astype(BF16), wo_ref[...], preferred_element_type=F32)
    x1_ref[...] = x1

    ms = jnp.mean(x1 * x1, axis=-1, keepdims=True)
    h2 = x1 * lax.rsqrt(ms + NORM_EPS) * g2_ref[...]
    _store_packed_rows(h2_ref, h2, tq)
    h_hi = h2.astype(BF16)
    h_lo = (h2 - h_hi.astype(F32)).astype(BF16)
    acc = jnp.dot(h_hi, wr_ref[...], preferred_element_type=F32)
    logits = (acc[:, 0:LANES] + acc[:, LANES:2 * LANES]
              + jnp.dot(h_lo, wr_ref[:, 0:LANES], preferred_element_type=F32) + br_ref[...])
    lg_ref[...] = logits


def _route_kernel(lg_ref, tri_ref, ri_ref, rf_ref, cnt_ref, base_ref, *, tr):
    @pl.when(pl.program_id(0) == 0)
    def _():
        base_ref[...] = jnp.zeros_like(base_ref)

    logits = lg_ref[...].T[0:ROUTER_ROWS, :]
    grow = lax.broadcasted_iota(I32, (8, tr), 0)
    gl = jnp.where(grow < N_GROUPS, logits[0:8, :], NEG_BIG)
    gmax = jnp.max(gl, axis=0, keepdims=True)
    grp = jnp.min(jnp.where(gl == gmax, grow, 8), axis=0, keepdims=True)
    p_grp = 1.0 / jnp.sum(jnp.exp(gl - gmax), axis=0, keepdims=True)
    el = logits[8:ROUTER_ROWS, :]
    erow = lax.broadcasted_iota(I32, (N_EXPERTS, tr), 0)
    in_grp = (erow // EXPERTS_PER_GROUP) == grp
    l_a = jnp.where(in_grp, el, NEG_BIG)
    l1 = jnp.max(l_a, axis=0, keepdims=True)
    i1 = jnp.min(jnp.where(l_a == l1, erow, N_EXPERTS), axis=0, keepdims=True)
    l_b = jnp.where(erow == i1, NEG_BIG, l_a)
    l2 = jnp.max(l_b, axis=0, keepdims=True)
    i2 = jnp.min(jnp.where(l_b == l2, erow, N_EXPERTS), axis=0, keepdims=True)
    r21 = jnp.exp(l2 - l1)
    t1 = 1.0 / (1.0 + r21)
    gate1 = p_grp * t1
    gate2 = p_grp * (r21 * t1)
    oh1 = (erow == i1).astype(F32)
    oh2 = (erow == i2).astype(F32)
    oh = oh1 + oh2
    base = base_ref[...][:, 0:1]
    r1, r2 = [], []
    for c in range(tr // RANK_CHUNK):
        cols = slice(c * RANK_CHUNK, (c + 1) * RANK_CHUNK)
        ohc = oh[:, cols]
        before = jnp.dot(ohc.astype(BF16), tri_ref[...], preferred_element_type=F32) + base
        r1.append(jnp.sum(oh1[:, cols] * before, axis=0, keepdims=True))
        r2.append(jnp.sum(oh2[:, cols] * before, axis=0, keepdims=True))
        base = base + jnp.sum(ohc, axis=1, keepdims=True)
    rank1 = jnp.concatenate(r1, axis=1)
    rank2 = jnp.concatenate(r2, axis=1)
    new_base = jnp.broadcast_to(base, base_ref.shape)
    base_ref[...] = new_base
    cnt_ref[...] = new_base
    zi = jnp.zeros((4, tr), I32)
    ri_ref[...] = jnp.concatenate([i1, i2, rank1.astype(I32), rank2.astype(I32), zi], axis=0)
    rf_ref[...] = jnp.concatenate([gate1, gate2, jnp.zeros((LANES - 2, tr), F32)], axis=0).T


def _moe_kernel(be_ref, nused_ref, x_ref, wg_ref, wu_ref, wd_ref, y_ref, wg_s, wu_s, wd_s):
    i = pl.program_id(0)

    changed = (i == 0) | (be_ref[i] != be_ref[jnp.maximum(i - 1, 0)])

    def narrow(_, carry):
        wg_s[...] = wg_ref[...].astype(BF16)
        wu_s[...] = wu_ref[...].astype(BF16)
        wd_s[...] = wd_ref[...].astype(BF16)
        return carry

    lax.fori_loop(0, changed.astype(I32), narrow, 0)

    @pl.when(i < nused_ref[0])
    def _():
        xb = jnp.concatenate([c.astype(BF16) for c in _load_packed_rows(x_ref, MOE_ROWS)], axis=1)
        g = jnp.dot(xb, wg_s[...], preferred_element_type=F32)
        u = jnp.dot(xb, wu_s[...], preferred_element_type=F32)
        hid = (g * jax.nn.sigmoid(g) * u).astype(BF16)
        _store_packed_rows(y_ref, jnp.dot(hid, wd_s[...], preferred_element_type=F32), MOE_ROWS)

    @pl.when(i >= nused_ref[0])
    def _():
        y_ref[...] = jnp.zeros_like(y_ref)


def _combine_kernel(x1_ref, gate_ref, y0_ref, y1_ref, *rest, tm):
    o_ref = rest[-1]
    g = gate_ref[...]
    g0, g1 = g[:, 0:1], g[:, 1:2]
    y0 = _load_packed_rows(y0_ref, tm)
    y1 = _load_packed_rows(y1_ref, tm)
    for c in range(2 * TOK_SUBLANES):
        cols = slice(c * LANES, (c + 1) * LANES)
        o_ref[:, cols] = x1_ref[:, cols] + g0 * y0[c] + g1 * y1[c]


def _sc_windows(n_tokens):
    info = plsc.get_sparse_core_info()
    n_workers = info.num_cores * info.num_subcores
    assert n_tokens % (n_workers * SC_WINDOW * 2) == 0
    return info.num_cores, n_tokens // (n_workers * SC_WINDOW)


def _sc_dispatch(h3, d0, d1, n_rows):
    t = h3.shape[0]
    n_cores, nwin = _sc_windows(t)
    mesh = plsc.VectorSubcoreMesh(core_axis_name="c", subcore_axis_name="s")
    rows = pltpu.VMEM((SC_WINDOW,) + h3.shape[1:], h3.dtype)
    idx = pltpu.VMEM((nwin, SC_WINDOW), I32)

    @functools.partial(
        pl.kernel, mesh=mesh,
        out_type=jax.ShapeDtypeStruct((n_rows,) + h3.shape[1:], h3.dtype),
        scratch_types=[idx, idx, idx, rows, rows] + [pltpu.SemaphoreType.DMA] * 3,
        name="sc_dispatch")
    def run(h_hbm, d0_hbm, d1_hbm, it_hbm, o_hbm, i0, i1, it, buf_a, buf_b, sem_a, sem_b, sem_s):
        wbase = (lax.axis_index("s") * n_cores + lax.axis_index("c")) * nwin
        pltpu.sync_copy(d0_hbm.at[pl.ds(wbase, nwin)], i0)
        pltpu.sync_copy(d1_hbm.at[pl.ds(wbase, nwin)], i1)
        pltpu.sync_copy(it_hbm.at[pl.ds(wbase, nwin)], it)

        @pl.loop(0, nwin, step=2)
        def _(w):
            ga = pltpu.async_copy(h_hbm.at[it.at[w]], buf_a, sem_a)
            gb = pltpu.async_copy(h_hbm.at[it.at[w + 1]], buf_b, sem_b)
            ga.wait()
            s0 = pltpu.async_copy(buf_a, o_hbm.at[i0.at[w]], sem_s)
            s1 = pltpu.async_copy(buf_a, o_hbm.at[i1.at[w]], sem_s)
            gb.wait()
            s2 = pltpu.async_copy(buf_b, o_hbm.at[i0.at[w + 1]], sem_s)
            s3 = pltpu.async_copy(buf_b, o_hbm.at[i1.at[w + 1]], sem_s)
            s0.wait()
            s1.wait()
            s2.wait()
            s3.wait()

    win = lambda a: a.reshape(-1, SC_WINDOW)
    return run(h3, win(d0), win(d1), win(jnp.arange(t, dtype=I32)))


def _sc_gather(y3, d0, d1):
    t = d0.shape[0]
    n_cores, nwin = _sc_windows(t)
    mesh = plsc.VectorSubcoreMesh(core_axis_name="c", subcore_axis_name="s")
    rows = pltpu.VMEM((SC_WINDOW,) + y3.shape[1:], y3.dtype)
    idx = pltpu.VMEM((nwin, SC_WINDOW), I32)

    @functools.partial(
        pl.kernel, mesh=mesh,
        out_type=jax.ShapeDtypeStruct((2 * t,) + y3.shape[1:], y3.dtype),
        scratch_types=[idx] * 4 + [rows] * 4 + [pltpu.SemaphoreType.DMA] * 5,
        name="sc_gather")
    def run(y_hbm, d0_hbm, d1_hbm, ita_hbm, itb_hbm, o_hbm, i0, i1, ita, itb,
            b0, b1, b2, b3, m0, m1, m2, m3, ms):
        wbase = (lax.axis_index("s") * n_cores + lax.axis_index("c")) * nwin
        pltpu.sync_copy(d0_hbm.at[pl.ds(wbase, nwin)], i0)
        pltpu.sync_copy(d1_hbm.at[pl.ds(wbase, nwin)], i1)
        pltpu.sync_copy(ita_hbm.at[pl.ds(wbase, nwin)], ita)
        pltpu.sync_copy(itb_hbm.at[pl.ds(wbase, nwin)], itb)

        @pl.loop(0, nwin, step=2)
        def _(w):
            g0 = pltpu.async_copy(y_hbm.at[i0.at[w]], b0, m0)
            g1 = pltpu.async_copy(y_hbm.at[i1.at[w]], b1, m1)
            g2 = pltpu.async_copy(y_hbm.at[i0.at[w + 1]], b2, m2)
            g3 = pltpu.async_copy(y_hbm.at[i1.at[w + 1]], b3, m3)
            g0.wait()
            s0 = pltpu.async_copy(b0, o_hbm.at[ita.at[w]], ms)
            g1.wait()
            s1 = pltpu.async_copy(b1, o_hbm.at[itb.at[w]], ms)
            g2.wait()
            s2 = pltpu.async_copy(b2, o_hbm.at[ita.at[w + 1]], ms)
            g3.wait()
            s3 = pltpu.async_copy(b3, o_hbm.at[itb.at[w + 1]], ms)
            s0.wait()
            s1.wait()
            s2.wait()
            s3.wait()

    win = lambda a: a.reshape(-1, SC_WINDOW)
    iota = jnp.arange(2 * t, dtype=I32)
    return run(y3, win(d0), win(d1), win(iota[:t]), win(iota[t:]))


def _rope_lane_tables(seq_len):
    half = ROT_DIM // 2
    inv_freq = ROPE_THETA ** (-np.arange(0, ROT_DIM, 2, dtype=np.float64) / ROT_DIM)
    ang = np.arange(seq_len, dtype=np.float64)[:, None] * inv_freq[None, :]
    cos, sin = np.cos(ang), np.sin(ang)
    m = np.arange(LANES) % HEAD_DIM
    f = m % half
    c = np.where(m[None, :] < ROT_DIM, cos[:, f], 1.0)
    s1 = np.where(m[None, :] < half, -sin[:, f], 0.0)
    s2 = np.where((m[None, :] >= half) & (m[None, :] < ROT_DIM), sin[:, f], 0.0)
    return tuple(jnp.asarray(a, F32) for a in (c, s1, s2))


def _layer(x, norm1_g, w_in, q_norm_g, k_norm_g, attn_sink, conv_w, conv_b, w_attn_proj,
           w_conv_proj, w_out, norm2_g, w_rg, b_rg, w_re, b_re, w_gate, w_up, w_down):
    bsz, seq, d = x.shape
    t = bsz * seq
    tm = min(512, seq)
    tq = min(1024, seq)
    assert d == D_MODEL and seq % tq == 0 and t % tm == 0 and seq % tm == 0 and tq % WINDOW == 0
    xf = x.reshape(t, d)

    hpg = N_HEADS // N_KV_HEADS
    wq_p = (w_in[:, :ATTN_DIM].reshape(d, N_KV_HEADS, hpg, HEAD_DIM).transpose(0, 2, 1, 3)
            .reshape(d, ATTN_DIM))
    wq_p = wq_p.astype(BF16)
    w_in_b = w_in.astype(BF16)
    wa_p = (w_attn_proj.reshape(N_KV_HEADS, hpg, HEAD_DIM, d).transpose(1, 0, 2, 3)
            .reshape(ATTN_DIM, d).astype(BF16))
    bd = jnp.asarray(np.kron(np.eye(N_HEADS), np.full((HEAD_DIM, HEAD_DIM), 1.0 / HEAD_DIM)), BF16)
    qg = jnp.tile(q_norm_g, LANES // HEAD_DIM).reshape(1, LANES)
    kg = jnp.tile(k_norm_g, LANES // HEAD_DIM).reshape(1, LANES)
    ctab, s1tab, s2tab = _rope_lane_tables(seq)
    nseq_t = seq // tm

    row = lambda w: pl.BlockSpec((tm, w), lambda i: (i, 0))
    full = lambda a: pl.BlockSpec(a.shape, lambda i: (0,) * a.ndim)
    tab = pl.BlockSpec((tm, LANES), lambda i: (i % nseq_t, 0))
    q, k, v, cb, u, ga, gc = pl.pallas_call(
        _proj_kernel,
        grid=(t // tm,),
        in_specs=[row(d), full(norm1_g.reshape(1, d)), full(w_in_b), full(wq_p), full(bd), full(qg), full(kg),
                  tab, tab, tab],
        out_specs=[row(ATTN_DIM), row(KV_DIM), row(KV_DIM), row(CONV_DIM), row(CONV_DIM),
                   row(D_MODEL), row(D_MODEL)],
        out_shape=[jax.ShapeDtypeStruct((t, w), BF16)
                   for w in (ATTN_DIM, KV_DIM, KV_DIM, CONV_DIM, CONV_DIM, D_MODEL, D_MODEL)],
        compiler_params=pltpu.CompilerParams(dimension_semantics=("arbitrary",),
                                             vmem_limit_bytes=VMEM_LIMIT),
        name="proj",
    )(xf, norm1_g.reshape(1, d), w_in_b, wq_p, bd, qg, kg, ctab, s1tab, s2tab)

    wr = jnp.concatenate([w_rg, jnp.zeros((d, 8 - N_GROUPS), F32), w_re,
                          jnp.zeros((d, LANES - ROUTER_ROWS), F32)], axis=1)
    wr_hi = wr.astype(BF16)
    wr2 = jnp.concatenate([wr_hi, (wr - wr_hi.astype(F32)).astype(BF16)], axis=1)
    br = jnp.concatenate([b_rg, jnp.zeros((8 - N_GROUPS,), F32), b_re,
                          jnp.zeros((LANES - ROUTER_ROWS,), F32)]).reshape(1, LANES)
    mix_params = (conv_w, conv_b.reshape(1, CONV_DIM), wa_p, w_conv_proj.astype(BF16), w_out.astype(BF16),
                  norm2_g.reshape(1, d), wr2, br)

    n_chunks = TOKEN_CHUNKS if bsz % TOKEN_CHUNKS == 0 else 1
    out = None
    for c in range(n_chunks):
        out = _chunk_tail(c, n_chunks, out, xf, (q, k, v, cb, u, ga, gc), attn_sink, mix_params,
                          (w_gate, w_up, w_down), bsz, seq, tq)
    return out.reshape(bsz, seq, d)


def _chunk_tail(c, n_chunks, out_prev, xf, acts, attn_sink, mix_params, expert_w, bsz, seq, tq):
    q, k, v, cb, u, ga, gc = acts
    d = D_MODEL
    t_all = bsz * seq
    bc = bsz // n_chunks
    b0 = c * bc
    t = bc * seq
    nj = seq // tq
    wpb = tq // WINDOW
    nb_seq = seq // WINDOW
    hpb = tq // 16
    row_in = lambda w: pl.BlockSpec((tq, w), lambda b, j: ((b0 + b) * nj + j, 0))
    row_out = lambda w: pl.BlockSpec((tq, w), lambda b, j: (b * nj + j, 0))
    prev_blk = pl.BlockSpec((WINDOW, KV_DIM),
                            lambda b, j: ((b0 + b) * nb_seq + jnp.maximum(j * wpb - 1, 0), 0))
    next_blk = pl.BlockSpec((WINDOW, KV_DIM),
                            lambda b, j: ((b0 + b) * nb_seq + jnp.minimum((j + 1) * wpb, nb_seq - 1), 0))
    uprev = pl.BlockSpec((16, CONV_DIM),
                         lambda b, j: (jnp.maximum(((b0 + b) * nj + j) * hpb - 1, 0), 0))
    unext = pl.BlockSpec((16, CONV_DIM),
                         lambda b, j: (jnp.minimum(((b0 + b) * nj + j + 1) * hpb, t_all // 16 - 1), 0))
    full2 = lambda a: pl.BlockSpec(a.shape, lambda b, j: (0,) * a.ndim)

    x1, h2, logits = pl.pallas_call(
        functools.partial(_mix_kernel, tq=tq),
        grid=(bc, nj),
        in_specs=[pl.BlockSpec(memory_space=pltpu.SMEM),
                  row_in(d), row_in(ATTN_DIM), prev_blk, row_in(KV_DIM), next_blk,
                  prev_blk, row_in(KV_DIM), next_blk,
                  row_in(CONV_DIM), uprev, row_in(CONV_DIM), unext, row_in(D_MODEL), row_in(D_MODEL)]
                 + [full2(p) for p in mix_params],
        out_specs=[row_out(d), pl.BlockSpec((tq * TOK_SUBLANES, LANES), lambda b, j: (b * nj + j, 0)),
                   row_out(LANES)],
        out_shape=[jax.ShapeDtypeStruct((t, d), F32),
                   jax.ShapeDtypeStruct((t * TOK_SUBLANES, LANES), U32),
                   jax.ShapeDtypeStruct((t, LANES), F32)],
        scratch_shapes=[pltpu.VMEM((tq + 16, CONV_DIM), F32), pltpu.VMEM((tq, ATTN_DIM), BF16)],
        compiler_params=pltpu.CompilerParams(dimension_semantics=("arbitrary", "arbitrary"),
                                             vmem_limit_bytes=VMEM_LIMIT),
        name="mix",
    )(attn_sink, xf, q, k, k, k, v, v, v, cb, u, u, u, ga, gc, *mix_params)

    tr = min(ROUTE_TILE, t)
    assert t % tr == 0 and tr % RANK_CHUNK == 0
    tri = jnp.asarray(np.triu(np.ones((RANK_CHUNK, RANK_CHUNK), np.float32), 1), BF16)
    colr = lambda r: pl.BlockSpec((r, tr), lambda i: (0, i))
    route_i, route_f, counts = pl.pallas_call(
        functools.partial(_route_kernel, tr=tr),
        grid=(t // tr,),
        in_specs=[pl.BlockSpec((tr, LANES), lambda i: (i, 0)), pl.BlockSpec(tri.shape, lambda i: (0, 0))],
        out_specs=[colr(8), pl.BlockSpec((tr, LANES), lambda i: (i, 0)),
                   pl.BlockSpec((N_EXPERTS, LANES), lambda i: (0, 0))],
        out_shape=[jax.ShapeDtypeStruct((8, t), I32), jax.ShapeDtypeStruct((t, LANES), F32),
                   jax.ShapeDtypeStruct((N_EXPERTS, LANES), F32)],
        scratch_shapes=[pltpu.VMEM((N_EXPERTS, LANES), F32)],
        compiler_params=pltpu.CompilerParams(dimension_semantics=("arbitrary",),
                                             vmem_limit_bytes=VMEM_LIMIT),
        name="route",
    )(logits, tri)

    n_rows = 2 * t + N_EXPERTS * MOE_ROWS
    nblk = n_rows // MOE_ROWS
    cnt = counts[:, 0].astype(I32)
    padded = ((cnt + MOE_ROWS - 1) // MOE_ROWS) * MOE_ROWS
    pad_ends = jnp.cumsum(padded)
    pad_starts = pad_ends - padded
    eids = jnp.arange(N_EXPERTS, dtype=I32)[:, None, None]
    dest = jnp.sum(jnp.where(route_i[None, 0:2] == eids, pad_starts[:, None, None], 0), axis=0) \
        + route_i[2:4]
    blk_start = jnp.arange(nblk, dtype=I32) * MOE_ROWS
    block_e = jnp.minimum(jnp.sum((pad_ends[None, :] <= blk_start[:, None]).astype(I32), axis=1),
                          N_EXPERTS - 1)
    nused = (pad_ends[-1:] // MOE_ROWS).astype(I32)

    x_pad = _sc_dispatch(h2.reshape(t, TOK_SUBLANES, LANES), dest[0], dest[1], n_rows)
    x_pad = x_pad.reshape(n_rows * TOK_SUBLANES, LANES)

    w_gate, w_up, w_down = expert_w
    blk_rows = MOE_ROWS * TOK_SUBLANES
    wspec = lambda a, b: pl.BlockSpec((None, a, b), lambda i, be, nu: (be[i], 0, 0))
    y_pad = pl.pallas_call(
        _moe_kernel,
        grid_spec=pltpu.PrefetchScalarGridSpec(
            num_scalar_prefetch=2,
            grid=(nblk,),
            in_specs=[pl.BlockSpec((blk_rows, LANES), lambda i, be, nu: (jnp.minimum(i, nu[0] - 1), 0)),
                      wspec(d, D_FF), wspec(d, D_FF), wspec(D_FF, d)],
            out_specs=pl.BlockSpec((blk_rows, LANES),
                                   lambda i, be, nu: (jnp.where(i < nu[0], i, nblk - 1), 0)),
            scratch_shapes=[pltpu.VMEM((d, D_FF), BF16), pltpu.VMEM((d, D_FF), BF16),
                            pltpu.VMEM((D_FF, d), BF16)]),
        out_shape=jax.ShapeDtypeStruct((n_rows * TOK_SUBLANES, LANES), U32),
        compiler_params=pltpu.CompilerParams(dimension_semantics=("arbitrary",),
                                             vmem_limit_bytes=VMEM_LIMIT),
        name="moe",
    )(block_e, nused, x_pad, w_gate, w_up, w_down)

    y3 = y_pad.reshape(n_rows, TOK_SUBLANES, LANES)
    parts = COMBINE_PARTS if t % (COMBINE_PARTS * 2048) == 0 else 1
    tp = t // parts
    tc = min(512, tp)
    nt = tp // tc
    out = out_prev
    for h in range(parts):
        tok = slice(h * tp, (h + 1) * tp)
        yk = _sc_gather(y3, dest[0, tok], dest[1, tok]).reshape(2 * tp * TOK_SUBLANES, LANES)
        row0 = h * nt
        in_specs = [pl.BlockSpec((tc, d), lambda i, row0=row0: (row0 + i, 0)),
                    pl.BlockSpec((tc, LANES), lambda i, row0=row0: (row0 + i, 0)),
                    pl.BlockSpec((tc * TOK_SUBLANES, LANES), lambda i: (i, 0)),
                    pl.BlockSpec((tc * TOK_SUBLANES, LANES), lambda i: (nt + i, 0))]
        args = [x1, route_f, yk, yk]
        aliases = {}
        if out is not None:
            in_specs.append(pl.BlockSpec(memory_space=pl.ANY))
            args.append(out)
            aliases = {len(args) - 1: 0}
        out = pl.pallas_call(
            functools.partial(_combine_kernel, tm=tc),
            grid=(nt,),
            in_specs=in_specs,
            out_specs=pl.BlockSpec((tc, d), lambda i, o=(c * parts + h) * nt: (o + i, 0)),
            out_shape=jax.ShapeDtypeStruct((t_all, d), F32),
            input_output_aliases=aliases,
            compiler_params=pltpu.CompilerParams(dimension_semantics=("arbitrary",),
                                                 vmem_limit_bytes=VMEM_LIMIT),
            name="combine",
        )(*args)
    return out


def kernel(x, norm1_g, w_in, q_norm_g, k_norm_g, attn_sink, conv_w, conv_b, w_attn_proj, w_conv_proj, w_out, norm2_g, w_router_group, b_router_group, w_router_expert, b_router_expert, w_gate_e, w_up_e, w_down_e):
    for l in range(norm1_g.shape[0]):
        x = _layer(x, norm1_g[l], w_in[l], q_norm_g[l], k_norm_g[l], attn_sink[l], conv_w[l],
                   conv_b[l], w_attn_proj[l], w_conv_proj[l], w_out[l], norm2_g[l],
                   w_router_group[l], b_router_group[l], w_router_expert[l], b_router_expert[l],
                   w_gate_e[l], w_up_e[l], w_down_e[l])
    return x
```

```python
import functools

import jax
import jax.numpy as jnp
import numpy as np
from jax import lax
from jax.experimental import pallas as pl
from jax.experimental.pallas import tpu as pltpu
from jax.experimental.pallas import tpu_sc as plsc

F32 = jnp.float32
BF16 = jnp.bfloat16
I32 = jnp.int32
U32 = jnp.uint32
HI_MASK = np.uint32(0xFFFF0000)

D_MODEL = 1024
N_HEADS = 8
N_KV_HEADS = 2
HEAD_DIM = 64
ATTN_DIM = N_HEADS * HEAD_DIM
KV_DIM = N_KV_HEADS * HEAD_DIM
WINDOW = 128
ROT_DIM = HEAD_DIM // 4
ROPE_THETA = 500000.0
CONV_DIM = D_MODEL // 2
N_GROUPS = 4
EXPERTS_PER_GROUP = 8
N_EXPERTS = N_GROUPS * EXPERTS_PER_GROUP
D_FF = D_MODEL // 4
NORM_EPS = 1e-6
MASK_VALUE = -1e30
NEG_BIG = -3.0e38
LOG2E = 1.4426950408889634

LANES = 128
TOK_SUBLANES = D_MODEL // (2 * LANES)
HALF_D = D_MODEL // 2
SC_WINDOW = 32
ROUTER_ROWS = 8 + N_EXPERTS
ROUTE_TILE = 4096
RANK_CHUNK = 256
MOE_ROWS = 512
TOKEN_CHUNKS = 1
COMBINE_PARTS = 1
VMEM_LIMIT = 56 * 1024 * 1024

_Q0, _K0, _V0, _CB0, _CC0, _CX0, _GA0, _GC0, _END = (
    0, 512, 640, 768, 1280, 1792, 2304, 3328, 4352)


def _store_packed_rows(ref, val, n):
    for c in range(TOK_SUBLANES):
        lo = val[:, c * LANES:(c + 1) * LANES].astype(BF16).astype(F32)
        hi = val[:, HALF_D + c * LANES:HALF_D + (c + 1) * LANES].astype(BF16).astype(F32)
        word = (lax.bitcast_convert_type(lo, U32) >> 16) | (lax.bitcast_convert_type(hi, U32) & HI_MASK)
        ref[pl.ds(c, n, stride=TOK_SUBLANES), :] = word


def _load_packed_rows(ref, n):
    words = [ref[pl.ds(c, n, stride=TOK_SUBLANES), :] for c in range(TOK_SUBLANES)]
    lo = [lax.bitcast_convert_type(w << 16, F32) for w in words]
    hi = [lax.bitcast_convert_type(w & HI_MASK, F32) for w in words]
    return lo + hi


def _rope_chunks(t, c, s1, s2):
    outs = []
    for j in range(t.shape[1] // LANES):
        tj = t[:, j * LANES:(j + 1) * LANES]
        outs.append(tj * c + pltpu.roll(tj, LANES - ROT_DIM // 2, 1) * s1
                    + pltpu.roll(tj, ROT_DIM // 2, 1) * s2)
    return outs[0] if len(outs) == 1 else jnp.concatenate(outs, axis=1)


def _proj_kernel(x_ref, g1_ref, w_ref, wq_ref, bd_ref, qg_ref, kg_ref, c_ref, s1_ref, s2_ref, cw_ref, cbias_ref,
                 q_ref, k_ref, v_ref, cb_ref, cp_ref, ue_ref, ga_ref, gc_ref, ubuf):
    x = x_ref[...]
    ms = jnp.mean(x * x, axis=-1, keepdims=True)
    h = (x * lax.rsqrt(ms + NORM_EPS) * g1_ref[...]).astype(BF16)

    def proj(a, b):
        return jnp.dot(h, w_ref[:, a:b], preferred_element_type=F32)

    c, s1, s2 = c_ref[...], s1_ref[...], s2_ref[...]

    q = jnp.dot(h, wq_ref[...], preferred_element_type=F32)
    qms = jnp.dot((q * q).astype(BF16), bd_ref[...], preferred_element_type=F32)
    qg = jnp.concatenate([qg_ref[...]] * (ATTN_DIM // LANES), axis=1)
    qn = q * lax.rsqrt(qms + NORM_EPS) * qg
    q_ref[...] = (_rope_chunks(qn, c, s1, s2) * (HEAD_DIM ** -0.5 * LOG2E)).astype(BF16)

    k = proj(_K0, _V0)
    kms = jnp.dot((k * k).astype(BF16), bd_ref[0:KV_DIM, 0:KV_DIM], preferred_element_type=F32)
    kn = k * lax.rsqrt(kms + NORM_EPS) * kg_ref[...]
    k_ref[...] = _rope_chunks(kn, c, s1, s2).astype(BF16)

    v_ref[...] = proj(_V0, _CB0).astype(BF16)
    cb_ref[...] = proj(_CB0, _CC0).astype(BF16)
    tm = x.shape[0]
    u = proj(_CC0, _CX0) * proj(_CX0, _GA0)
    ubuf[0:8, :] = jnp.zeros((8, CONV_DIM), F32)
    ubuf[8 + tm:16 + tm, :] = jnp.zeros((8, CONV_DIM), F32)
    ubuf[8:8 + tm, :] = u
    cw = cw_ref[...]
    cp_ref[...] = (ubuf[7:7 + tm, :] * cw[0:1, :] + u * cw[1:2, :] + ubuf[9:9 + tm, :] * cw[2:3, :]
                   + cbias_ref[...]).astype(BF16)
    ue_ref[...] = jnp.concatenate([u[0:8, :], u[tm - 8:tm, :]], axis=0).astype(BF16)
    ga_ref[...] = jax.nn.sigmoid(proj(_GA0, _GC0)).astype(BF16)
    gc_ref[...] = jax.nn.sigmoid(proj(_GC0, _END)).astype(BF16)


def _mix_kernel(sink_ref, x_ref, q_ref, kp_ref, kc_ref, kn_ref, vp_ref, vc_ref, vn_ref,
                cb_ref, cp_ref, uep_ref, uec_ref, uen_ref, ga_ref, gc_ref,
                cw_ref, wa_ref, wc_ref, wo_ref, g2_ref,
                wr_ref, br_ref,
                x1_ref, h2_ref, lg_ref,
                cscr, attn_scr, *, tq, tm):
    j = pl.program_id(1)
    nj = pl.num_programs(1)
    nsub = tq // WINDOW

    kbuf = jnp.concatenate([kp_ref[...], kc_ref[...], kn_ref[...]], axis=0)
    vbuf = jnp.concatenate([vp_ref[...], vc_ref[...], vn_ref[...]], axis=0)
    low_half = lax.broadcasted_iota(I32, (WINDOW, LANES), 1) < HEAD_DIM
    low_half3 = lax.broadcasted_iota(I32, (3 * WINDOW, LANES), 1) < HEAD_DIM
    qrow = lax.broadcasted_iota(I32, (WINDOW, WINDOW), 0)
    kcol = lax.broadcasted_iota(I32, (WINDOW, WINDOW), 1)
    bias_lo = jnp.where(kcol >= qrow, 0.0, MASK_VALUE)
    bias_hi = jnp.where(kcol <= qrow, 0.0, MASK_VALUE)
    one = jnp.ones((3 * WINDOW, LANES), BF16)
    for n in range(nsub):
        blo = jnp.where(j > 0, bias_lo, MASK_VALUE) if n == 0 else bias_lo
        bhi = jnp.where(j < nj - 1, bias_hi, MASK_VALUE) if n == nsub - 1 else bias_hi
        qs = q_ref[n * WINDOW:(n + 1) * WINDOW, :]
        zero = jnp.zeros((WINDOW, LANES), BF16)
        q_stack = []
        for hd in range(N_HEADS):
            col = qs[:, (hd % 4) * LANES:(hd % 4 + 1) * LANES]
            keep = low_half if hd < 4 else jnp.logical_not(low_half)
            q_stack.append(jnp.where(keep, col, zero))
        q_stack = jnp.concatenate(q_stack, axis=0)
        kwin = kbuf[n * WINDOW:(n + 3) * WINDOW, :]
        vwin = vbuf[n * WINDOW:(n + 3) * WINDOW, :]
        s = lax.dot_general(q_stack, kwin, (((1,), (1,)), ((), ())),
                            preferred_element_type=F32)
        p_list, es_list = [], []
        for hd in range(N_HEADS):
            sink = sink_ref[hd] * LOG2E
            rows = slice(hd * WINDOW, (hd + 1) * WINDOW)
            s0 = s[rows, 0:WINDOW] + blo
            s1 = s[rows, WINDOW:2 * WINDOW]
            s2 = s[rows, 2 * WINDOW:3 * WINDOW] + bhi
            m = jnp.max(jnp.maximum(jnp.maximum(s0, s1), s2), axis=-1, keepdims=True)
            m = jnp.maximum(m, sink)
            p_list.append(jnp.concatenate(
                [jnp.exp2(s0 - m).astype(BF16), jnp.exp2(s1 - m).astype(BF16),
                 jnp.exp2(s2 - m).astype(BF16)], axis=1))
            es_list.append(jnp.exp2(sink - m))
        half = N_HEADS // 2
        o_lo = jnp.dot(jnp.concatenate(p_list[:half], axis=0), jnp.where(low_half3, vwin, one),
                       preferred_element_type=F32)
        o_hi = jnp.dot(jnp.concatenate(p_list[half:], axis=0), jnp.where(low_half3, one, vwin),
                       preferred_element_type=F32)
        cols = []
        for c4 in range(half):
            rows = slice(c4 * WINDOW, (c4 + 1) * WINDOW)
            oa, ob = o_lo[rows, :], o_hi[rows, :]
            da = pltpu.roll(oa, HEAD_DIM, 1) + es_list[c4]
            db = pltpu.roll(ob, HEAD_DIM, 1) + es_list[c4 + half]
            cols.append(jnp.where(low_half, oa / da, ob / db))
        attn_scr[n * WINDOW:(n + 1) * WINDOW, :] = jnp.concatenate(cols, axis=1).astype(BF16)
    attn = attn_scr[...]

    cscr[...] = cp_ref[...].astype(F32)
    cw = cw_ref[...]
    uec = uec_ref[...].astype(F32)
    n_tiles = tq // tm
    for p in range(n_tiles):
        if p == 0:
            prev_last = uep_ref[...].astype(F32)[15:16, :] * (j > 0).astype(F32)
        else:
            prev_last = uec[16 * p - 1:16 * p, :]
        cscr[p * tm:p * tm + 1, :] = cscr[p * tm:p * tm + 1, :] + prev_last * cw[0:1, :]
        if p == n_tiles - 1:
            next_first = uen_ref[...].astype(F32)[0:1, :] * (j < nj - 1).astype(F32)
        else:
            next_first = uec[16 * (p + 1):16 * (p + 1) + 1, :]
        last = (p + 1) * tm - 1
        cscr[last:last + 1, :] = cscr[last:last + 1, :] + next_first * cw[2:3, :]
    cgate = (cb_ref[...].astype(F32) * cscr[...]).astype(BF16)

    merged = (ga_ref[...].astype(F32) * jnp.dot(attn, wa_ref[...], preferred_element_type=F32)
              + gc_ref[...].astype(F32) * jnp.dot(cgate, wc_ref[...], preferred_element_type=F32))
    x1 = x_ref[...] + jnp.dot(merged.astype(BF16), wo_ref[...], preferred_element_type=F32)
    x1_ref[...] = x1

    ms = jnp.mean(x1 * x1, axis=-1, keepdims=True)
    h2 = x1 * lax.rsqrt(ms + NORM_EPS) * g2_ref[...]
    _store_packed_rows(h2_ref, h2, tq)
    h_hi = h2.astype(BF16)
    h_lo = (h2 - h_hi.astype(F32)).astype(BF16)
    acc = jnp.dot(h_hi, wr_ref[...], preferred_element_type=F32)
    logits = (acc[:, 0:LANES] + acc[:, LANES:2 * LANES]
              + jnp.dot(h_lo, wr_ref[:, 0:LANES], preferred_element_type=F32) + br_ref[...])
    lg_ref[...] = logits


def _route_kernel(lg_ref, tri_ref, ri_ref, rf_ref, cnt_ref, base_ref, *, tr):
    @pl.when(pl.program_id(0) == 0)
    def _():
        base_ref[...] = jnp.zeros_like(base_ref)

    logits = lg_ref[...].T[0:ROUTER_ROWS, :]
    grow = lax.broadcasted_iota(I32, (8, tr), 0)
    gl = jnp.where(grow < N_GROUPS, logits[0:8, :], NEG_BIG)
    gmax = jnp.max(gl, axis=0, keepdims=True)
    grp = jnp.min(jnp.where(gl == gmax, grow, 8), axis=0, keepdims=True)
    p_grp = 1.0 / jnp.sum(jnp.exp(gl - gmax), axis=0, keepdims=True)
    el = logits[8:ROUTER_ROWS, :]
    erow = lax.broadcasted_iota(I32, (N_EXPERTS, tr), 0)
    in_grp = (erow // EXPERTS_PER_GROUP) == grp
    l_a = jnp.where(in_grp, el, NEG_BIG)
    l1 = jnp.max(l_a, axis=0, keepdims=True)
    i1 = jnp.min(jnp.where(l_a == l1, erow, N_EXPERTS), axis=0, keepdims=True)
    l_b = jnp.where(erow == i1, NEG_BIG, l_a)
    l2 = jnp.max(l_b, axis=0, keepdims=True)
    i2 = jnp.min(jnp.where(l_b == l2, erow, N_EXPERTS), axis=0, keepdims=True)
    r21 = jnp.exp(l2 - l1)
    t1 = 1.0 / (1.0 + r21)
    gate1 = p_grp * t1
    gate2 = p_grp * (r21 * t1)
    oh1 = (erow == i1).astype(F32)
    oh2 = (erow == i2).astype(F32)
    oh = oh1 + oh2
    base = base_ref[...][:, 0:1]
    r1, r2 = [], []
    for c in range(tr // RANK_CHUNK):
        cols = slice(c * RANK_CHUNK, (c + 1) * RANK_CHUNK)
        ohc = oh[:, cols]
        before = jnp.dot(ohc.astype(BF16), tri_ref[...], preferred_element_type=F32) + base
        r1.append(jnp.sum(oh1[:, cols] * before, axis=0, keepdims=True))
        r2.append(jnp.sum(oh2[:, cols] * before, axis=0, keepdims=True))
        base = base + jnp.sum(ohc, axis=1, keepdims=True)
    rank1 = jnp.concatenate(r1, axis=1)
    rank2 = jnp.concatenate(r2, axis=1)
    new_base = jnp.broadcast_to(base, base_ref.shape)
    base_ref[...] = new_base
    cnt_ref[...] = new_base
    zi = jnp.zeros((4, tr), I32)
    ri_ref[...] = jnp.concatenate([i1, i2, rank1.astype(I32), rank2.astype(I32), zi], axis=0)
    rf_ref[...] = jnp.concatenate([gate1, gate2, jnp.zeros((LANES - 2, tr), F32)], axis=0).T


def _moe_kernel(be_ref, nused_ref, x_ref, wg_ref, wu_ref, wd_ref, y_ref, wg_s, wu_s, wd_s):
    i = pl.program_id(0)

    changed = (i == 0) | (be_ref[i] != be_ref[jnp.maximum(i - 1, 0)])

    def narrow(_, carry):
        wg_s[...] = wg_ref[...].astype(BF16)
        wu_s[...] = wu_ref[...].astype(BF16)
        wd_s[...] = wd_ref[...].astype(BF16)
        return carry

    lax.fori_loop(0, changed.astype(I32), narrow, 0)

    @pl.when(i < nused_ref[0])
    def _():
        xb = jnp.concatenate([c.astype(BF16) for c in _load_packed_rows(x_ref, MOE_ROWS)], axis=1)
        g = jnp.dot(xb, wg_s[...], preferred_element_type=F32)
        u = jnp.dot(xb, wu_s[...], preferred_element_type=F32)
        hid = (g * jax.nn.sigmoid(g) * u).astype(BF16)
        _store_packed_rows(y_ref, jnp.dot(hid, wd_s[...], preferred_element_type=F32), MOE_ROWS)

    @pl.when(i >= nused_ref[0])
    def _():
        y_ref[...] = jnp.zeros_like(y_ref)


def _combine_kernel(x1_ref, gate_ref, y0_ref, y1_ref, *rest, tm):
    o_ref = rest[-1]
    g = gate_ref[...]
    g0, g1 = g[:, 0:1], g[:, 1:2]
    y0 = _load_packed_rows(y0_ref, tm)
    y1 = _load_packed_rows(y1_ref, tm)
    for c in range(2 * TOK_SUBLANES):
        cols = slice(c * LANES, (c + 1) * LANES)
        o_ref[:, cols] = x1_ref[:, cols] + g0 * y0[c] + g1 * y1[c]


def _sc_windows(n_tokens):
    info = plsc.get_sparse_core_info()
    n_workers = info.num_cores * info.num_subcores
    assert n_tokens % (n_workers * SC_WINDOW * 2) == 0
    return info.num_cores, n_tokens // (n_workers * SC_WINDOW)


def _sc_dispatch(h3, d0, d1, n_rows):
    t = h3.shape[0]
    n_cores, nwin = _sc_windows(t)
    mesh = plsc.VectorSubcoreMesh(core_axis_name="c", subcore_axis_name="s")
    rows = pltpu.VMEM((SC_WINDOW,) + h3.shape[1:], h3.dtype)
    idx = pltpu.VMEM((nwin, SC_WINDOW), I32)

    @functools.partial(
        pl.kernel, mesh=mesh,
        out_type=jax.ShapeDtypeStruct((n_rows,) + h3.shape[1:], h3.dtype),
        scratch_types=[idx, idx, idx, rows, rows] + [pltpu.SemaphoreType.DMA] * 3,
        name="sc_dispatch")
    def run(h_hbm, d0_hbm, d1_hbm, it_hbm, o_hbm, i0, i1, it, buf_a, buf_b, sem_a, sem_b, sem_s):
        wbase = (lax.axis_index("s") * n_cores + lax.axis_index("c")) * nwin
        pltpu.sync_copy(d0_hbm.at[pl.ds(wbase, nwin)], i0)
        pltpu.sync_copy(d1_hbm.at[pl.ds(wbase, nwin)], i1)
        pltpu.sync_copy(it_hbm.at[pl.ds(wbase, nwin)], it)

        @pl.loop(0, nwin, step=2)
        def _(w):
            ga = pltpu.async_copy(h_hbm.at[it.at[w]], buf_a, sem_a)
            gb = pltpu.async_copy(h_hbm.at[it.at[w + 1]], buf_b, sem_b)
            ga.wait()
            s0 = pltpu.async_copy(buf_a, o_hbm.at[i0.at[w]], sem_s)
            s1 = pltpu.async_copy(buf_a, o_hbm.at[i1.at[w]], sem_s)
            gb.wait()
            s2 = pltpu.async_copy(buf_b, o_hbm.at[i0.at[w + 1]], sem_s)
            s3 = pltpu.async_copy(buf_b, o_hbm.at[i1.at[w + 1]], sem_s)
            s0.wait()
            s1.wait()
            s2.wait()
            s3.wait()

    win = lambda a: a.reshape(-1, SC_WINDOW)
    return run(h3, win(d0), win(d1), win(jnp.arange(t, dtype=I32)))


def _sc_gather(y3, d0, d1):
    t = d0.shape[0]
    n_cores, nwin = _sc_windows(t)
    mesh = plsc.VectorSubcoreMesh(core_axis_name="c", subcore_axis_name="s")
    rows = pltpu.VMEM((SC_WINDOW,) + y3.shape[1:], y3.dtype)
    idx = pltpu.VMEM((nwin, SC_WINDOW), I32)

    @functools.partial(
        pl.kernel, mesh=mesh,
        out_type=jax.ShapeDtypeStruct((2 * t,) + y3.shape[1:], y3.dtype),
        scratch_types=[idx] * 4 + [rows] * 4 + [pltpu.SemaphoreType.DMA] * 5,
        name="sc_gather")
    def run(y_hbm, d0_hbm, d1_hbm, ita_hbm, itb_hbm, o_hbm, i0, i1, ita, itb,
            b0, b1, b2, b3, m0, m1, m2, m3, ms):
        wbase = (lax.axis_index("s") * n_cores + lax.axis_index("c")) * nwin
        pltpu.sync_copy(d0_hbm.at[pl.ds(wbase, nwin)], i0)
        pltpu.sync_copy(d1_hbm.at[pl.ds(wbase, nwin)], i1)
        pltpu.sync_copy(ita_hbm.at[pl.ds(wbase, nwin)], ita)
        pltpu.sync_copy(itb_hbm.at[pl.ds(wbase, nwin)], itb)

        @pl.loop(0, nwin, step=2)
        def _(w):
            g0 = pltpu.async_copy(y_hbm.at[i0.at[w]], b0, m0)
            g1 = pltpu.async_copy(y_hbm.at[i1.at[w]], b1, m1)
            g2 = pltpu.async_copy(y_hbm.at[i0.at[w + 1]], b2, m2)
            g3 = pltpu.async_copy(y_hbm.at[i1.at[w + 1]], b3, m3)
            g0.wait()
            s0 = pltpu.async_copy(b0, o_hbm.at[ita.at[w]], ms)
            g1.wait()
            s1 = pltpu.async_copy(b1, o_hbm.at[itb.at[w]], ms)
            g2.wait()
            s2 = pltpu.async_copy(b2, o_hbm.at[ita.at[w + 1]], ms)
            g3.wait()
            s3 = pltpu.async_copy(b3, o_hbm.at[itb.at[w + 1]], ms)
            s0.wait()
            s1.wait()
            s2.wait()
            s3.wait()

    win = lambda a: a.reshape(-1, SC_WINDOW)
    iota = jnp.arange(2 * t, dtype=I32)
    return run(y3, win(d0), win(d1), win(iota[:t]), win(iota[t:]))


def _rope_lane_tables(seq_len):
    half = ROT_DIM // 2
    inv_freq = ROPE_THETA ** (-np.arange(0, ROT_DIM, 2, dtype=np.float64) / ROT_DIM)
    ang = np.arange(seq_len, dtype=np.float64)[:, None] * inv_freq[None, :]
    cos, sin = np.cos(ang), np.sin(ang)
    m = np.arange(LANES) % HEAD_DIM
    f = m % half
    c = np.where(m[None, :] < ROT_DIM, cos[:, f], 1.0)
    s1 = np.where(m[None, :] < half, -sin[:, f], 0.0)
    s2 = np.where((m[None, :] >= half) & (m[None, :] < ROT_DIM), sin[:, f], 0.0)
    return tuple(jnp.asarray(a, F32) for a in (c, s1, s2))


def _layer(x, norm1_g, w_in, q_norm_g, k_norm_g, attn_sink, conv_w, conv_b, w_attn_proj,
           w_conv_proj, w_out, norm2_g, w_rg, b_rg, w_re, b_re, w_gate, w_up, w_down):
    bsz, seq, d = x.shape
    t = bsz * seq
    tm = min(512, seq)
    tq = min(1024, seq)
    assert d == D_MODEL and seq % tq == 0 and seq % tm == 0 and tq % tm == 0 and tq % WINDOW == 0
    xf = x.reshape(t, d)

    hpg = N_HEADS // N_KV_HEADS
    wq_p = (w_in[:, :ATTN_DIM].reshape(d, N_KV_HEADS, hpg, HEAD_DIM).transpose(0, 2, 1, 3)
            .reshape(d, ATTN_DIM))
    wq_p = wq_p.astype(BF16)
    w_in_b = w_in.astype(BF16)
    wa_p = (w_attn_proj.reshape(N_KV_HEADS, hpg, HEAD_DIM, d).transpose(1, 0, 2, 3)
            .reshape(ATTN_DIM, d).astype(BF16))
    bd = jnp.asarray(np.kron(np.eye(N_HEADS), np.full((HEAD_DIM, HEAD_DIM), 1.0 / HEAD_DIM)), BF16)
    qg = jnp.tile(q_norm_g, LANES // HEAD_DIM).reshape(1, LANES)
    kg = jnp.tile(k_norm_g, LANES // HEAD_DIM).reshape(1, LANES)
    ctab, s1tab, s2tab = _rope_lane_tables(seq)
    nseq_t = seq // tm

    row = lambda w: pl.BlockSpec((tm, w), lambda i: (i, 0))
    full = lambda a: pl.BlockSpec(a.shape, lambda i: (0,) * a.ndim)
    tab = pl.BlockSpec((tm, LANES), lambda i: (i % nseq_t, 0))
    cbias = conv_b.reshape(1, CONV_DIM)
    q, k, v, cb, cp, ue, ga, gc = pl.pallas_call(
        _proj_kernel,
        grid=(t // tm,),
        in_specs=[row(d), full(norm1_g.reshape(1, d)), full(w_in_b), full(wq_p), full(bd), full(qg), full(kg),
                  tab, tab, tab, full(conv_w), full(cbias)],
        out_specs=[row(ATTN_DIM), row(KV_DIM), row(KV_DIM), row(CONV_DIM), row(CONV_DIM),
                   pl.BlockSpec((16, CONV_DIM), lambda i: (i, 0)), row(D_MODEL), row(D_MODEL)],
        out_shape=[jax.ShapeDtypeStruct((t, w), BF16) for w in (ATTN_DIM, KV_DIM, KV_DIM, CONV_DIM, CONV_DIM)]
                  + [jax.ShapeDtypeStruct((t // tm * 16, CONV_DIM), BF16),
                     jax.ShapeDtypeStruct((t, D_MODEL), BF16), jax.ShapeDtypeStruct((t, D_MODEL), BF16)],
        scratch_shapes=[pltpu.VMEM((tm + 16, CONV_DIM), F32)],
        compiler_params=pltpu.CompilerParams(dimension_semantics=("arbitrary",),
                                             vmem_limit_bytes=VMEM_LIMIT),
        name="proj",
    )(xf, norm1_g.reshape(1, d), w_in_b, wq_p, bd, qg, kg, ctab, s1tab, s2tab, conv_w, cbias)

    wr = jnp.concatenate([w_rg, jnp.zeros((d, 8 - N_GROUPS), F32), w_re,
                          jnp.zeros((d, LANES - ROUTER_ROWS), F32)], axis=1)
    wr_hi = wr.astype(BF16)
    wr2 = jnp.concatenate([wr_hi, (wr - wr_hi.astype(F32)).astype(BF16)], axis=1)
    br = jnp.concatenate([b_rg, jnp.zeros((8 - N_GROUPS,), F32), b_re,
                          jnp.zeros((LANES - ROUTER_ROWS,), F32)]).reshape(1, LANES)
    mix_params = (conv_w, wa_p, w_conv_proj.astype(BF16), w_out.astype(BF16),
                  norm2_g.reshape(1, d), wr2, br)

    n_chunks = TOKEN_CHUNKS if bsz % TOKEN_CHUNKS == 0 else 1
    out = None
    for c in range(n_chunks):
        out = _chunk_tail(c, n_chunks, out, xf, (q, k, v, cb, cp, ue, ga, gc), attn_sink, mix_params,
                          (w_gate, w_up, w_down), bsz, seq, tq, tm)
    return out.reshape(bsz, seq, d)


def _chunk_tail(c, n_chunks, out_prev, xf, acts, attn_sink, mix_params, expert_w, bsz, seq, tq, tm):
    q, k, v, cb, cp, ue, ga, gc = acts
    d = D_MODEL
    t_all = bsz * seq
    bc = bsz // n_chunks
    b0 = c * bc
    t = bc * seq
    nj = seq // tq
    wpb = tq // WINDOW
    nb_seq = seq // WINDOW
    tpt = tq // tm
    n_edge = t_all // tm
    row_in = lambda w: pl.BlockSpec((tq, w), lambda b, j: ((b0 + b) * nj + j, 0))
    row_out = lambda w: pl.BlockSpec((tq, w), lambda b, j: (b * nj + j, 0))
    prev_blk = pl.BlockSpec((WINDOW, KV_DIM),
                            lambda b, j: ((b0 + b) * nb_seq + jnp.maximum(j * wpb - 1, 0), 0))
    next_blk = pl.BlockSpec((WINDOW, KV_DIM),
                            lambda b, j: ((b0 + b) * nb_seq + jnp.minimum((j + 1) * wpb, nb_seq - 1), 0))
    edge_prev = pl.BlockSpec((16, CONV_DIM),
                             lambda b, j: (jnp.maximum(((b0 + b) * nj + j) * tpt - 1, 0), 0))
    edge_cur = pl.BlockSpec((16 * tpt, CONV_DIM), lambda b, j: ((b0 + b) * nj + j, 0))
    edge_next = pl.BlockSpec((16, CONV_DIM),
                             lambda b, j: (jnp.minimum(((b0 + b) * nj + j + 1) * tpt, n_edge - 1), 0))
    full2 = lambda a: pl.BlockSpec(a.shape, lambda b, j: (0,) * a.ndim)

    x1, h2, logits = pl.pallas_call(
        functools.partial(_mix_kernel, tq=tq, tm=tm),
        grid=(bc, nj),
        in_specs=[pl.BlockSpec(memory_space=pltpu.SMEM),
                  row_in(d), row_in(ATTN_DIM), prev_blk, row_in(KV_DIM), next_blk,
                  prev_blk, row_in(KV_DIM), next_blk,
                  row_in(CONV_DIM), row_in(CONV_DIM), edge_prev, edge_cur, edge_next,
                  row_in(D_MODEL), row_in(D_MODEL)]
                 + [full2(p) for p in mix_params],
        out_specs=[row_out(d), pl.BlockSpec((tq * TOK_SUBLANES, LANES), lambda b, j: (b * nj + j, 0)),
                   row_out(LANES)],
        out_shape=[jax.ShapeDtypeStruct((t, d), F32),
                   jax.ShapeDtypeStruct((t * TOK_SUBLANES, LANES), U32),
                   jax.ShapeDtypeStruct((t, LANES), F32)],
        scratch_shapes=[pltpu.VMEM((tq, CONV_DIM), F32), pltpu.VMEM((tq, ATTN_DIM), BF16)],
        compiler_params=pltpu.CompilerParams(dimension_semantics=("arbitrary", "arbitrary"),
                                             vmem_limit_bytes=VMEM_LIMIT),
        name="mix",
    )(attn_sink, xf, q, k, k, k, v, v, v, cb, cp, ue, ue, ue, ga, gc, *mix_params)

    tr = min(ROUTE_TILE, t)
    assert t % tr == 0 and tr % RANK_CHUNK == 0
    tri = jnp.asarray(np.triu(np.ones((RANK_CHUNK, RANK_CHUNK), np.float32), 1), BF16)
    colr = lambda r: pl.BlockSpec((r, tr), lambda i: (0, i))
    route_i, route_f, counts = pl.pallas_call(
        functools.partial(_route_kernel, tr=tr),
        grid=(t // tr,),
        in_specs=[pl.BlockSpec((tr, LANES), lambda i: (i, 0)), pl.BlockSpec(tri.shape, lambda i: (0, 0))],
        out_specs=[colr(8), pl.BlockSpec((tr, LANES), lambda i: (i, 0)),
                   pl.BlockSpec((N_EXPERTS, LANES), lambda i: (0, 0))],
        out_shape=[jax.ShapeDtypeStruct((8, t), I32), jax.ShapeDtypeStruct((t, LANES), F32),
                   jax.ShapeDtypeStruct((N_EXPERTS, LANES), F32)],
        scratch_shapes=[pltpu.VMEM((N_EXPERTS, LANES), F32)],
        compiler_params=pltpu.CompilerParams(dimension_semantics=("arbitrary",),
                                             vmem_limit_bytes=VMEM_LIMIT),
        name="route",
    )(logits, tri)

    n_rows = 2 * t + N_EXPERTS * MOE_ROWS
    nblk = n_rows // MOE_ROWS
    cnt = counts[:, 0].astype(I32)
    padded = ((cnt + MOE_ROWS - 1) // MOE_ROWS) * MOE_ROWS
    pad_ends = jnp.cumsum(padded)
    pad_starts = pad_ends - padded
    eids = jnp.arange(N_EXPERTS, dtype=I32)[:, None, None]
    dest = jnp.sum(jnp.where(route_i[None, 0:2] == eids, pad_starts[:, None, None], 0), axis=0) \
        + route_i[2:4]
    blk_start = jnp.arange(nblk, dtype=I32) * MOE_ROWS
    block_e = jnp.minimum(jnp.sum((pad_ends[None, :] <= blk_start[:, None]).astype(I32), axis=1),
                          N_EXPERTS - 1)
    nused = (pad_ends[-1:] // MOE_ROWS).astype(I32)

    x_pad = _sc_dispatch(h2.reshape(t, TOK_SUBLANES, LANES), dest[0], dest[1], n_rows)
    x_pad = x_pad.reshape(n_rows * TOK_SUBLANES, LANES)

    w_gate, w_up, w_down = expert_w
    blk_rows = MOE_ROWS * TOK_SUBLANES
    wspec = lambda a, b: pl.BlockSpec((None, a, b), lambda i, be, nu: (be[i], 0, 0))
    y_pad = pl.pallas_call(
        _moe_kernel,
        grid_spec=pltpu.PrefetchScalarGridSpec(
            num_scalar_prefetch=2,
            grid=(nblk,),
            in_specs=[pl.BlockSpec((blk_rows, LANES), lambda i, be, nu: (jnp.minimum(i, nu[0] - 1), 0)),
                      wspec(d, D_FF), wspec(d, D_FF), wspec(D_FF, d)],
            out_specs=pl.BlockSpec((blk_rows, LANES),
                                   lambda i, be, nu: (jnp.where(i < nu[0], i, nblk - 1), 0)),
            scratch_shapes=[pltpu.VMEM((d, D_FF), BF16), pltpu.VMEM((d, D_FF), BF16),
                            pltpu.VMEM((D_FF, d), BF16)]),
        out_shape=jax.ShapeDtypeStruct((n_rows * TOK_SUBLANES, LANES), U32),
        compiler_params=pltpu.CompilerParams(dimension_semantics=("arbitrary",),
                                             vmem_limit_bytes=VMEM_LIMIT),
        name="moe",
    )(block_e, nused, x_pad, w_gate, w_up, w_down)

    y3 = y_pad.reshape(n_rows, TOK_SUBLANES, LANES)
    parts = COMBINE_PARTS if t % (COMBINE_PARTS * 2048) == 0 else 1
    tp = t // parts
    tc = min(512, tp)
    nt = tp // tc
    out = out_prev
    for h in range(parts):
        tok = slice(h * tp, (h + 1) * tp)
        yk = _sc_gather(y3, dest[0, tok], dest[1, tok]).reshape(2 * tp * TOK_SUBLANES, LANES)
        row0 = h * nt
        in_specs = [pl.BlockSpec((tc, d), lambda i, row0=row0: (row0 + i, 0)),
                    pl.BlockSpec((tc, LANES), lambda i, row0=row0: (row0 + i, 0)),
                    pl.BlockSpec((tc * TOK_SUBLANES, LANES), lambda i: (i, 0)),
                    pl.BlockSpec((tc * TOK_SUBLANES, LANES), lambda i: (nt + i, 0))]
        args = [x1, route_f, yk, yk]
        aliases = {}
        if out is not None:
            in_specs.append(pl.BlockSpec(memory_space=pl.ANY))
            args.append(out)
            aliases = {len(args) - 1: 0}
        out = pl.pallas_call(
            functools.partial(_combine_kernel, tm=tc),
            grid=(nt,),
            in_specs=in_specs,
            out_specs=pl.BlockSpec((tc, d), lambda i, o=(c * parts + h) * nt: (o + i, 0)),
            out_shape=jax.ShapeDtypeStruct((t_all, d), F32),
            input_output_aliases=aliases,
            compiler_params=pltpu.CompilerParams(dimension_semantics=("arbitrary",),
                                                 vmem_limit_bytes=VMEM_LIMIT),
            name="combine",
        )(*args)
    return out


def kernel(x, norm1_g, w_in, q_norm_g, k_norm_g, attn_sink, conv_w, conv_b, w_attn_proj, w_conv_proj, w_out, norm2_g, w_router_group, b_router_group, w_router_expert, b_router_expert, w_gate_e, w_up_e, w_down_e):
    for l in range(norm1_g.shape[0]):
        x = _layer(x, norm1_g[l], w_in[l], q_norm_g[l], k_norm_g[l], attn_sink[l], conv_w[l],
                   conv_b[l], w_attn_proj[l], w_conv_proj[l], w_out[l], norm2_g[l],
                   w_router_group[l], b_router_group[l], w_router_expert[l], b_router_expert[l],
                   w_gate_e[l], w_up_e[l], w_down_e[l])
    return x
```

```python
import functools

import jax
import jax.numpy as jnp
import numpy as np
from jax import lax
from jax.experimental import pallas as pl
from jax.experimental.pallas import tpu as pltpu
from jax.experimental.pallas import tpu_sc as plsc

F32 = jnp.float32
BF16 = jnp.bfloat16
I32 = jnp.int32
U32 = jnp.uint32
HI_MASK = np.uint32(0xFFFF0000)

D_MODEL = 1024
N_HEADS = 8
N_KV_HEADS = 2
HEAD_DIM = 64
ATTN_DIM = N_HEADS * HEAD_DIM
KV_DIM = N_KV_HEADS * HEAD_DIM
WINDOW = 128
ROT_DIM = HEAD_DIM // 4
ROPE_THETA = 500000.0
CONV_DIM = D_MODEL // 2
N_GROUPS = 4
EXPERTS_PER_GROUP = 8
N_EXPERTS = N_GROUPS * EXPERTS_PER_GROUP
D_FF = D_MODEL // 4
NORM_EPS = 1e-6
MASK_VALUE = -1e30
NEG_BIG = -3.0e38
LOG2E = 1.4426950408889634

LANES = 128
TOK_SUBLANES = D_MODEL // (2 * LANES)
HALF_D = D_MODEL // 2
SC_WINDOW = 32
ROUTER_ROWS = 8 + N_EXPERTS
ROUTE_TILE = 4096
RANK_CHUNK = 256
MOE_ROWS = 512
MOE_X_SLOTS = 3
TOKEN_CHUNKS = 1
COMBINE_PARTS = 1
VMEM_LIMIT = 56 * 1024 * 1024

_Q0, _K0, _V0, _CB0, _CC0, _CX0, _GA0, _GC0, _END = (
    0, 512, 640, 768, 1280, 1792, 2304, 3328, 4352)


def _store_packed_rows(ref, val, n):
    for c in range(TOK_SUBLANES):
        lo = val[:, c * LANES:(c + 1) * LANES].astype(BF16).astype(F32)
        hi = val[:, HALF_D + c * LANES:HALF_D + (c + 1) * LANES].astype(BF16).astype(F32)
        word = (lax.bitcast_convert_type(lo, U32) >> 16) | (lax.bitcast_convert_type(hi, U32) & HI_MASK)
        ref[pl.ds(c, n, stride=TOK_SUBLANES), :] = word


def _load_packed_rows(ref, n, row0=0):
    words = [ref[pl.ds(row0 + c, n, stride=TOK_SUBLANES), :] for c in range(TOK_SUBLANES)]
    lo = [lax.bitcast_convert_type(w << 16, F32) for w in words]
    hi = [lax.bitcast_convert_type(w & HI_MASK, F32) for w in words]
    return lo + hi


def _rope_chunks(t, c, s1, s2):
    outs = []
    for j in range(t.shape[1] // LANES):
        tj = t[:, j * LANES:(j + 1) * LANES]
        outs.append(tj * c + pltpu.roll(tj, LANES - ROT_DIM // 2, 1) * s1
                    + pltpu.roll(tj, ROT_DIM // 2, 1) * s2)
    return outs[0] if len(outs) == 1 else jnp.concatenate(outs, axis=1)


def _proj_kernel(x_ref, g1_ref, w_ref, wq_ref, bd_ref, qg_ref, kg_ref, c_ref, s1_ref, s2_ref, cw_ref, cbias_ref,
                 q_ref, k_ref, v_ref, cb_ref, cp_ref, ue_ref, ga_ref, gc_ref, ubuf):
    x = x_ref[...]
    ms = jnp.mean(x * x, axis=-1, keepdims=True)
    h = (x * lax.rsqrt(ms + NORM_EPS) * g1_ref[...]).astype(BF16)

    def proj(a, b):
        return jnp.dot(h, w_ref[:, a:b], preferred_element_type=F32)

    c, s1, s2 = c_ref[...], s1_ref[...], s2_ref[...]

    q = jnp.dot(h, wq_ref[...], preferred_element_type=F32)
    qms = jnp.dot((q * q).astype(BF16), bd_ref[...], preferred_element_type=F32)
    qg = jnp.concatenate([qg_ref[...]] * (ATTN_DIM // LANES), axis=1)
    qn = q * lax.rsqrt(qms + NORM_EPS) * qg
    q_ref[...] = (_rope_chunks(qn, c, s1, s2) * (HEAD_DIM ** -0.5 * LOG2E)).astype(BF16)

    k = proj(_K0, _V0)
    kms = jnp.dot((k * k).astype(BF16), bd_ref[0:KV_DIM, 0:KV_DIM], preferred_element_type=F32)
    kn = k * lax.rsqrt(kms + NORM_EPS) * kg_ref[...]
    k_ref[...] = _rope_chunks(kn, c, s1, s2).astype(BF16)

    v_ref[...] = proj(_V0, _CB0).astype(BF16)
    cb_ref[...] = proj(_CB0, _CC0).astype(BF16)
    tm = x.shape[0]
    u = proj(_CC0, _CX0) * proj(_CX0, _GA0)
    ubuf[0:8, :] = jnp.zeros((8, CONV_DIM), F32)
    ubuf[8 + tm:16 + tm, :] = jnp.zeros((8, CONV_DIM), F32)
    ubuf[8:8 + tm, :] = u
    cw = cw_ref[...]
    cp_ref[...] = (ubuf[7:7 + tm, :] * cw[0:1, :] + u * cw[1:2, :] + ubuf[9:9 + tm, :] * cw[2:3, :]
                   + cbias_ref[...]).astype(BF16)
    ue_ref[...] = jnp.concatenate([u[0:8, :], u[tm - 8:tm, :]], axis=0).astype(BF16)
    ga_ref[...] = jax.nn.sigmoid(proj(_GA0, _GC0)).astype(BF16)
    gc_ref[...] = jax.nn.sigmoid(proj(_GC0, _END)).astype(BF16)


def _mix_kernel(sink_ref, x_ref, q_ref, kp_ref, kc_ref, kn_ref, vp_ref, vc_ref, vn_ref,
                cb_ref, cp_ref, uep_ref, uec_ref, uen_ref, ga_ref, gc_ref,
                cw_ref, wa_ref, wc_ref, wo_ref, g2_ref,
                wr_ref, br_ref,
                x1_ref, h2_ref, lg_ref,
                cscr, attn_scr, *, tq, tm):
    j = pl.program_id(1)
    nj = pl.num_programs(1)
    nsub = tq // WINDOW

    kbuf = jnp.concatenate([kp_ref[...], kc_ref[...], kn_ref[...]], axis=0)
    vbuf = jnp.concatenate([vp_ref[...], vc_ref[...], vn_ref[...]], axis=0)
    low_half = lax.broadcasted_iota(I32, (WINDOW, LANES), 1) < HEAD_DIM
    low_half3 = lax.broadcasted_iota(I32, (3 * WINDOW, LANES), 1) < HEAD_DIM
    qrow = lax.broadcasted_iota(I32, (WINDOW, WINDOW), 0)
    kcol = lax.broadcasted_iota(I32, (WINDOW, WINDOW), 1)
    bias_lo = jnp.where(kcol >= qrow, 0.0, MASK_VALUE)
    bias_hi = jnp.where(kcol <= qrow, 0.0, MASK_VALUE)
    one = jnp.ones((3 * WINDOW, LANES), BF16)
    for n in range(nsub):
        blo = jnp.where(j > 0, bias_lo, MASK_VALUE) if n == 0 else bias_lo
        bhi = jnp.where(j < nj - 1, bias_hi, MASK_VALUE) if n == nsub - 1 else bias_hi
        qs = q_ref[n * WINDOW:(n + 1) * WINDOW, :]
        zero = jnp.zeros((WINDOW, LANES), BF16)
        q_stack = []
        for hd in range(N_HEADS):
            col = qs[:, (hd % 4) * LANES:(hd % 4 + 1) * LANES]
            keep = low_half if hd < 4 else jnp.logical_not(low_half)
            q_stack.append(jnp.where(keep, col, zero))
        q_stack = jnp.concatenate(q_stack, axis=0)
        kwin = kbuf[n * WINDOW:(n + 3) * WINDOW, :]
        vwin = vbuf[n * WINDOW:(n + 3) * WINDOW, :]
        s = lax.dot_general(q_stack, kwin, (((1,), (1,)), ((), ())),
                            preferred_element_type=F32)
        p_list, es_list = [], []
        for hd in range(N_HEADS):
            sink = sink_ref[hd] * LOG2E
            rows = slice(hd * WINDOW, (hd + 1) * WINDOW)
            s0 = s[rows, 0:WINDOW] + blo
            s1 = s[rows, WINDOW:2 * WINDOW]
            s2 = s[rows, 2 * WINDOW:3 * WINDOW] + bhi
            m = jnp.max(jnp.maximum(jnp.maximum(s0, s1), s2), axis=-1, keepdims=True)
            m = jnp.maximum(m, sink)
            p_list.append(jnp.concatenate(
                [jnp.exp2(s0 - m).astype(BF16), jnp.exp2(s1 - m).astype(BF16),
                 jnp.exp2(s2 - m).astype(BF16)], axis=1))
            es_list.append(jnp.exp2(sink - m))
        half = N_HEADS // 2
        o_lo = jnp.dot(jnp.concatenate(p_list[:half], axis=0), jnp.where(low_half3, vwin, one),
                       preferred_element_type=F32)
        o_hi = jnp.dot(jnp.concatenate(p_list[half:], axis=0), jnp.where(low_half3, one, vwin),
                       preferred_element_type=F32)
        cols = []
        for c4 in range(half):
            rows = slice(c4 * WINDOW, (c4 + 1) * WINDOW)
            oa, ob = o_lo[rows, :], o_hi[rows, :]
            da = pltpu.roll(oa, HEAD_DIM, 1) + es_list[c4]
            db = pltpu.roll(ob, HEAD_DIM, 1) + es_list[c4 + half]
            cols.append(jnp.where(low_half, oa / da, ob / db))
        attn_scr[n * WINDOW:(n + 1) * WINDOW, :] = jnp.concatenate(cols, axis=1).astype(BF16)
    attn = attn_scr[...]

    cscr[...] = cp_ref[...].astype(F32)
    cw = cw_ref[...]
    uec = uec_ref[...].astype(F32)
    n_tiles = tq // tm
    for p in range(n_tiles):
        if p == 0:
            prev_last = uep_ref[...].astype(F32)[15:16, :] * (j > 0).astype(F32)
        else:
            prev_last = uec[16 * p - 1:16 * p, :]
        cscr[p * tm:p * tm + 1, :] = cscr[p * tm:p * tm + 1, :] + prev_last * cw[0:1, :]
        if p == n_tiles - 1:
            next_first = uen_ref[...].astype(F32)[0:1, :] * (j < nj - 1).astype(F32)
        else:
            next_first = uec[16 * (p + 1):16 * (p + 1) + 1, :]
        last = (p + 1) * tm - 1
        cscr[last:last + 1, :] = cscr[last:last + 1, :] + next_first * cw[2:3, :]
    cgate = (cb_ref[...].astype(F32) * cscr[...]).astype(BF16)

    merged = (ga_ref[...].astype(F32) * jnp.dot(attn, wa_ref[...], preferred_element_type=F32)
              + gc_ref[...].astype(F32) * jnp.dot(cgate, wc_ref[...], preferred_element_type=F32))
    x1 = x_ref[...] + jnp.dot(merged.astype(BF16), wo_ref[...], preferred_element_type=F32)
    x1_ref[...] = x1

    ms = jnp.mean(x1 * x1, axis=-1, keepdims=True)
    h2 = x1 * lax.rsqrt(ms + NORM_EPS) * g2_ref[...]
    _store_packed_rows(h2_ref, h2, tq)
    h_hi = h2.astype(BF16)
    h_lo = (h2 - h_hi.astype(F32)).astype(BF16)
    acc = jnp.dot(h_hi, wr_ref[...], preferred_element_type=F32)
    logits = (acc[:, 0:LANES] + acc[:, LANES:2 * LANES]
              + jnp.dot(h_lo, wr_ref[:, 0:LANES], preferred_element_type=F32) + br_ref[...])
    lg_ref[...] = logits


def _route_kernel(lg_ref, tri_ref, ri_ref, rf_ref, cnt_ref, base_ref, *, tr):
    @pl.when(pl.program_id(0) == 0)
    def _():
        base_ref[...] = jnp.zeros_like(base_ref)

    logits = lg_ref[...].T[0:ROUTER_ROWS, :]
    grow = lax.broadcasted_iota(I32, (8, tr), 0)
    gl = jnp.where(grow < N_GROUPS, logits[0:8, :], NEG_BIG)
    gmax = jnp.max(gl, axis=0, keepdims=True)
    grp = jnp.min(jnp.where(gl == gmax, grow, 8), axis=0, keepdims=True)
    p_grp = 1.0 / jnp.sum(jnp.exp(gl - gmax), axis=0, keepdims=True)
    el = logits[8:ROUTER_ROWS, :]
    erow = lax.broadcasted_iota(I32, (N_EXPERTS, tr), 0)
    in_grp = (erow // EXPERTS_PER_GROUP) == grp
    l_a = jnp.where(in_grp, el, NEG_BIG)
    l1 = jnp.max(l_a, axis=0, keepdims=True)
    i1 = jnp.min(jnp.where(l_a == l1, erow, N_EXPERTS), axis=0, keepdims=True)
    l_b = jnp.where(erow == i1, NEG_BIG, l_a)
    l2 = jnp.max(l_b, axis=0, keepdims=True)
    i2 = jnp.min(jnp.where(l_b == l2, erow, N_EXPERTS), axis=0, keepdims=True)
    r21 = jnp.exp(l2 - l1)
    t1 = 1.0 / (1.0 + r21)
    gate1 = p_grp * t1
    gate2 = p_grp * (r21 * t1)
    oh1 = (erow == i1).astype(F32)
    oh2 = (erow == i2).astype(F32)
    oh = oh1 + oh2
    base = base_ref[...][:, 0:1]
    r1, r2 = [], []
    for c in range(tr // RANK_CHUNK):
        cols = slice(c * RANK_CHUNK, (c + 1) * RANK_CHUNK)
        ohc = oh[:, cols]
        before = jnp.dot(ohc.astype(BF16), tri_ref[...], preferred_element_type=F32) + base
        r1.append(jnp.sum(oh1[:, cols] * before, axis=0, keepdims=True))
        r2.append(jnp.sum(oh2[:, cols] * before, axis=0, keepdims=True))
        base = base + jnp.sum(ohc, axis=1, keepdims=True)
    rank1 = jnp.concatenate(r1, axis=1)
    rank2 = jnp.concatenate(r2, axis=1)
    new_base = jnp.broadcast_to(base, base_ref.shape)
    base_ref[...] = new_base
    cnt_ref[...] = new_base
    zi = jnp.zeros((4, tr), I32)
    ri_ref[...] = jnp.concatenate([i1, i2, rank1.astype(I32), rank2.astype(I32), zi], axis=0)
    rf_ref[...] = jnp.concatenate([gate1, gate2, jnp.zeros((LANES - 2, tr), F32)], axis=0).T


def _moe_kernel(be_ref, nused_ref, x_hbm, wg_ref, wu_ref, wd_ref, y_ref, wg_s, wu_s, wd_s, xbuf, xsem):
    i = pl.program_id(0)
    nused = nused_ref[0]
    blk_rows = MOE_ROWS * TOK_SUBLANES

    def x_copy(blk):
        slot = lax.rem(blk, MOE_X_SLOTS)
        return pltpu.make_async_copy(
            x_hbm.at[pl.ds(pl.multiple_of(blk * blk_rows, blk_rows), blk_rows)],
            xbuf.at[pl.ds(pl.multiple_of(slot * blk_rows, blk_rows), blk_rows)],
            xsem.at[slot])

    @pl.when(i == 0)
    def _():
        x_copy(0).start()

        @pl.when(nused > 1)
        def _():
            x_copy(1).start()

    @pl.when(i + 2 < nused)
    def _():
        x_copy(i + 2).start()

    changed = (i == 0) | (be_ref[i] != be_ref[jnp.maximum(i - 1, 0)])

    def narrow(_, carry):
        wg_s[...] = wg_ref[...].astype(BF16)
        wu_s[...] = wu_ref[...].astype(BF16)
        wd_s[...] = wd_ref[...].astype(BF16)
        return carry

    lax.fori_loop(0, changed.astype(I32), narrow, 0)

    @pl.when(i < nused)
    def _():
        x_copy(i).wait()
        row0 = pl.multiple_of(lax.rem(i, MOE_X_SLOTS) * blk_rows, blk_rows)
        xb = jnp.concatenate([c.astype(BF16) for c in _load_packed_rows(xbuf, MOE_ROWS, row0)], axis=1)
        g = jnp.dot(xb, wg_s[...], preferred_element_type=F32)
        u = jnp.dot(xb, wu_s[...], preferred_element_type=F32)
        hid = (g * jax.nn.sigmoid(g) * u).astype(BF16)
        _store_packed_rows(y_ref, jnp.dot(hid, wd_s[...], preferred_element_type=F32), MOE_ROWS)

    @pl.when(i >= nused)
    def _():
        y_ref[...] = jnp.zeros_like(y_ref)


def _combine_kernel(x1_ref, gate_ref, y0_ref, y1_ref, *rest, tm):
    o_ref = rest[-1]
    g = gate_ref[...]
    g0, g1 = g[:, 0:1], g[:, 1:2]
    y0 = _load_packed_rows(y0_ref, tm)
    y1 = _load_packed_rows(y1_ref, tm)
    for c in range(2 * TOK_SUBLANES):
        cols = slice(c * LANES, (c + 1) * LANES)
        o_ref[:, cols] = x1_ref[:, cols] + g0 * y0[c] + g1 * y1[c]


def _sc_windows(n_tokens):
    info = plsc.get_sparse_core_info()
    n_workers = info.num_cores * info.num_subcores
    assert n_tokens % (n_workers * SC_WINDOW * 2) == 0
    return info.num_cores, n_tokens // (n_workers * SC_WINDOW)


def _sc_dispatch(h3, d0, d1, n_rows):
    t = h3.shape[0]
    n_cores, nwin = _sc_windows(t)
    mesh = plsc.VectorSubcoreMesh(core_axis_name="c", subcore_axis_name="s")
    rows = pltpu.VMEM((SC_WINDOW,) + h3.shape[1:], h3.dtype)
    idx = pltpu.VMEM((nwin, SC_WINDOW), I32)

    @functools.partial(
        pl.kernel, mesh=mesh,
        out_type=jax.ShapeDtypeStruct((n_rows,) + h3.shape[1:], h3.dtype),
        scratch_types=[idx, idx, idx, rows, rows] + [pltpu.SemaphoreType.DMA] * 3,
        name="sc_dispatch")
    def run(h_hbm, d0_hbm, d1_hbm, it_hbm, o_hbm, i0, i1, it, buf_a, buf_b, sem_a, sem_b, sem_s):
        wbase = (lax.axis_index("s") * n_cores + lax.axis_index("c")) * nwin
        pltpu.sync_copy(d0_hbm.at[pl.ds(wbase, nwin)], i0)
        pltpu.sync_copy(d1_hbm.at[pl.ds(wbase, nwin)], i1)
        pltpu.sync_copy(it_hbm.at[pl.ds(wbase, nwin)], it)

        @pl.loop(0, nwin, step=2)
        def _(w):
            ga = pltpu.async_copy(h_hbm.at[it.at[w]], buf_a, sem_a)
            gb = pltpu.async_copy(h_hbm.at[it.at[w + 1]], buf_b, sem_b)
            ga.wait()
            s0 = pltpu.async_copy(buf_a, o_hbm.at[i0.at[w]], sem_s)
            s1 = pltpu.async_copy(buf_a, o_hbm.at[i1.at[w]], sem_s)
            gb.wait()
            s2 = pltpu.async_copy(buf_b, o_hbm.at[i0.at[w + 1]], sem_s)
            s3 = pltpu.async_copy(buf_b, o_hbm.at[i1.at[w + 1]], sem_s)
            s0.wait()
            s1.wait()
            s2.wait()
            s3.wait()

    win = lambda a: a.reshape(-1, SC_WINDOW)
    return run(h3, win(d0), win(d1), win(jnp.arange(t, dtype=I32)))


def _sc_gather(y3, d0, d1):
    t = d0.shape[0]
    n_cores, nwin = _sc_windows(t)
    mesh = plsc.VectorSubcoreMesh(core_axis_name="c", subcore_axis_name="s")
    rows = pltpu.VMEM((SC_WINDOW,) + y3.shape[1:], y3.dtype)
    idx = pltpu.VMEM((nwin, SC_WINDOW), I32)

    @functools.partial(
        pl.kernel, mesh=mesh,
        out_type=jax.ShapeDtypeStruct((2 * t,) + y3.shape[1:], y3.dtype),
        scratch_types=[idx] * 4 + [rows] * 4 + [pltpu.SemaphoreType.DMA] * 5,
        name="sc_gather")
    def run(y_hbm, d0_hbm, d1_hbm, ita_hbm, itb_hbm, o_hbm, i0, i1, ita, itb,
            b0, b1, b2, b3, m0, m1, m2, m3, ms):
        wbase = (lax.axis_index("s") * n_cores + lax.axis_index("c")) * nwin
        pltpu.sync_copy(d0_hbm.at[pl.ds(wbase, nwin)], i0)
        pltpu.sync_copy(d1_hbm.at[pl.ds(wbase, nwin)], i1)
        pltpu.sync_copy(ita_hbm.at[pl.ds(wbase, nwin)], ita)
        pltpu.sync_copy(itb_hbm.at[pl.ds(wbase, nwin)], itb)

        @pl.loop(0, nwin, step=2)
        def _(w):
            g0 = pltpu.async_copy(y_hbm.at[i0.at[w]], b0, m0)
            g1 = pltpu.async_copy(y_hbm.at[i1.at[w]], b1, m1)
            g2 = pltpu.async_copy(y_hbm.at[i0.at[w + 1]], b2, m2)
            g3 = pltpu.async_copy(y_hbm.at[i1.at[w + 1]], b3, m3)
            g0.wait()
            s0 = pltpu.async_copy(b0, o_hbm.at[ita.at[w]], ms)
            g1.wait()
            s1 = pltpu.async_copy(b1, o_hbm.at[itb.at[w]], ms)
            g2.wait()
            s2 = pltpu.async_copy(b2, o_hbm.at[ita.at[w + 1]], ms)
            g3.wait()
            s3 = pltpu.async_copy(b3, o_hbm.at[itb.at[w + 1]], ms)
            s0.wait()
            s1.wait()
            s2.wait()
            s3.wait()

    win = lambda a: a.reshape(-1, SC_WINDOW)
    iota = jnp.arange(2 * t, dtype=I32)
    return run(y3, win(d0), win(d1), win(iota[:t]), win(iota[t:]))


def _rope_lane_tables(seq_len):
    half = ROT_DIM // 2
    inv_freq = ROPE_THETA ** (-np.arange(0, ROT_DIM, 2, dtype=np.float64) / ROT_DIM)
    ang = np.arange(seq_len, dtype=np.float64)[:, None] * inv_freq[None, :]
    cos, sin = np.cos(ang), np.sin(ang)
    m = np.arange(LANES) % HEAD_DIM
    f = m % half
    c = np.where(m[None, :] < ROT_DIM, cos[:, f], 1.0)
    s1 = np.where(m[None, :] < half, -sin[:, f], 0.0)
    s2 = np.where((m[None, :] >= half) & (m[None, :] < ROT_DIM), sin[:, f], 0.0)
    return tuple(jnp.asarray(a, F32) for a in (c, s1, s2))


def _layer(x, norm1_g, w_in, q_norm_g, k_norm_g, attn_sink, conv_w, conv_b, w_attn_proj,
           w_conv_proj, w_out, norm2_g, w_rg, b_rg, w_re, b_re, w_gate, w_up, w_down):
    bsz, seq, d = x.shape
    t = bsz * seq
    tm = min(512, seq)
    tq = min(1024, seq)
    assert d == D_MODEL and seq % tq == 0 and seq % tm == 0 and tq % tm == 0 and tq % WINDOW == 0
    xf = x.reshape(t, d)

    hpg = N_HEADS // N_KV_HEADS
    wq_p = (w_in[:, :ATTN_DIM].reshape(d, N_KV_HEADS, hpg, HEAD_DIM).transpose(0, 2, 1, 3)
            .reshape(d, ATTN_DIM))
    wq_p = wq_p.astype(BF16)
    w_in_b = w_in.astype(BF16)
    wa_p = (w_attn_proj.reshape(N_KV_HEADS, hpg, HEAD_DIM, d).transpose(1, 0, 2, 3)
            .reshape(ATTN_DIM, d).astype(BF16))
    bd = jnp.asarray(np.kron(np.eye(N_HEADS), np.full((HEAD_DIM, HEAD_DIM), 1.0 / HEAD_DIM)), BF16)
    qg = jnp.tile(q_norm_g, LANES // HEAD_DIM).reshape(1, LANES)
    kg = jnp.tile(k_norm_g, LANES // HEAD_DIM).reshape(1, LANES)
    ctab, s1tab, s2tab = _rope_lane_tables(seq)
    nseq_t = seq // tm

    row = lambda w: pl.BlockSpec((tm, w), lambda i: (i, 0))
    full = lambda a: pl.BlockSpec(a.shape, lambda i: (0,) * a.ndim)
    tab = pl.BlockSpec((tm, LANES), lambda i: (i % nseq_t, 0))
    cbias = conv_b.reshape(1, CONV_DIM)
    q, k, v, cb, cp, ue, ga, gc = pl.pallas_call(
        _proj_kernel,
        grid=(t // tm,),
        in_specs=[row(d), full(norm1_g.reshape(1, d)), full(w_in_b), full(wq_p), full(bd), full(qg), full(kg),
                  tab, tab, tab, full(conv_w), full(cbias)],
        out_specs=[row(ATTN_DIM), row(KV_DIM), row(KV_DIM), row(CONV_DIM), row(CONV_DIM),
                   pl.BlockSpec((16, CONV_DIM), lambda i: (i, 0)), row(D_MODEL), row(D_MODEL)],
        out_shape=[jax.ShapeDtypeStruct((t, w), BF16) for w in (ATTN_DIM, KV_DIM, KV_DIM, CONV_DIM, CONV_DIM)]
                  + [jax.ShapeDtypeStruct((t // tm * 16, CONV_DIM), BF16),
                     jax.ShapeDtypeStruct((t, D_MODEL), BF16), jax.ShapeDtypeStruct((t, D_MODEL), BF16)],
        scratch_shapes=[pltpu.VMEM((tm + 16, CONV_DIM), F32)],
        compiler_params=pltpu.CompilerParams(dimension_semantics=("arbitrary",),
                                             vmem_limit_bytes=VMEM_LIMIT),
        name="proj",
    )(xf, norm1_g.reshape(1, d), w_in_b, wq_p, bd, qg, kg, ctab, s1tab, s2tab, conv_w, cbias)

    wr = jnp.concatenate([w_rg, jnp.zeros((d, 8 - N_GROUPS), F32), w_re,
                          jnp.zeros((d, LANES - ROUTER_ROWS), F32)], axis=1)
    wr_hi = wr.astype(BF16)
    wr2 = jnp.concatenate([wr_hi, (wr - wr_hi.astype(F32)).astype(BF16)], axis=1)
    br = jnp.concatenate([b_rg, jnp.zeros((8 - N_GROUPS,), F32), b_re,
                          jnp.zeros((LANES - ROUTER_ROWS,), F32)]).reshape(1, LANES)
    mix_params = (conv_w, wa_p, w_conv_proj.astype(BF16), w_out.astype(BF16),
                  norm2_g.reshape(1, d), wr2, br)

    n_chunks = TOKEN_CHUNKS if bsz % TOKEN_CHUNKS == 0 else 1
    out = None
    for c in range(n_chunks):
        out = _chunk_tail(c, n_chunks, out, xf, (q, k, v, cb, cp, ue, ga, gc), attn_sink, mix_params,
                          (w_gate, w_up, w_down), bsz, seq, tq, tm)
    return out.reshape(bsz, seq, d)


def _chunk_tail(c, n_chunks, out_prev, xf, acts, attn_sink, mix_params, expert_w, bsz, seq, tq, tm):
    q, k, v, cb, cp, ue, ga, gc = acts
    d = D_MODEL
    t_all = bsz * seq
    bc = bsz // n_chunks
    b0 = c * bc
    t = bc * seq
    nj = seq // tq
    wpb = tq // WINDOW
    nb_seq = seq // WINDOW
    tpt = tq // tm
    n_edge = t_all // tm
    row_in = lambda w: pl.BlockSpec((tq, w), lambda b, j: ((b0 + b) * nj + j, 0))
    row_out = lambda w: pl.BlockSpec((tq, w), lambda b, j: (b * nj + j, 0))
    prev_blk = pl.BlockSpec((WINDOW, KV_DIM),
                            lambda b, j: ((b0 + b) * nb_seq + jnp.maximum(j * wpb - 1, 0), 0))
    next_blk = pl.BlockSpec((WINDOW, KV_DIM),
                            lambda b, j: ((b0 + b) * nb_seq + jnp.minimum((j + 1) * wpb, nb_seq - 1), 0))
    edge_prev = pl.BlockSpec((16, CONV_DIM),
                             lambda b, j: (jnp.maximum(((b0 + b) * nj + j) * tpt - 1, 0), 0))
    edge_cur = pl.BlockSpec((16 * tpt, CONV_DIM), lambda b, j: ((b0 + b) * nj + j, 0))
    edge_next = pl.BlockSpec((16, CONV_DIM),
                             lambda b, j: (jnp.minimum(((b0 + b) * nj + j + 1) * tpt, n_edge - 1), 0))
    full2 = lambda a: pl.BlockSpec(a.shape, lambda b, j: (0,) * a.ndim)

    x1, h2, logits = pl.pallas_call(
        functools.partial(_mix_kernel, tq=tq, tm=tm),
        grid=(bc, nj),
        in_specs=[pl.BlockSpec(memory_space=pltpu.SMEM),
                  row_in(d), row_in(ATTN_DIM), prev_blk, row_in(KV_DIM), next_blk,
                  prev_blk, row_in(KV_DIM), next_blk,
                  row_in(CONV_DIM), row_in(CONV_DIM), edge_prev, edge_cur, edge_next,
                  row_in(D_MODEL), row_in(D_MODEL)]
                 + [full2(p) for p in mix_params],
        out_specs=[row_out(d), pl.BlockSpec((tq * TOK_SUBLANES, LANES), lambda b, j: (b * nj + j, 0)),
                   row_out(LANES)],
        out_shape=[jax.ShapeDtypeStruct((t, d), F32),
                   jax.ShapeDtypeStruct((t * TOK_SUBLANES, LANES), U32),
                   jax.ShapeDtypeStruct((t, LANES), F32)],
        scratch_shapes=[pltpu.VMEM((tq, CONV_DIM), F32), pltpu.VMEM((tq, ATTN_DIM), BF16)],
        compiler_params=pltpu.CompilerParams(dimension_semantics=("arbitrary", "arbitrary"),
                                             vmem_limit_bytes=VMEM_LIMIT),
        name="mix",
    )(attn_sink, xf, q, k, k, k, v, v, v, cb, cp, ue, ue, ue, ga, gc, *mix_params)

    tr = min(ROUTE_TILE, t)
    assert t % tr == 0 and tr % RANK_CHUNK == 0
    tri = jnp.asarray(np.triu(np.ones((RANK_CHUNK, RANK_CHUNK), np.float32), 1), BF16)
    colr = lambda r: pl.BlockSpec((r, tr), lambda i: (0, i))
    route_i, route_f, counts = pl.pallas_call(
        functools.partial(_route_kernel, tr=tr),
        grid=(t // tr,),
        in_specs=[pl.BlockSpec((tr, LANES), lambda i: (i, 0)), pl.BlockSpec(tri.shape, lambda i: (0, 0))],
        out_specs=[colr(8), pl.BlockSpec((tr, LANES), lambda i: (i, 0)),
                   pl.BlockSpec((N_EXPERTS, LANES), lambda i: (0, 0))],
        out_shape=[jax.ShapeDtypeStruct((8, t), I32), jax.ShapeDtypeStruct((t, LANES), F32),
                   jax.ShapeDtypeStruct((N_EXPERTS, LANES), F32)],
        scratch_shapes=[pltpu.VMEM((N_EXPERTS, LANES), F32)],
        compiler_params=pltpu.CompilerParams(dimension_semantics=("arbitrary",),
                                             vmem_limit_bytes=VMEM_LIMIT),
        name="route",
    )(logits, tri)

    n_rows = 2 * t + N_EXPERTS * MOE_ROWS
    nblk = n_rows // MOE_ROWS
    cnt = counts[:, 0].astype(I32)
    padded = ((cnt + MOE_ROWS - 1) // MOE_ROWS) * MOE_ROWS
    pad_ends = jnp.cumsum(padded)
    pad_starts = pad_ends - padded
    eids = jnp.arange(N_EXPERTS, dtype=I32)[:, None, None]
    dest = jnp.sum(jnp.where(route_i[None, 0:2] == eids, pad_starts[:, None, None], 0), axis=0) \
        + route_i[2:4]
    blk_start = jnp.arange(nblk, dtype=I32) * MOE_ROWS
    block_e = jnp.minimum(jnp.sum((pad_ends[None, :] <= blk_start[:, None]).astype(I32), axis=1),
                          N_EXPERTS - 1)
    nused = (pad_ends[-1:] // MOE_ROWS).astype(I32)

    x_pad = _sc_dispatch(h2.reshape(t, TOK_SUBLANES, LANES), dest[0], dest[1], n_rows)
    x_pad = x_pad.reshape(n_rows * TOK_SUBLANES, LANES)

    w_gate, w_up, w_down = expert_w
    blk_rows = MOE_ROWS * TOK_SUBLANES
    wspec = lambda a, b: pl.BlockSpec((None, a, b), lambda i, be, nu: (be[i], 0, 0))
    y_pad = pl.pallas_call(
        _moe_kernel,
        grid_spec=pltpu.PrefetchScalarGridSpec(
            num_scalar_prefetch=2,
            grid=(nblk,),
            in_specs=[pl.BlockSpec(memory_space=pl.ANY),
                      wspec(d, D_FF), wspec(d, D_FF), wspec(D_FF, d)],
            out_specs=pl.BlockSpec((blk_rows, LANES),
                                   lambda i, be, nu: (jnp.where(i < nu[0], i, nblk - 1), 0)),
            scratch_shapes=[pltpu.VMEM((d, D_FF), BF16), pltpu.VMEM((d, D_FF), BF16),
                            pltpu.VMEM((D_FF, d), BF16),
                            pltpu.VMEM((MOE_X_SLOTS * blk_rows, LANES), U32),
                            pltpu.SemaphoreType.DMA((MOE_X_SLOTS,))]),
        out_shape=jax.ShapeDtypeStruct((n_rows * TOK_SUBLANES, LANES), U32),
        compiler_params=pltpu.CompilerParams(dimension_semantics=("arbitrary",),
                                             vmem_limit_bytes=VMEM_LIMIT),
        name="moe",
    )(block_e, nused, x_pad, w_gate, w_up, w_down)

    y3 = y_pad.reshape(n_rows, TOK_SUBLANES, LANES)
    parts = COMBINE_PARTS if t % (COMBINE_PARTS * 2048) == 0 else 1
    tp = t // parts
    tc = min(512, tp)
    nt = tp // tc
    out = out_prev
    for h in range(parts):
        tok = slice(h * tp, (h + 1) * tp)
        yk = _sc_gather(y3, dest[0, tok], dest[1, tok]).reshape(2 * tp * TOK_SUBLANES, LANES)
        row0 = h * nt
        in_specs = [pl.BlockSpec((tc, d), lambda i, row0=row0: (row0 + i, 0)),
                    pl.BlockSpec((tc, LANES), lambda i, row0=row0: (row0 + i, 0)),
                    pl.BlockSpec((tc * TOK_SUBLANES, LANES), lambda i: (i, 0)),
                    pl.BlockSpec((tc * TOK_SUBLANES, LANES), lambda i: (nt + i, 0))]
        args = [x1, route_f, yk, yk]
        aliases = {}
        if out is not None:
            in_specs.append(pl.BlockSpec(memory_space=pl.ANY))
            args.append(out)
            aliases = {len(args) - 1: 0}
        out = pl.pallas_call(
            functools.partial(_combine_kernel, tm=tc),
            grid=(nt,),
            in_specs=in_specs,
            out_specs=pl.BlockSpec((tc, d), lambda i, o=(c * parts + h) * nt: (o + i, 0)),
            out_shape=jax.ShapeDtypeStruct((t_all, d), F32),
            input_output_aliases=aliases,
            compiler_params=pltpu.CompilerParams(dimension_semantics=("arbitrary",),
                                                 vmem_limit_bytes=VMEM_LIMIT),
            name="combine",
        )(*args)
    return out


def kernel(x, norm1_g, w_in, q_norm_g, k_norm_g, attn_sink, conv_w, conv_b, w_attn_proj, w_conv_proj, w_out, norm2_g, w_router_group, b_router_group, w_router_expert, b_router_expert, w_gate_e, w_up_e, w_down_e):
    for l in range(norm1_g.shape[0]):
        x = _layer(x, norm1_g[l], w_in[l], q_norm_g[l], k_norm_g[l], attn_sink[l], conv_w[l],
                   conv_b[l], w_attn_proj[l], w_conv_proj[l], w_out[l], norm2_g[l],
                   w_router_group[l], b_router_group[l], w_router_expert[l], b_router_expert[l],
                   w_gate_e[l], w_up_e[l], w_down_e[l])
    return x
```

```python
import functools

import jax
import jax.numpy as jnp
import numpy as np
from jax import lax
from jax.experimental import pallas as pl
from jax.experimental.pallas import tpu as pltpu
from jax.experimental.pallas import tpu_sc as plsc

F32 = jnp.float32
BF16 = jnp.bfloat16
I32 = jnp.int32
U32 = jnp.uint32
HI_MASK = np.uint32(0xFFFF0000)

D_MODEL = 1024
N_HEADS = 8
N_KV_HEADS = 2
HEAD_DIM = 64
ATTN_DIM = N_HEADS * HEAD_DIM
KV_DIM = N_KV_HEADS * HEAD_DIM
WINDOW = 128
ROT_DIM = HEAD_DIM // 4
ROPE_THETA = 500000.0
CONV_DIM = D_MODEL // 2
N_GROUPS = 4
EXPERTS_PER_GROUP = 8
N_EXPERTS = N_GROUPS * EXPERTS_PER_GROUP
D_FF = D_MODEL // 4
NORM_EPS = 1e-6
MASK_VALUE = -1e30
NEG_BIG = -3.0e38
LOG2E = 1.4426950408889634

LANES = 128
TOK_SUBLANES = D_MODEL // (2 * LANES)
HALF_D = D_MODEL // 2
SC_WINDOW = 32
ROUTER_ROWS = 8 + N_EXPERTS
ROUTE_TILE = 4096
RANK_CHUNK = 256
MOE_ROWS = 512
MOE_X_SLOTS = 3
TOKEN_CHUNKS = 1
COMBINE_PARTS = 1
VMEM_LIMIT = 56 * 1024 * 1024

_Q0, _K0, _V0, _CB0, _CC0, _CX0, _GA0, _GC0, _END = (
    0, 512, 640, 768, 1280, 1792, 2304, 3328, 4352)


def _store_packed_rows(ref, val, n):
    for c in range(TOK_SUBLANES):
        lo = val[:, c * LANES:(c + 1) * LANES].astype(BF16).astype(F32)
        hi = val[:, HALF_D + c * LANES:HALF_D + (c + 1) * LANES].astype(BF16).astype(F32)
        word = (lax.bitcast_convert_type(lo, U32) >> 16) | (lax.bitcast_convert_type(hi, U32) & HI_MASK)
        ref[pl.ds(c, n, stride=TOK_SUBLANES), :] = word


def _load_packed_rows(ref, n, row0=0):
    words = [ref[pl.ds(row0 + c, n, stride=TOK_SUBLANES), :] for c in range(TOK_SUBLANES)]
    lo = [lax.bitcast_convert_type(w << 16, F32) for w in words]
    hi = [lax.bitcast_convert_type(w & HI_MASK, F32) for w in words]
    return lo + hi


def _rope_chunks(t, c, s1, s2):
    outs = []
    for j in range(t.shape[1] // LANES):
        tj = t[:, j * LANES:(j + 1) * LANES]
        outs.append(tj * c + pltpu.roll(tj, LANES - ROT_DIM // 2, 1) * s1
                    + pltpu.roll(tj, ROT_DIM // 2, 1) * s2)
    return outs[0] if len(outs) == 1 else jnp.concatenate(outs, axis=1)


def _proj_kernel(x_ref, g1_ref, w_ref, wq_ref, bd_ref, qg_ref, kg_ref, c_ref, s1_ref, s2_ref, cw_ref, cbias_ref,
                 q_ref, k_ref, v_ref, cb_ref, cp_ref, ue_ref, ga_ref, gc_ref, ubuf):
    x = x_ref[...]
    ms = jnp.mean(x * x, axis=-1, keepdims=True)
    h = (x * lax.rsqrt(ms + NORM_EPS) * g1_ref[...]).astype(BF16)

    def proj(a, b):
        return jnp.dot(h, w_ref[:, a:b], preferred_element_type=F32)

    c, s1, s2 = c_ref[...], s1_ref[...], s2_ref[...]

    ga_ref[...] = jax.nn.sigmoid(proj(_GA0, _GC0)).astype(BF16)
    gc_ref[...] = jax.nn.sigmoid(proj(_GC0, _END)).astype(BF16)
    cb_ref[...] = proj(_CB0, _CC0).astype(BF16)
    tm = x.shape[0]
    u = proj(_CC0, _CX0) * proj(_CX0, _GA0)
    ubuf[0:8, :] = jnp.zeros((8, CONV_DIM), F32)
    ubuf[8 + tm:16 + tm, :] = jnp.zeros((8, CONV_DIM), F32)
    ubuf[8:8 + tm, :] = u
    cw = cw_ref[...]
    cp_ref[...] = (ubuf[7:7 + tm, :] * cw[0:1, :] + u * cw[1:2, :] + ubuf[9:9 + tm, :] * cw[2:3, :]
                   + cbias_ref[...]).astype(BF16)
    ue_ref[...] = jnp.concatenate([u[0:8, :], u[tm - 8:tm, :]], axis=0).astype(BF16)
    v_ref[...] = proj(_V0, _CB0).astype(BF16)

    q = jnp.dot(h, wq_ref[...], preferred_element_type=F32)
    qms = jnp.dot((q * q).astype(BF16), bd_ref[...], preferred_element_type=F32)
    qg = jnp.concatenate([qg_ref[...]] * (ATTN_DIM // LANES), axis=1)
    qn = q * lax.rsqrt(qms + NORM_EPS) * qg
    q_ref[...] = (_rope_chunks(qn, c, s1, s2) * (HEAD_DIM ** -0.5 * LOG2E)).astype(BF16)

    k = proj(_K0, _V0)
    kms = jnp.dot((k * k).astype(BF16), bd_ref[0:KV_DIM, 0:KV_DIM], preferred_element_type=F32)
    kn = k * lax.rsqrt(kms + NORM_EPS) * kg_ref[...]
    k_ref[...] = _rope_chunks(kn, c, s1, s2).astype(BF16)


def _mix_kernel(sink_ref, x_ref, q_ref, kp_ref, kc_ref, kn_ref, vp_ref, vc_ref, vn_ref,
                cb_ref, cp_ref, uep_ref, uec_ref, uen_ref, ga_ref, gc_ref,
                cw_ref, wa_ref, wc_ref, wo_ref, g2_ref,
                wr_ref, br_ref,
                x1_ref, h2_ref, lg_ref,
                cscr, attn_scr, *, tq, tm):
    j = pl.program_id(1)
    nj = pl.num_programs(1)
    nsub = tq // WINDOW

    kbuf = jnp.concatenate([kp_ref[...], kc_ref[...], kn_ref[...]], axis=0)
    vbuf = jnp.concatenate([vp_ref[...], vc_ref[...], vn_ref[...]], axis=0)
    low_half = lax.broadcasted_iota(I32, (WINDOW, LANES), 1) < HEAD_DIM
    low_half3 = lax.broadcasted_iota(I32, (3 * WINDOW, LANES), 1) < HEAD_DIM
    qrow = lax.broadcasted_iota(I32, (WINDOW, WINDOW), 0)
    kcol = lax.broadcasted_iota(I32, (WINDOW, WINDOW), 1)
    bias_lo = jnp.where(kcol >= qrow, 0.0, MASK_VALUE)
    bias_hi = jnp.where(kcol <= qrow, 0.0, MASK_VALUE)
    one = jnp.ones((3 * WINDOW, LANES), BF16)
    for n in range(nsub):
        blo = jnp.where(j > 0, bias_lo, MASK_VALUE) if n == 0 else bias_lo
        bhi = jnp.where(j < nj - 1, bias_hi, MASK_VALUE) if n == nsub - 1 else bias_hi
        qs = q_ref[n * WINDOW:(n + 1) * WINDOW, :]
        zero = jnp.zeros((WINDOW, LANES), BF16)
        q_stack = []
        for hd in range(N_HEADS):
            col = qs[:, (hd % 4) * LANES:(hd % 4 + 1) * LANES]
            keep = low_half if hd < 4 else jnp.logical_not(low_half)
            q_stack.append(jnp.where(keep, col, zero))
        q_stack = jnp.concatenate(q_stack, axis=0)
        kwin = kbuf[n * WINDOW:(n + 3) * WINDOW, :]
        vwin = vbuf[n * WINDOW:(n + 3) * WINDOW, :]
        s = lax.dot_general(q_stack, kwin, (((1,), (1,)), ((), ())),
                            preferred_element_type=F32)
        p_list, es_list = [], []
        for hd in range(N_HEADS):
            sink = sink_ref[hd] * LOG2E
            rows = slice(hd * WINDOW, (hd + 1) * WINDOW)
            s0 = s[rows, 0:WINDOW] + blo
            s1 = s[rows, WINDOW:2 * WINDOW]
            s2 = s[rows, 2 * WINDOW:3 * WINDOW] + bhi
            m = jnp.max(jnp.maximum(jnp.maximum(s0, s1), s2), axis=-1, keepdims=True)
            m = jnp.maximum(m, sink)
            p_list.append(jnp.concatenate(
                [jnp.exp2(s0 - m).astype(BF16), jnp.exp2(s1 - m).astype(BF16),
                 jnp.exp2(s2 - m).astype(BF16)], axis=1))
            es_list.append(jnp.exp2(sink - m))
        half = N_HEADS // 2
        o_lo = jnp.dot(jnp.concatenate(p_list[:half], axis=0), jnp.where(low_half3, vwin, one),
                       preferred_element_type=F32)
        o_hi = jnp.dot(jnp.concatenate(p_list[half:], axis=0), jnp.where(low_half3, one, vwin),
                       preferred_element_type=F32)
        cols = []
        for c4 in range(half):
            rows = slice(c4 * WINDOW, (c4 + 1) * WINDOW)
            oa, ob = o_lo[rows, :], o_hi[rows, :]
            da = pltpu.roll(oa, HEAD_DIM, 1) + es_list[c4]
            db = pltpu.roll(ob, HEAD_DIM, 1) + es_list[c4 + half]
            cols.append(jnp.where(low_half, oa / da, ob / db))
        attn_scr[n * WINDOW:(n + 1) * WINDOW, :] = jnp.concatenate(cols, axis=1).astype(BF16)
    attn = attn_scr[...]

    cscr[...] = cp_ref[...].astype(F32)
    cw = cw_ref[...]
    uec = uec_ref[...].astype(F32)
    n_tiles = tq // tm
    for p in range(n_tiles):
        if p == 0:
            prev_last = uep_ref[...].astype(F32)[15:16, :] * (j > 0).astype(F32)
        else:
            prev_last = uec[16 * p - 1:16 * p, :]
        cscr[p * tm:p * tm + 1, :] = cscr[p * tm:p * tm + 1, :] + prev_last * cw[0:1, :]
        if p == n_tiles - 1:
            next_first = uen_ref[...].astype(F32)[0:1, :] * (j < nj - 1).astype(F32)
        else:
            next_first = uec[16 * (p + 1):16 * (p + 1) + 1, :]
        last = (p + 1) * tm - 1
        cscr[last:last + 1, :] = cscr[last:last + 1, :] + next_first * cw[2:3, :]
    cgate = (cb_ref[...].astype(F32) * cscr[...]).astype(BF16)

    merged = (ga_ref[...].astype(F32) * jnp.dot(attn, wa_ref[...], preferred_element_type=F32)
              + gc_ref[...].astype(F32) * jnp.dot(cgate, wc_ref[...], preferred_element_type=F32))
    x1 = x_ref[...] + jnp.dot(merged.astype(BF16), wo_ref[...], preferred_element_type=F32)
    x1_ref[...] = x1

    ms = jnp.mean(x1 * x1, axis=-1, keepdims=True)
    h2 = x1 * lax.rsqrt(ms + NORM_EPS) * g2_ref[...]
    _store_packed_rows(h2_ref, h2, tq)
    h_hi = h2.astype(BF16)
    h_lo = (h2 - h_hi.astype(F32)).astype(BF16)
    acc = jnp.dot(h_hi, wr_ref[...], preferred_element_type=F32)
    logits = (acc[:, 0:LANES] + acc[:, LANES:2 * LANES]
              + jnp.dot(h_lo, wr_ref[:, 0:LANES], preferred_element_type=F32) + br_ref[...])
    lg_ref[...] = logits


def _route_kernel(lg_ref, tri_ref, ri_ref, rf_ref, cnt_ref, base_ref, *, tr):
    @pl.when(pl.program_id(0) == 0)
    def _():
        base_ref[...] = jnp.zeros_like(base_ref)

    logits = lg_ref[...].T[0:ROUTER_ROWS, :]
    grow = lax.broadcasted_iota(I32, (8, tr), 0)
    gl = jnp.where(grow < N_GROUPS, logits[0:8, :], NEG_BIG)
    gmax = jnp.max(gl, axis=0, keepdims=True)
    grp = jnp.min(jnp.where(gl == gmax, grow, 8), axis=0, keepdims=True)
    p_grp = 1.0 / jnp.sum(jnp.exp(gl - gmax), axis=0, keepdims=True)
    el = logits[8:ROUTER_ROWS, :]
    erow = lax.broadcasted_iota(I32, (N_EXPERTS, tr), 0)
    in_grp = (erow // EXPERTS_PER_GROUP) == grp
    l_a = jnp.where(in_grp, el, NEG_BIG)
    l1 = jnp.max(l_a, axis=0, keepdims=True)
    i1 = jnp.min(jnp.where(l_a == l1, erow, N_EXPERTS), axis=0, keepdims=True)
    l_b = jnp.where(erow == i1, NEG_BIG, l_a)
    l2 = jnp.max(l_b, axis=0, keepdims=True)
    i2 = jnp.min(jnp.where(l_b == l2, erow, N_EXPERTS), axis=0, keepdims=True)
    r21 = jnp.exp(l2 - l1)
    t1 = 1.0 / (1.0 + r21)
    gate1 = p_grp * t1
    gate2 = p_grp * (r21 * t1)
    oh1 = (erow == i1).astype(F32)
    oh2 = (erow == i2).astype(F32)
    oh = oh1 + oh2
    base = base_ref[...][:, 0:1]
    r1, r2 = [], []
    for c in range(tr // RANK_CHUNK):
        cols = slice(c * RANK_CHUNK, (c + 1) * RANK_CHUNK)
        ohc = oh[:, cols]
        before = jnp.dot(ohc.astype(BF16), tri_ref[...], preferred_element_type=F32) + base
        r1.append(jnp.sum(oh1[:, cols] * before, axis=0, keepdims=True))
        r2.append(jnp.sum(oh2[:, cols] * before, axis=0, keepdims=True))
        base = base + jnp.sum(ohc, axis=1, keepdims=True)
    rank1 = jnp.concatenate(r1, axis=1)
    rank2 = jnp.concatenate(r2, axis=1)
    new_base = jnp.broadcast_to(base, base_ref.shape)
    base_ref[...] = new_base
    cnt_ref[...] = new_base
    zi = jnp.zeros((4, tr), I32)
    ri_ref[...] = jnp.concatenate([i1, i2, rank1.astype(I32), rank2.astype(I32), zi], axis=0)
    rf_ref[...] = jnp.concatenate([gate1, gate2, jnp.zeros((LANES - 2, tr), F32)], axis=0).T


def _moe_kernel(be_ref, nused_ref, x_hbm, wg_ref, wu_ref, wd_ref, y_ref, wg_s, wu_s, wd_s, xbuf, xsem):
    i = pl.program_id(0)
    nused = nused_ref[0]
    blk_rows = MOE_ROWS * TOK_SUBLANES

    def x_copy(blk):
        slot = lax.rem(blk, MOE_X_SLOTS)
        return pltpu.make_async_copy(
            x_hbm.at[pl.ds(pl.multiple_of(blk * blk_rows, blk_rows), blk_rows)],
            xbuf.at[pl.ds(pl.multiple_of(slot * blk_rows, blk_rows), blk_rows)],
            xsem.at[slot])

    @pl.when(i == 0)
    def _():
        x_copy(0).start()

        @pl.when(nused > 1)
        def _():
            x_copy(1).start()

    @pl.when(i + 2 < nused)
    def _():
        x_copy(i + 2).start()

    changed = (i == 0) | (be_ref[i] != be_ref[jnp.maximum(i - 1, 0)])

    def narrow(_, carry):
        wg_s[...] = wg_ref[...].astype(BF16)
        wu_s[...] = wu_ref[...].astype(BF16)
        wd_s[...] = wd_ref[...].astype(BF16)
        return carry

    lax.fori_loop(0, changed.astype(I32), narrow, 0)

    @pl.when(i < nused)
    def _():
        x_copy(i).wait()
        row0 = pl.multiple_of(lax.rem(i, MOE_X_SLOTS) * blk_rows, blk_rows)
        xb = jnp.concatenate([c.astype(BF16) for c in _load_packed_rows(xbuf, MOE_ROWS, row0)], axis=1)
        g = jnp.dot(xb, wg_s[...], preferred_element_type=F32)
        u = jnp.dot(xb, wu_s[...], preferred_element_type=F32)
        hid = (g * jax.nn.sigmoid(g) * u).astype(BF16)
        _store_packed_rows(y_ref, jnp.dot(hid, wd_s[...], preferred_element_type=F32), MOE_ROWS)

    @pl.when(i >= nused)
    def _():
        y_ref[...] = jnp.zeros_like(y_ref)


def _combine_kernel(x1_ref, gate_ref, y0_ref, y1_ref, *rest, tm):
    o_ref = rest[-1]
    g = gate_ref[...]
    g0, g1 = g[:, 0:1], g[:, 1:2]
    y0 = _load_packed_rows(y0_ref, tm)
    y1 = _load_packed_rows(y1_ref, tm)
    for c in range(2 * TOK_SUBLANES):
        cols = slice(c * LANES, (c + 1) * LANES)
        o_ref[:, cols] = x1_ref[:, cols] + g0 * y0[c] + g1 * y1[c]


def _sc_windows(n_tokens):
    info = plsc.get_sparse_core_info()
    n_workers = info.num_cores * info.num_subcores
    assert n_tokens % (n_workers * SC_WINDOW * 2) == 0
    return info.num_cores, n_tokens // (n_workers * SC_WINDOW)


def _sc_dispatch(h3, d0, d1, n_rows):
    t = h3.shape[0]
    n_cores, nwin = _sc_windows(t)
    mesh = plsc.VectorSubcoreMesh(core_axis_name="c", subcore_axis_name="s")
    rows = pltpu.VMEM((SC_WINDOW,) + h3.shape[1:], h3.dtype)
    idx = pltpu.VMEM((nwin, SC_WINDOW), I32)

    @functools.partial(
        pl.kernel, mesh=mesh,
        out_type=jax.ShapeDtypeStruct((n_rows,) + h3.shape[1:], h3.dtype),
        scratch_types=[idx, idx, idx, rows, rows] + [pltpu.SemaphoreType.DMA] * 3,
        name="sc_dispatch")
    def run(h_hbm, d0_hbm, d1_hbm, it_hbm, o_hbm, i0, i1, it, buf_a, buf_b, sem_a, sem_b, sem_s):
        wbase = (lax.axis_index("s") * n_cores + lax.axis_index("c")) * nwin
        pltpu.sync_copy(d0_hbm.at[pl.ds(wbase, nwin)], i0)
        pltpu.sync_copy(d1_hbm.at[pl.ds(wbase, nwin)], i1)
        pltpu.sync_copy(it_hbm.at[pl.ds(wbase, nwin)], it)

        @pl.loop(0, nwin, step=2)
        def _(w):
            ga = pltpu.async_copy(h_hbm.at[it.at[w]], buf_a, sem_a)
            gb = pltpu.async_copy(h_hbm.at[it.at[w + 1]], buf_b, sem_b)
            ga.wait()
            s0 = pltpu.async_copy(buf_a, o_hbm.at[i0.at[w]], sem_s)
            s1 = pltpu.async_copy(buf_a, o_hbm.at[i1.at[w]], sem_s)
            gb.wait()
            s2 = pltpu.async_copy(buf_b, o_hbm.at[i0.at[w + 1]], sem_s)
            s3 = pltpu.async_copy(buf_b, o_hbm.at[i1.at[w + 1]], sem_s)
            s0.wait()
            s1.wait()
            s2.wait()
            s3.wait()

    win = lambda a: a.reshape(-1, SC_WINDOW)
    return run(h3, win(d0), win(d1), win(jnp.arange(t, dtype=I32)))


def _sc_gather(y3, d0, d1):
    t = d0.shape[0]
    n_cores, nwin = _sc_windows(t)
    mesh = plsc.VectorSubcoreMesh(core_axis_name="c", subcore_axis_name="s")
    rows = pltpu.VMEM((SC_WINDOW,) + y3.shape[1:], y3.dtype)
    idx = pltpu.VMEM((nwin, SC_WINDOW), I32)

    @functools.partial(
        pl.kernel, mesh=mesh,
        out_type=jax.ShapeDtypeStruct((2 * t,) + y3.shape[1:], y3.dtype),
        scratch_types=[idx] * 4 + [rows] * 4 + [pltpu.SemaphoreType.DMA] * 5,
        name="sc_gather")
    def run(y_hbm, d0_hbm, d1_hbm, ita_hbm, itb_hbm, o_hbm, i0, i1, ita, itb,
            b0, b1, b2, b3, m0, m1, m2, m3, ms):
        wbase = (lax.axis_index("s") * n_cores + lax.axis_index("c")) * nwin
        pltpu.sync_copy(d0_hbm.at[pl.ds(wbase, nwin)], i0)
        pltpu.sync_copy(d1_hbm.at[pl.ds(wbase, nwin)], i1)
        pltpu.sync_copy(ita_hbm.at[pl.ds(wbase, nwin)], ita)
        pltpu.sync_copy(itb_hbm.at[pl.ds(wbase, nwin)], itb)

        @pl.loop(0, nwin, step=2)
        def _(w):
            g0 = pltpu.async_copy(y_hbm.at[i0.at[w]], b0, m0)
            g1 = pltpu.async_copy(y_hbm.at[i1.at[w]], b1, m1)
            g2 = pltpu.async_copy(y_hbm.at[i0.at[w + 1]], b2, m2)
            g3 = pltpu.async_copy(y_hbm.at[i1.at[w + 1]], b3, m3)
            g0.wait()
            s0 = pltpu.async_copy(b0, o_hbm.at[ita.at[w]], ms)
            g1.wait()
            s1 = pltpu.async_copy(b1, o_hbm.at[itb.at[w]], ms)
            g2.wait()
            s2 = pltpu.async_copy(b2, o_hbm.at[ita.at[w + 1]], ms)
            g3.wait()
            s3 = pltpu.async_copy(b3, o_hbm.at[itb.at[w + 1]], ms)
            s0.wait()
            s1.wait()
            s2.wait()
            s3.wait()

    win = lambda a: a.reshape(-1, SC_WINDOW)
    iota = jnp.arange(2 * t, dtype=I32)
    return run(y3, win(d0), win(d1), win(iota[:t]), win(iota[t:]))


def _rope_lane_tables(seq_len):
    half = ROT_DIM // 2
    inv_freq = ROPE_THETA ** (-np.arange(0, ROT_DIM, 2, dtype=np.float64) / ROT_DIM)
    ang = np.arange(seq_len, dtype=np.float64)[:, None] * inv_freq[None, :]
    cos, sin = np.cos(ang), np.sin(ang)
    m = np.arange(LANES) % HEAD_DIM
    f = m % half
    c = np.where(m[None, :] < ROT_DIM, cos[:, f], 1.0)
    s1 = np.where(m[None, :] < half, -sin[:, f], 0.0)
    s2 = np.where((m[None, :] >= half) & (m[None, :] < ROT_DIM), sin[:, f], 0.0)
    return tuple(jnp.asarray(a, F32) for a in (c, s1, s2))


def _layer(x, norm1_g, w_in, q_norm_g, k_norm_g, attn_sink, conv_w, conv_b, w_attn_proj,
           w_conv_proj, w_out, norm2_g, w_rg, b_rg, w_re, b_re, w_gate, w_up, w_down):
    bsz, seq, d = x.shape
    t = bsz * seq
    tm = min(512, seq)
    tq = min(1024, seq)
    assert d == D_MODEL and seq % tq == 0 and seq % tm == 0 and tq % tm == 0 and tq % WINDOW == 0
    xf = x.reshape(t, d)

    hpg = N_HEADS // N_KV_HEADS
    wq_p = (w_in[:, :ATTN_DIM].reshape(d, N_KV_HEADS, hpg, HEAD_DIM).transpose(0, 2, 1, 3)
            .reshape(d, ATTN_DIM))
    wq_p = wq_p.astype(BF16)
    w_in_b = w_in.astype(BF16)
    wa_p = (w_attn_proj.reshape(N_KV_HEADS, hpg, HEAD_DIM, d).transpose(1, 0, 2, 3)
            .reshape(ATTN_DIM, d).astype(BF16))
    bd = jnp.asarray(np.kron(np.eye(N_HEADS), np.full((HEAD_DIM, HEAD_DIM), 1.0 / HEAD_DIM)), BF16)
    qg = jnp.tile(q_norm_g, LANES // HEAD_DIM).reshape(1, LANES)
    kg = jnp.tile(k_norm_g, LANES // HEAD_DIM).reshape(1, LANES)
    ctab, s1tab, s2tab = _rope_lane_tables(seq)
    nseq_t = seq // tm

    row = lambda w: pl.BlockSpec((tm, w), lambda i: (i, 0))
    full = lambda a: pl.BlockSpec(a.shape, lambda i: (0,) * a.ndim)
    tab = pl.BlockSpec((tm, LANES), lambda i: (i % nseq_t, 0))
    cbias = conv_b.reshape(1, CONV_DIM)
    q, k, v, cb, cp, ue, ga, gc = pl.pallas_call(
        _proj_kernel,
        grid=(t // tm,),
        in_specs=[row(d), full(norm1_g.reshape(1, d)), full(w_in_b), full(wq_p), full(bd), full(qg), full(kg),
                  tab, tab, tab, full(conv_w), full(cbias)],
        out_specs=[row(ATTN_DIM), row(KV_DIM), row(KV_DIM), row(CONV_DIM), row(CONV_DIM),
                   pl.BlockSpec((16, CONV_DIM), lambda i: (i, 0)), row(D_MODEL), row(D_MODEL)],
        out_shape=[jax.ShapeDtypeStruct((t, w), BF16) for w in (ATTN_DIM, KV_DIM, KV_DIM, CONV_DIM, CONV_DIM)]
                  + [jax.ShapeDtypeStruct((t // tm * 16, CONV_DIM), BF16),
                     jax.ShapeDtypeStruct((t, D_MODEL), BF16), jax.ShapeDtypeStruct((t, D_MODEL), BF16)],
        scratch_shapes=[pltpu.VMEM((tm + 16, CONV_DIM), F32)],
        compiler_params=pltpu.CompilerParams(dimension_semantics=("arbitrary",),
                                             vmem_limit_bytes=VMEM_LIMIT),
        name="proj",
    )(xf, norm1_g.reshape(1, d), w_in_b, wq_p, bd, qg, kg, ctab, s1tab, s2tab, conv_w, cbias)

    wr = jnp.concatenate([w_rg, jnp.zeros((d, 8 - N_GROUPS), F32), w_re,
                          jnp.zeros((d, LANES - ROUTER_ROWS), F32)], axis=1)
    wr_hi = wr.astype(BF16)
    wr2 = jnp.concatenate([wr_hi, (wr - wr_hi.astype(F32)).astype(BF16)], axis=1)
    br = jnp.concatenate([b_rg, jnp.zeros((8 - N_GROUPS,), F32), b_re,
                          jnp.zeros((LANES - ROUTER_ROWS,), F32)]).reshape(1, LANES)
    mix_params = (conv_w, wa_p, w_conv_proj.astype(BF16), w_out.astype(BF16),
                  norm2_g.reshape(1, d), wr2, br)

    n_chunks = TOKEN_CHUNKS if bsz % TOKEN_CHUNKS == 0 else 1
    out = None
    for c in range(n_chunks):
        out = _chunk_tail(c, n_chunks, out, xf, (q, k, v, cb, cp, ue, ga, gc), attn_sink, mix_params,
                          (w_gate, w_up, w_down), bsz, seq, tq, tm)
    return out.reshape(bsz, seq, d)


def _chunk_tail(c, n_chunks, out_prev, xf, acts, attn_sink, mix_params, expert_w, bsz, seq, tq, tm):
    q, k, v, cb, cp, ue, ga, gc = acts
    d = D_MODEL
    t_all = bsz * seq
    bc = bsz // n_chunks
    b0 = c * bc
    t = bc * seq
    nj = seq // tq
    wpb = tq // WINDOW
    nb_seq = seq // WINDOW
    tpt = tq // tm
    n_edge = t_all // tm
    row_in = lambda w: pl.BlockSpec((tq, w), lambda b, j: ((b0 + b) * nj + j, 0))
    row_out = lambda w: pl.BlockSpec((tq, w), lambda b, j: (b * nj + j, 0))
    prev_blk = pl.BlockSpec((WINDOW, KV_DIM),
                            lambda b, j: ((b0 + b) * nb_seq + jnp.maximum(j * wpb - 1, 0), 0))
    next_blk = pl.BlockSpec((WINDOW, KV_DIM),
                            lambda b, j: ((b0 + b) * nb_seq + jnp.minimum((j + 1) * wpb, nb_seq - 1), 0))
    edge_prev = pl.BlockSpec((16, CONV_DIM),
                             lambda b, j: (jnp.maximum(((b0 + b) * nj + j) * tpt - 1, 0), 0))
    edge_cur = pl.BlockSpec((16 * tpt, CONV_DIM), lambda b, j: ((b0 + b) * nj + j, 0))
    edge_next = pl.BlockSpec((16, CONV_DIM),
                             lambda b, j: (jnp.minimum(((b0 + b) * nj + j + 1) * tpt, n_edge - 1), 0))
    full2 = lambda a: pl.BlockSpec(a.shape, lambda b, j: (0,) * a.ndim)

    x1, h2, logits = pl.pallas_call(
        functools.partial(_mix_kernel, tq=tq, tm=tm),
        grid=(bc, nj),
        in_specs=[pl.BlockSpec(memory_space=pltpu.SMEM),
                  row_in(d), row_in(ATTN_DIM), prev_blk, row_in(KV_DIM), next_blk,
                  prev_blk, row_in(KV_DIM), next_blk,
                  row_in(CONV_DIM), row_in(CONV_DIM), edge_prev, edge_cur, edge_next,
                  row_in(D_MODEL), row_in(D_MODEL)]
                 + [full2(p) for p in mix_params],
        out_specs=[row_out(d), pl.BlockSpec((tq * TOK_SUBLANES, LANES), lambda b, j: (b * nj + j, 0)),
                   row_out(LANES)],
        out_shape=[jax.ShapeDtypeStruct((t, d), F32),
                   jax.ShapeDtypeStruct((t * TOK_SUBLANES, LANES), U32),
                   jax.ShapeDtypeStruct((t, LANES), F32)],
        scratch_shapes=[pltpu.VMEM((tq, CONV_DIM), F32), pltpu.VMEM((tq, ATTN_DIM), BF16)],
        compiler_params=pltpu.CompilerParams(dimension_semantics=("arbitrary", "arbitrary"),
                                             vmem_limit_bytes=VMEM_LIMIT),
        name="mix",
    )(attn_sink, xf, q, k, k, k, v, v, v, cb, cp, ue, ue, ue, ga, gc, *mix_params)

    tr = min(ROUTE_TILE, t)
    assert t % tr == 0 and tr % RANK_CHUNK == 0
    tri = jnp.asarray(np.triu(np.ones((RANK_CHUNK, RANK_CHUNK), np.float32), 1), BF16)
    colr = lambda r: pl.BlockSpec((r, tr), lambda i: (0, i))
    route_i, route_f, counts = pl.pallas_call(
        functools.partial(_route_kernel, tr=tr),
        grid=(t // tr,),
        in_specs=[pl.BlockSpec((tr, LANES), lambda i: (i, 0)), pl.BlockSpec(tri.shape, lambda i: (0, 0))],
        out_specs=[colr(8), pl.BlockSpec((tr, LANES), lambda i: (i, 0)),
                   pl.BlockSpec((N_EXPERTS, LANES), lambda i: (0, 0))],
        out_shape=[jax.ShapeDtypeStruct((8, t), I32), jax.ShapeDtypeStruct((t, LANES), F32),
                   jax.ShapeDtypeStruct((N_EXPERTS, LANES), F32)],
        scratch_shapes=[pltpu.VMEM((N_EXPERTS, LANES), F32)],
        compiler_params=pltpu.CompilerParams(dimension_semantics=("arbitrary",),
                                             vmem_limit_bytes=VMEM_LIMIT),
        name="route",
    )(logits, tri)

    n_rows = 2 * t + N_EXPERTS * MOE_ROWS
    nblk = n_rows // MOE_ROWS
    cnt = counts[:, 0].astype(I32)
    padded = ((cnt + MOE_ROWS - 1) // MOE_ROWS) * MOE_ROWS
    pad_ends = jnp.cumsum(padded)
    pad_starts = pad_ends - padded
    eids = jnp.arange(N_EXPERTS, dtype=I32)[:, None, None]
    dest = jnp.sum(jnp.where(route_i[None, 0:2] == eids, pad_starts[:, None, None], 0), axis=0) \
        + route_i[2:4]
    blk_start = jnp.arange(nblk, dtype=I32) * MOE_ROWS
    block_e = jnp.minimum(jnp.sum((pad_ends[None, :] <= blk_start[:, None]).astype(I32), axis=1),
                          N_EXPERTS - 1)
    nused = (pad_ends[-1:] // MOE_ROWS).astype(I32)

    x_pad = _sc_dispatch(h2.reshape(t, TOK_SUBLANES, LANES), dest[0], dest[1], n_rows)
    x_pad = x_pad.reshape(n_rows * TOK_SUBLANES, LANES)

    w_gate, w_up, w_down = expert_w
    blk_rows = MOE_ROWS * TOK_SUBLANES
    wspec = lambda a, b: pl.BlockSpec((None, a, b), lambda i, be, nu: (be[i], 0, 0))
    y_pad = pl.pallas_call(
        _moe_kernel,
        grid_spec=pltpu.PrefetchScalarGridSpec(
            num_scalar_prefetch=2,
            grid=(nblk,),
            in_specs=[pl.BlockSpec(memory_space=pl.ANY),
                      wspec(d, D_FF), wspec(d, D_FF), wspec(D_FF, d)],
            out_specs=pl.BlockSpec((blk_rows, LANES),
                                   lambda i, be, nu: (jnp.where(i < nu[0], i, nblk - 1), 0)),
            scratch_shapes=[pltpu.VMEM((d, D_FF), BF16), pltpu.VMEM((d, D_FF), BF16),
                            pltpu.VMEM((D_FF, d), BF16),
                            pltpu.VMEM((MOE_X_SLOTS * blk_rows, LANES), U32),
                            pltpu.SemaphoreType.DMA((MOE_X_SLOTS,))]),
        out_shape=jax.ShapeDtypeStruct((n_rows * TOK_SUBLANES, LANES), U32),
        compiler_params=pltpu.CompilerParams(dimension_semantics=("arbitrary",),
                                             vmem_limit_bytes=VMEM_LIMIT),
        name="moe",
    )(block_e, nused, x_pad, w_gate, w_up, w_down)

    y3 = y_pad.reshape(n_rows, TOK_SUBLANES, LANES)
    parts = COMBINE_PARTS if t % (COMBINE_PARTS * 2048) == 0 else 1
    tp = t // parts
    tc = min(512, tp)
    nt = tp // tc
    out = out_prev
    for h in range(parts):
        tok = slice(h * tp, (h + 1) * tp)
        yk = _sc_gather(y3, dest[0, tok], dest[1, tok]).reshape(2 * tp * TOK_SUBLANES, LANES)
        row0 = h * nt
        in_specs = [pl.BlockSpec((tc, d), lambda i, row0=row0: (row0 + i, 0)),
                    pl.BlockSpec((tc, LANES), lambda i, row0=row0: (row0 + i, 0)),
                    pl.BlockSpec((tc * TOK_SUBLANES, LANES), lambda i: (i, 0)),
                    pl.BlockSpec((tc * TOK_SUBLANES, LANES), lambda i: (nt + i, 0))]
        args = [x1, route_f, yk, yk]
        aliases = {}
        if out is not None:
            in_specs.append(pl.BlockSpec(memory_space=pl.ANY))
            args.append(out)
            aliases = {len(args) - 1: 0}
        out = pl.pallas_call(
            functools.partial(_combine_kernel, tm=tc),
            grid=(nt,),
            in_specs=in_specs,
            out_specs=pl.BlockSpec((tc, d), lambda i, o=(c * parts + h) * nt: (o + i, 0)),
            out_shape=jax.ShapeDtypeStruct((t_all, d), F32),
            input_output_aliases=aliases,
            compiler_params=pltpu.CompilerParams(dimension_semantics=("arbitrary",),
                                                 vmem_limit_bytes=VMEM_LIMIT),
            name="combine",
        )(*args)
    return out


def kernel(x, norm1_g, w_in, q_norm_g, k_norm_g, attn_sink, conv_w, conv_b, w_attn_proj, w_conv_proj, w_out, norm2_g, w_router_group, b_router_group, w_router_expert, b_router_expert, w_gate_e, w_up_e, w_down_e):
    for l in range(norm1_g.shape[0]):
        x = _layer(x, norm1_g[l], w_in[l], q_norm_g[l], k_norm_g[l], attn_sink[l], conv_w[l],
                   conv_b[l], w_attn_proj[l], w_conv_proj[l], w_out[l], norm2_g[l],
                   w_router_group[l], b_router_group[l], w_router_expert[l], b_router_expert[l],
                   w_gate_e[l], w_up_e[l], w_down_e[l])
    return x
```

```python
import functools

import jax
import jax.numpy as jnp
import numpy as np
from jax import lax
from jax.experimental import pallas as pl
from jax.experimental.pallas import tpu as pltpu
from jax.experimental.pallas import tpu_sc as plsc

F32 = jnp.float32
BF16 = jnp.bfloat16
I32 = jnp.int32
U32 = jnp.uint32
HI_MASK = np.uint32(0xFFFF0000)

D_MODEL = 1024
N_HEADS = 8
N_KV_HEADS = 2
HEAD_DIM = 64
ATTN_DIM = N_HEADS * HEAD_DIM
KV_DIM = N_KV_HEADS * HEAD_DIM
WINDOW = 128
ROT_DIM = HEAD_DIM // 4
ROPE_THETA = 500000.0
CONV_DIM = D_MODEL // 2
N_GROUPS = 4
EXPERTS_PER_GROUP = 8
N_EXPERTS = N_GROUPS * EXPERTS_PER_GROUP
D_FF = D_MODEL // 4
NORM_EPS = 1e-6
MASK_VALUE = -1e30
NEG_BIG = -3.0e38
LOG2E = 1.4426950408889634

LANES = 128
TOK_SUBLANES = D_MODEL // (2 * LANES)
HALF_D = D_MODEL // 2
SC_WINDOW = 32
ROUTER_ROWS = 8 + N_EXPERTS
ROUTE_TILE = 4096
RANK_CHUNK = 256
MOE_ROWS = 512
MOE_X_SLOTS = 3
TOKEN_CHUNKS = 1
COMBINE_PARTS = 1
VMEM_LIMIT = 56 * 1024 * 1024

_Q0, _K0, _V0, _CB0, _CC0, _CX0, _GA0, _GC0, _END = (
    0, 512, 640, 768, 1280, 1792, 2304, 3328, 4352)


def _store_packed_rows(ref, val, n):
    for c in range(TOK_SUBLANES):
        lo = val[:, c * LANES:(c + 1) * LANES].astype(BF16).astype(F32)
        hi = val[:, HALF_D + c * LANES:HALF_D + (c + 1) * LANES].astype(BF16).astype(F32)
        word = (lax.bitcast_convert_type(lo, U32) >> 16) | (lax.bitcast_convert_type(hi, U32) & HI_MASK)
        ref[pl.ds(c, n, stride=TOK_SUBLANES), :] = word


def _load_packed_rows(ref, n, row0=0):
    words = [ref[pl.ds(row0 + c, n, stride=TOK_SUBLANES), :] for c in range(TOK_SUBLANES)]
    lo = [lax.bitcast_convert_type(w << 16, F32) for w in words]
    hi = [lax.bitcast_convert_type(w & HI_MASK, F32) for w in words]
    return lo + hi


def _rope_chunks(t, c, s1, s2):
    outs = []
    for j in range(t.shape[1] // LANES):
        tj = t[:, j * LANES:(j + 1) * LANES]
        outs.append(tj * c + pltpu.roll(tj, LANES - ROT_DIM // 2, 1) * s1
                    + pltpu.roll(tj, ROT_DIM // 2, 1) * s2)
    return outs[0] if len(outs) == 1 else jnp.concatenate(outs, axis=1)


def _proj_kernel(x_ref, g1_ref, w_ref, wq_ref, bd_ref, qg_ref, kg_ref, c_ref, s1_ref, s2_ref, cw_ref, cbias_ref,
                 q_ref, k_ref, v_ref, cb_ref, cp_ref, ue_ref, ga_ref, gc_ref, ubuf):
    x = x_ref[...]
    ms = jnp.mean(x * x, axis=-1, keepdims=True)
    h = (x * lax.rsqrt(ms + NORM_EPS) * g1_ref[...]).astype(BF16)

    def proj(a, b):
        return jnp.dot(h, w_ref[:, a:b], preferred_element_type=F32)

    c, s1, s2 = c_ref[...], s1_ref[...], s2_ref[...]

    ga_ref[...] = jax.nn.sigmoid(proj(_GA0, _GC0)).astype(BF16)
    gc_ref[...] = jax.nn.sigmoid(proj(_GC0, _END)).astype(BF16)
    cb_ref[...] = proj(_CB0, _CC0).astype(BF16)
    tm = x.shape[0]
    u = proj(_CC0, _CX0) * proj(_CX0, _GA0)
    ubuf[0:8, :] = jnp.zeros((8, CONV_DIM), F32)
    ubuf[8 + tm:16 + tm, :] = jnp.zeros((8, CONV_DIM), F32)
    ubuf[8:8 + tm, :] = u
    cw = cw_ref[...]
    cp_ref[...] = (ubuf[7:7 + tm, :] * cw[0:1, :] + u * cw[1:2, :] + ubuf[9:9 + tm, :] * cw[2:3, :]
                   + cbias_ref[...]).astype(BF16)
    ue_ref[...] = jnp.concatenate([u[0:8, :], u[tm - 8:tm, :]], axis=0).astype(BF16)
    v_ref[...] = proj(_V0, _CB0).astype(BF16)

    q = jnp.dot(h, wq_ref[...], preferred_element_type=F32)
    qms = jnp.dot((q * q).astype(BF16), bd_ref[...], preferred_element_type=F32)
    qg = jnp.concatenate([qg_ref[...]] * (ATTN_DIM // LANES), axis=1)
    qn = q * lax.rsqrt(qms + NORM_EPS) * qg
    q_ref[...] = (_rope_chunks(qn, c, s1, s2) * (HEAD_DIM ** -0.5 * LOG2E)).astype(BF16)

    k = proj(_K0, _V0)
    kms = jnp.dot((k * k).astype(BF16), bd_ref[0:KV_DIM, 0:KV_DIM], preferred_element_type=F32)
    kn = k * lax.rsqrt(kms + NORM_EPS) * kg_ref[...]
    k_ref[...] = _rope_chunks(kn, c, s1, s2).astype(BF16)


def _mix_kernel(sink_ref, x_ref, q_ref, kp_ref, kc_ref, kn_ref, vp_ref, vc_ref, vn_ref,
                cb_ref, cp_ref, uep_ref, uec_ref, uen_ref, ga_ref, gc_ref,
                cw_ref, wa_ref, wc_ref, wo_ref, g2_ref,
                wr_ref, br_ref,
                x1_ref, h2_ref, lg_ref,
                cscr, attn_scr, *, tq, tm):
    j = pl.program_id(1)
    nj = pl.num_programs(1)
    nsub = tq // WINDOW

    kbuf = jnp.concatenate([kp_ref[...], kc_ref[...], kn_ref[...]], axis=0)
    vbuf = jnp.concatenate([vp_ref[...], vc_ref[...], vn_ref[...]], axis=0)
    low_half = lax.broadcasted_iota(I32, (WINDOW, LANES), 1) < HEAD_DIM
    low_half3 = lax.broadcasted_iota(I32, (3 * WINDOW, LANES), 1) < HEAD_DIM
    qrow = lax.broadcasted_iota(I32, (WINDOW, WINDOW), 0)
    kcol = lax.broadcasted_iota(I32, (WINDOW, WINDOW), 1)
    bias_lo = jnp.where(kcol >= qrow, 0.0, MASK_VALUE)
    bias_hi = jnp.where(kcol <= qrow, 0.0, MASK_VALUE)
    one = jnp.ones((3 * WINDOW, LANES), BF16)
    def scores(n):
        qs = q_ref[n * WINDOW:(n + 1) * WINDOW, :]
        zero = jnp.zeros((WINDOW, LANES), BF16)
        q_stack = []
        for hd in range(N_HEADS):
            col = qs[:, (hd % 4) * LANES:(hd % 4 + 1) * LANES]
            keep = low_half if hd < 4 else jnp.logical_not(low_half)
            q_stack.append(jnp.where(keep, col, zero))
        q_stack = jnp.concatenate(q_stack, axis=0)
        kwin = kbuf[n * WINDOW:(n + 3) * WINDOW, :]
        return lax.dot_general(q_stack, kwin, (((1,), (1,)), ((), ())),
                               preferred_element_type=F32)

    s_next = scores(0)
    for n in range(nsub):
        blo = jnp.where(j > 0, bias_lo, MASK_VALUE) if n == 0 else bias_lo
        bhi = jnp.where(j < nj - 1, bias_hi, MASK_VALUE) if n == nsub - 1 else bias_hi
        s = s_next
        if n + 1 < nsub:
            s_next = scores(n + 1)
        vwin = vbuf[n * WINDOW:(n + 3) * WINDOW, :]
        p_list, es_list = [], []
        for hd in range(N_HEADS):
            sink = sink_ref[hd] * LOG2E
            rows = slice(hd * WINDOW, (hd + 1) * WINDOW)
            s0 = s[rows, 0:WINDOW] + blo
            s1 = s[rows, WINDOW:2 * WINDOW]
            s2 = s[rows, 2 * WINDOW:3 * WINDOW] + bhi
            m = jnp.max(jnp.maximum(jnp.maximum(s0, s1), s2), axis=-1, keepdims=True)
            m = jnp.maximum(m, sink)
            p_list.append(jnp.concatenate(
                [jnp.exp2(s0 - m).astype(BF16), jnp.exp2(s1 - m).astype(BF16),
                 jnp.exp2(s2 - m).astype(BF16)], axis=1))
            es_list.append(jnp.exp2(sink - m))
        half = N_HEADS // 2
        o_lo = jnp.dot(jnp.concatenate(p_list[:half], axis=0), jnp.where(low_half3, vwin, one),
                       preferred_element_type=F32)
        o_hi = jnp.dot(jnp.concatenate(p_list[half:], axis=0), jnp.where(low_half3, one, vwin),
                       preferred_element_type=F32)
        cols = []
        for c4 in range(half):
            rows = slice(c4 * WINDOW, (c4 + 1) * WINDOW)
            oa, ob = o_lo[rows, :], o_hi[rows, :]
            da = pltpu.roll(oa, HEAD_DIM, 1) + es_list[c4]
            db = pltpu.roll(ob, HEAD_DIM, 1) + es_list[c4 + half]
            cols.append(jnp.where(low_half, oa / da, ob / db))
        attn_scr[n * WINDOW:(n + 1) * WINDOW, :] = jnp.concatenate(cols, axis=1).astype(BF16)
    attn = attn_scr[...]

    cscr[...] = cp_ref[...].astype(F32)
    cw = cw_ref[...]
    uec = uec_ref[...].astype(F32)
    n_tiles = tq // tm
    for p in range(n_tiles):
        if p == 0:
            prev_last = uep_ref[...].astype(F32)[15:16, :] * (j > 0).astype(F32)
        else:
            prev_last = uec[16 * p - 1:16 * p, :]
        cscr[p * tm:p * tm + 1, :] = cscr[p * tm:p * tm + 1, :] + prev_last * cw[0:1, :]
        if p == n_tiles - 1:
            next_first = uen_ref[...].astype(F32)[0:1, :] * (j < nj - 1).astype(F32)
        else:
            next_first = uec[16 * (p + 1):16 * (p + 1) + 1, :]
        last = (p + 1) * tm - 1
        cscr[last:last + 1, :] = cscr[last:last + 1, :] + next_first * cw[2:3, :]
    cgate = (cb_ref[...].astype(F32) * cscr[...]).astype(BF16)

    merged = (ga_ref[...].astype(F32) * jnp.dot(attn, wa_ref[...], preferred_element_type=F32)
              + gc_ref[...].astype(F32) * jnp.dot(cgate, wc_ref[...], preferred_element_type=F32))
    x1 = x_ref[...] + jnp.dot(merged.astype(BF16), wo_ref[...], preferred_element_type=F32)
    x1_ref[...] = x1

    ms = jnp.mean(x1 * x1, axis=-1, keepdims=True)
    h2 = x1 * lax.rsqrt(ms + NORM_EPS) * g2_ref[...]
    _store_packed_rows(h2_ref, h2, tq)
    h_hi = h2.astype(BF16)
    h_lo = (h2 - h_hi.astype(F32)).astype(BF16)
    acc = jnp.dot(h_hi, wr_ref[...], preferred_element_type=F32)
    logits = (acc[:, 0:LANES] + acc[:, LANES:2 * LANES]
              + jnp.dot(h_lo, wr_ref[:, 0:LANES], preferred_element_type=F32) + br_ref[...])
    lg_ref[...] = logits


def _route_kernel(lg_ref, tri_ref, ri_ref, rf_ref, cnt_ref, base_ref, *, tr):
    @pl.when(pl.program_id(0) == 0)
    def _():
        base_ref[...] = jnp.zeros_like(base_ref)

    logits = lg_ref[...].T[0:ROUTER_ROWS, :]
    grow = lax.broadcasted_iota(I32, (8, tr), 0)
    gl = jnp.where(grow < N_GROUPS, logits[0:8, :], NEG_BIG)
    gmax = jnp.max(gl, axis=0, keepdims=True)
    grp = jnp.min(jnp.where(gl == gmax, grow, 8), axis=0, keepdims=True)
    p_grp = 1.0 / jnp.sum(jnp.exp(gl - gmax), axis=0, keepdims=True)
    el = logits[8:ROUTER_ROWS, :]
    erow = lax.broadcasted_iota(I32, (N_EXPERTS, tr), 0)
    in_grp = (erow // EXPERTS_PER_GROUP) == grp
    l_a = jnp.where(in_grp, el, NEG_BIG)
    l1 = jnp.max(l_a, axis=0, keepdims=True)
    i1 = jnp.min(jnp.where(l_a == l1, erow, N_EXPERTS), axis=0, keepdims=True)
    l_b = jnp.where(erow == i1, NEG_BIG, l_a)
    l2 = jnp.max(l_b, axis=0, keepdims=True)
    i2 = jnp.min(jnp.where(l_b == l2, erow, N_EXPERTS), axis=0, keepdims=True)
    r21 = jnp.exp(l2 - l1)
    t1 = 1.0 / (1.0 + r21)
    gate1 = p_grp * t1
    gate2 = p_grp * (r21 * t1)
    oh1 = (erow == i1).astype(F32)
    oh2 = (erow == i2).astype(F32)
    oh = oh1 + oh2
    base = base_ref[...][:, 0:1]
    r1, r2 = [], []
    for c in range(tr // RANK_CHUNK):
        cols = slice(c * RANK_CHUNK, (c + 1) * RANK_CHUNK)
        ohc = oh[:, cols]
        before = jnp.dot(ohc.astype(BF16), tri_ref[...], preferred_element_type=F32) + base
        r1.append(jnp.sum(oh1[:, cols] * before, axis=0, keepdims=True))
        r2.append(jnp.sum(oh2[:, cols] * before, axis=0, keepdims=True))
        base = base + jnp.sum(ohc, axis=1, keepdims=True)
    rank1 = jnp.concatenate(r1, axis=1)
    rank2 = jnp.concatenate(r2, axis=1)
    new_base = jnp.broadcast_to(base, base_ref.shape)
    base_ref[...] = new_base
    cnt_ref[...] = new_base
    zi = jnp.zeros((4, tr), I32)
    ri_ref[...] = jnp.concatenate([i1, i2, rank1.astype(I32), rank2.astype(I32), zi], axis=0)
    rf_ref[...] = jnp.concatenate([gate1, gate2, jnp.zeros((LANES - 2, tr), F32)], axis=0).T


def _moe_kernel(be_ref, nused_ref, x_hbm, wg_ref, wu_ref, wd_ref, y_ref, wg_s, wu_s, wd_s, xbuf, xsem):
    i = pl.program_id(0)
    nused = nused_ref[0]
    blk_rows = MOE_ROWS * TOK_SUBLANES

    def x_copy(blk):
        slot = lax.rem(blk, MOE_X_SLOTS)
        return pltpu.make_async_copy(
            x_hbm.at[pl.ds(pl.multiple_of(blk * blk_rows, blk_rows), blk_rows)],
            xbuf.at[pl.ds(pl.multiple_of(slot * blk_rows, blk_rows), blk_rows)],
            xsem.at[slot])

    @pl.when(i == 0)
    def _():
        x_copy(0).start()

        @pl.when(nused > 1)
        def _():
            x_copy(1).start()

    @pl.when(i + 2 < nused)
    def _():
        x_copy(i + 2).start()

    changed = (i == 0) | (be_ref[i] != be_ref[jnp.maximum(i - 1, 0)])

    def narrow(_, carry):
        wg_s[...] = wg_ref[...].astype(BF16)
        wu_s[...] = wu_ref[...].astype(BF16)
        wd_s[...] = wd_ref[...].astype(BF16)
        return carry

    lax.fori_loop(0, changed.astype(I32), narrow, 0)

    @pl.when(i < nused)
    def _():
        x_copy(i).wait()
        row0 = pl.multiple_of(lax.rem(i, MOE_X_SLOTS) * blk_rows, blk_rows)
        xb = jnp.concatenate([c.astype(BF16) for c in _load_packed_rows(xbuf, MOE_ROWS, row0)], axis=1)
        g = jnp.dot(xb, wg_s[...], preferred_element_type=F32)
        u = jnp.dot(xb, wu_s[...], preferred_element_type=F32)
        hid = (g * jax.nn.sigmoid(g) * u).astype(BF16)
        _store_packed_rows(y_ref, jnp.dot(hid, wd_s[...], preferred_element_type=F32), MOE_ROWS)

    @pl.when(i >= nused)
    def _():
        y_ref[...] = jnp.zeros_like(y_ref)


def _combine_kernel(x1_ref, gate_ref, y0_ref, y1_ref, *rest, tm):
    o_ref = rest[-1]
    g = gate_ref[...]
    g0, g1 = g[:, 0:1], g[:, 1:2]
    y0 = _load_packed_rows(y0_ref, tm)
    y1 = _load_packed_rows(y1_ref, tm)
    for c in range(2 * TOK_SUBLANES):
        cols = slice(c * LANES, (c + 1) * LANES)
        o_ref[:, cols] = x1_ref[:, cols] + g0 * y0[c] + g1 * y1[c]


def _sc_windows(n_tokens):
    info = plsc.get_sparse_core_info()
    n_workers = info.num_cores * info.num_subcores
    assert n_tokens % (n_workers * SC_WINDOW * 2) == 0
    return info.num_cores, n_tokens // (n_workers * SC_WINDOW)


def _sc_dispatch(h3, d0, d1, n_rows):
    t = h3.shape[0]
    n_cores, nwin = _sc_windows(t)
    mesh = plsc.VectorSubcoreMesh(core_axis_name="c", subcore_axis_name="s")
    rows = pltpu.VMEM((SC_WINDOW,) + h3.shape[1:], h3.dtype)
    idx = pltpu.VMEM((nwin, SC_WINDOW), I32)

    @functools.partial(
        pl.kernel, mesh=mesh,
        out_type=jax.ShapeDtypeStruct((n_rows,) + h3.shape[1:], h3.dtype),
        scratch_types=[idx, idx, idx, rows, rows] + [pltpu.SemaphoreType.DMA] * 3,
        name="sc_dispatch")
    def run(h_hbm, d0_hbm, d1_hbm, it_hbm, o_hbm, i0, i1, it, buf_a, buf_b, sem_a, sem_b, sem_s):
        wbase = (lax.axis_index("s") * n_cores + lax.axis_index("c")) * nwin
        pltpu.sync_copy(d0_hbm.at[pl.ds(wbase, nwin)], i0)
        pltpu.sync_copy(d1_hbm.at[pl.ds(wbase, nwin)], i1)
        pltpu.sync_copy(it_hbm.at[pl.ds(wbase, nwin)], it)

        @pl.loop(0, nwin, step=2)
        def _(w):
            ga = pltpu.async_copy(h_hbm.at[it.at[w]], buf_a, sem_a)
            gb = pltpu.async_copy(h_hbm.at[it.at[w + 1]], buf_b, sem_b)
            ga.wait()
            s0 = pltpu.async_copy(buf_a, o_hbm.at[i0.at[w]], sem_s)
            s1 = pltpu.async_copy(buf_a, o_hbm.at[i1.at[w]], sem_s)
            gb.wait()
            s2 = pltpu.async_copy(buf_b, o_hbm.at[i0.at[w + 1]], sem_s)
            s3 = pltpu.async_copy(buf_b, o_hbm.at[i1.at[w + 1]], sem_s)
            s0.wait()
            s1.wait()
            s2.wait()
            s3.wait()

    win = lambda a: a.reshape(-1, SC_WINDOW)
    return run(h3, win(d0), win(d1), win(jnp.arange(t, dtype=I32)))


def _sc_gather(y3, d0, d1):
    t = d0.shape[0]
    n_cores, nwin = _sc_windows(t)
    mesh = plsc.VectorSubcoreMesh(core_axis_name="c", subcore_axis_name="s")
    rows = pltpu.VMEM((SC_WINDOW,) + y3.shape[1:], y3.dtype)
    idx = pltpu.VMEM((nwin, SC_WINDOW), I32)

    @functools.partial(
        pl.kernel, mesh=mesh,
        out_type=jax.ShapeDtypeStruct((2 * t,) + y3.shape[1:], y3.dtype),
        scratch_types=[idx] * 4 + [rows] * 4 + [pltpu.SemaphoreType.DMA] * 5,
        name="sc_gather")
    def run(y_hbm, d0_hbm, d1_hbm, ita_hbm, itb_hbm, o_hbm, i0, i1, ita, itb,
            b0, b1, b2, b3, m0, m1, m2, m3, ms):
        wbase = (lax.axis_index("s") * n_cores + lax.axis_index("c")) * nwin
        pltpu.sync_copy(d0_hbm.at[pl.ds(wbase, nwin)], i0)
        pltpu.sync_copy(d1_hbm.at[pl.ds(wbase, nwin)], i1)
        pltpu.sync_copy(ita_hbm.at[pl.ds(wbase, nwin)], ita)
        pltpu.sync_copy(itb_hbm.at[pl.ds(wbase, nwin)], itb)

        @pl.loop(0, nwin, step=2)
        def _(w):
            g0 = pltpu.async_copy(y_hbm.at[i0.at[w]], b0, m0)
            g1 = pltpu.async_copy(y_hbm.at[i1.at[w]], b1, m1)
            g2 = pltpu.async_copy(y_hbm.at[i0.at[w + 1]], b2, m2)
            g3 = pltpu.async_copy(y_hbm.at[i1.at[w + 1]], b3, m3)
            g0.wait()
            s0 = pltpu.async_copy(b0, o_hbm.at[ita.at[w]], ms)
            g1.wait()
            s1 = pltpu.async_copy(b1, o_hbm.at[itb.at[w]], ms)
            g2.wait()
            s2 = pltpu.async_copy(b2, o_hbm.at[ita.at[w + 1]], ms)
            g3.wait()
            s3 = pltpu.async_copy(b3, o_hbm.at[itb.at[w + 1]], ms)
            s0.wait()
            s1.wait()
            s2.wait()
            s3.wait()

    win = lambda a: a.reshape(-1, SC_WINDOW)
    iota = jnp.arange(2 * t, dtype=I32)
    return run(y3, win(d0), win(d1), win(iota[:t]), win(iota[t:]))


def _rope_lane_tables(seq_len):
    half = ROT_DIM // 2
    inv_freq = ROPE_THETA ** (-np.arange(0, ROT_DIM, 2, dtype=np.float64) / ROT_DIM)
    ang = np.arange(seq_len, dtype=np.float64)[:, None] * inv_freq[None, :]
    cos, sin = np.cos(ang), np.sin(ang)
    m = np.arange(LANES) % HEAD_DIM
    f = m % half
    c = np.where(m[None, :] < ROT_DIM, cos[:, f], 1.0)
    s1 = np.where(m[None, :] < half, -sin[:, f], 0.0)
    s2 = np.where((m[None, :] >= half) & (m[None, :] < ROT_DIM), sin[:, f], 0.0)
    return tuple(jnp.asarray(a, F32) for a in (c, s1, s2))


def _layer(x, norm1_g, w_in, q_norm_g, k_norm_g, attn_sink, conv_w, conv_b, w_attn_proj,
           w_conv_proj, w_out, norm2_g, w_rg, b_rg, w_re, b_re, w_gate, w_up, w_down):
    bsz, seq, d = x.shape
    t = bsz * seq
    tm = min(512, seq)
    tq = min(1024, seq)
    assert d == D_MODEL and seq % tq == 0 and seq % tm == 0 and tq % tm == 0 and tq % WINDOW == 0
    xf = x.reshape(t, d)

    hpg = N_HEADS // N_KV_HEADS
    wq_p = (w_in[:, :ATTN_DIM].reshape(d, N_KV_HEADS, hpg, HEAD_DIM).transpose(0, 2, 1, 3)
            .reshape(d, ATTN_DIM))
    wq_p = wq_p.astype(BF16)
    w_in_b = w_in.astype(BF16)
    wa_p = (w_attn_proj.reshape(N_KV_HEADS, hpg, HEAD_DIM, d).transpose(1, 0, 2, 3)
            .reshape(ATTN_DIM, d).astype(BF16))
    bd = jnp.asarray(np.kron(np.eye(N_HEADS), np.full((HEAD_DIM, HEAD_DIM), 1.0 / HEAD_DIM)), BF16)
    qg = jnp.tile(q_norm_g, LANES // HEAD_DIM).reshape(1, LANES)
    kg = jnp.tile(k_norm_g, LANES // HEAD_DIM).reshape(1, LANES)
    ctab, s1tab, s2tab = _rope_lane_tables(seq)
    nseq_t = seq // tm

    row = lambda w: pl.BlockSpec((tm, w), lambda i: (i, 0))
    full = lambda a: pl.BlockSpec(a.shape, lambda i: (0,) * a.ndim)
    tab = pl.BlockSpec((tm, LANES), lambda i: (i % nseq_t, 0))
    cbias = conv_b.reshape(1, CONV_DIM)
    q, k, v, cb, cp, ue, ga, gc = pl.pallas_call(
        _proj_kernel,
        grid=(t // tm,),
        in_specs=[row(d), full(norm1_g.reshape(1, d)), full(w_in_b), full(wq_p), full(bd), full(qg), full(kg),
                  tab, tab, tab, full(conv_w), full(cbias)],
        out_specs=[row(ATTN_DIM), row(KV_DIM), row(KV_DIM), row(CONV_DIM), row(CONV_DIM),
                   pl.BlockSpec((16, CONV_DIM), lambda i: (i, 0)), row(D_MODEL), row(D_MODEL)],
        out_shape=[jax.ShapeDtypeStruct((t, w), BF16) for w in (ATTN_DIM, KV_DIM, KV_DIM, CONV_DIM, CONV_DIM)]
                  + [jax.ShapeDtypeStruct((t // tm * 16, CONV_DIM), BF16),
                     jax.ShapeDtypeStruct((t, D_MODEL), BF16), jax.ShapeDtypeStruct((t, D_MODEL), BF16)],
        scratch_shapes=[pltpu.VMEM((tm + 16, CONV_DIM), F32)],
        compiler_params=pltpu.CompilerParams(dimension_semantics=("arbitrary",),
                                             vmem_limit_bytes=VMEM_LIMIT),
        name="proj",
    )(xf, norm1_g.reshape(1, d), w_in_b, wq_p, bd, qg, kg, ctab, s1tab, s2tab, conv_w, cbias)

    wr = jnp.concatenate([w_rg, jnp.zeros((d, 8 - N_GROUPS), F32), w_re,
                          jnp.zeros((d, LANES - ROUTER_ROWS), F32)], axis=1)
    wr_hi = wr.astype(BF16)
    wr2 = jnp.concatenate([wr_hi, (wr - wr_hi.astype(F32)).astype(BF16)], axis=1)
    br = jnp.concatenate([b_rg, jnp.zeros((8 - N_GROUPS,), F32), b_re,
                          jnp.zeros((LANES - ROUTER_ROWS,), F32)]).reshape(1, LANES)
    mix_params = (conv_w, wa_p, w_conv_proj.astype(BF16), w_out.astype(BF16),
                  norm2_g.reshape(1, d), wr2, br)

    n_chunks = TOKEN_CHUNKS if bsz % TOKEN_CHUNKS == 0 else 1
    out = None
    for c in range(n_chunks):
        out = _chunk_tail(c, n_chunks, out, xf, (q, k, v, cb, cp, ue, ga, gc), attn_sink, mix_params,
                          (w_gate, w_up, w_down), bsz, seq, tq, tm)
    return out.reshape(bsz, seq, d)


def _chunk_tail(c, n_chunks, out_prev, xf, acts, attn_sink, mix_params, expert_w, bsz, seq, tq, tm):
    q, k, v, cb, cp, ue, ga, gc = acts
    d = D_MODEL
    t_all = bsz * seq
    bc = bsz // n_chunks
    b0 = c * bc
    t = bc * seq
    nj = seq // tq
    wpb = tq // WINDOW
    nb_seq = seq // WINDOW
    tpt = tq // tm
    n_edge = t_all // tm
    row_in = lambda w: pl.BlockSpec((tq, w), lambda b, j: ((b0 + b) * nj + j, 0))
    row_out = lambda w: pl.BlockSpec((tq, w), lambda b, j: (b * nj + j, 0))
    prev_blk = pl.BlockSpec((WINDOW, KV_DIM),
                            lambda b, j: ((b0 + b) * nb_seq + jnp.maximum(j * wpb - 1, 0), 0))
    next_blk = pl.BlockSpec((WINDOW, KV_DIM),
                            lambda b, j: ((b0 + b) * nb_seq + jnp.minimum((j + 1) * wpb, nb_seq - 1), 0))
    edge_prev = pl.BlockSpec((16, CONV_DIM),
                             lambda b, j: (jnp.maximum(((b0 + b) * nj + j) * tpt - 1, 0), 0))
    edge_cur = pl.BlockSpec((16 * tpt, CONV_DIM), lambda b, j: ((b0 + b) * nj + j, 0))
    edge_next = pl.BlockSpec((16, CONV_DIM),
                             lambda b, j: (jnp.minimum(((b0 + b) * nj + j + 1) * tpt, n_edge - 1), 0))
    full2 = lambda a: pl.BlockSpec(a.shape, lambda b, j: (0,) * a.ndim)

    x1, h2, logits = pl.pallas_call(
        functools.partial(_mix_kernel, tq=tq, tm=tm),
        grid=(bc, nj),
        in_specs=[pl.BlockSpec(memory_space=pltpu.SMEM),
                  row_in(d), row_in(ATTN_DIM), prev_blk, row_in(KV_DIM), next_blk,
                  prev_blk, row_in(KV_DIM), next_blk,
                  row_in(CONV_DIM), row_in(CONV_DIM), edge_prev, edge_cur, edge_next,
                  row_in(D_MODEL), row_in(D_MODEL)]
                 + [full2(p) for p in mix_params],
        out_specs=[row_out(d), pl.BlockSpec((tq * TOK_SUBLANES, LANES), lambda b, j: (b * nj + j, 0)),
                   row_out(LANES)],
        out_shape=[jax.ShapeDtypeStruct((t, d), F32),
                   jax.ShapeDtypeStruct((t * TOK_SUBLANES, LANES), U32),
                   jax.ShapeDtypeStruct((t, LANES), F32)],
        scratch_shapes=[pltpu.VMEM((tq, CONV_DIM), F32), pltpu.VMEM((tq, ATTN_DIM), BF16)],
        compiler_params=pltpu.CompilerParams(dimension_semantics=("arbitrary", "arbitrary"),
                                             vmem_limit_bytes=VMEM_LIMIT),
        name="mix",
    )(attn_sink, xf, q, k, k, k, v, v, v, cb, cp, ue, ue, ue, ga, gc, *mix_params)

    tr = min(ROUTE_TILE, t)
    assert t % tr == 0 and tr % RANK_CHUNK == 0
    tri = jnp.asarray(np.triu(np.ones((RANK_CHUNK, RANK_CHUNK), np.float32), 1), BF16)
    colr = lambda r: pl.BlockSpec((r, tr), lambda i: (0, i))
    route_i, route_f, counts = pl.pallas_call(
        functools.partial(_route_kernel, tr=tr),
        grid=(t // tr,),
        in_specs=[pl.BlockSpec((tr, LANES), lambda i: (i, 0)), pl.BlockSpec(tri.shape, lambda i: (0, 0))],
        out_specs=[colr(8), pl.BlockSpec((tr, LANES), lambda i: (i, 0)),
                   pl.BlockSpec((N_EXPERTS, LANES), lambda i: (0, 0))],
        out_shape=[jax.ShapeDtypeStruct((8, t), I32), jax.ShapeDtypeStruct((t, LANES), F32),
                   jax.ShapeDtypeStruct((N_EXPERTS, LANES), F32)],
        scratch_shapes=[pltpu.VMEM((N_EXPERTS, LANES), F32)],
        compiler_params=pltpu.CompilerParams(dimension_semantics=("arbitrary",),
                                             vmem_limit_bytes=VMEM_LIMIT),
        name="route",
    )(logits, tri)

    n_rows = 2 * t + N_EXPERTS * MOE_ROWS
    nblk = n_rows // MOE_ROWS
    cnt = counts[:, 0].astype(I32)
    padded = ((cnt + MOE_ROWS - 1) // MOE_ROWS) * MOE_ROWS
    pad_ends = jnp.cumsum(padded)
    pad_starts = pad_ends - padded
    eids = jnp.arange(N_EXPERTS, dtype=I32)[:, None, None]
    dest = jnp.sum(jnp.where(route_i[None, 0:2] == eids, pad_starts[:, None, None], 0), axis=0) \
        + route_i[2:4]
    blk_start = jnp.arange(nblk, dtype=I32) * MOE_ROWS
    block_e = jnp.minimum(jnp.sum((pad_ends[None, :] <= blk_start[:, None]).astype(I32), axis=1),
                          N_EXPERTS - 1)
    nused = (pad_ends[-1:] // MOE_ROWS).astype(I32)

    x_pad = _sc_dispatch(h2.reshape(t, TOK_SUBLANES, LANES), dest[0], dest[1], n_rows)
    x_pad = x_pad.reshape(n_rows * TOK_SUBLANES, LANES)

    w_gate, w_up, w_down = expert_w
    blk_rows = MOE_ROWS * TOK_SUBLANES
    wspec = lambda a, b: pl.BlockSpec((None, a, b), lambda i, be, nu: (be[i], 0, 0))
    y_pad = pl.pallas_call(
        _moe_kernel,
        grid_spec=pltpu.PrefetchScalarGridSpec(
            num_scalar_prefetch=2,
            grid=(nblk,),
            in_specs=[pl.BlockSpec(memory_space=pl.ANY),
                      wspec(d, D_FF), wspec(d, D_FF), wspec(D_FF, d)],
            out_specs=pl.BlockSpec((blk_rows, LANES),
                                   lambda i, be, nu: (jnp.where(i < nu[0], i, nblk - 1), 0)),
            scratch_shapes=[pltpu.VMEM((d, D_FF), BF16), pltpu.VMEM((d, D_FF), BF16),
                            pltpu.VMEM((D_FF, d), BF16),
                            pltpu.VMEM((MOE_X_SLOTS * blk_rows, LANES), U32),
                            pltpu.SemaphoreType.DMA((MOE_X_SLOTS,))]),
        out_shape=jax.ShapeDtypeStruct((n_rows * TOK_SUBLANES, LANES), U32),
        compiler_params=pltpu.CompilerParams(dimension_semantics=("arbitrary",),
                                             vmem_limit_bytes=VMEM_LIMIT),
        name="moe",
    )(block_e, nused, x_pad, w_gate, w_up, w_down)

    y3 = y_pad.reshape(n_rows, TOK_SUBLANES, LANES)
    parts = COMBINE_PARTS if t % (COMBINE_PARTS * 2048) == 0 else 1
    tp = t // parts
    tc = min(512, tp)
    nt = tp // tc
    out = out_prev
    for h in range(parts):
        tok = slice(h * tp, (h + 1) * tp)
        yk = _sc_gather(y3, dest[0, tok], dest[1, tok]).reshape(2 * tp * TOK_SUBLANES, LANES)
        row0 = h * nt
        in_specs = [pl.BlockSpec((tc, d), lambda i, row0=row0: (row0 + i, 0)),
                    pl.BlockSpec((tc, LANES), lambda i, row0=row0: (row0 + i, 0)),
                    pl.BlockSpec((tc * TOK_SUBLANES, LANES), lambda i: (i, 0)),
                    pl.BlockSpec((tc * TOK_SUBLANES, LANES), lambda i: (nt + i, 0))]
        args = [x1, route_f, yk, yk]
        aliases = {}
        if out is not None:
            in_specs.append(pl.BlockSpec(memory_space=pl.ANY))
            args.append(out)
            aliases = {len(args) - 1: 0}
        out = pl.pallas_call(
            functools.partial(_combine_kernel, tm=tc),
            grid=(nt,),
            in_specs=in_specs,
            out_specs=pl.BlockSpec((tc, d), lambda i, o=(c * parts + h) * nt: (o + i, 0)),
            out_shape=jax.ShapeDtypeStruct((t_all, d), F32),
            input_output_aliases=aliases,
            compiler_params=pltpu.CompilerParams(dimension_semantics=("arbitrary",),
                                                 vmem_limit_bytes=VMEM_LIMIT),
            name="combine",
        )(*args)
    return out


def kernel(x, norm1_g, w_in, q_norm_g, k_norm_g, attn_sink, conv_w, conv_b, w_attn_proj, w_conv_proj, w_out, norm2_g, w_router_group, b_router_group, w_router_expert, b_router_expert, w_gate_e, w_up_e, w_down_e):
    for l in range(norm1_g.shape[0]):
        x = _layer(x, norm1_g[l], w_in[l], q_norm_g[l], k_norm_g[l], attn_sink[l], conv_w[l],
                   conv_b[l], w_attn_proj[l], w_conv_proj[l], w_out[l], norm2_g[l],
                   w_router_group[l], b_router_group[l], w_router_expert[l], b_router_expert[l],
                   w_gate_e[l], w_up_e[l], w_down_e[l])
    return x
```

```python
import functools

import jax
import jax.numpy as jnp
import numpy as np
from jax import lax
from jax.experimental import pallas as pl
from jax.experimental.pallas import tpu as pltpu
from jax.experimental.pallas import tpu_sc as plsc

F32 = jnp.float32
BF16 = jnp.bfloat16
I32 = jnp.int32
U32 = jnp.uint32
HI_MASK = np.uint32(0xFFFF0000)

D_MODEL = 1024
N_HEADS = 8
N_KV_HEADS = 2
HEAD_DIM = 64
ATTN_DIM = N_HEADS * HEAD_DIM
KV_DIM = N_KV_HEADS * HEAD_DIM
WINDOW = 128
ROT_DIM = HEAD_DIM // 4
ROPE_THETA = 500000.0
CONV_DIM = D_MODEL // 2
N_GROUPS = 4
EXPERTS_PER_GROUP = 8
N_EXPERTS = N_GROUPS * EXPERTS_PER_GROUP
D_FF = D_MODEL // 4
NORM_EPS = 1e-6
MASK_VALUE = -1e30
NEG_BIG = -3.0e38
LOG2E = 1.4426950408889634

LANES = 128
TOK_SUBLANES = D_MODEL // (2 * LANES)
HALF_D = D_MODEL // 2
SC_WINDOW = 32
ROUTER_ROWS = 8 + N_EXPERTS
ROUTE_TILE = 4096
RANK_CHUNK = 256
MOE_ROWS = 512
MOE_X_SLOTS = 3
TOKEN_CHUNKS = 1
COMBINE_PARTS = 1
VMEM_LIMIT = 56 * 1024 * 1024

_Q0, _K0, _V0, _CB0, _CC0, _CX0, _GA0, _GC0, _END = (
    0, 512, 640, 768, 1280, 1792, 2304, 3328, 4352)


def _store_packed_rows(ref, val, n):
    for c in range(TOK_SUBLANES):
        lo = val[:, c * LANES:(c + 1) * LANES].astype(BF16).astype(F32)
        hi = val[:, HALF_D + c * LANES:HALF_D + (c + 1) * LANES].astype(BF16).astype(F32)
        word = (lax.bitcast_convert_type(lo, U32) >> 16) | (lax.bitcast_convert_type(hi, U32) & HI_MASK)
        ref[pl.ds(c, n, stride=TOK_SUBLANES), :] = word


def _load_packed_rows(ref, n, row0=0):
    words = [ref[pl.ds(row0 + c, n, stride=TOK_SUBLANES), :] for c in range(TOK_SUBLANES)]
    lo = [lax.bitcast_convert_type(w << 16, F32) for w in words]
    hi = [lax.bitcast_convert_type(w & HI_MASK, F32) for w in words]
    return lo + hi


def _rope_chunks(t, c, s1, s2):
    outs = []
    for j in range(t.shape[1] // LANES):
        tj = t[:, j * LANES:(j + 1) * LANES]
        outs.append(tj * c + pltpu.roll(tj, LANES - ROT_DIM // 2, 1) * s1
                    + pltpu.roll(tj, ROT_DIM // 2, 1) * s2)
    return outs[0] if len(outs) == 1 else jnp.concatenate(outs, axis=1)


def _proj_kernel(x_ref, g1_ref, w_ref, wq_ref, bd_ref, qg_ref, kg_ref, c_ref, s1_ref, s2_ref, cw_ref, cbias_ref,
                 q_ref, k_ref, v_ref, cb_ref, cp_ref, ue_ref, ga_ref, gc_ref, ubuf):
    x = x_ref[...]
    ms = jnp.mean(x * x, axis=-1, keepdims=True)
    h = (x * lax.rsqrt(ms + NORM_EPS) * g1_ref[...]).astype(BF16)

    def proj(a, b):
        return jnp.dot(h, w_ref[:, a:b], preferred_element_type=F32)

    c, s1, s2 = c_ref[...], s1_ref[...], s2_ref[...]

    ga_ref[...] = jax.nn.sigmoid(proj(_GA0, _GC0)).astype(BF16)
    gc_ref[...] = jax.nn.sigmoid(proj(_GC0, _END)).astype(BF16)
    cb_ref[...] = proj(_CB0, _CC0).astype(BF16)
    tm = x.shape[0]
    u = proj(_CC0, _CX0) * proj(_CX0, _GA0)
    ubuf[0:8, :] = jnp.zeros((8, CONV_DIM), F32)
    ubuf[8 + tm:16 + tm, :] = jnp.zeros((8, CONV_DIM), F32)
    ubuf[8:8 + tm, :] = u
    cw = cw_ref[...]
    cp_ref[...] = (ubuf[7:7 + tm, :] * cw[0:1, :] + u * cw[1:2, :] + ubuf[9:9 + tm, :] * cw[2:3, :]
                   + cbias_ref[...]).astype(BF16)
    ue_ref[...] = jnp.concatenate([u[0:8, :], u[tm - 8:tm, :]], axis=0).astype(BF16)
    v_ref[...] = proj(_V0, _CB0).astype(BF16)

    q = jnp.dot(h, wq_ref[...], preferred_element_type=F32)
    qms = jnp.dot((q * q).astype(BF16), bd_ref[...], preferred_element_type=F32)
    qg = jnp.concatenate([qg_ref[...]] * (ATTN_DIM // LANES), axis=1)
    qn = q * lax.rsqrt(qms + NORM_EPS) * qg
    q_ref[...] = (_rope_chunks(qn, c, s1, s2) * (HEAD_DIM ** -0.5 * LOG2E)).astype(BF16)

    k = proj(_K0, _V0)
    kms = jnp.dot((k * k).astype(BF16), bd_ref[0:KV_DIM, 0:KV_DIM], preferred_element_type=F32)
    kn = k * lax.rsqrt(kms + NORM_EPS) * kg_ref[...]
    k_ref[...] = _rope_chunks(kn, c, s1, s2).astype(BF16)


def _mix_kernel(sink_ref, x_ref, q_ref, kp_ref, kc_ref, kn_ref, vp_ref, vc_ref, vn_ref,
                cb_ref, cp_ref, uep_ref, uec_ref, uen_ref, ga_ref, gc_ref,
                cw_ref, wa_ref, wc_ref, wo_ref, g2_ref,
                wr_ref, br_ref,
                x1_ref, h2_ref, lg_ref,
                cscr, attn_scr, *, tq, tm):
    j = pl.program_id(1)
    nj = pl.num_programs(1)
    nsub = tq // WINDOW

    kbuf = jnp.concatenate([kp_ref[...], kc_ref[...], kn_ref[...]], axis=0)
    vbuf = jnp.concatenate([vp_ref[...], vc_ref[...], vn_ref[...]], axis=0)
    low_half = lax.broadcasted_iota(I32, (WINDOW, LANES), 1) < HEAD_DIM
    low_half3 = lax.broadcasted_iota(I32, (3 * WINDOW, LANES), 1) < HEAD_DIM
    qrow = lax.broadcasted_iota(I32, (WINDOW, WINDOW), 0)
    kcol = lax.broadcasted_iota(I32, (WINDOW, WINDOW), 1)
    bias_lo = jnp.where(kcol >= qrow, 0.0, MASK_VALUE)
    bias_hi = jnp.where(kcol <= qrow, 0.0, MASK_VALUE)
    one = jnp.ones((3 * WINDOW, LANES), BF16)
    def scores(n):
        qs = q_ref[n * WINDOW:(n + 1) * WINDOW, :]
        zero = jnp.zeros((WINDOW, LANES), BF16)
        q_stack = []
        for hd in range(N_HEADS):
            col = qs[:, (hd % 4) * LANES:(hd % 4 + 1) * LANES]
            keep = low_half if hd < 4 else jnp.logical_not(low_half)
            q_stack.append(jnp.where(keep, col, zero))
        q_stack = jnp.concatenate(q_stack, axis=0)
        kwin = kbuf[n * WINDOW:(n + 3) * WINDOW, :]
        return lax.dot_general(q_stack, kwin, (((1,), (1,)), ((), ())),
                               preferred_element_type=F32)

    s_next = scores(0)
    for n in range(nsub):
        blo = jnp.where(j > 0, bias_lo, MASK_VALUE) if n == 0 else bias_lo
        bhi = jnp.where(j < nj - 1, bias_hi, MASK_VALUE) if n == nsub - 1 else bias_hi
        s = s_next
        if n + 1 < nsub:
            s_next = scores(n + 1)
        vwin = vbuf[n * WINDOW:(n + 3) * WINDOW, :]
        p_list, es_list = [], []
        for hd in range(N_HEADS):
            sink = sink_ref[hd] * LOG2E
            rows = slice(hd * WINDOW, (hd + 1) * WINDOW)
            s0 = s[rows, 0:WINDOW] + blo
            s1 = s[rows, WINDOW:2 * WINDOW]
            s2 = s[rows, 2 * WINDOW:3 * WINDOW] + bhi
            m = jnp.max(jnp.maximum(jnp.maximum(s0, s1), s2), axis=-1, keepdims=True)
            m = jnp.maximum(m, sink)
            p_list.append(jnp.concatenate(
                [jnp.exp2(s0 - m).astype(BF16), jnp.exp2(s1 - m).astype(BF16),
                 jnp.exp2(s2 - m).astype(BF16)], axis=1))
            es_list.append(jnp.exp2(sink - m))
        half = N_HEADS // 2
        o_lo = jnp.dot(jnp.concatenate(p_list[:half], axis=0), jnp.where(low_half3, vwin, one),
                       preferred_element_type=F32)
        o_hi = jnp.dot(jnp.concatenate(p_list[half:], axis=0), jnp.where(low_half3, one, vwin),
                       preferred_element_type=F32)
        cols = []
        for c4 in range(half):
            rows = slice(c4 * WINDOW, (c4 + 1) * WINDOW)
            oa, ob = o_lo[rows, :], o_hi[rows, :]
            da = pltpu.roll(oa, HEAD_DIM, 1) + es_list[c4]
            db = pltpu.roll(ob, HEAD_DIM, 1) + es_list[c4 + half]
            cols.append(jnp.where(low_half, oa / da, ob / db))
        attn_scr[n * WINDOW:(n + 1) * WINDOW, :] = jnp.concatenate(cols, axis=1).astype(BF16)
    attn = attn_scr[...]

    cscr[...] = cp_ref[...].astype(F32)
    cw = cw_ref[...]
    uec = uec_ref[...].astype(F32)
    n_tiles = tq // tm
    for p in range(n_tiles):
        if p == 0:
            prev_last = uep_ref[...].astype(F32)[15:16, :] * (j > 0).astype(F32)
        else:
            prev_last = uec[16 * p - 1:16 * p, :]
        cscr[p * tm:p * tm + 1, :] = cscr[p * tm:p * tm + 1, :] + prev_last * cw[0:1, :]
        if p == n_tiles - 1:
            next_first = uen_ref[...].astype(F32)[0:1, :] * (j < nj - 1).astype(F32)
        else:
            next_first = uec[16 * (p + 1):16 * (p + 1) + 1, :]
        last = (p + 1) * tm - 1
        cscr[last:last + 1, :] = cscr[last:last + 1, :] + next_first * cw[2:3, :]
    cgate = (cb_ref[...].astype(F32) * cscr[...]).astype(BF16)

    merged = (ga_ref[...].astype(F32) * jnp.dot(attn, wa_ref[...], preferred_element_type=F32)
              + gc_ref[...].astype(F32) * jnp.dot(cgate, wc_ref[...], preferred_element_type=F32))
    x1 = x_ref[...] + jnp.dot(merged.astype(BF16), wo_ref[...], preferred_element_type=F32)
    x1_ref[...] = x1

    ms = jnp.mean(x1 * x1, axis=-1, keepdims=True)
    h2 = x1 * lax.rsqrt(ms + NORM_EPS) * g2_ref[...]
    _store_packed_rows(h2_ref, h2, tq)
    h_hi = h2.astype(BF16)
    h_lo = (h2 - h_hi.astype(F32)).astype(BF16)
    acc = jnp.dot(h_hi, wr_ref[...], preferred_element_type=F32)
    logits = (acc[:, 0:LANES] + acc[:, LANES:2 * LANES]
              + jnp.dot(h_lo, wr_ref[:, 0:LANES], preferred_element_type=F32) + br_ref[...])
    lg_ref[...] = logits


def _route_kernel(lg_ref, tri_ref, ri_ref, rf_ref, cnt_ref, base_ref, *, tr):
    @pl.when(pl.program_id(0) == 0)
    def _():
        base_ref[...] = jnp.zeros_like(base_ref)

    logits = lg_ref[...].T[0:ROUTER_ROWS, :]
    grow = lax.broadcasted_iota(I32, (8, tr), 0)
    gl = jnp.where(grow < N_GROUPS, logits[0:8, :], NEG_BIG)
    gmax = jnp.max(gl, axis=0, keepdims=True)
    grp = jnp.min(jnp.where(gl == gmax, grow, 8), axis=0, keepdims=True)
    p_grp = 1.0 / jnp.sum(jnp.exp(gl - gmax), axis=0, keepdims=True)
    el = logits[8:ROUTER_ROWS, :]
    erow = lax.broadcasted_iota(I32, (N_EXPERTS, tr), 0)
    in_grp = (erow // EXPERTS_PER_GROUP) == grp
    l_a = jnp.where(in_grp, el, NEG_BIG)
    l1 = jnp.max(l_a, axis=0, keepdims=True)
    i1 = jnp.min(jnp.where(l_a == l1, erow, N_EXPERTS), axis=0, keepdims=True)
    l_b = jnp.where(erow == i1, NEG_BIG, l_a)
    l2 = jnp.max(l_b, axis=0, keepdims=True)
    i2 = jnp.min(jnp.where(l_b == l2, erow, N_EXPERTS), axis=0, keepdims=True)
    r21 = jnp.exp(l2 - l1)
    t1 = 1.0 / (1.0 + r21)
    gate1 = p_grp * t1
    gate2 = p_grp * (r21 * t1)
    oh1 = (erow == i1).astype(F32)
    oh2 = (erow == i2).astype(F32)
    oh = oh1 + oh2
    base = base_ref[...][:, 0:1]
    r1, r2 = [], []
    for c in range(tr // RANK_CHUNK):
        cols = slice(c * RANK_CHUNK, (c + 1) * RANK_CHUNK)
        ohc = oh[:, cols]
        before = jnp.dot(ohc.astype(BF16), tri_ref[...], preferred_element_type=F32) + base
        r1.append(jnp.sum(oh1[:, cols] * before, axis=0, keepdims=True))
        r2.append(jnp.sum(oh2[:, cols] * before, axis=0, keepdims=True))
        base = base + jnp.sum(ohc, axis=1, keepdims=True)
    rank1 = jnp.concatenate(r1, axis=1)
    rank2 = jnp.concatenate(r2, axis=1)
    new_base = jnp.broadcast_to(base, base_ref.shape)
    base_ref[...] = new_base
    cnt_ref[...] = new_base
    zi = jnp.zeros((4, tr), I32)
    ri_ref[...] = jnp.concatenate([i1, i2, rank1.astype(I32), rank2.astype(I32), zi], axis=0)
    rf_ref[...] = jnp.concatenate([gate1, gate2, jnp.zeros((LANES - 2, tr), F32)], axis=0).T


def _moe_kernel(be_ref, first_ref, wslot_ref, nxt_ref, nused_ref, x_hbm, wg_hbm, wu_hbm, wd_hbm, y_ref,
                wg_s, wu_s, wd_s, xbuf, xsem, wst_g, wst_u, wst_d, wsem):
    i = pl.program_id(0)
    nused = nused_ref[0]
    blk_rows = MOE_ROWS * TOK_SUBLANES

    def x_copy(blk):
        slot = lax.rem(blk, MOE_X_SLOTS)
        return pltpu.make_async_copy(
            x_hbm.at[pl.ds(pl.multiple_of(blk * blk_rows, blk_rows), blk_rows)],
            xbuf.at[pl.ds(pl.multiple_of(slot * blk_rows, blk_rows), blk_rows)],
            xsem.at[slot])

    @pl.when(i == 0)
    def _():
        x_copy(0).start()

        @pl.when(nused > 1)
        def _():
            x_copy(1).start()

    @pl.when(i + 2 < nused)
    def _():
        x_copy(i + 2).start()

    def w_copies(e, slot):
        return (pltpu.make_async_copy(wg_hbm.at[e], wst_g.at[pl.ds(slot * D_MODEL, D_MODEL)], wsem.at[slot]),
                pltpu.make_async_copy(wu_hbm.at[e], wst_u.at[pl.ds(slot * D_MODEL, D_MODEL)], wsem.at[slot]),
                pltpu.make_async_copy(wd_hbm.at[e], wst_d.at[pl.ds(slot * D_FF, D_FF)], wsem.at[slot]))

    @pl.when(i == 0)
    def _():
        for cp in w_copies(be_ref[0], 0):
            cp.start()

    def new_run(_, carry):
        slot = wslot_ref[i]
        for cp in w_copies(be_ref[i], slot):
            cp.wait()
        wg_s[...] = wst_g[pl.ds(pl.multiple_of(slot * D_MODEL, D_MODEL), D_MODEL), :].astype(BF16)
        wu_s[...] = wst_u[pl.ds(pl.multiple_of(slot * D_MODEL, D_MODEL), D_MODEL), :].astype(BF16)
        wd_s[...] = wst_d[pl.ds(pl.multiple_of(slot * D_FF, D_FF), D_FF), :].astype(BF16)

        @pl.when(nxt_ref[i] >= 0)
        def _():
            for cp in w_copies(nxt_ref[i], 1 - slot):
                cp.start()
        return carry

    lax.fori_loop(0, first_ref[i], new_run, 0)

    @pl.when(i < nused)
    def _():
        x_copy(i).wait()
        row0 = pl.multiple_of(lax.rem(i, MOE_X_SLOTS) * blk_rows, blk_rows)
        xb = jnp.concatenate([c.astype(BF16) for c in _load_packed_rows(xbuf, MOE_ROWS, row0)], axis=1)
        g = jnp.dot(xb, wg_s[...], preferred_element_type=F32)
        u = jnp.dot(xb, wu_s[...], preferred_element_type=F32)
        hid = (g * jax.nn.sigmoid(g) * u).astype(BF16)
        _store_packed_rows(y_ref, jnp.dot(hid, wd_s[...], preferred_element_type=F32), MOE_ROWS)

    @pl.when(i >= nused)
    def _():
        y_ref[...] = jnp.zeros_like(y_ref)


def _combine_kernel(x1_ref, gate_ref, y0_ref, y1_ref, *rest, tm):
    o_ref = rest[-1]
    g = gate_ref[...]
    g0, g1 = g[:, 0:1], g[:, 1:2]
    y0 = _load_packed_rows(y0_ref, tm)
    y1 = _load_packed_rows(y1_ref, tm)
    for c in range(2 * TOK_SUBLANES):
        cols = slice(c * LANES, (c + 1) * LANES)
        o_ref[:, cols] = x1_ref[:, cols] + g0 * y0[c] + g1 * y1[c]


def _sc_windows(n_tokens):
    info = plsc.get_sparse_core_info()
    n_workers = info.num_cores * info.num_subcores
    assert n_tokens % (n_workers * SC_WINDOW * 2) == 0
    return info.num_cores, n_tokens // (n_workers * SC_WINDOW)


def _sc_dispatch(h3, d0, d1, n_rows):
    t = h3.shape[0]
    n_cores, nwin = _sc_windows(t)
    mesh = plsc.VectorSubcoreMesh(core_axis_name="c", subcore_axis_name="s")
    rows = pltpu.VMEM((SC_WINDOW,) + h3.shape[1:], h3.dtype)
    idx = pltpu.VMEM((nwin, SC_WINDOW), I32)

    @functools.partial(
        pl.kernel, mesh=mesh,
        out_type=jax.ShapeDtypeStruct((n_rows,) + h3.shape[1:], h3.dtype),
        scratch_types=[idx, idx, idx, rows, rows] + [pltpu.SemaphoreType.DMA] * 3,
        name="sc_dispatch")
    def run(h_hbm, d0_hbm, d1_hbm, it_hbm, o_hbm, i0, i1, it, buf_a, buf_b, sem_a, sem_b, sem_s):
        wbase = (lax.axis_index("s") * n_cores + lax.axis_index("c")) * nwin
        pltpu.sync_copy(d0_hbm.at[pl.ds(wbase, nwin)], i0)
        pltpu.sync_copy(d1_hbm.at[pl.ds(wbase, nwin)], i1)
        pltpu.sync_copy(it_hbm.at[pl.ds(wbase, nwin)], it)

        @pl.loop(0, nwin, step=2)
        def _(w):
            ga = pltpu.async_copy(h_hbm.at[it.at[w]], buf_a, sem_a)
            gb = pltpu.async_copy(h_hbm.at[it.at[w + 1]], buf_b, sem_b)
            ga.wait()
            s0 = pltpu.async_copy(buf_a, o_hbm.at[i0.at[w]], sem_s)
            s1 = pltpu.async_copy(buf_a, o_hbm.at[i1.at[w]], sem_s)
            gb.wait()
            s2 = pltpu.async_copy(buf_b, o_hbm.at[i0.at[w + 1]], sem_s)
            s3 = pltpu.async_copy(buf_b, o_hbm.at[i1.at[w + 1]], sem_s)
            s0.wait()
            s1.wait()
            s2.wait()
            s3.wait()

    win = lambda a: a.reshape(-1, SC_WINDOW)
    return run(h3, win(d0), win(d1), win(jnp.arange(t, dtype=I32)))


def _sc_gather(y3, d0, d1):
    t = d0.shape[0]
    n_cores, nwin = _sc_windows(t)
    mesh = plsc.VectorSubcoreMesh(core_axis_name="c", subcore_axis_name="s")
    rows = pltpu.VMEM((SC_WINDOW,) + y3.shape[1:], y3.dtype)
    idx = pltpu.VMEM((nwin, SC_WINDOW), I32)

    @functools.partial(
        pl.kernel, mesh=mesh,
        out_type=jax.ShapeDtypeStruct((2 * t,) + y3.shape[1:], y3.dtype),
        scratch_types=[idx] * 4 + [rows] * 4 + [pltpu.SemaphoreType.DMA] * 5,
        name="sc_gather")
    def run(y_hbm, d0_hbm, d1_hbm, ita_hbm, itb_hbm, o_hbm, i0, i1, ita, itb,
            b0, b1, b2, b3, m0, m1, m2, m3, ms):
        wbase = (lax.axis_index("s") * n_cores + lax.axis_index("c")) * nwin
        pltpu.sync_copy(d0_hbm.at[pl.ds(wbase, nwin)], i0)
        pltpu.sync_copy(d1_hbm.at[pl.ds(wbase, nwin)], i1)
        pltpu.sync_copy(ita_hbm.at[pl.ds(wbase, nwin)], ita)
        pltpu.sync_copy(itb_hbm.at[pl.ds(wbase, nwin)], itb)

        @pl.loop(0, nwin, step=2)
        def _(w):
            g0 = pltpu.async_copy(y_hbm.at[i0.at[w]], b0, m0)
            g1 = pltpu.async_copy(y_hbm.at[i1.at[w]], b1, m1)
            g2 = pltpu.async_copy(y_hbm.at[i0.at[w + 1]], b2, m2)
            g3 = pltpu.async_copy(y_hbm.at[i1.at[w + 1]], b3, m3)
            g0.wait()
            s0 = pltpu.async_copy(b0, o_hbm.at[ita.at[w]], ms)
            g1.wait()
            s1 = pltpu.async_copy(b1, o_hbm.at[itb.at[w]], ms)
            g2.wait()
            s2 = pltpu.async_copy(b2, o_hbm.at[ita.at[w + 1]], ms)
            g3.wait()
            s3 = pltpu.async_copy(b3, o_hbm.at[itb.at[w + 1]], ms)
            s0.wait()
            s1.wait()
            s2.wait()
            s3.wait()

    win = lambda a: a.reshape(-1, SC_WINDOW)
    iota = jnp.arange(2 * t, dtype=I32)
    return run(y3, win(d0), win(d1), win(iota[:t]), win(iota[t:]))


def _rope_lane_tables(seq_len):
    half = ROT_DIM // 2
    inv_freq = ROPE_THETA ** (-np.arange(0, ROT_DIM, 2, dtype=np.float64) / ROT_DIM)
    ang = np.arange(seq_len, dtype=np.float64)[:, None] * inv_freq[None, :]
    cos, sin = np.cos(ang), np.sin(ang)
    m = np.arange(LANES) % HEAD_DIM
    f = m % half
    c = np.where(m[None, :] < ROT_DIM, cos[:, f], 1.0)
    s1 = np.where(m[None, :] < half, -sin[:, f], 0.0)
    s2 = np.where((m[None, :] >= half) & (m[None, :] < ROT_DIM), sin[:, f], 0.0)
    return tuple(jnp.asarray(a, F32) for a in (c, s1, s2))


def _layer(x, norm1_g, w_in, q_norm_g, k_norm_g, attn_sink, conv_w, conv_b, w_attn_proj,
           w_conv_proj, w_out, norm2_g, w_rg, b_rg, w_re, b_re, w_gate, w_up, w_down):
    bsz, seq, d = x.shape
    t = bsz * seq
    tm = min(512, seq)
    tq = min(1024, seq)
    assert d == D_MODEL and seq % tq == 0 and seq % tm == 0 and tq % tm == 0 and tq % WINDOW == 0
    xf = x.reshape(t, d)

    hpg = N_HEADS // N_KV_HEADS
    wq_p = (w_in[:, :ATTN_DIM].reshape(d, N_KV_HEADS, hpg, HEAD_DIM).transpose(0, 2, 1, 3)
            .reshape(d, ATTN_DIM))
    wq_p = wq_p.astype(BF16)
    w_in_b = w_in.astype(BF16)
    wa_p = (w_attn_proj.reshape(N_KV_HEADS, hpg, HEAD_DIM, d).transpose(1, 0, 2, 3)
            .reshape(ATTN_DIM, d).astype(BF16))
    bd = jnp.asarray(np.kron(np.eye(N_HEADS), np.full((HEAD_DIM, HEAD_DIM), 1.0 / HEAD_DIM)), BF16)
    qg = jnp.tile(q_norm_g, LANES // HEAD_DIM).reshape(1, LANES)
    kg = jnp.tile(k_norm_g, LANES // HEAD_DIM).reshape(1, LANES)
    ctab, s1tab, s2tab = _rope_lane_tables(seq)
    nseq_t = seq // tm

    row = lambda w: pl.BlockSpec((tm, w), lambda i: (i, 0))
    full = lambda a: pl.BlockSpec(a.shape, lambda i: (0,) * a.ndim)
    tab = pl.BlockSpec((tm, LANES), lambda i: (i % nseq_t, 0))
    cbias = conv_b.reshape(1, CONV_DIM)
    q, k, v, cb, cp, ue, ga, gc = pl.pallas_call(
        _proj_kernel,
        grid=(t // tm,),
        in_specs=[row(d), full(norm1_g.reshape(1, d)), full(w_in_b), full(wq_p), full(bd), full(qg), full(kg),
                  tab, tab, tab, full(conv_w), full(cbias)],
        out_specs=[row(ATTN_DIM), row(KV_DIM), row(KV_DIM), row(CONV_DIM), row(CONV_DIM),
                   pl.BlockSpec((16, CONV_DIM), lambda i: (i, 0)), row(D_MODEL), row(D_MODEL)],
        out_shape=[jax.ShapeDtypeStruct((t, w), BF16) for w in (ATTN_DIM, KV_DIM, KV_DIM, CONV_DIM, CONV_DIM)]
                  + [jax.ShapeDtypeStruct((t // tm * 16, CONV_DIM), BF16),
                     jax.ShapeDtypeStruct((t, D_MODEL), BF16), jax.ShapeDtypeStruct((t, D_MODEL), BF16)],
        scratch_shapes=[pltpu.VMEM((tm + 16, CONV_DIM), F32)],
        compiler_params=pltpu.CompilerParams(dimension_semantics=("arbitrary",),
                                             vmem_limit_bytes=VMEM_LIMIT),
        name="proj",
    )(xf, norm1_g.reshape(1, d), w_in_b, wq_p, bd, qg, kg, ctab, s1tab, s2tab, conv_w, cbias)

    wr = jnp.concatenate([w_rg, jnp.zeros((d, 8 - N_GROUPS), F32), w_re,
                          jnp.zeros((d, LANES - ROUTER_ROWS), F32)], axis=1)
    wr_hi = wr.astype(BF16)
    wr2 = jnp.concatenate([wr_hi, (wr - wr_hi.astype(F32)).astype(BF16)], axis=1)
    br = jnp.concatenate([b_rg, jnp.zeros((8 - N_GROUPS,), F32), b_re,
                          jnp.zeros((LANES - ROUTER_ROWS,), F32)]).reshape(1, LANES)
    mix_params = (conv_w, wa_p, w_conv_proj.astype(BF16), w_out.astype(BF16),
                  norm2_g.reshape(1, d), wr2, br)

    n_chunks = TOKEN_CHUNKS if bsz % TOKEN_CHUNKS == 0 else 1
    out = None
    for c in range(n_chunks):
        out = _chunk_tail(c, n_chunks, out, xf, (q, k, v, cb, cp, ue, ga, gc), attn_sink, mix_params,
                          (w_gate, w_up, w_down), bsz, seq, tq, tm)
    return out.reshape(bsz, seq, d)


def _chunk_tail(c, n_chunks, out_prev, xf, acts, attn_sink, mix_params, expert_w, bsz, seq, tq, tm):
    q, k, v, cb, cp, ue, ga, gc = acts
    d = D_MODEL
    t_all = bsz * seq
    bc = bsz // n_chunks
    b0 = c * bc
    t = bc * seq
    nj = seq // tq
    wpb = tq // WINDOW
    nb_seq = seq // WINDOW
    tpt = tq // tm
    n_edge = t_all // tm
    row_in = lambda w: pl.BlockSpec((tq, w), lambda b, j: ((b0 + b) * nj + j, 0))
    row_out = lambda w: pl.BlockSpec((tq, w), lambda b, j: (b * nj + j, 0))
    prev_blk = pl.BlockSpec((WINDOW, KV_DIM),
                            lambda b, j: ((b0 + b) * nb_seq + jnp.maximum(j * wpb - 1, 0), 0))
    next_blk = pl.BlockSpec((WINDOW, KV_DIM),
                            lambda b, j: ((b0 + b) * nb_seq + jnp.minimum((j + 1) * wpb, nb_seq - 1), 0))
    edge_prev = pl.BlockSpec((16, CONV_DIM),
                             lambda b, j: (jnp.maximum(((b0 + b) * nj + j) * tpt - 1, 0), 0))
    edge_cur = pl.BlockSpec((16 * tpt, CONV_DIM), lambda b, j: ((b0 + b) * nj + j, 0))
    edge_next = pl.BlockSpec((16, CONV_DIM),
                             lambda b, j: (jnp.minimum(((b0 + b) * nj + j + 1) * tpt, n_edge - 1), 0))
    full2 = lambda a: pl.BlockSpec(a.shape, lambda b, j: (0,) * a.ndim)

    x1, h2, logits = pl.pallas_call(
        functools.partial(_mix_kernel, tq=tq, tm=tm),
        grid=(bc, nj),
        in_specs=[pl.BlockSpec(memory_space=pltpu.SMEM),
                  row_in(d), row_in(ATTN_DIM), prev_blk, row_in(KV_DIM), next_blk,
                  prev_blk, row_in(KV_DIM), next_blk,
                  row_in(CONV_DIM), row_in(CONV_DIM), edge_prev, edge_cur, edge_next,
                  row_in(D_MODEL), row_in(D_MODEL)]
                 + [full2(p) for p in mix_params],
        out_specs=[row_out(d), pl.BlockSpec((tq * TOK_SUBLANES, LANES), lambda b, j: (b * nj + j, 0)),
                   row_out(LANES)],
        out_shape=[jax.ShapeDtypeStruct((t, d), F32),
                   jax.ShapeDtypeStruct((t * TOK_SUBLANES, LANES), U32),
                   jax.ShapeDtypeStruct((t, LANES), F32)],
        scratch_shapes=[pltpu.VMEM((tq, CONV_DIM), F32), pltpu.VMEM((tq, ATTN_DIM), BF16)],
        compiler_params=pltpu.CompilerParams(dimension_semantics=("arbitrary", "arbitrary"),
                                             vmem_limit_bytes=VMEM_LIMIT),
        name="mix",
    )(attn_sink, xf, q, k, k, k, v, v, v, cb, cp, ue, ue, ue, ga, gc, *mix_params)

    tr = min(ROUTE_TILE, t)
    assert t % tr == 0 and tr % RANK_CHUNK == 0
    tri = jnp.asarray(np.triu(np.ones((RANK_CHUNK, RANK_CHUNK), np.float32), 1), BF16)
    colr = lambda r: pl.BlockSpec((r, tr), lambda i: (0, i))
    route_i, route_f, counts = pl.pallas_call(
        functools.partial(_route_kernel, tr=tr),
        grid=(t // tr,),
        in_specs=[pl.BlockSpec((tr, LANES), lambda i: (i, 0)), pl.BlockSpec(tri.shape, lambda i: (0, 0))],
        out_specs=[colr(8), pl.BlockSpec((tr, LANES), lambda i: (i, 0)),
                   pl.BlockSpec((N_EXPERTS, LANES), lambda i: (0, 0))],
        out_shape=[jax.ShapeDtypeStruct((8, t), I32), jax.ShapeDtypeStruct((t, LANES), F32),
                   jax.ShapeDtypeStruct((N_EXPERTS, LANES), F32)],
        scratch_shapes=[pltpu.VMEM((N_EXPERTS, LANES), F32)],
        compiler_params=pltpu.CompilerParams(dimension_semantics=("arbitrary",),
                                             vmem_limit_bytes=VMEM_LIMIT),
        name="route",
    )(logits, tri)

    n_rows = 2 * t + N_EXPERTS * MOE_ROWS
    nblk = n_rows // MOE_ROWS
    cnt = counts[:, 0].astype(I32)
    padded = ((cnt + MOE_ROWS - 1) // MOE_ROWS) * MOE_ROWS
    pad_ends = jnp.cumsum(padded)
    pad_starts = pad_ends - padded
    eids = jnp.arange(N_EXPERTS, dtype=I32)[:, None, None]
    dest = jnp.sum(jnp.where(route_i[None, 0:2] == eids, pad_starts[:, None, None], 0), axis=0) \
        + route_i[2:4]
    blk_start = jnp.arange(nblk, dtype=I32) * MOE_ROWS
    block_e = jnp.minimum(jnp.sum((pad_ends[None, :] <= blk_start[:, None]).astype(I32), axis=1),
                          N_EXPERTS - 1)
    nused = (pad_ends[-1:] // MOE_ROWS).astype(I32)

    x_pad = _sc_dispatch(h2.reshape(t, TOK_SUBLANES, LANES), dest[0], dest[1], n_rows)
    x_pad = x_pad.reshape(n_rows * TOK_SUBLANES, LANES)

    blk = jnp.arange(nblk, dtype=I32)
    first = (blk < nused[0]) & (block_e != jnp.concatenate([jnp.full((1,), -1, I32), block_e[:-1]]))
    wslot = ((jnp.cumsum(first.astype(I32)) - 1) % 2).astype(I32)
    later_first = first[None, :] & (blk[None, :] > blk[:, None])
    nxt = jnp.where(jnp.any(later_first, axis=1), block_e[jnp.argmax(later_first, axis=1)], -1).astype(I32)

    w_gate, w_up, w_down = expert_w
    blk_rows = MOE_ROWS * TOK_SUBLANES
    hbm = pl.BlockSpec(memory_space=pl.ANY)
    y_pad = pl.pallas_call(
        _moe_kernel,
        grid_spec=pltpu.PrefetchScalarGridSpec(
            num_scalar_prefetch=5,
            grid=(nblk,),
            in_specs=[hbm, hbm, hbm, hbm],
            out_specs=pl.BlockSpec((blk_rows, LANES),
                                   lambda i, be, fi, ws, nx, nu: (jnp.where(i < nu[0], i, nblk - 1), 0)),
            scratch_shapes=[pltpu.VMEM((d, D_FF), BF16), pltpu.VMEM((d, D_FF), BF16),
                            pltpu.VMEM((D_FF, d), BF16),
                            pltpu.VMEM((MOE_X_SLOTS * blk_rows, LANES), U32),
                            pltpu.SemaphoreType.DMA((MOE_X_SLOTS,)),
                            pltpu.VMEM((2 * d, D_FF), F32), pltpu.VMEM((2 * d, D_FF), F32),
                            pltpu.VMEM((2 * D_FF, d), F32), pltpu.SemaphoreType.DMA((2,))]),
        out_shape=jax.ShapeDtypeStruct((n_rows * TOK_SUBLANES, LANES), U32),
        compiler_params=pltpu.CompilerParams(dimension_semantics=("arbitrary",),
                                             vmem_limit_bytes=VMEM_LIMIT),
        name="moe",
    )(block_e, first.astype(I32), wslot, nxt, nused, x_pad, w_gate, w_up, w_down)

    y3 = y_pad.reshape(n_rows, TOK_SUBLANES, LANES)
    parts = COMBINE_PARTS if t % (COMBINE_PARTS * 2048) == 0 else 1
    tp = t // parts
    tc = min(512, tp)
    nt = tp // tc
    out = out_prev
    for h in range(parts):
        tok = slice(h * tp, (h + 1) * tp)
        yk = _sc_gather(y3, dest[0, tok], dest[1, tok]).reshape(2 * tp * TOK_SUBLANES, LANES)
        row0 = h * nt
        in_specs = [pl.BlockSpec((tc, d), lambda i, row0=row0: (row0 + i, 0)),
                    pl.BlockSpec((tc, LANES), lambda i, row0=row0: (row0 + i, 0)),
                    pl.BlockSpec((tc * TOK_SUBLANES, LANES), lambda i: (i, 0)),
                    pl.BlockSpec((tc * TOK_SUBLANES, LANES), lambda i: (nt + i, 0))]
        args = [x1, route_f, yk, yk]
        aliases = {}
        if out is not None:
            in_specs.append(pl.BlockSpec(memory_space=pl.ANY))
            args.append(out)
            aliases = {len(args) - 1: 0}
        out = pl.pallas_call(
            functools.partial(_combine_kernel, tm=tc),
            grid=(nt,),
            in_specs=in_specs,
            out_specs=pl.BlockSpec((tc, d), lambda i, o=(c * parts + h) * nt: (o + i, 0)),
            out_shape=jax.ShapeDtypeStruct((t_all, d), F32),
            input_output_aliases=aliases,
            compiler_params=pltpu.CompilerParams(dimension_semantics=("arbitrary",),
                                                 vmem_limit_bytes=VMEM_LIMIT),
            name="combine",
        )(*args)
    return out


def kernel(x, norm1_g, w_in, q_norm_g, k_norm_g, attn_sink, conv_w, conv_b, w_attn_proj, w_conv_proj, w_out, norm2_g, w_router_group, b_router_group, w_router_expert, b_router_expert, w_gate_e, w_up_e, w_down_e):
    for l in range(norm1_g.shape[0]):
        x = _layer(x, norm1_g[l], w_in[l], q_norm_g[l], k_norm_g[l], attn_sink[l], conv_w[l],
                   conv_b[l], w_attn_proj[l], w_conv_proj[l], w_out[l], norm2_g[l],
                   w_router_group[l], b_router_group[l], w_router_expert[l], b_router_expert[l],
                   w_gate_e[l], w_up_e[l], w_down_e[l])
    return x
```

```python
import functools

import jax
import jax.numpy as jnp
import numpy as np
from jax import lax
from jax.experimental import pallas as pl
from jax.experimental.pallas import tpu as pltpu
from jax.experimental.pallas import tpu_sc as plsc

F32 = jnp.float32
BF16 = jnp.bfloat16
I32 = jnp.int32
U32 = jnp.uint32
HI_MASK = np.uint32(0xFFFF0000)

D_MODEL = 1024
N_HEADS = 8
N_KV_HEADS = 2
HEAD_DIM = 64
ATTN_DIM = N_HEADS * HEAD_DIM
KV_DIM = N_KV_HEADS * HEAD_DIM
WINDOW = 128
ROT_DIM = HEAD_DIM // 4
ROPE_THETA = 500000.0
CONV_DIM = D_MODEL // 2
N_GROUPS = 4
EXPERTS_PER_GROUP = 8
N_EXPERTS = N_GROUPS * EXPERTS_PER_GROUP
D_FF = D_MODEL // 4
NORM_EPS = 1e-6
MASK_VALUE = -1e30
NEG_BIG = -3.0e38
LOG2E = 1.4426950408889634

LANES = 128
TOK_SUBLANES = D_MODEL // (2 * LANES)
HALF_D = D_MODEL // 2
SC_WINDOW = 32
ROUTER_ROWS = 8 + N_EXPERTS
ROUTE_TILE = 4096
RANK_CHUNK = 256
MOE_ROWS = 512
MOE_X_SLOTS = 3
VMEM_LIMIT = 56 * 1024 * 1024

_Q0, _K0, _V0, _CB0, _CC0, _CX0, _GA0, _GC0, _END = (
    0, 512, 640, 768, 1280, 1792, 2304, 3328, 4352)


def _store_packed_rows(ref, val, n):
    for c in range(TOK_SUBLANES):
        lo = val[:, c * LANES:(c + 1) * LANES].astype(BF16).astype(F32)
        hi = val[:, HALF_D + c * LANES:HALF_D + (c + 1) * LANES].astype(BF16).astype(F32)
        word = (lax.bitcast_convert_type(lo, U32) >> 16) | (lax.bitcast_convert_type(hi, U32) & HI_MASK)
        ref[pl.ds(c, n, stride=TOK_SUBLANES), :] = word


def _load_packed_rows(ref, n, row0=0):
    words = [ref[pl.ds(row0 + c, n, stride=TOK_SUBLANES), :] for c in range(TOK_SUBLANES)]
    lo = [lax.bitcast_convert_type(w << 16, F32) for w in words]
    hi = [lax.bitcast_convert_type(w & HI_MASK, F32) for w in words]
    return lo + hi


def _rope_chunks(t, c, s1, s2):
    outs = []
    for j in range(t.shape[1] // LANES):
        tj = t[:, j * LANES:(j + 1) * LANES]
        outs.append(tj * c + pltpu.roll(tj, LANES - ROT_DIM // 2, 1) * s1
                    + pltpu.roll(tj, ROT_DIM // 2, 1) * s2)
    return outs[0] if len(outs) == 1 else jnp.concatenate(outs, axis=1)


def _proj_kernel(x_ref, g1_ref, w_ref, wq_ref, bd_ref, qg_ref, kg_ref, c_ref, s1_ref, s2_ref, cw_ref, cbias_ref,
                 q_ref, k_ref, v_ref, cb_ref, cp_ref, ue_ref, ga_ref, gc_ref, ubuf):
    x = x_ref[...]
    ms = jnp.mean(x * x, axis=-1, keepdims=True)
    h = (x * lax.rsqrt(ms + NORM_EPS) * g1_ref[...]).astype(BF16)

    def proj(a, b):
        return jnp.dot(h, w_ref[:, a:b], preferred_element_type=F32)

    c, s1, s2 = c_ref[...], s1_ref[...], s2_ref[...]

    ga_ref[...] = jax.nn.sigmoid(proj(_GA0, _GC0)).astype(BF16)
    gc_ref[...] = jax.nn.sigmoid(proj(_GC0, _END)).astype(BF16)
    cb_ref[...] = proj(_CB0, _CC0).astype(BF16)
    tm = x.shape[0]
    u = proj(_CC0, _CX0) * proj(_CX0, _GA0)
    ubuf[0:8, :] = jnp.zeros((8, CONV_DIM), F32)
    ubuf[8 + tm:16 + tm, :] = jnp.zeros((8, CONV_DIM), F32)
    ubuf[8:8 + tm, :] = u
    cw = cw_ref[...]
    cp_ref[...] = (ubuf[7:7 + tm, :] * cw[0:1, :] + u * cw[1:2, :] + ubuf[9:9 + tm, :] * cw[2:3, :]
                   + cbias_ref[...]).astype(BF16)
    ue_ref[...] = jnp.concatenate([u[0:8, :], u[tm - 8:tm, :]], axis=0).astype(BF16)
    v_ref[...] = proj(_V0, _CB0).astype(BF16)

    q = jnp.dot(h, wq_ref[...], preferred_element_type=F32)
    qms = jnp.dot((q * q).astype(BF16), bd_ref[...], preferred_element_type=F32)
    qg = jnp.concatenate([qg_ref[...]] * (ATTN_DIM // LANES), axis=1)
    qn = q * lax.rsqrt(qms + NORM_EPS) * qg
    q_ref[...] = (_rope_chunks(qn, c, s1, s2) * (HEAD_DIM ** -0.5 * LOG2E)).astype(BF16)

    k = proj(_K0, _V0)
    kms = jnp.dot((k * k).astype(BF16), bd_ref[0:KV_DIM, 0:KV_DIM], preferred_element_type=F32)
    kn = k * lax.rsqrt(kms + NORM_EPS) * kg_ref[...]
    k_ref[...] = _rope_chunks(kn, c, s1, s2).astype(BF16)


def _mix_kernel(sink_ref, x_ref, q_ref, kp_ref, kc_ref, kn_ref, vp_ref, vc_ref, vn_ref,
                cb_ref, cp_ref, uep_ref, uec_ref, uen_ref, ga_ref, gc_ref,
                cw_ref, wa_ref, wc_ref, wo_ref, g2_ref,
                wr_ref, br_ref,
                x1_ref, h2_ref, lg_ref,
                cscr, attn_scr, *, tq, tm):
    j = pl.program_id(1)
    nj = pl.num_programs(1)
    nsub = tq // WINDOW

    kbuf = jnp.concatenate([kp_ref[...], kc_ref[...], kn_ref[...]], axis=0)
    vbuf = jnp.concatenate([vp_ref[...], vc_ref[...], vn_ref[...]], axis=0)
    low_half = lax.broadcasted_iota(I32, (WINDOW, LANES), 1) < HEAD_DIM
    low_half3 = lax.broadcasted_iota(I32, (3 * WINDOW, LANES), 1) < HEAD_DIM
    qrow = lax.broadcasted_iota(I32, (WINDOW, WINDOW), 0)
    kcol = lax.broadcasted_iota(I32, (WINDOW, WINDOW), 1)
    bias_lo = jnp.where(kcol >= qrow, 0.0, MASK_VALUE)
    bias_hi = jnp.where(kcol <= qrow, 0.0, MASK_VALUE)
    one = jnp.ones((3 * WINDOW, LANES), BF16)
    def scores(n):
        qs = q_ref[n * WINDOW:(n + 1) * WINDOW, :]
        zero = jnp.zeros((WINDOW, LANES), BF16)
        q_stack = []
        for hd in range(N_HEADS):
            col = qs[:, (hd % 4) * LANES:(hd % 4 + 1) * LANES]
            keep = low_half if hd < 4 else jnp.logical_not(low_half)
            q_stack.append(jnp.where(keep, col, zero))
        q_stack = jnp.concatenate(q_stack, axis=0)
        kwin = kbuf[n * WINDOW:(n + 3) * WINDOW, :]
        return lax.dot_general(q_stack, kwin, (((1,), (1,)), ((), ())),
                               preferred_element_type=F32)

    s_next = scores(0)
    for n in range(nsub):
        blo = jnp.where(j > 0, bias_lo, MASK_VALUE) if n == 0 else bias_lo
        bhi = jnp.where(j < nj - 1, bias_hi, MASK_VALUE) if n == nsub - 1 else bias_hi
        s = s_next
        if n + 1 < nsub:
            s_next = scores(n + 1)
        vwin = vbuf[n * WINDOW:(n + 3) * WINDOW, :]
        p_list, es_list = [], []
        for hd in range(N_HEADS):
            sink = sink_ref[hd] * LOG2E
            rows = slice(hd * WINDOW, (hd + 1) * WINDOW)
            s0 = s[rows, 0:WINDOW] + blo
            s1 = s[rows, WINDOW:2 * WINDOW]
            s2 = s[rows, 2 * WINDOW:3 * WINDOW] + bhi
            m = jnp.max(jnp.maximum(jnp.maximum(s0, s1), s2), axis=-1, keepdims=True)
            m = jnp.maximum(m, sink)
            p_list.append(jnp.concatenate(
                [jnp.exp2(s0 - m).astype(BF16), jnp.exp2(s1 - m).astype(BF16),
                 jnp.exp2(s2 - m).astype(BF16)], axis=1))
            es_list.append(jnp.exp2(sink - m))
        half = N_HEADS // 2
        o_lo = jnp.dot(jnp.concatenate(p_list[:half], axis=0), jnp.where(low_half3, vwin, one),
                       preferred_element_type=F32)
        o_hi = jnp.dot(jnp.concatenate(p_list[half:], axis=0), jnp.where(low_half3, one, vwin),
                       preferred_element_type=F32)
        cols = []
        for c4 in range(half):
            rows = slice(c4 * WINDOW, (c4 + 1) * WINDOW)
            oa, ob = o_lo[rows, :], o_hi[rows, :]
            da = pltpu.roll(oa, HEAD_DIM, 1) + es_list[c4]
            db = pltpu.roll(ob, HEAD_DIM, 1) + es_list[c4 + half]
            cols.append(jnp.where(low_half, oa / da, ob / db))
        attn_scr[n * WINDOW:(n + 1) * WINDOW, :] = jnp.concatenate(cols, axis=1).astype(BF16)
    attn = attn_scr[...]

    cscr[...] = cp_ref[...].astype(F32)
    cw = cw_ref[...]
    uec = uec_ref[...].astype(F32)
    n_tiles = tq // tm
    for p in range(n_tiles):
        if p == 0:
            prev_last = uep_ref[...].astype(F32)[15:16, :] * (j > 0).astype(F32)
        else:
            prev_last = uec[16 * p - 1:16 * p, :]
        cscr[p * tm:p * tm + 1, :] = cscr[p * tm:p * tm + 1, :] + prev_last * cw[0:1, :]
        if p == n_tiles - 1:
            next_first = uen_ref[...].astype(F32)[0:1, :] * (j < nj - 1).astype(F32)
        else:
            next_first = uec[16 * (p + 1):16 * (p + 1) + 1, :]
        last = (p + 1) * tm - 1
        cscr[last:last + 1, :] = cscr[last:last + 1, :] + next_first * cw[2:3, :]
    cgate = (cb_ref[...].astype(F32) * cscr[...]).astype(BF16)

    merged = (ga_ref[...].astype(F32) * jnp.dot(attn, wa_ref[...], preferred_element_type=F32)
              + gc_ref[...].astype(F32) * jnp.dot(cgate, wc_ref[...], preferred_element_type=F32))
    x1 = x_ref[...] + jnp.dot(merged.astype(BF16), wo_ref[...], preferred_element_type=F32)
    x1_ref[...] = x1

    ms = jnp.mean(x1 * x1, axis=-1, keepdims=True)
    h2 = x1 * lax.rsqrt(ms + NORM_EPS) * g2_ref[...]
    _store_packed_rows(h2_ref, h2, tq)
    h_hi = h2.astype(BF16)
    h_lo = (h2 - h_hi.astype(F32)).astype(BF16)
    acc = jnp.dot(h_hi, wr_ref[...], preferred_element_type=F32)
    logits = (acc[:, 0:LANES] + acc[:, LANES:2 * LANES]
              + jnp.dot(h_lo, wr_ref[:, 0:LANES], preferred_element_type=F32) + br_ref[...])
    lg_ref[...] = logits


def _route_kernel(lg_ref, tri_ref, ri_ref, rf_ref, cnt_ref, base_ref, *, tr):
    @pl.when(pl.program_id(0) == 0)
    def _():
        base_ref[...] = jnp.zeros_like(base_ref)

    logits = lg_ref[...].T[0:ROUTER_ROWS, :]
    grow = lax.broadcasted_iota(I32, (8, tr), 0)
    gl = jnp.where(grow < N_GROUPS, logits[0:8, :], NEG_BIG)
    gmax = jnp.max(gl, axis=0, keepdims=True)
    grp = jnp.min(jnp.where(gl == gmax, grow, 8), axis=0, keepdims=True)
    p_grp = 1.0 / jnp.sum(jnp.exp(gl - gmax), axis=0, keepdims=True)
    el = logits[8:ROUTER_ROWS, :]
    erow = lax.broadcasted_iota(I32, (N_EXPERTS, tr), 0)
    in_grp = (erow // EXPERTS_PER_GROUP) == grp
    l_a = jnp.where(in_grp, el, NEG_BIG)
    l1 = jnp.max(l_a, axis=0, keepdims=True)
    i1 = jnp.min(jnp.where(l_a == l1, erow, N_EXPERTS), axis=0, keepdims=True)
    l_b = jnp.where(erow == i1, NEG_BIG, l_a)
    l2 = jnp.max(l_b, axis=0, keepdims=True)
    i2 = jnp.min(jnp.where(l_b == l2, erow, N_EXPERTS), axis=0, keepdims=True)
    r21 = jnp.exp(l2 - l1)
    t1 = 1.0 / (1.0 + r21)
    gate1 = p_grp * t1
    gate2 = p_grp * (r21 * t1)
    oh1 = (erow == i1).astype(F32)
    oh2 = (erow == i2).astype(F32)
    oh = oh1 + oh2
    base = base_ref[...][:, 0:1]
    r1, r2 = [], []
    for c in range(tr // RANK_CHUNK):
        cols = slice(c * RANK_CHUNK, (c + 1) * RANK_CHUNK)
        ohc = oh[:, cols]
        before = jnp.dot(ohc.astype(BF16), tri_ref[...], preferred_element_type=F32) + base
        r1.append(jnp.sum(oh1[:, cols] * before, axis=0, keepdims=True))
        r2.append(jnp.sum(oh2[:, cols] * before, axis=0, keepdims=True))
        base = base + jnp.sum(ohc, axis=1, keepdims=True)
    rank1 = jnp.concatenate(r1, axis=1)
    rank2 = jnp.concatenate(r2, axis=1)
    new_base = jnp.broadcast_to(base, base_ref.shape)
    base_ref[...] = new_base
    cnt_ref[...] = new_base
    zi = jnp.zeros((4, tr), I32)
    ri_ref[...] = jnp.concatenate([i1, i2, rank1.astype(I32), rank2.astype(I32), zi], axis=0)
    rf_ref[...] = jnp.concatenate([gate1, gate2, jnp.zeros((LANES - 2, tr), F32)], axis=0).T


def _moe_kernel(be_ref, first_ref, wslot_ref, nxt_ref, nused_ref, x_hbm, wg_hbm, wu_hbm, wd_hbm, y_ref,
                wg_s, wu_s, wd_s, xbuf, xsem, wst_g, wst_u, wst_d, wsem):
    i = pl.program_id(0)
    nused = nused_ref[0]
    blk_rows = MOE_ROWS * TOK_SUBLANES

    def x_copy(blk):
        slot = lax.rem(blk, MOE_X_SLOTS)
        return pltpu.make_async_copy(
            x_hbm.at[pl.ds(pl.multiple_of(blk * blk_rows, blk_rows), blk_rows)],
            xbuf.at[pl.ds(pl.multiple_of(slot * blk_rows, blk_rows), blk_rows)],
            xsem.at[slot])

    @pl.when(i == 0)
    def _():
        x_copy(0).start()

        @pl.when(nused > 1)
        def _():
            x_copy(1).start()

    @pl.when(i + 2 < nused)
    def _():
        x_copy(i + 2).start()

    def w_copies(e, slot):
        return (pltpu.make_async_copy(wg_hbm.at[e], wst_g.at[pl.ds(slot * D_MODEL, D_MODEL)], wsem.at[slot]),
                pltpu.make_async_copy(wu_hbm.at[e], wst_u.at[pl.ds(slot * D_MODEL, D_MODEL)], wsem.at[slot]),
                pltpu.make_async_copy(wd_hbm.at[e], wst_d.at[pl.ds(slot * D_FF, D_FF)], wsem.at[slot]))

    @pl.when(i == 0)
    def _():
        for cp in w_copies(be_ref[0], 0):
            cp.start()

    def new_run(_, carry):
        slot = wslot_ref[i]
        for cp in w_copies(be_ref[i], slot):
            cp.wait()
        wg_s[...] = wst_g[pl.ds(pl.multiple_of(slot * D_MODEL, D_MODEL), D_MODEL), :].astype(BF16)
        wu_s[...] = wst_u[pl.ds(pl.multiple_of(slot * D_MODEL, D_MODEL), D_MODEL), :].astype(BF16)
        wd_s[...] = wst_d[pl.ds(pl.multiple_of(slot * D_FF, D_FF), D_FF), :].astype(BF16)

        @pl.when(nxt_ref[i] >= 0)
        def _():
            for cp in w_copies(nxt_ref[i], 1 - slot):
                cp.start()
        return carry

    lax.fori_loop(0, first_ref[i], new_run, 0)

    @pl.when(i < nused)
    def _():
        x_copy(i).wait()
        row0 = pl.multiple_of(lax.rem(i, MOE_X_SLOTS) * blk_rows, blk_rows)
        xb = jnp.concatenate([c.astype(BF16) for c in _load_packed_rows(xbuf, MOE_ROWS, row0)], axis=1)
        g = jnp.dot(xb, wg_s[...], preferred_element_type=F32)
        u = jnp.dot(xb, wu_s[...], preferred_element_type=F32)
        hid = (g * jax.nn.sigmoid(g) * u).astype(BF16)
        _store_packed_rows(y_ref, jnp.dot(hid, wd_s[...], preferred_element_type=F32), MOE_ROWS)

    @pl.when(i >= nused)
    def _():
        y_ref[...] = jnp.zeros_like(y_ref)


def _combine_kernel(x1_ref, gate_ref, y0_ref, y1_ref, o_ref, *, tm):
    g = gate_ref[...]
    g0, g1 = g[:, 0:1], g[:, 1:2]
    y0 = _load_packed_rows(y0_ref, tm)
    y1 = _load_packed_rows(y1_ref, tm)
    for c in range(2 * TOK_SUBLANES):
        cols = slice(c * LANES, (c + 1) * LANES)
        o_ref[:, cols] = x1_ref[:, cols] + g0 * y0[c] + g1 * y1[c]


def _sc_windows(n_tokens):
    info = plsc.get_sparse_core_info()
    n_workers = info.num_cores * info.num_subcores
    assert n_tokens % (n_workers * SC_WINDOW * 2) == 0
    return info.num_cores, n_tokens // (n_workers * SC_WINDOW)


def _sc_dispatch(h3, d0, d1, n_rows):
    t = h3.shape[0]
    n_cores, nwin = _sc_windows(t)
    mesh = plsc.VectorSubcoreMesh(core_axis_name="c", subcore_axis_name="s")
    rows = pltpu.VMEM((SC_WINDOW,) + h3.shape[1:], h3.dtype)
    idx = pltpu.VMEM((nwin, SC_WINDOW), I32)

    @functools.partial(
        pl.kernel, mesh=mesh,
        out_type=jax.ShapeDtypeStruct((n_rows,) + h3.shape[1:], h3.dtype),
        scratch_types=[idx, idx, idx, rows, rows] + [pltpu.SemaphoreType.DMA] * 3,
        name="sc_dispatch")
    def run(h_hbm, d0_hbm, d1_hbm, it_hbm, o_hbm, i0, i1, it, buf_a, buf_b, sem_a, sem_b, sem_s):
        wbase = (lax.axis_index("s") * n_cores + lax.axis_index("c")) * nwin
        pltpu.sync_copy(d0_hbm.at[pl.ds(wbase, nwin)], i0)
        pltpu.sync_copy(d1_hbm.at[pl.ds(wbase, nwin)], i1)
        pltpu.sync_copy(it_hbm.at[pl.ds(wbase, nwin)], it)

        @pl.loop(0, nwin, step=2)
        def _(w):
            ga = pltpu.async_copy(h_hbm.at[it.at[w]], buf_a, sem_a)
            gb = pltpu.async_copy(h_hbm.at[it.at[w + 1]], buf_b, sem_b)
            ga.wait()
            s0 = pltpu.async_copy(buf_a, o_hbm.at[i0.at[w]], sem_s)
            s1 = pltpu.async_copy(buf_a, o_hbm.at[i1.at[w]], sem_s)
            gb.wait()
            s2 = pltpu.async_copy(buf_b, o_hbm.at[i0.at[w + 1]], sem_s)
            s3 = pltpu.async_copy(buf_b, o_hbm.at[i1.at[w + 1]], sem_s)
            s0.wait()
            s1.wait()
            s2.wait()
            s3.wait()

    win = lambda a: a.reshape(-1, SC_WINDOW)
    return run(h3, win(d0), win(d1), win(jnp.arange(t, dtype=I32)))


def _sc_gather(y3, d0, d1):
    t = d0.shape[0]
    n_cores, nwin = _sc_windows(t)
    mesh = plsc.VectorSubcoreMesh(core_axis_name="c", subcore_axis_name="s")
    rows = pltpu.VMEM((SC_WINDOW,) + y3.shape[1:], y3.dtype)
    idx = pltpu.VMEM((nwin, SC_WINDOW), I32)

    @functools.partial(
        pl.kernel, mesh=mesh,
        out_type=jax.ShapeDtypeStruct((2 * t,) + y3.shape[1:], y3.dtype),
        scratch_types=[idx] * 4 + [rows] * 4 + [pltpu.SemaphoreType.DMA] * 5,
        name="sc_gather")
    def run(y_hbm, d0_hbm, d1_hbm, ita_hbm, itb_hbm, o_hbm, i0, i1, ita, itb,
            b0, b1, b2, b3, m0, m1, m2, m3, ms):
        wbase = (lax.axis_index("s") * n_cores + lax.axis_index("c")) * nwin
        pltpu.sync_copy(d0_hbm.at[pl.ds(wbase, nwin)], i0)
        pltpu.sync_copy(d1_hbm.at[pl.ds(wbase, nwin)], i1)
        pltpu.sync_copy(ita_hbm.at[pl.ds(wbase, nwin)], ita)
        pltpu.sync_copy(itb_hbm.at[pl.ds(wbase, nwin)], itb)

        @pl.loop(0, nwin, step=2)
        def _(w):
            g0 = pltpu.async_copy(y_hbm.at[i0.at[w]], b0, m0)
            g1 = pltpu.async_copy(y_hbm.at[i1.at[w]], b1, m1)
            g2 = pltpu.async_copy(y_hbm.at[i0.at[w + 1]], b2, m2)
            g3 = pltpu.async_copy(y_hbm.at[i1.at[w + 1]], b3, m3)
            g0.wait()
            s0 = pltpu.async_copy(b0, o_hbm.at[ita.at[w]], ms)
            g1.wait()
            s1 = pltpu.async_copy(b1, o_hbm.at[itb.at[w]], ms)
            g2.wait()
            s2 = pltpu.async_copy(b2, o_hbm.at[ita.at[w + 1]], ms)
            g3.wait()
            s3 = pltpu.async_copy(b3, o_hbm.at[itb.at[w + 1]], ms)
            s0.wait()
            s1.wait()
            s2.wait()
            s3.wait()

    win = lambda a: a.reshape(-1, SC_WINDOW)
    iota = jnp.arange(2 * t, dtype=I32)
    return run(y3, win(d0), win(d1), win(iota[:t]), win(iota[t:]))


def _rope_lane_tables(seq_len):
    half = ROT_DIM // 2
    inv_freq = ROPE_THETA ** (-np.arange(0, ROT_DIM, 2, dtype=np.float64) / ROT_DIM)
    ang = np.arange(seq_len, dtype=np.float64)[:, None] * inv_freq[None, :]
    cos, sin = np.cos(ang), np.sin(ang)
    m = np.arange(LANES) % HEAD_DIM
    f = m % half
    c = np.where(m[None, :] < ROT_DIM, cos[:, f], 1.0)
    s1 = np.where(m[None, :] < half, -sin[:, f], 0.0)
    s2 = np.where((m[None, :] >= half) & (m[None, :] < ROT_DIM), sin[:, f], 0.0)
    return tuple(jnp.asarray(a, F32) for a in (c, s1, s2))


def _layer(x, norm1_g, w_in, q_norm_g, k_norm_g, attn_sink, conv_w, conv_b, w_attn_proj,
           w_conv_proj, w_out, norm2_g, w_rg, b_rg, w_re, b_re, w_gate, w_up, w_down):
    bsz, seq, d = x.shape
    t = bsz * seq
    tm = min(512, seq)
    tq = min(1024, seq)
    assert d == D_MODEL and seq % tq == 0 and seq % tm == 0 and tq % tm == 0 and tq % WINDOW == 0
    xf = x.reshape(t, d)

    hpg = N_HEADS // N_KV_HEADS
    wq_p = (w_in[:, :ATTN_DIM].reshape(d, N_KV_HEADS, hpg, HEAD_DIM).transpose(0, 2, 1, 3)
            .reshape(d, ATTN_DIM))
    wq_p = wq_p.astype(BF16)
    w_in_b = w_in.astype(BF16)
    wa_p = (w_attn_proj.reshape(N_KV_HEADS, hpg, HEAD_DIM, d).transpose(1, 0, 2, 3)
            .reshape(ATTN_DIM, d).astype(BF16))
    bd = jnp.asarray(np.kron(np.eye(N_HEADS), np.full((HEAD_DIM, HEAD_DIM), 1.0 / HEAD_DIM)), BF16)
    qg = jnp.tile(q_norm_g, LANES // HEAD_DIM).reshape(1, LANES)
    kg = jnp.tile(k_norm_g, LANES // HEAD_DIM).reshape(1, LANES)
    ctab, s1tab, s2tab = _rope_lane_tables(seq)
    nseq_t = seq // tm

    row = lambda w: pl.BlockSpec((tm, w), lambda i: (i, 0))
    full = lambda a: pl.BlockSpec(a.shape, lambda i: (0,) * a.ndim)
    tab = pl.BlockSpec((tm, LANES), lambda i: (i % nseq_t, 0))
    cbias = conv_b.reshape(1, CONV_DIM)
    q, k, v, cb, cp, ue, ga, gc = pl.pallas_call(
        _proj_kernel,
        grid=(t // tm,),
        in_specs=[row(d), full(norm1_g.reshape(1, d)), full(w_in_b), full(wq_p), full(bd), full(qg), full(kg),
                  tab, tab, tab, full(conv_w), full(cbias)],
        out_specs=[row(ATTN_DIM), row(KV_DIM), row(KV_DIM), row(CONV_DIM), row(CONV_DIM),
                   pl.BlockSpec((16, CONV_DIM), lambda i: (i, 0)), row(D_MODEL), row(D_MODEL)],
        out_shape=[jax.ShapeDtypeStruct((t, w), BF16) for w in (ATTN_DIM, KV_DIM, KV_DIM, CONV_DIM, CONV_DIM)]
                  + [jax.ShapeDtypeStruct((t // tm * 16, CONV_DIM), BF16),
                     jax.ShapeDtypeStruct((t, D_MODEL), BF16), jax.ShapeDtypeStruct((t, D_MODEL), BF16)],
        scratch_shapes=[pltpu.VMEM((tm + 16, CONV_DIM), F32)],
        compiler_params=pltpu.CompilerParams(dimension_semantics=("arbitrary",),
                                             vmem_limit_bytes=VMEM_LIMIT),
        name="proj",
    )(xf, norm1_g.reshape(1, d), w_in_b, wq_p, bd, qg, kg, ctab, s1tab, s2tab, conv_w, cbias)

    wr = jnp.concatenate([w_rg, jnp.zeros((d, 8 - N_GROUPS), F32), w_re,
                          jnp.zeros((d, LANES - ROUTER_ROWS), F32)], axis=1)
    wr_hi = wr.astype(BF16)
    wr2 = jnp.concatenate([wr_hi, (wr - wr_hi.astype(F32)).astype(BF16)], axis=1)
    br = jnp.concatenate([b_rg, jnp.zeros((8 - N_GROUPS,), F32), b_re,
                          jnp.zeros((LANES - ROUTER_ROWS,), F32)]).reshape(1, LANES)
    mix_params = (conv_w, wa_p, w_conv_proj.astype(BF16), w_out.astype(BF16),
                  norm2_g.reshape(1, d), wr2, br)

    out = _mixers_and_moe(xf, (q, k, v, cb, cp, ue, ga, gc), attn_sink, mix_params,
                          (w_gate, w_up, w_down), bsz, seq, tq, tm)
    return out.reshape(bsz, seq, d)


def _mixers_and_moe(xf, acts, attn_sink, mix_params, expert_w, bsz, seq, tq, tm):
    q, k, v, cb, cp, ue, ga, gc = acts
    d = D_MODEL
    t = bsz * seq
    nj = seq // tq
    wpb = tq // WINDOW
    nb_seq = seq // WINDOW
    tpt = tq // tm
    n_edge = t // tm
    row_in = lambda w: pl.BlockSpec((tq, w), lambda b, j: (b * nj + j, 0))
    row_out = row_in
    prev_blk = pl.BlockSpec((WINDOW, KV_DIM),
                            lambda b, j: (b * nb_seq + jnp.maximum(j * wpb - 1, 0), 0))
    next_blk = pl.BlockSpec((WINDOW, KV_DIM),
                            lambda b, j: (b * nb_seq + jnp.minimum((j + 1) * wpb, nb_seq - 1), 0))
    edge_prev = pl.BlockSpec((16, CONV_DIM),
                             lambda b, j: (jnp.maximum((b * nj + j) * tpt - 1, 0), 0))
    edge_cur = pl.BlockSpec((16 * tpt, CONV_DIM), lambda b, j: (b * nj + j, 0))
    edge_next = pl.BlockSpec((16, CONV_DIM),
                             lambda b, j: (jnp.minimum((b * nj + j + 1) * tpt, n_edge - 1), 0))
    full2 = lambda a: pl.BlockSpec(a.shape, lambda b, j: (0,) * a.ndim)

    x1, h2, logits = pl.pallas_call(
        functools.partial(_mix_kernel, tq=tq, tm=tm),
        grid=(bsz, nj),
        in_specs=[pl.BlockSpec(memory_space=pltpu.SMEM),
                  row_in(d), row_in(ATTN_DIM), prev_blk, row_in(KV_DIM), next_blk,
                  prev_blk, row_in(KV_DIM), next_blk,
                  row_in(CONV_DIM), row_in(CONV_DIM), edge_prev, edge_cur, edge_next,
                  row_in(D_MODEL), row_in(D_MODEL)]
                 + [full2(p) for p in mix_params],
        out_specs=[row_out(d), pl.BlockSpec((tq * TOK_SUBLANES, LANES), lambda b, j: (b * nj + j, 0)),
                   row_out(LANES)],
        out_shape=[jax.ShapeDtypeStruct((t, d), F32),
                   jax.ShapeDtypeStruct((t * TOK_SUBLANES, LANES), U32),
                   jax.ShapeDtypeStruct((t, LANES), F32)],
        scratch_shapes=[pltpu.VMEM((tq, CONV_DIM), F32), pltpu.VMEM((tq, ATTN_DIM), BF16)],
        compiler_params=pltpu.CompilerParams(dimension_semantics=("arbitrary", "arbitrary"),
                                             vmem_limit_bytes=VMEM_LIMIT),
        name="mix",
    )(attn_sink, xf, q, k, k, k, v, v, v, cb, cp, ue, ue, ue, ga, gc, *mix_params)

    tr = min(ROUTE_TILE, t)
    assert t % tr == 0 and tr % RANK_CHUNK == 0
    tri = jnp.asarray(np.triu(np.ones((RANK_CHUNK, RANK_CHUNK), np.float32), 1), BF16)
    colr = lambda r: pl.BlockSpec((r, tr), lambda i: (0, i))
    route_i, route_f, counts = pl.pallas_call(
        functools.partial(_route_kernel, tr=tr),
        grid=(t // tr,),
        in_specs=[pl.BlockSpec((tr, LANES), lambda i: (i, 0)), pl.BlockSpec(tri.shape, lambda i: (0, 0))],
        out_specs=[colr(8), pl.BlockSpec((tr, LANES), lambda i: (i, 0)),
                   pl.BlockSpec((N_EXPERTS, LANES), lambda i: (0, 0))],
        out_shape=[jax.ShapeDtypeStruct((8, t), I32), jax.ShapeDtypeStruct((t, LANES), F32),
                   jax.ShapeDtypeStruct((N_EXPERTS, LANES), F32)],
        scratch_shapes=[pltpu.VMEM((N_EXPERTS, LANES), F32)],
        compiler_params=pltpu.CompilerParams(dimension_semantics=("arbitrary",),
                                             vmem_limit_bytes=VMEM_LIMIT),
        name="route",
    )(logits, tri)

    n_rows = 2 * t + N_EXPERTS * MOE_ROWS
    nblk = n_rows // MOE_ROWS
    cnt = counts[:, 0].astype(I32)
    padded = ((cnt + MOE_ROWS - 1) // MOE_ROWS) * MOE_ROWS
    pad_ends = jnp.cumsum(padded)
    pad_starts = pad_ends - padded
    eids = jnp.arange(N_EXPERTS, dtype=I32)[:, None, None]
    dest = jnp.sum(jnp.where(route_i[None, 0:2] == eids, pad_starts[:, None, None], 0), axis=0) \
        + route_i[2:4]
    blk_start = jnp.arange(nblk, dtype=I32) * MOE_ROWS
    block_e = jnp.minimum(jnp.sum((pad_ends[None, :] <= blk_start[:, None]).astype(I32), axis=1),
                          N_EXPERTS - 1)
    nused = (pad_ends[-1:] // MOE_ROWS).astype(I32)

    x_pad = _sc_dispatch(h2.reshape(t, TOK_SUBLANES, LANES), dest[0], dest[1], n_rows)
    x_pad = x_pad.reshape(n_rows * TOK_SUBLANES, LANES)

    blk = jnp.arange(nblk, dtype=I32)
    first = (blk < nused[0]) & (block_e != jnp.concatenate([jnp.full((1,), -1, I32), block_e[:-1]]))
    wslot = ((jnp.cumsum(first.astype(I32)) - 1) % 2).astype(I32)
    later_first = first[None, :] & (blk[None, :] > blk[:, None])
    nxt = jnp.where(jnp.any(later_first, axis=1), block_e[jnp.argmax(later_first, axis=1)], -1).astype(I32)

    w_gate, w_up, w_down = expert_w
    blk_rows = MOE_ROWS * TOK_SUBLANES
    hbm = pl.BlockSpec(memory_space=pl.ANY)
    y_pad = pl.pallas_call(
        _moe_kernel,
        grid_spec=pltpu.PrefetchScalarGridSpec(
            num_scalar_prefetch=5,
            grid=(nblk,),
            in_specs=[hbm, hbm, hbm, hbm],
            out_specs=pl.BlockSpec((blk_rows, LANES),
                                   lambda i, be, fi, ws, nx, nu: (jnp.where(i < nu[0], i, nblk - 1), 0)),
            scratch_shapes=[pltpu.VMEM((d, D_FF), BF16), pltpu.VMEM((d, D_FF), BF16),
                            pltpu.VMEM((D_FF, d), BF16),
                            pltpu.VMEM((MOE_X_SLOTS * blk_rows, LANES), U32),
                            pltpu.SemaphoreType.DMA((MOE_X_SLOTS,)),
                            pltpu.VMEM((2 * d, D_FF), F32), pltpu.VMEM((2 * d, D_FF), F32),
                            pltpu.VMEM((2 * D_FF, d), F32), pltpu.SemaphoreType.DMA((2,))]),
        out_shape=jax.ShapeDtypeStruct((n_rows * TOK_SUBLANES, LANES), U32),
        compiler_params=pltpu.CompilerParams(dimension_semantics=("arbitrary",),
                                             vmem_limit_bytes=VMEM_LIMIT),
        name="moe",
    )(block_e, first.astype(I32), wslot, nxt, nused, x_pad, w_gate, w_up, w_down)

    yk = _sc_gather(y_pad.reshape(n_rows, TOK_SUBLANES, LANES), dest[0], dest[1])
    yk = yk.reshape(2 * t * TOK_SUBLANES, LANES)
    tc = min(512, t)
    nt = t // tc
    return pl.pallas_call(
        functools.partial(_combine_kernel, tm=tc),
        grid=(nt,),
        in_specs=[pl.BlockSpec((tc, d), lambda i: (i, 0)),
                  pl.BlockSpec((tc, LANES), lambda i: (i, 0)),
                  pl.BlockSpec((tc * TOK_SUBLANES, LANES), lambda i: (i, 0)),
                  pl.BlockSpec((tc * TOK_SUBLANES, LANES), lambda i: (nt + i, 0))],
        out_specs=pl.BlockSpec((tc, d), lambda i: (i, 0)),
        out_shape=jax.ShapeDtypeStruct((t, d), F32),
        compiler_params=pltpu.CompilerParams(dimension_semantics=("arbitrary",),
                                             vmem_limit_bytes=VMEM_LIMIT),
        name="combine",
    )(x1, route_f, yk, yk)


def kernel(x, norm1_g, w_in, q_norm_g, k_norm_g, attn_sink, conv_w, conv_b, w_attn_proj, w_conv_proj, w_out, norm2_g, w_router_group, b_router_group, w_router_expert, b_router_expert, w_gate_e, w_up_e, w_down_e):
    for l in range(norm1_g.shape[0]):
        x = _layer(x, norm1_g[l], w_in[l], q_norm_g[l], k_norm_g[l], attn_sink[l], conv_w[l],
                   conv_b[l], w_attn_proj[l], w_conv_proj[l], w_out[l], norm2_g[l],
                   w_router_group[l], b_router_group[l], w_router_expert[l], b_router_expert[l],
                   w_gate_e[l], w_up_e[l], w_down_e[l])
    return x
```

```python
import functools

import jax
import jax.numpy as jnp
import numpy as np
from jax import lax
from jax.experimental import pallas as pl
from jax.experimental.pallas import tpu as pltpu
from jax.experimental.pallas import tpu_sc as plsc

F32 = jnp.float32
BF16 = jnp.bfloat16
I32 = jnp.int32
U32 = jnp.uint32
HI_MASK = np.uint32(0xFFFF0000)

D_MODEL = 1024
N_HEADS = 8
N_KV_HEADS = 2
HEAD_DIM = 64
ATTN_DIM = N_HEADS * HEAD_DIM
KV_DIM = N_KV_HEADS * HEAD_DIM
WINDOW = 128
ROT_DIM = HEAD_DIM // 4
ROPE_THETA = 500000.0
CONV_DIM = D_MODEL // 2
N_GROUPS = 4
EXPERTS_PER_GROUP = 8
N_EXPERTS = N_GROUPS * EXPERTS_PER_GROUP
D_FF = D_MODEL // 4
NORM_EPS = 1e-6
MASK_VALUE = -1e30
NEG_BIG = -3.0e38
LOG2E = 1.4426950408889634

LANES = 128
TOK_SUBLANES = D_MODEL // (2 * LANES)
HALF_D = D_MODEL // 2
SC_WINDOW = 32
ROUTER_ROWS = 8 + N_EXPERTS
ROUTE_TILE = 4096
RANK_CHUNK = 256
MOE_ROWS = 512
MOE_X_SLOTS = 3
VMEM_LIMIT = 56 * 1024 * 1024

_Q0, _K0, _V0, _CB0, _CC0, _CX0, _GA0, _GC0, _END = (
    0, 512, 640, 768, 1280, 1792, 2304, 3328, 4352)


def _store_packed_rows(ref, val, n):
    for c in range(TOK_SUBLANES):
        lo = val[:, c * LANES:(c + 1) * LANES].astype(BF16).astype(F32)
        hi = val[:, HALF_D + c * LANES:HALF_D + (c + 1) * LANES].astype(BF16).astype(F32)
        word = (lax.bitcast_convert_type(lo, U32) >> 16) | (lax.bitcast_convert_type(hi, U32) & HI_MASK)
        ref[pl.ds(c, n, stride=TOK_SUBLANES), :] = word


def _load_packed_rows(ref, n, row0=0):
    words = [ref[pl.ds(row0 + c, n, stride=TOK_SUBLANES), :] for c in range(TOK_SUBLANES)]
    lo = [lax.bitcast_convert_type(w << 16, F32) for w in words]
    hi = [lax.bitcast_convert_type(w & HI_MASK, F32) for w in words]
    return lo + hi


def _rope_chunks(t, c, s1, s2):
    outs = []
    for j in range(t.shape[1] // LANES):
        tj = t[:, j * LANES:(j + 1) * LANES]
        outs.append(tj * c + pltpu.roll(tj, LANES - ROT_DIM // 2, 1) * s1
                    + pltpu.roll(tj, ROT_DIM // 2, 1) * s2)
    return outs[0] if len(outs) == 1 else jnp.concatenate(outs, axis=1)


def _proj_kernel(x_ref, g1_ref, w_ref, wq_ref, bd_ref, qg_ref, kg_ref, c_ref, s1_ref, s2_ref, cw_ref, cbias_ref,
                 q_ref, k_ref, v_ref, cb_ref, cp_ref, ue_ref, ga_ref, gc_ref, ubuf):
    x = x_ref[...]
    ms = jnp.mean(x * x, axis=-1, keepdims=True)
    h = (x * lax.rsqrt(ms + NORM_EPS) * g1_ref[...]).astype(BF16)

    def proj(a, b):
        return jnp.dot(h, w_ref[:, a:b], preferred_element_type=F32)

    c, s1, s2 = c_ref[...], s1_ref[...], s2_ref[...]

    ga_ref[...] = jax.nn.sigmoid(proj(_GA0, _GC0)).astype(BF16)
    gc_ref[...] = jax.nn.sigmoid(proj(_GC0, _END)).astype(BF16)
    cb_ref[...] = proj(_CB0, _CC0).astype(BF16)
    tm = x.shape[0]
    u = proj(_CC0, _CX0) * proj(_CX0, _GA0)
    ubuf[0:8, :] = jnp.zeros((8, CONV_DIM), F32)
    ubuf[8 + tm:16 + tm, :] = jnp.zeros((8, CONV_DIM), F32)
    ubuf[8:8 + tm, :] = u
    cw = cw_ref[...]
    cp_ref[...] = (ubuf[7:7 + tm, :] * cw[0:1, :] + u * cw[1:2, :] + ubuf[9:9 + tm, :] * cw[2:3, :]
                   + cbias_ref[...]).astype(BF16)
    ue_ref[...] = jnp.concatenate([u[0:8, :], u[tm - 8:tm, :]], axis=0).astype(BF16)
    v_ref[...] = proj(_V0, _CB0).astype(BF16)

    q = jnp.dot(h, wq_ref[...], preferred_element_type=F32)
    qms = jnp.dot((q * q).astype(BF16), bd_ref[...], preferred_element_type=F32)
    qg = jnp.concatenate([qg_ref[...]] * (ATTN_DIM // LANES), axis=1)
    qn = q * lax.rsqrt(qms + NORM_EPS) * qg
    q_ref[...] = (_rope_chunks(qn, c, s1, s2) * (HEAD_DIM ** -0.5 * LOG2E)).astype(BF16)

    k = proj(_K0, _V0)
    kms = jnp.dot((k * k).astype(BF16), bd_ref[0:KV_DIM, 0:KV_DIM], preferred_element_type=F32)
    kn = k * lax.rsqrt(kms + NORM_EPS) * kg_ref[...]
    k_ref[...] = _rope_chunks(kn, c, s1, s2).astype(BF16)


def _mix_kernel(sink_ref, x_ref, q_ref, kp_ref, kc_ref, kn_ref, vp_ref, vc_ref, vn_ref,
                cb_ref, cp_ref, uep_ref, uec_ref, uen_ref, ga_ref, gc_ref,
                cw_ref, wa_ref, wc_ref, wo_ref, g2_ref,
                wr_ref, br_ref,
                x1_ref, h2_ref, lg_ref,
                cscr, attn_scr, *, tq, tm):
    j = pl.program_id(1)
    nj = pl.num_programs(1)
    nsub = tq // WINDOW

    kbuf = jnp.concatenate([kp_ref[...], kc_ref[...], kn_ref[...]], axis=0)
    vbuf = jnp.concatenate([vp_ref[...], vc_ref[...], vn_ref[...]], axis=0)
    low_half = lax.broadcasted_iota(I32, (WINDOW, LANES), 1) < HEAD_DIM
    low_half3 = lax.broadcasted_iota(I32, (3 * WINDOW, LANES), 1) < HEAD_DIM
    qrow = lax.broadcasted_iota(I32, (WINDOW, WINDOW), 0)
    kcol = lax.broadcasted_iota(I32, (WINDOW, WINDOW), 1)
    bias_lo = jnp.where(kcol >= qrow, 0.0, MASK_VALUE)
    bias_hi = jnp.where(kcol <= qrow, 0.0, MASK_VALUE)
    one = jnp.ones((3 * WINDOW, LANES), BF16)
    def scores(n):
        qs = q_ref[n * WINDOW:(n + 1) * WINDOW, :]
        zero = jnp.zeros((WINDOW, LANES), BF16)
        q_stack = []
        for hd in range(N_HEADS):
            col = qs[:, (hd % 4) * LANES:(hd % 4 + 1) * LANES]
            keep = low_half if hd < 4 else jnp.logical_not(low_half)
            q_stack.append(jnp.where(keep, col, zero))
        q_stack = jnp.concatenate(q_stack, axis=0)
        kwin = kbuf[n * WINDOW:(n + 3) * WINDOW, :]
        return lax.dot_general(q_stack, kwin, (((1,), (1,)), ((), ())),
                               preferred_element_type=F32)

    s_next = scores(0)
    for n in range(nsub):
        blo = jnp.where(j > 0, bias_lo, MASK_VALUE) if n == 0 else bias_lo
        bhi = jnp.where(j < nj - 1, bias_hi, MASK_VALUE) if n == nsub - 1 else bias_hi
        s = s_next
        if n + 1 < nsub:
            s_next = scores(n + 1)
        vwin = vbuf[n * WINDOW:(n + 3) * WINDOW, :]
        p_list, es_list = [], []
        for hd in range(N_HEADS):
            sink = sink_ref[hd] * LOG2E
            rows = slice(hd * WINDOW, (hd + 1) * WINDOW)
            s0 = s[rows, 0:WINDOW] + blo
            s1 = s[rows, WINDOW:2 * WINDOW]
            s2 = s[rows, 2 * WINDOW:3 * WINDOW] + bhi
            m = jnp.max(jnp.maximum(jnp.maximum(s0, s1), s2), axis=-1, keepdims=True)
            m = jnp.maximum(m, sink)
            p_list.append(jnp.concatenate(
                [jnp.exp2(s0 - m).astype(BF16), jnp.exp2(s1 - m).astype(BF16),
                 jnp.exp2(s2 - m).astype(BF16)], axis=1))
            es_list.append(jnp.exp2(sink - m))
        half = N_HEADS // 2
        o_lo = jnp.dot(jnp.concatenate(p_list[:half], axis=0), jnp.where(low_half3, vwin, one),
                       preferred_element_type=F32)
        o_hi = jnp.dot(jnp.concatenate(p_list[half:], axis=0), jnp.where(low_half3, one, vwin),
                       preferred_element_type=F32)
        cols = []
        for c4 in range(half):
            rows = slice(c4 * WINDOW, (c4 + 1) * WINDOW)
            oa, ob = o_lo[rows, :], o_hi[rows, :]
            da = pltpu.roll(oa, HEAD_DIM, 1) + es_list[c4]
            db = pltpu.roll(ob, HEAD_DIM, 1) + es_list[c4 + half]
            cols.append(jnp.where(low_half, oa / da, ob / db))
        attn_scr[n * WINDOW:(n + 1) * WINDOW, :] = jnp.concatenate(cols, axis=1).astype(BF16)
    attn = attn_scr[...]

    cscr[...] = cp_ref[...].astype(F32)
    cw = cw_ref[...]
    uec = uec_ref[...].astype(F32)
    n_tiles = tq // tm
    for p in range(n_tiles):
        if p == 0:
            prev_last = uep_ref[...].astype(F32)[15:16, :] * (j > 0).astype(F32)
        else:
            prev_last = uec[16 * p - 1:16 * p, :]
        cscr[p * tm:p * tm + 1, :] = cscr[p * tm:p * tm + 1, :] + prev_last * cw[0:1, :]
        if p == n_tiles - 1:
            next_first = uen_ref[...].astype(F32)[0:1, :] * (j < nj - 1).astype(F32)
        else:
            next_first = uec[16 * (p + 1):16 * (p + 1) + 1, :]
        last = (p + 1) * tm - 1
        cscr[last:last + 1, :] = cscr[last:last + 1, :] + next_first * cw[2:3, :]
    cgate = (cb_ref[...].astype(F32) * cscr[...]).astype(BF16)

    conv_out = gc_ref[...].astype(F32) * jnp.dot(cgate, wc_ref[...], preferred_element_type=F32)
    merged = ga_ref[...].astype(F32) * jnp.dot(attn, wa_ref[...], preferred_element_type=F32) + conv_out
    x1 = x_ref[...] + jnp.dot(merged.astype(BF16), wo_ref[...], preferred_element_type=F32)
    x1_ref[...] = x1

    ms = jnp.mean(x1 * x1, axis=-1, keepdims=True)
    h2 = x1 * lax.rsqrt(ms + NORM_EPS) * g2_ref[...]
    _store_packed_rows(h2_ref, h2, tq)
    h_hi = h2.astype(BF16)
    h_lo = (h2 - h_hi.astype(F32)).astype(BF16)
    acc = jnp.dot(h_hi, wr_ref[...], preferred_element_type=F32)
    logits = (acc[:, 0:LANES] + acc[:, LANES:2 * LANES]
              + jnp.dot(h_lo, wr_ref[:, 0:LANES], preferred_element_type=F32) + br_ref[...])
    lg_ref[...] = logits


def _route_kernel(lg_ref, tri_ref, ri_ref, rf_ref, cnt_ref, base_ref, *, tr):
    @pl.when(pl.program_id(0) == 0)
    def _():
        base_ref[...] = jnp.zeros_like(base_ref)

    logits = lg_ref[...].T[0:ROUTER_ROWS, :]
    grow = lax.broadcasted_iota(I32, (8, tr), 0)
    gl = jnp.where(grow < N_GROUPS, logits[0:8, :], NEG_BIG)
    gmax = jnp.max(gl, axis=0, keepdims=True)
    grp = jnp.min(jnp.where(gl == gmax, grow, 8), axis=0, keepdims=True)
    p_grp = 1.0 / jnp.sum(jnp.exp(gl - gmax), axis=0, keepdims=True)
    el = logits[8:ROUTER_ROWS, :]
    erow = lax.broadcasted_iota(I32, (N_EXPERTS, tr), 0)
    in_grp = (erow // EXPERTS_PER_GROUP) == grp
    l_a = jnp.where(in_grp, el, NEG_BIG)
    l1 = jnp.max(l_a, axis=0, keepdims=True)
    i1 = jnp.min(jnp.where(l_a == l1, erow, N_EXPERTS), axis=0, keepdims=True)
    l_b = jnp.where(erow == i1, NEG_BIG, l_a)
    l2 = jnp.max(l_b, axis=0, keepdims=True)
    i2 = jnp.min(jnp.where(l_b == l2, erow, N_EXPERTS), axis=0, keepdims=True)
    r21 = jnp.exp(l2 - l1)
    t1 = 1.0 / (1.0 + r21)
    gate1 = p_grp * t1
    gate2 = p_grp * (r21 * t1)
    oh1 = (erow == i1).astype(F32)
    oh2 = (erow == i2).astype(F32)
    oh = oh1 + oh2
    base = base_ref[...][:, 0:1]
    r1, r2 = [], []
    for c in range(tr // RANK_CHUNK):
        cols = slice(c * RANK_CHUNK, (c + 1) * RANK_CHUNK)
        ohc = oh[:, cols]
        before = jnp.dot(ohc.astype(BF16), tri_ref[...], preferred_element_type=F32) + base
        r1.append(jnp.sum(oh1[:, cols] * before, axis=0, keepdims=True))
        r2.append(jnp.sum(oh2[:, cols] * before, axis=0, keepdims=True))
        base = base + jnp.sum(ohc, axis=1, keepdims=True)
    rank1 = jnp.concatenate(r1, axis=1)
    rank2 = jnp.concatenate(r2, axis=1)
    new_base = jnp.broadcast_to(base, base_ref.shape)
    base_ref[...] = new_base
    cnt_ref[...] = new_base
    zi = jnp.zeros((4, tr), I32)
    ri_ref[...] = jnp.concatenate([i1, i2, rank1.astype(I32), rank2.astype(I32), zi], axis=0)
    rf_ref[...] = jnp.concatenate([gate1, gate2, jnp.zeros((LANES - 2, tr), F32)], axis=0).T


def _moe_kernel(be_ref, first_ref, wslot_ref, nxt_ref, nused_ref, x_hbm, wg_hbm, wu_hbm, wd_hbm, y_ref,
                wg_s, wu_s, wd_s, xbuf, xsem, wst_g, wst_u, wst_d, wsem):
    i = pl.program_id(0)
    nused = nused_ref[0]
    blk_rows = MOE_ROWS * TOK_SUBLANES

    def x_copy(blk):
        slot = lax.rem(blk, MOE_X_SLOTS)
        return pltpu.make_async_copy(
            x_hbm.at[pl.ds(pl.multiple_of(blk * blk_rows, blk_rows), blk_rows)],
            xbuf.at[pl.ds(pl.multiple_of(slot * blk_rows, blk_rows), blk_rows)],
            xsem.at[slot])

    @pl.when(i == 0)
    def _():
        x_copy(0).start()

        @pl.when(nused > 1)
        def _():
            x_copy(1).start()

    @pl.when(i + 2 < nused)
    def _():
        x_copy(i + 2).start()

    def w_copies(e, slot):
        return (pltpu.make_async_copy(wg_hbm.at[e], wst_g.at[pl.ds(slot * D_MODEL, D_MODEL)], wsem.at[slot]),
                pltpu.make_async_copy(wu_hbm.at[e], wst_u.at[pl.ds(slot * D_MODEL, D_MODEL)], wsem.at[slot]),
                pltpu.make_async_copy(wd_hbm.at[e], wst_d.at[pl.ds(slot * D_FF, D_FF)], wsem.at[slot]))

    @pl.when(i == 0)
    def _():
        for cp in w_copies(be_ref[0], 0):
            cp.start()

    def new_run(_, carry):
        slot = wslot_ref[i]
        for cp in w_copies(be_ref[i], slot):
            cp.wait()
        wg_s[...] = wst_g[pl.ds(pl.multiple_of(slot * D_MODEL, D_MODEL), D_MODEL), :].astype(BF16)
        wu_s[...] = wst_u[pl.ds(pl.multiple_of(slot * D_MODEL, D_MODEL), D_MODEL), :].astype(BF16)
        wd_s[...] = wst_d[pl.ds(pl.multiple_of(slot * D_FF, D_FF), D_FF), :].astype(BF16)

        @pl.when(nxt_ref[i] >= 0)
        def _():
            for cp in w_copies(nxt_ref[i], 1 - slot):
                cp.start()
        return carry

    lax.fori_loop(0, first_ref[i], new_run, 0)

    @pl.when(i < nused)
    def _():
        x_copy(i).wait()
        row0 = pl.multiple_of(lax.rem(i, MOE_X_SLOTS) * blk_rows, blk_rows)
        xb = jnp.concatenate([c.astype(BF16) for c in _load_packed_rows(xbuf, MOE_ROWS, row0)], axis=1)
        g = jnp.dot(xb, wg_s[...], preferred_element_type=F32)
        u = jnp.dot(xb, wu_s[...], preferred_element_type=F32)
        hid = (g * jax.nn.sigmoid(g) * u).astype(BF16)
        _store_packed_rows(y_ref, jnp.dot(hid, wd_s[...], preferred_element_type=F32), MOE_ROWS)

    @pl.when(i >= nused)
    def _():
        y_ref[...] = jnp.zeros_like(y_ref)


def _combine_kernel(x1_ref, gate_ref, y0_ref, y1_ref, o_ref, *, tm):
    g = gate_ref[...]
    g0, g1 = g[:, 0:1], g[:, 1:2]
    y0 = _load_packed_rows(y0_ref, tm)
    y1 = _load_packed_rows(y1_ref, tm)
    for c in range(2 * TOK_SUBLANES):
        cols = slice(c * LANES, (c + 1) * LANES)
        o_ref[:, cols] = x1_ref[:, cols] + g0 * y0[c] + g1 * y1[c]


def _sc_windows(n_tokens):
    info = plsc.get_sparse_core_info()
    n_workers = info.num_cores * info.num_subcores
    assert n_tokens % (n_workers * SC_WINDOW * 2) == 0
    return info.num_cores, n_tokens // (n_workers * SC_WINDOW)


def _sc_dispatch(h3, d0, d1, n_rows):
    t = h3.shape[0]
    n_cores, nwin = _sc_windows(t)
    mesh = plsc.VectorSubcoreMesh(core_axis_name="c", subcore_axis_name="s")
    rows = pltpu.VMEM((SC_WINDOW,) + h3.shape[1:], h3.dtype)
    idx = pltpu.VMEM((nwin, SC_WINDOW), I32)

    @functools.partial(
        pl.kernel, mesh=mesh,
        out_type=jax.ShapeDtypeStruct((n_rows,) + h3.shape[1:], h3.dtype),
        scratch_types=[idx, idx, idx, rows, rows] + [pltpu.SemaphoreType.DMA] * 3,
        name="sc_dispatch")
    def run(h_hbm, d0_hbm, d1_hbm, it_hbm, o_hbm, i0, i1, it, buf_a, buf_b, sem_a, sem_b, sem_s):
        wbase = (lax.axis_index("s") * n_cores + lax.axis_index("c")) * nwin
        pltpu.sync_copy(d0_hbm.at[pl.ds(wbase, nwin)], i0)
        pltpu.sync_copy(d1_hbm.at[pl.ds(wbase, nwin)], i1)
        pltpu.sync_copy(it_hbm.at[pl.ds(wbase, nwin)], it)

        @pl.loop(0, nwin, step=2)
        def _(w):
            ga = pltpu.async_copy(h_hbm.at[it.at[w]], buf_a, sem_a)
            gb = pltpu.async_copy(h_hbm.at[it.at[w + 1]], buf_b, sem_b)
            ga.wait()
            s0 = pltpu.async_copy(buf_a, o_hbm.at[i0.at[w]], sem_s)
            s1 = pltpu.async_copy(buf_a, o_hbm.at[i1.at[w]], sem_s)
            gb.wait()
            s2 = pltpu.async_copy(buf_b, o_hbm.at[i0.at[w + 1]], sem_s)
            s3 = pltpu.async_copy(buf_b, o_hbm.at[i1.at[w + 1]], sem_s)
            s0.wait()
            s1.wait()
            s2.wait()
            s3.wait()

    win = lambda a: a.reshape(-1, SC_WINDOW)
    return run(h3, win(d0), win(d1), win(jnp.arange(t, dtype=I32)))


def _sc_gather(y3, d0, d1):
    t = d0.shape[0]
    n_cores, nwin = _sc_windows(t)
    mesh = plsc.VectorSubcoreMesh(core_axis_name="c", subcore_axis_name="s")
    rows = pltpu.VMEM((SC_WINDOW,) + y3.shape[1:], y3.dtype)
    idx = pltpu.VMEM((nwin, SC_WINDOW), I32)

    @functools.partial(
        pl.kernel, mesh=mesh,
        out_type=jax.ShapeDtypeStruct((2 * t,) + y3.shape[1:], y3.dtype),
        scratch_types=[idx] * 4 + [rows] * 4 + [pltpu.SemaphoreType.DMA] * 5,
        name="sc_gather")
    def run(y_hbm, d0_hbm, d1_hbm, ita_hbm, itb_hbm, o_hbm, i0, i1, ita, itb,
            b0, b1, b2, b3, m0, m1, m2, m3, ms):
        wbase = (lax.axis_index("s") * n_cores + lax.axis_index("c")) * nwin
        pltpu.sync_copy(d0_hbm.at[pl.ds(wbase, nwin)], i0)
        pltpu.sync_copy(d1_hbm.at[pl.ds(wbase, nwin)], i1)
        pltpu.sync_copy(ita_hbm.at[pl.ds(wbase, nwin)], ita)
        pltpu.sync_copy(itb_hbm.at[pl.ds(wbase, nwin)], itb)

        @pl.loop(0, nwin, step=2)
        def _(w):
            g0 = pltpu.async_copy(y_hbm.at[i0.at[w]], b0, m0)
            g1 = pltpu.async_copy(y_hbm.at[i1.at[w]], b1, m1)
            g2 = pltpu.async_copy(y_hbm.at[i0.at[w + 1]], b2, m2)
            g3 = pltpu.async_copy(y_hbm.at[i1.at[w + 1]], b3, m3)
            g0.wait()
            s0 = pltpu.async_copy(b0, o_hbm.at[ita.at[w]], ms)
            g1.wait()
            s1 = pltpu.async_copy(b1, o_hbm.at[itb.at[w]], ms)
            g2.wait()
            s2 = pltpu.async_copy(b2, o_hbm.at[ita.at[w + 1]], ms)
            g3.wait()
            s3 = pltpu.async_copy(b3, o_hbm.at[itb.at[w + 1]], ms)
            s0.wait()
            s1.wait()
            s2.wait()
            s3.wait()

    win = lambda a: a.reshape(-1, SC_WINDOW)
    iota = jnp.arange(2 * t, dtype=I32)
    return run(y3, win(d0), win(d1), win(iota[:t]), win(iota[t:]))


def _rope_lane_tables(seq_len):
    half = ROT_DIM // 2
    inv_freq = ROPE_THETA ** (-np.arange(0, ROT_DIM, 2, dtype=np.float64) / ROT_DIM)
    ang = np.arange(seq_len, dtype=np.float64)[:, None] * inv_freq[None, :]
    cos, sin = np.cos(ang), np.sin(ang)
    m = np.arange(LANES) % HEAD_DIM
    f = m % half
    c = np.where(m[None, :] < ROT_DIM, cos[:, f], 1.0)
    s1 = np.where(m[None, :] < half, -sin[:, f], 0.0)
    s2 = np.where((m[None, :] >= half) & (m[None, :] < ROT_DIM), sin[:, f], 0.0)
    return tuple(jnp.asarray(a, F32) for a in (c, s1, s2))


def _layer(x, norm1_g, w_in, q_norm_g, k_norm_g, attn_sink, conv_w, conv_b, w_attn_proj,
           w_conv_proj, w_out, norm2_g, w_rg, b_rg, w_re, b_re, w_gate, w_up, w_down):
    bsz, seq, d = x.shape
    t = bsz * seq
    tm = min(512, seq)
    tq = min(1024, seq)
    assert d == D_MODEL and seq % tq == 0 and seq % tm == 0 and tq % tm == 0 and tq % WINDOW == 0
    xf = x.reshape(t, d)

    hpg = N_HEADS // N_KV_HEADS
    wq_p = (w_in[:, :ATTN_DIM].reshape(d, N_KV_HEADS, hpg, HEAD_DIM).transpose(0, 2, 1, 3)
            .reshape(d, ATTN_DIM))
    wq_p = wq_p.astype(BF16)
    w_in_b = w_in.astype(BF16)
    wa_p = (w_attn_proj.reshape(N_KV_HEADS, hpg, HEAD_DIM, d).transpose(1, 0, 2, 3)
            .reshape(ATTN_DIM, d).astype(BF16))
    bd = jnp.asarray(np.kron(np.eye(N_HEADS), np.full((HEAD_DIM, HEAD_DIM), 1.0 / HEAD_DIM)), BF16)
    qg = jnp.tile(q_norm_g, LANES // HEAD_DIM).reshape(1, LANES)
    kg = jnp.tile(k_norm_g, LANES // HEAD_DIM).reshape(1, LANES)
    ctab, s1tab, s2tab = _rope_lane_tables(seq)
    nseq_t = seq // tm

    row = lambda w: pl.BlockSpec((tm, w), lambda i: (i, 0))
    full = lambda a: pl.BlockSpec(a.shape, lambda i: (0,) * a.ndim)
    tab = pl.BlockSpec((tm, LANES), lambda i: (i % nseq_t, 0))
    cbias = conv_b.reshape(1, CONV_DIM)
    q, k, v, cb, cp, ue, ga, gc = pl.pallas_call(
        _proj_kernel,
        grid=(t // tm,),
        in_specs=[row(d), full(norm1_g.reshape(1, d)), full(w_in_b), full(wq_p), full(bd), full(qg), full(kg),
                  tab, tab, tab, full(conv_w), full(cbias)],
        out_specs=[row(ATTN_DIM), row(KV_DIM), row(KV_DIM), row(CONV_DIM), row(CONV_DIM),
                   pl.BlockSpec((16, CONV_DIM), lambda i: (i, 0)), row(D_MODEL), row(D_MODEL)],
        out_shape=[jax.ShapeDtypeStruct((t, w), BF16) for w in (ATTN_DIM, KV_DIM, KV_DIM, CONV_DIM, CONV_DIM)]
                  + [jax.ShapeDtypeStruct((t // tm * 16, CONV_DIM), BF16),
                     jax.ShapeDtypeStruct((t, D_MODEL), BF16), jax.ShapeDtypeStruct((t, D_MODEL), BF16)],
        scratch_shapes=[pltpu.VMEM((tm + 16, CONV_DIM), F32)],
        compiler_params=pltpu.CompilerParams(dimension_semantics=("arbitrary",),
                                             vmem_limit_bytes=VMEM_LIMIT),
        name="proj",
    )(xf, norm1_g.reshape(1, d), w_in_b, wq_p, bd, qg, kg, ctab, s1tab, s2tab, conv_w, cbias)

    wr = jnp.concatenate([w_rg, jnp.zeros((d, 8 - N_GROUPS), F32), w_re,
                          jnp.zeros((d, LANES - ROUTER_ROWS), F32)], axis=1)
    wr_hi = wr.astype(BF16)
    wr2 = jnp.concatenate([wr_hi, (wr - wr_hi.astype(F32)).astype(BF16)], axis=1)
    br = jnp.concatenate([b_rg, jnp.zeros((8 - N_GROUPS,), F32), b_re,
                          jnp.zeros((LANES - ROUTER_ROWS,), F32)]).reshape(1, LANES)
    mix_params = (conv_w, wa_p, w_conv_proj.astype(BF16), w_out.astype(BF16),
                  norm2_g.reshape(1, d), wr2, br)

    out = _mixers_and_moe(xf, (q, k, v, cb, cp, ue, ga, gc), attn_sink, mix_params,
                          (w_gate, w_up, w_down), bsz, seq, tq, tm)
    return out.reshape(bsz, seq, d)


def _mixers_and_moe(xf, acts, attn_sink, mix_params, expert_w, bsz, seq, tq, tm):
    q, k, v, cb, cp, ue, ga, gc = acts
    d = D_MODEL
    t = bsz * seq
    nj = seq // tq
    wpb = tq // WINDOW
    nb_seq = seq // WINDOW
    tpt = tq // tm
    n_edge = t // tm
    row_in = lambda w: pl.BlockSpec((tq, w), lambda b, j: (b * nj + j, 0))
    row_out = row_in
    prev_blk = pl.BlockSpec((WINDOW, KV_DIM),
                            lambda b, j: (b * nb_seq + jnp.maximum(j * wpb - 1, 0), 0))
    next_blk = pl.BlockSpec((WINDOW, KV_DIM),
                            lambda b, j: (b * nb_seq + jnp.minimum((j + 1) * wpb, nb_seq - 1), 0))
    edge_prev = pl.BlockSpec((16, CONV_DIM),
                             lambda b, j: (jnp.maximum((b * nj + j) * tpt - 1, 0), 0))
    edge_cur = pl.BlockSpec((16 * tpt, CONV_DIM), lambda b, j: (b * nj + j, 0))
    edge_next = pl.BlockSpec((16, CONV_DIM),
                             lambda b, j: (jnp.minimum((b * nj + j + 1) * tpt, n_edge - 1), 0))
    full2 = lambda a: pl.BlockSpec(a.shape, lambda b, j: (0,) * a.ndim)

    x1, h2, logits = pl.pallas_call(
        functools.partial(_mix_kernel, tq=tq, tm=tm),
        grid=(bsz, nj),
        in_specs=[pl.BlockSpec(memory_space=pltpu.SMEM),
                  row_in(d), row_in(ATTN_DIM), prev_blk, row_in(KV_DIM), next_blk,
                  prev_blk, row_in(KV_DIM), next_blk,
                  row_in(CONV_DIM), row_in(CONV_DIM), edge_prev, edge_cur, edge_next,
                  row_in(D_MODEL), row_in(D_MODEL)]
                 + [full2(p) for p in mix_params],
        out_specs=[row_out(d), pl.BlockSpec((tq * TOK_SUBLANES, LANES), lambda b, j: (b * nj + j, 0)),
                   row_out(LANES)],
        out_shape=[jax.ShapeDtypeStruct((t, d), F32),
                   jax.ShapeDtypeStruct((t * TOK_SUBLANES, LANES), U32),
                   jax.ShapeDtypeStruct((t, LANES), F32)],
        scratch_shapes=[pltpu.VMEM((tq, CONV_DIM), F32), pltpu.VMEM((tq, ATTN_DIM), BF16)],
        compiler_params=pltpu.CompilerParams(dimension_semantics=("arbitrary", "arbitrary"),
                                             vmem_limit_bytes=VMEM_LIMIT),
        name="mix",
    )(attn_sink, xf, q, k, k, k, v, v, v, cb, cp, ue, ue, ue, ga, gc, *mix_params)

    tr = min(ROUTE_TILE, t)
    assert t % tr == 0 and tr % RANK_CHUNK == 0
    tri = jnp.asarray(np.triu(np.ones((RANK_CHUNK, RANK_CHUNK), np.float32), 1), BF16)
    colr = lambda r: pl.BlockSpec((r, tr), lambda i: (0, i))
    route_i, route_f, counts = pl.pallas_call(
        functools.partial(_route_kernel, tr=tr),
        grid=(t // tr,),
        in_specs=[pl.BlockSpec((tr, LANES), lambda i: (i, 0)), pl.BlockSpec(tri.shape, lambda i: (0, 0))],
        out_specs=[colr(8), pl.BlockSpec((tr, LANES), lambda i: (i, 0)),
                   pl.BlockSpec((N_EXPERTS, LANES), lambda i: (0, 0))],
        out_shape=[jax.ShapeDtypeStruct((8, t), I32), jax.ShapeDtypeStruct((t, LANES), F32),
                   jax.ShapeDtypeStruct((N_EXPERTS, LANES), F32)],
        scratch_shapes=[pltpu.VMEM((N_EXPERTS, LANES), F32)],
        compiler_params=pltpu.CompilerParams(dimension_semantics=("arbitrary",),
                                             vmem_limit_bytes=VMEM_LIMIT),
        name="route",
    )(logits, tri)

    n_rows = 2 * t + N_EXPERTS * MOE_ROWS
    nblk = n_rows // MOE_ROWS
    cnt = counts[:, 0].astype(I32)
    padded = ((cnt + MOE_ROWS - 1) // MOE_ROWS) * MOE_ROWS
    pad_ends = jnp.cumsum(padded)
    pad_starts = pad_ends - padded
    eids = jnp.arange(N_EXPERTS, dtype=I32)[:, None, None]
    dest = jnp.sum(jnp.where(route_i[None, 0:2] == eids, pad_starts[:, None, None], 0), axis=0) \
        + route_i[2:4]
    blk_start = jnp.arange(nblk, dtype=I32) * MOE_ROWS
    block_e = jnp.minimum(jnp.sum((pad_ends[None, :] <= blk_start[:, None]).astype(I32), axis=1),
                          N_EXPERTS - 1)
    nused = (pad_ends[-1:] // MOE_ROWS).astype(I32)

    x_pad = _sc_dispatch(h2.reshape(t, TOK_SUBLANES, LANES), dest[0], dest[1], n_rows)
    x_pad = x_pad.reshape(n_rows * TOK_SUBLANES, LANES)

    blk = jnp.arange(nblk, dtype=I32)
    first = (blk < nused[0]) & (block_e != jnp.concatenate([jnp.full((1,), -1, I32), block_e[:-1]]))
    wslot = ((jnp.cumsum(first.astype(I32)) - 1) % 2).astype(I32)
    later_first = first[None, :] & (blk[None, :] > blk[:, None])
    nxt = jnp.where(jnp.any(later_first, axis=1), block_e[jnp.argmax(later_first, axis=1)], -1).astype(I32)

    w_gate, w_up, w_down = expert_w
    blk_rows = MOE_ROWS * TOK_SUBLANES
    hbm = pl.BlockSpec(memory_space=pl.ANY)
    y_pad = pl.pallas_call(
        _moe_kernel,
        grid_spec=pltpu.PrefetchScalarGridSpec(
            num_scalar_prefetch=5,
            grid=(nblk,),
            in_specs=[hbm, hbm, hbm, hbm],
            out_specs=pl.BlockSpec((blk_rows, LANES),
                                   lambda i, be, fi, ws, nx, nu: (jnp.where(i < nu[0], i, nblk - 1), 0)),
            scratch_shapes=[pltpu.VMEM((d, D_FF), BF16), pltpu.VMEM((d, D_FF), BF16),
                            pltpu.VMEM((D_FF, d), BF16),
                            pltpu.VMEM((MOE_X_SLOTS * blk_rows, LANES), U32),
                            pltpu.SemaphoreType.DMA((MOE_X_SLOTS,)),
                            pltpu.VMEM((2 * d, D_FF), F32), pltpu.VMEM((2 * d, D_FF), F32),
                            pltpu.VMEM((2 * D_FF, d), F32), pltpu.SemaphoreType.DMA((2,))]),
        out_shape=jax.ShapeDtypeStruct((n_rows * TOK_SUBLANES, LANES), U32),
        compiler_params=pltpu.CompilerParams(dimension_semantics=("arbitrary",),
                                             vmem_limit_bytes=VMEM_LIMIT),
        name="moe",
    )(block_e, first.astype(I32), wslot, nxt, nused, x_pad, w_gate, w_up, w_down)

    yk = _sc_gather(y_pad.reshape(n_rows, TOK_SUBLANES, LANES), dest[0], dest[1])
    yk = yk.reshape(2 * t * TOK_SUBLANES, LANES)
    tc = min(512, t)
    nt = t // tc
    return pl.pallas_call(
        functools.partial(_combine_kernel, tm=tc),
        grid=(nt,),
        in_specs=[pl.BlockSpec((tc, d), lambda i: (i, 0)),
                  pl.BlockSpec((tc, LANES), lambda i: (i, 0)),
                  pl.BlockSpec((tc * TOK_SUBLANES, LANES), lambda i: (i, 0)),
                  pl.BlockSpec((tc * TOK_SUBLANES, LANES), lambda i: (nt + i, 0))],
        out_specs=pl.BlockSpec((tc, d), lambda i: (i, 0)),
        out_shape=jax.ShapeDtypeStruct((t, d), F32),
        compiler_params=pltpu.CompilerParams(dimension_semantics=("arbitrary",),
                                             vmem_limit_bytes=VMEM_LIMIT),
        name="combine",
    )(x1, route_f, yk, yk)


def kernel(x, norm1_g, w_in, q_norm_g, k_norm_g, attn_sink, conv_w, conv_b, w_attn_proj, w_conv_proj, w_out, norm2_g, w_router_group, b_router_group, w_router_expert, b_router_expert, w_gate_e, w_up_e, w_down_e):
    for l in range(norm1_g.shape[0]):
        x = _layer(x, norm1_g[l], w_in[l], q_norm_g[l], k_norm_g[l], attn_sink[l], conv_w[l],
                   conv_b[l], w_attn_proj[l], w_conv_proj[l], w_out[l], norm2_g[l],
                   w_router_group[l], b_router_group[l], w_router_expert[l], b_router_expert[l],
                   w_gate_e[l], w_up_e[l], w_down_e[l])
    return x
```

```python
import functools

import jax
import jax.numpy as jnp
import numpy as np
from jax import lax
from jax.experimental import pallas as pl
from jax.experimental.pallas import tpu as pltpu
from jax.experimental.pallas import tpu_sc as plsc

F32 = jnp.float32
BF16 = jnp.bfloat16
I32 = jnp.int32
U32 = jnp.uint32
HI_MASK = np.uint32(0xFFFF0000)

D_MODEL = 1024
N_HEADS = 8
N_KV_HEADS = 2
HEAD_DIM = 64
ATTN_DIM = N_HEADS * HEAD_DIM
KV_DIM = N_KV_HEADS * HEAD_DIM
WINDOW = 128
ROT_DIM = HEAD_DIM // 4
ROPE_THETA = 500000.0
CONV_DIM = D_MODEL // 2
N_GROUPS = 4
EXPERTS_PER_GROUP = 8
N_EXPERTS = N_GROUPS * EXPERTS_PER_GROUP
D_FF = D_MODEL // 4
NORM_EPS = 1e-6
MASK_VALUE = -1e30
NEG_BIG = -3.0e38
LOG2E = 1.4426950408889634

LANES = 128
TOK_SUBLANES = D_MODEL // (2 * LANES)
HALF_D = D_MODEL // 2
SC_WINDOW = 32
ROUTER_ROWS = 8 + N_EXPERTS
ROUTE_TILE = 4096
RANK_CHUNK = 256
MOE_ROWS = 512
MOE_X_SLOTS = 3
VMEM_LIMIT = 56 * 1024 * 1024

_Q0, _K0, _V0, _CB0, _CC0, _CX0, _GA0, _GC0, _END = (
    0, 512, 640, 768, 1280, 1792, 2304, 3328, 4352)


def _store_packed_rows(ref, val, n, row0=0):
    for c in range(TOK_SUBLANES):
        lo = val[:, c * LANES:(c + 1) * LANES].astype(BF16).astype(F32)
        hi = val[:, HALF_D + c * LANES:HALF_D + (c + 1) * LANES].astype(BF16).astype(F32)
        word = (lax.bitcast_convert_type(lo, U32) >> 16) | (lax.bitcast_convert_type(hi, U32) & HI_MASK)
        ref[pl.ds(row0 + c, n, stride=TOK_SUBLANES), :] = word


def _load_packed_rows(ref, n, row0=0):
    words = [ref[pl.ds(row0 + c, n, stride=TOK_SUBLANES), :] for c in range(TOK_SUBLANES)]
    lo = [lax.bitcast_convert_type(w << 16, F32) for w in words]
    hi = [lax.bitcast_convert_type(w & HI_MASK, F32) for w in words]
    return lo + hi


def _rope_chunks(t, c, s1, s2):
    outs = []
    for j in range(t.shape[1] // LANES):
        tj = t[:, j * LANES:(j + 1) * LANES]
        outs.append(tj * c + pltpu.roll(tj, LANES - ROT_DIM // 2, 1) * s1
                    + pltpu.roll(tj, ROT_DIM // 2, 1) * s2)
    return outs[0] if len(outs) == 1 else jnp.concatenate(outs, axis=1)


def _proj_kernel(x_ref, g1_ref, w_ref, wq_ref, bd_ref, qg_ref, kg_ref, c_ref, s1_ref, s2_ref, cw_ref, cbias_ref,
                 q_ref, k_ref, v_ref, cb_ref, cp_ref, ue_ref, ga_ref, gc_ref, ubuf):
    x = x_ref[...]
    ms = jnp.mean(x * x, axis=-1, keepdims=True)
    h = (x * lax.rsqrt(ms + NORM_EPS) * g1_ref[...]).astype(BF16)

    def proj(a, b):
        return jnp.dot(h, w_ref[:, a:b], preferred_element_type=F32)

    c, s1, s2 = c_ref[...], s1_ref[...], s2_ref[...]

    ga_ref[...] = jax.nn.sigmoid(proj(_GA0, _GC0)).astype(BF16)
    gc_ref[...] = jax.nn.sigmoid(proj(_GC0, _END)).astype(BF16)
    cb_ref[...] = proj(_CB0, _CC0).astype(BF16)
    tm = x.shape[0]
    u = proj(_CC0, _CX0) * proj(_CX0, _GA0)
    ubuf[0:8, :] = jnp.zeros((8, CONV_DIM), F32)
    ubuf[8 + tm:16 + tm, :] = jnp.zeros((8, CONV_DIM), F32)
    ubuf[8:8 + tm, :] = u
    cw = cw_ref[...]
    cp_ref[...] = (ubuf[7:7 + tm, :] * cw[0:1, :] + u * cw[1:2, :] + ubuf[9:9 + tm, :] * cw[2:3, :]
                   + cbias_ref[...]).astype(BF16)
    ue_ref[...] = jnp.concatenate([u[0:8, :], u[tm - 8:tm, :]], axis=0).astype(BF16)
    v_ref[...] = proj(_V0, _CB0).astype(BF16)

    q = jnp.dot(h, wq_ref[...], preferred_element_type=F32)
    qms = jnp.dot((q * q).astype(BF16), bd_ref[...], preferred_element_type=F32)
    qg = jnp.concatenate([qg_ref[...]] * (ATTN_DIM // LANES), axis=1)
    qn = q * lax.rsqrt(qms + NORM_EPS) * qg
    q_ref[...] = (_rope_chunks(qn, c, s1, s2) * (HEAD_DIM ** -0.5 * LOG2E)).astype(BF16)

    k = proj(_K0, _V0)
    kms = jnp.dot((k * k).astype(BF16), bd_ref[0:KV_DIM, 0:KV_DIM], preferred_element_type=F32)
    kn = k * lax.rsqrt(kms + NORM_EPS) * kg_ref[...]
    k_ref[...] = _rope_chunks(kn, c, s1, s2).astype(BF16)


def _mix_kernel(sink_ref, x_ref, q_ref, kp_ref, kc_ref, kn_ref, vp_ref, vc_ref, vn_ref,
                cb_ref, cp_ref, uep_ref, uec_ref, uen_ref, ga_ref, gc_ref,
                cw_ref, wa_ref, wc_ref, wo_ref, g2_ref,
                wr_ref, br_ref,
                x1_ref, h2_ref, lg_ref,
                cscr, attn_scr, *, tq, tm):
    j = pl.program_id(1)
    nj = pl.num_programs(1)
    nsub = tq // WINDOW

    kbuf = jnp.concatenate([kp_ref[...], kc_ref[...], kn_ref[...]], axis=0)
    vbuf = jnp.concatenate([vp_ref[...], vc_ref[...], vn_ref[...]], axis=0)
    low_half = lax.broadcasted_iota(I32, (WINDOW, LANES), 1) < HEAD_DIM
    low_half3 = lax.broadcasted_iota(I32, (3 * WINDOW, LANES), 1) < HEAD_DIM
    qrow = lax.broadcasted_iota(I32, (WINDOW, WINDOW), 0)
    kcol = lax.broadcasted_iota(I32, (WINDOW, WINDOW), 1)
    bias_lo = jnp.where(kcol >= qrow, 0.0, MASK_VALUE)
    bias_hi = jnp.where(kcol <= qrow, 0.0, MASK_VALUE)
    one = jnp.ones((3 * WINDOW, LANES), BF16)
    def scores(n):
        qs = q_ref[n * WINDOW:(n + 1) * WINDOW, :]
        zero = jnp.zeros((WINDOW, LANES), BF16)
        q_stack = []
        for hd in range(N_HEADS):
            col = qs[:, (hd % 4) * LANES:(hd % 4 + 1) * LANES]
            keep = low_half if hd < 4 else jnp.logical_not(low_half)
            q_stack.append(jnp.where(keep, col, zero))
        q_stack = jnp.concatenate(q_stack, axis=0)
        kwin = kbuf[n * WINDOW:(n + 3) * WINDOW, :]
        return lax.dot_general(q_stack, kwin, (((1,), (1,)), ((), ())),
                               preferred_element_type=F32)

    s_next = scores(0)
    for n in range(nsub):
        blo = jnp.where(j > 0, bias_lo, MASK_VALUE) if n == 0 else bias_lo
        bhi = jnp.where(j < nj - 1, bias_hi, MASK_VALUE) if n == nsub - 1 else bias_hi
        s = s_next
        if n + 1 < nsub:
            s_next = scores(n + 1)
        vwin = vbuf[n * WINDOW:(n + 3) * WINDOW, :]
        p_list, es_list = [], []
        for hd in range(N_HEADS):
            sink = sink_ref[hd] * LOG2E
            rows = slice(hd * WINDOW, (hd + 1) * WINDOW)
            s0 = s[rows, 0:WINDOW] + blo
            s1 = s[rows, WINDOW:2 * WINDOW]
            s2 = s[rows, 2 * WINDOW:3 * WINDOW] + bhi
            m = jnp.max(jnp.maximum(jnp.maximum(s0, s1), s2), axis=-1, keepdims=True)
            m = jnp.maximum(m, sink)
            p_list.append(jnp.concatenate(
                [jnp.exp2(s0 - m).astype(BF16), jnp.exp2(s1 - m).astype(BF16),
                 jnp.exp2(s2 - m).astype(BF16)], axis=1))
            es_list.append(jnp.exp2(sink - m))
        half = N_HEADS // 2
        o_lo = jnp.dot(jnp.concatenate(p_list[:half], axis=0), jnp.where(low_half3, vwin, one),
                       preferred_element_type=F32)
        o_hi = jnp.dot(jnp.concatenate(p_list[half:], axis=0), jnp.where(low_half3, one, vwin),
                       preferred_element_type=F32)
        cols = []
        for c4 in range(half):
            rows = slice(c4 * WINDOW, (c4 + 1) * WINDOW)
            oa, ob = o_lo[rows, :], o_hi[rows, :]
            da = pltpu.roll(oa, HEAD_DIM, 1) + es_list[c4]
            db = pltpu.roll(ob, HEAD_DIM, 1) + es_list[c4 + half]
            cols.append(jnp.where(low_half, oa / da, ob / db))
        attn_scr[n * WINDOW:(n + 1) * WINDOW, :] = jnp.concatenate(cols, axis=1).astype(BF16)
    attn = attn_scr[...]

    cscr[...] = cp_ref[...].astype(F32)
    cw = cw_ref[...]
    uec = uec_ref[...].astype(F32)
    n_tiles = tq // tm
    for p in range(n_tiles):
        if p == 0:
            prev_last = uep_ref[...].astype(F32)[15:16, :] * (j > 0).astype(F32)
        else:
            prev_last = uec[16 * p - 1:16 * p, :]
        cscr[p * tm:p * tm + 1, :] = cscr[p * tm:p * tm + 1, :] + prev_last * cw[0:1, :]
        if p == n_tiles - 1:
            next_first = uen_ref[...].astype(F32)[0:1, :] * (j < nj - 1).astype(F32)
        else:
            next_first = uec[16 * (p + 1):16 * (p + 1) + 1, :]
        last = (p + 1) * tm - 1
        cscr[last:last + 1, :] = cscr[last:last + 1, :] + next_first * cw[2:3, :]
    cgate = (cb_ref[...].astype(F32) * cscr[...]).astype(BF16)

    conv_out = gc_ref[...].astype(F32) * jnp.dot(cgate, wc_ref[...], preferred_element_type=F32)
    merged = ga_ref[...].astype(F32) * jnp.dot(attn, wa_ref[...], preferred_element_type=F32) + conv_out
    x1 = x_ref[...] + jnp.dot(merged.astype(BF16), wo_ref[...], preferred_element_type=F32)
    x1_ref[...] = x1

    ms = jnp.mean(x1 * x1, axis=-1, keepdims=True)
    h2 = x1 * lax.rsqrt(ms + NORM_EPS) * g2_ref[...]
    _store_packed_rows(h2_ref, h2, tq)
    h_hi = h2.astype(BF16)
    h_lo = (h2 - h_hi.astype(F32)).astype(BF16)
    acc = jnp.dot(h_hi, wr_ref[...], preferred_element_type=F32)
    logits = (acc[:, 0:LANES] + acc[:, LANES:2 * LANES]
              + jnp.dot(h_lo, wr_ref[:, 0:LANES], preferred_element_type=F32) + br_ref[...])
    lg_ref[...] = logits


def _route_kernel(lg_ref, tri_ref, ri_ref, rf_ref, cnt_ref, base_ref, *, tr):
    @pl.when(pl.program_id(0) == 0)
    def _():
        base_ref[...] = jnp.zeros_like(base_ref)

    logits = lg_ref[...].T[0:ROUTER_ROWS, :]
    grow = lax.broadcasted_iota(I32, (8, tr), 0)
    gl = jnp.where(grow < N_GROUPS, logits[0:8, :], NEG_BIG)
    gmax = jnp.max(gl, axis=0, keepdims=True)
    grp = jnp.min(jnp.where(gl == gmax, grow, 8), axis=0, keepdims=True)
    p_grp = 1.0 / jnp.sum(jnp.exp(gl - gmax), axis=0, keepdims=True)
    el = logits[8:ROUTER_ROWS, :]
    erow = lax.broadcasted_iota(I32, (N_EXPERTS, tr), 0)
    in_grp = (erow // EXPERTS_PER_GROUP) == grp
    l_a = jnp.where(in_grp, el, NEG_BIG)
    l1 = jnp.max(l_a, axis=0, keepdims=True)
    i1 = jnp.min(jnp.where(l_a == l1, erow, N_EXPERTS), axis=0, keepdims=True)
    l_b = jnp.where(erow == i1, NEG_BIG, l_a)
    l2 = jnp.max(l_b, axis=0, keepdims=True)
    i2 = jnp.min(jnp.where(l_b == l2, erow, N_EXPERTS), axis=0, keepdims=True)
    r21 = jnp.exp(l2 - l1)
    t1 = 1.0 / (1.0 + r21)
    gate1 = p_grp * t1
    gate2 = p_grp * (r21 * t1)
    oh1 = (erow == i1).astype(F32)
    oh2 = (erow == i2).astype(F32)
    oh = oh1 + oh2
    base = base_ref[...][:, 0:1]
    r1, r2 = [], []
    for c in range(tr // RANK_CHUNK):
        cols = slice(c * RANK_CHUNK, (c + 1) * RANK_CHUNK)
        ohc = oh[:, cols]
        before = jnp.dot(ohc.astype(BF16), tri_ref[...], preferred_element_type=F32) + base
        r1.append(jnp.sum(oh1[:, cols] * before, axis=0, keepdims=True))
        r2.append(jnp.sum(oh2[:, cols] * before, axis=0, keepdims=True))
        base = base + jnp.sum(ohc, axis=1, keepdims=True)
    rank1 = jnp.concatenate(r1, axis=1)
    rank2 = jnp.concatenate(r2, axis=1)
    new_base = jnp.broadcast_to(base, base_ref.shape)
    base_ref[...] = new_base
    cnt_ref[...] = new_base
    zi = jnp.zeros((4, tr), I32)
    ri_ref[...] = jnp.concatenate([i1, i2, rank1.astype(I32), rank2.astype(I32), zi], axis=0)
    rf_ref[...] = jnp.concatenate([gate1, gate2, jnp.zeros((LANES - 2, tr), F32)], axis=0).T


def _moe_kernel(be_ref, first_ref, wslot_ref, nxt_ref, nused_ref, x_hbm, wg_hbm, wu_hbm, wd_hbm, y_hbm,
                wg_s, wu_s, wd_s, xbuf, xsem, wst_g, wst_u, wst_d, wsem, ybuf, ysem):
    i = pl.program_id(0)
    nused = nused_ref[0]
    blk_rows = MOE_ROWS * TOK_SUBLANES

    def x_copy(blk):
        slot = lax.rem(blk, MOE_X_SLOTS)
        return pltpu.make_async_copy(
            x_hbm.at[pl.ds(pl.multiple_of(blk * blk_rows, blk_rows), blk_rows)],
            xbuf.at[pl.ds(pl.multiple_of(slot * blk_rows, blk_rows), blk_rows)],
            xsem.at[slot])

    @pl.when(i == 0)
    def _():
        x_copy(0).start()

        @pl.when(nused > 1)
        def _():
            x_copy(1).start()

    @pl.when(i + 2 < nused)
    def _():
        x_copy(i + 2).start()

    def w_copies(e, slot):
        return (pltpu.make_async_copy(wg_hbm.at[e], wst_g.at[pl.ds(slot * D_MODEL, D_MODEL)], wsem.at[slot]),
                pltpu.make_async_copy(wu_hbm.at[e], wst_u.at[pl.ds(slot * D_MODEL, D_MODEL)], wsem.at[slot]),
                pltpu.make_async_copy(wd_hbm.at[e], wst_d.at[pl.ds(slot * D_FF, D_FF)], wsem.at[slot]))

    @pl.when(i == 0)
    def _():
        for cp in w_copies(be_ref[0], 0):
            cp.start()

    def new_run(_, carry):
        slot = wslot_ref[i]
        for cp in w_copies(be_ref[i], slot):
            cp.wait()
        wg_s[...] = wst_g[pl.ds(pl.multiple_of(slot * D_MODEL, D_MODEL), D_MODEL), :].astype(BF16)
        wu_s[...] = wst_u[pl.ds(pl.multiple_of(slot * D_MODEL, D_MODEL), D_MODEL), :].astype(BF16)
        wd_s[...] = wst_d[pl.ds(pl.multiple_of(slot * D_FF, D_FF), D_FF), :].astype(BF16)

        @pl.when(nxt_ref[i] >= 0)
        def _():
            for cp in w_copies(nxt_ref[i], 1 - slot):
                cp.start()
        return carry

    lax.fori_loop(0, first_ref[i], new_run, 0)

    def y_copy(blk):
        slot = lax.rem(blk, 2)
        return pltpu.make_async_copy(
            ybuf.at[pl.ds(pl.multiple_of(slot * blk_rows, blk_rows), blk_rows)],
            y_hbm.at[pl.ds(pl.multiple_of(blk * blk_rows, blk_rows), blk_rows)],
            ysem.at[slot])

    @pl.when(i < nused)
    def _():
        x_copy(i).wait()
        row0 = pl.multiple_of(lax.rem(i, MOE_X_SLOTS) * blk_rows, blk_rows)
        xb = jnp.concatenate([c.astype(BF16) for c in _load_packed_rows(xbuf, MOE_ROWS, row0)], axis=1)
        g = jnp.dot(xb, wg_s[...], preferred_element_type=F32)
        u = jnp.dot(xb, wu_s[...], preferred_element_type=F32)
        hid = (g * jax.nn.sigmoid(g) * u).astype(BF16)
        y = jnp.dot(hid, wd_s[...], preferred_element_type=F32)

        @pl.when(i >= 2)
        def _():
            y_copy(i - 2).wait()

        _store_packed_rows(ybuf, y, MOE_ROWS, pl.multiple_of(lax.rem(i, 2) * blk_rows, blk_rows))
        y_copy(i).start(priority=1)

    @pl.when(i == pl.num_programs(0) - 1)
    def _():
        @pl.when(nused >= 2)
        def _():
            y_copy(nused - 2).wait()

        y_copy(nused - 1).wait()


def _combine_kernel(x1_ref, gate_ref, y0_ref, y1_ref, o_ref, *, tm):
    g = gate_ref[...]
    g0, g1 = g[:, 0:1], g[:, 1:2]
    y0 = _load_packed_rows(y0_ref, tm)
    y1 = _load_packed_rows(y1_ref, tm)
    for c in range(2 * TOK_SUBLANES):
        cols = slice(c * LANES, (c + 1) * LANES)
        o_ref[:, cols] = x1_ref[:, cols] + g0 * y0[c] + g1 * y1[c]


def _sc_windows(n_tokens):
    info = plsc.get_sparse_core_info()
    n_workers = info.num_cores * info.num_subcores
    assert n_tokens % (n_workers * SC_WINDOW * 2) == 0
    return info.num_cores, n_tokens // (n_workers * SC_WINDOW)


def _sc_dispatch(h3, d0, d1, n_rows):
    t = h3.shape[0]
    n_cores, nwin = _sc_windows(t)
    mesh = plsc.VectorSubcoreMesh(core_axis_name="c", subcore_axis_name="s")
    rows = pltpu.VMEM((SC_WINDOW,) + h3.shape[1:], h3.dtype)
    idx = pltpu.VMEM((nwin, SC_WINDOW), I32)

    @functools.partial(
        pl.kernel, mesh=mesh,
        out_type=jax.ShapeDtypeStruct((n_rows,) + h3.shape[1:], h3.dtype),
        scratch_types=[idx, idx, idx, rows, rows] + [pltpu.SemaphoreType.DMA] * 3,
        name="sc_dispatch")
    def run(h_hbm, d0_hbm, d1_hbm, it_hbm, o_hbm, i0, i1, it, buf_a, buf_b, sem_a, sem_b, sem_s):
        wbase = (lax.axis_index("s") * n_cores + lax.axis_index("c")) * nwin
        pltpu.sync_copy(d0_hbm.at[pl.ds(wbase, nwin)], i0)
        pltpu.sync_copy(d1_hbm.at[pl.ds(wbase, nwin)], i1)
        pltpu.sync_copy(it_hbm.at[pl.ds(wbase, nwin)], it)

        @pl.loop(0, nwin, step=2)
        def _(w):
            ga = pltpu.async_copy(h_hbm.at[it.at[w]], buf_a, sem_a)
            gb = pltpu.async_copy(h_hbm.at[it.at[w + 1]], buf_b, sem_b)
            ga.wait()
            s0 = pltpu.async_copy(buf_a, o_hbm.at[i0.at[w]], sem_s)
            s1 = pltpu.async_copy(buf_a, o_hbm.at[i1.at[w]], sem_s)
            gb.wait()
            s2 = pltpu.async_copy(buf_b, o_hbm.at[i0.at[w + 1]], sem_s)
            s3 = pltpu.async_copy(buf_b, o_hbm.at[i1.at[w + 1]], sem_s)
            s0.wait()
            s1.wait()
            s2.wait()
            s3.wait()

    win = lambda a: a.reshape(-1, SC_WINDOW)
    return run(h3, win(d0), win(d1), win(jnp.arange(t, dtype=I32)))


def _sc_gather(y3, d0, d1):
    t = d0.shape[0]
    n_cores, nwin = _sc_windows(t)
    mesh = plsc.VectorSubcoreMesh(core_axis_name="c", subcore_axis_name="s")
    rows = pltpu.VMEM((SC_WINDOW,) + y3.shape[1:], y3.dtype)
    idx = pltpu.VMEM((nwin, SC_WINDOW), I32)

    @functools.partial(
        pl.kernel, mesh=mesh,
        out_type=jax.ShapeDtypeStruct((2 * t,) + y3.shape[1:], y3.dtype),
        scratch_types=[idx] * 4 + [rows] * 4 + [pltpu.SemaphoreType.DMA] * 5,
        name="sc_gather")
    def run(y_hbm, d0_hbm, d1_hbm, ita_hbm, itb_hbm, o_hbm, i0, i1, ita, itb,
            b0, b1, b2, b3, m0, m1, m2, m3, ms):
        wbase = (lax.axis_index("s") * n_cores + lax.axis_index("c")) * nwin
        pltpu.sync_copy(d0_hbm.at[pl.ds(wbase, nwin)], i0)
        pltpu.sync_copy(d1_hbm.at[pl.ds(wbase, nwin)], i1)
        pltpu.sync_copy(ita_hbm.at[pl.ds(wbase, nwin)], ita)
        pltpu.sync_copy(itb_hbm.at[pl.ds(wbase, nwin)], itb)

        @pl.loop(0, nwin, step=2)
        def _(w):
            g0 = pltpu.async_copy(y_hbm.at[i0.at[w]], b0, m0)
            g1 = pltpu.async_copy(y_hbm.at[i1.at[w]], b1, m1)
            g2 = pltpu.async_copy(y_hbm.at[i0.at[w + 1]], b2, m2)
            g3 = pltpu.async_copy(y_hbm.at[i1.at[w + 1]], b3, m3)
            g0.wait()
            s0 = pltpu.async_copy(b0, o_hbm.at[ita.at[w]], ms)
            g1.wait()
            s1 = pltpu.async_copy(b1, o_hbm.at[itb.at[w]], ms)
            g2.wait()
            s2 = pltpu.async_copy(b2, o_hbm.at[ita.at[w + 1]], ms)
            g3.wait()
            s3 = pltpu.async_copy(b3, o_hbm.at[itb.at[w + 1]], ms)
            s0.wait()
            s1.wait()
            s2.wait()
            s3.wait()

    win = lambda a: a.reshape(-1, SC_WINDOW)
    iota = jnp.arange(2 * t, dtype=I32)
    return run(y3, win(d0), win(d1), win(iota[:t]), win(iota[t:]))


def _rope_lane_tables(seq_len):
    half = ROT_DIM // 2
    inv_freq = ROPE_THETA ** (-np.arange(0, ROT_DIM, 2, dtype=np.float64) / ROT_DIM)
    ang = np.arange(seq_len, dtype=np.float64)[:, None] * inv_freq[None, :]
    cos, sin = np.cos(ang), np.sin(ang)
    m = np.arange(LANES) % HEAD_DIM
    f = m % half
    c = np.where(m[None, :] < ROT_DIM, cos[:, f], 1.0)
    s1 = np.where(m[None, :] < half, -sin[:, f], 0.0)
    s2 = np.where((m[None, :] >= half) & (m[None, :] < ROT_DIM), sin[:, f], 0.0)
    return tuple(jnp.asarray(a, F32) for a in (c, s1, s2))


def _layer(x, norm1_g, w_in, q_norm_g, k_norm_g, attn_sink, conv_w, conv_b, w_attn_proj,
           w_conv_proj, w_out, norm2_g, w_rg, b_rg, w_re, b_re, w_gate, w_up, w_down):
    bsz, seq, d = x.shape
    t = bsz * seq
    tm = min(512, seq)
    tq = min(1024, seq)
    assert d == D_MODEL and seq % tq == 0 and seq % tm == 0 and tq % tm == 0 and tq % WINDOW == 0
    xf = x.reshape(t, d)

    hpg = N_HEADS // N_KV_HEADS
    wq_p = (w_in[:, :ATTN_DIM].reshape(d, N_KV_HEADS, hpg, HEAD_DIM).transpose(0, 2, 1, 3)
            .reshape(d, ATTN_DIM))
    wq_p = wq_p.astype(BF16)
    w_in_b = w_in.astype(BF16)
    wa_p = (w_attn_proj.reshape(N_KV_HEADS, hpg, HEAD_DIM, d).transpose(1, 0, 2, 3)
            .reshape(ATTN_DIM, d).astype(BF16))
    bd = jnp.asarray(np.kron(np.eye(N_HEADS), np.full((HEAD_DIM, HEAD_DIM), 1.0 / HEAD_DIM)), BF16)
    qg = jnp.tile(q_norm_g, LANES // HEAD_DIM).reshape(1, LANES)
    kg = jnp.tile(k_norm_g, LANES // HEAD_DIM).reshape(1, LANES)
    ctab, s1tab, s2tab = _rope_lane_tables(seq)
    nseq_t = seq // tm

    row = lambda w: pl.BlockSpec((tm, w), lambda i: (i, 0))
    full = lambda a: pl.BlockSpec(a.shape, lambda i: (0,) * a.ndim)
    tab = pl.BlockSpec((tm, LANES), lambda i: (i % nseq_t, 0))
    cbias = conv_b.reshape(1, CONV_DIM)
    q, k, v, cb, cp, ue, ga, gc = pl.pallas_call(
        _proj_kernel,
        grid=(t // tm,),
        in_specs=[row(d), full(norm1_g.reshape(1, d)), full(w_in_b), full(wq_p), full(bd), full(qg), full(kg),
                  tab, tab, tab, full(conv_w), full(cbias)],
        out_specs=[row(ATTN_DIM), row(KV_DIM), row(KV_DIM), row(CONV_DIM), row(CONV_DIM),
                   pl.BlockSpec((16, CONV_DIM), lambda i: (i, 0)), row(D_MODEL), row(D_MODEL)],
        out_shape=[jax.ShapeDtypeStruct((t, w), BF16) for w in (ATTN_DIM, KV_DIM, KV_DIM, CONV_DIM, CONV_DIM)]
                  + [jax.ShapeDtypeStruct((t // tm * 16, CONV_DIM), BF16),
                     jax.ShapeDtypeStruct((t, D_MODEL), BF16), jax.ShapeDtypeStruct((t, D_MODEL), BF16)],
        scratch_shapes=[pltpu.VMEM((tm + 16, CONV_DIM), F32)],
        compiler_params=pltpu.CompilerParams(dimension_semantics=("arbitrary",),
                                             vmem_limit_bytes=VMEM_LIMIT),
        name="proj",
    )(xf, norm1_g.reshape(1, d), w_in_b, wq_p, bd, qg, kg, ctab, s1tab, s2tab, conv_w, cbias)

    wr = jnp.concatenate([w_rg, jnp.zeros((d, 8 - N_GROUPS), F32), w_re,
                          jnp.zeros((d, LANES - ROUTER_ROWS), F32)], axis=1)
    wr_hi = wr.astype(BF16)
    wr2 = jnp.concatenate([wr_hi, (wr - wr_hi.astype(F32)).astype(BF16)], axis=1)
    br = jnp.concatenate([b_rg, jnp.zeros((8 - N_GROUPS,), F32), b_re,
                          jnp.zeros((LANES - ROUTER_ROWS,), F32)]).reshape(1, LANES)
    mix_params = (conv_w, wa_p, w_conv_proj.astype(BF16), w_out.astype(BF16),
                  norm2_g.reshape(1, d), wr2, br)

    out = _mixers_and_moe(xf, (q, k, v, cb, cp, ue, ga, gc), attn_sink, mix_params,
                          (w_gate, w_up, w_down), bsz, seq, tq, tm)
    return out.reshape(bsz, seq, d)


def _mixers_and_moe(xf, acts, attn_sink, mix_params, expert_w, bsz, seq, tq, tm):
    q, k, v, cb, cp, ue, ga, gc = acts
    d = D_MODEL
    t = bsz * seq
    nj = seq // tq
    wpb = tq // WINDOW
    nb_seq = seq // WINDOW
    tpt = tq // tm
    n_edge = t // tm
    row_in = lambda w: pl.BlockSpec((tq, w), lambda b, j: (b * nj + j, 0))
    row_out = row_in
    prev_blk = pl.BlockSpec((WINDOW, KV_DIM),
                            lambda b, j: (b * nb_seq + jnp.maximum(j * wpb - 1, 0), 0))
    next_blk = pl.BlockSpec((WINDOW, KV_DIM),
                            lambda b, j: (b * nb_seq + jnp.minimum((j + 1) * wpb, nb_seq - 1), 0))
    edge_prev = pl.BlockSpec((16, CONV_DIM),
                             lambda b, j: (jnp.maximum((b * nj + j) * tpt - 1, 0), 0))
    edge_cur = pl.BlockSpec((16 * tpt, CONV_DIM), lambda b, j: (b * nj + j, 0))
    edge_next = pl.BlockSpec((16, CONV_DIM),
                             lambda b, j: (jnp.minimum((b * nj + j + 1) * tpt, n_edge - 1), 0))
    full2 = lambda a: pl.BlockSpec(a.shape, lambda b, j: (0,) * a.ndim)

    x1, h2, logits = pl.pallas_call(
        functools.partial(_mix_kernel, tq=tq, tm=tm),
        grid=(bsz, nj),
        in_specs=[pl.BlockSpec(memory_space=pltpu.SMEM),
                  row_in(d), row_in(ATTN_DIM), prev_blk, row_in(KV_DIM), next_blk,
                  prev_blk, row_in(KV_DIM), next_blk,
                  row_in(CONV_DIM), row_in(CONV_DIM), edge_prev, edge_cur, edge_next,
                  row_in(D_MODEL), row_in(D_MODEL)]
                 + [full2(p) for p in mix_params],
        out_specs=[row_out(d), pl.BlockSpec((tq * TOK_SUBLANES, LANES), lambda b, j: (b * nj + j, 0)),
                   row_out(LANES)],
        out_shape=[jax.ShapeDtypeStruct((t, d), F32),
                   jax.ShapeDtypeStruct((t * TOK_SUBLANES, LANES), U32),
                   jax.ShapeDtypeStruct((t, LANES), F32)],
        scratch_shapes=[pltpu.VMEM((tq, CONV_DIM), F32), pltpu.VMEM((tq, ATTN_DIM), BF16)],
        compiler_params=pltpu.CompilerParams(dimension_semantics=("arbitrary", "arbitrary"),
                                             vmem_limit_bytes=VMEM_LIMIT),
        name="mix",
    )(attn_sink, xf, q, k, k, k, v, v, v, cb, cp, ue, ue, ue, ga, gc, *mix_params)

    tr = min(ROUTE_TILE, t)
    assert t % tr == 0 and tr % RANK_CHUNK == 0
    tri = jnp.asarray(np.triu(np.ones((RANK_CHUNK, RANK_CHUNK), np.float32), 1), BF16)
    colr = lambda r: pl.BlockSpec((r, tr), lambda i: (0, i))
    route_i, route_f, counts = pl.pallas_call(
        functools.partial(_route_kernel, tr=tr),
        grid=(t // tr,),
        in_specs=[pl.BlockSpec((tr, LANES), lambda i: (i, 0)), pl.BlockSpec(tri.shape, lambda i: (0, 0))],
        out_specs=[colr(8), pl.BlockSpec((tr, LANES), lambda i: (i, 0)),
                   pl.BlockSpec((N_EXPERTS, LANES), lambda i: (0, 0))],
        out_shape=[jax.ShapeDtypeStruct((8, t), I32), jax.ShapeDtypeStruct((t, LANES), F32),
                   jax.ShapeDtypeStruct((N_EXPERTS, LANES), F32)],
        scratch_shapes=[pltpu.VMEM((N_EXPERTS, LANES), F32)],
        compiler_params=pltpu.CompilerParams(dimension_semantics=("arbitrary",),
                                             vmem_limit_bytes=VMEM_LIMIT),
        name="route",
    )(logits, tri)

    n_rows = 2 * t + N_EXPERTS * MOE_ROWS
    nblk = n_rows // MOE_ROWS
    cnt = counts[:, 0].astype(I32)
    padded = ((cnt + MOE_ROWS - 1) // MOE_ROWS) * MOE_ROWS
    pad_ends = jnp.cumsum(padded)
    pad_starts = pad_ends - padded
    eids = jnp.arange(N_EXPERTS, dtype=I32)[:, None, None]
    dest = jnp.sum(jnp.where(route_i[None, 0:2] == eids, pad_starts[:, None, None], 0), axis=0) \
        + route_i[2:4]
    blk_start = jnp.arange(nblk, dtype=I32) * MOE_ROWS
    block_e = jnp.minimum(jnp.sum((pad_ends[None, :] <= blk_start[:, None]).astype(I32), axis=1),
                          N_EXPERTS - 1)
    nused = (pad_ends[-1:] // MOE_ROWS).astype(I32)

    x_pad = _sc_dispatch(h2.reshape(t, TOK_SUBLANES, LANES), dest[0], dest[1], n_rows)
    x_pad = x_pad.reshape(n_rows * TOK_SUBLANES, LANES)

    blk = jnp.arange(nblk, dtype=I32)
    first = (blk < nused[0]) & (block_e != jnp.concatenate([jnp.full((1,), -1, I32), block_e[:-1]]))
    wslot = ((jnp.cumsum(first.astype(I32)) - 1) % 2).astype(I32)
    later_first = first[None, :] & (blk[None, :] > blk[:, None])
    nxt = jnp.where(jnp.any(later_first, axis=1), block_e[jnp.argmax(later_first, axis=1)], -1).astype(I32)

    w_gate, w_up, w_down = expert_w
    blk_rows = MOE_ROWS * TOK_SUBLANES
    hbm = pl.BlockSpec(memory_space=pl.ANY)
    y_pad = pl.pallas_call(
        _moe_kernel,
        grid_spec=pltpu.PrefetchScalarGridSpec(
            num_scalar_prefetch=5,
            grid=(nblk,),
            in_specs=[hbm, hbm, hbm, hbm],
            out_specs=hbm,
            scratch_shapes=[pltpu.VMEM((d, D_FF), BF16), pltpu.VMEM((d, D_FF), BF16),
                            pltpu.VMEM((D_FF, d), BF16),
                            pltpu.VMEM((MOE_X_SLOTS * blk_rows, LANES), U32),
                            pltpu.SemaphoreType.DMA((MOE_X_SLOTS,)),
                            pltpu.VMEM((2 * d, D_FF), F32), pltpu.VMEM((2 * d, D_FF), F32),
                            pltpu.VMEM((2 * D_FF, d), F32), pltpu.SemaphoreType.DMA((2,)),
                            pltpu.VMEM((2 * blk_rows, LANES), U32), pltpu.SemaphoreType.DMA((2,))]),
        out_shape=jax.ShapeDtypeStruct((n_rows * TOK_SUBLANES, LANES), U32),
        compiler_params=pltpu.CompilerParams(dimension_semantics=("arbitrary",),
                                             vmem_limit_bytes=VMEM_LIMIT),
        name="moe",
    )(block_e, first.astype(I32), wslot, nxt, nused, x_pad, w_gate, w_up, w_down)

    yk = _sc_gather(y_pad.reshape(n_rows, TOK_SUBLANES, LANES), dest[0], dest[1])
    yk = yk.reshape(2 * t * TOK_SUBLANES, LANES)
    tc = min(512, t)
    nt = t // tc
    return pl.pallas_call(
        functools.partial(_combine_kernel, tm=tc),
        grid=(nt,),
        in_specs=[pl.BlockSpec((tc, d), lambda i: (i, 0)),
                  pl.BlockSpec((tc, LANES), lambda i: (i, 0)),
                  pl.BlockSpec((tc * TOK_SUBLANES, LANES), lambda i: (i, 0)),
                  pl.BlockSpec((tc * TOK_SUBLANES, LANES), lambda i: (nt + i, 0))],
        out_specs=pl.BlockSpec((tc, d), lambda i: (i, 0)),
        out_shape=jax.ShapeDtypeStruct((t, d), F32),
        compiler_params=pltpu.CompilerParams(dimension_semantics=("arbitrary",),
                                             vmem_limit_bytes=VMEM_LIMIT),
        name="combine",
    )(x1, route_f, yk, yk)


def kernel(x, norm1_g, w_in, q_norm_g, k_norm_g, attn_sink, conv_w, conv_b, w_attn_proj, w_conv_proj, w_out, norm2_g, w_router_group, b_router_group, w_router_expert, b_router_expert, w_gate_e, w_up_e, w_down_e):
    for l in range(norm1_g.shape[0]):
        x = _layer(x, norm1_g[l], w_in[l], q_norm_g[l], k_norm_g[l], attn_sink[l], conv_w[l],
                   conv_b[l], w_attn_proj[l], w_conv_proj[l], w_out[l], norm2_g[l],
                   w_router_group[l], b_router_group[l], w_router_expert[l], b_router_expert[l],
                   w_gate_e[l], w_up_e[l], w_down_e[l])
    return x
```

```python
import functools

import jax
import jax.numpy as jnp
import numpy as np
from jax import lax
from jax.experimental import pallas as pl
from jax.experimental.pallas import tpu as pltpu
from jax.experimental.pallas import tpu_sc as plsc

F32 = jnp.float32
BF16 = jnp.bfloat16
I32 = jnp.int32
U32 = jnp.uint32
HI_MASK = np.uint32(0xFFFF0000)

D_MODEL = 1024
N_HEADS = 8
N_KV_HEADS = 2
HEAD_DIM = 64
ATTN_DIM = N_HEADS * HEAD_DIM
KV_DIM = N_KV_HEADS * HEAD_DIM
WINDOW = 128
ROT_DIM = HEAD_DIM // 4
ROPE_THETA = 500000.0
CONV_DIM = D_MODEL // 2
N_GROUPS = 4
EXPERTS_PER_GROUP = 8
N_EXPERTS = N_GROUPS * EXPERTS_PER_GROUP
D_FF = D_MODEL // 4
NORM_EPS = 1e-6
MASK_VALUE = -1e30
NEG_BIG = -3.0e38
LOG2E = 1.4426950408889634

LANES = 128
TOK_SUBLANES = D_MODEL // (2 * LANES)
HALF_D = D_MODEL // 2
SC_WINDOW = 32
ROUTER_ROWS = 8 + N_EXPERTS
ROUTE_TILE = 4096
RANK_CHUNK = 256
MOE_ROWS = 512
MOE_X_SLOTS = 3
VMEM_LIMIT = 56 * 1024 * 1024

_Q0, _K0, _V0, _CB0, _CC0, _CX0, _GA0, _GC0, _END = (
    0, 512, 640, 768, 1280, 1792, 2304, 3328, 4352)


def _store_packed_rows(ref, val, n):
    for c in range(TOK_SUBLANES):
        lo = val[:, c * LANES:(c + 1) * LANES].astype(BF16).astype(F32)
        hi = val[:, HALF_D + c * LANES:HALF_D + (c + 1) * LANES].astype(BF16).astype(F32)
        word = (lax.bitcast_convert_type(lo, U32) >> 16) | (lax.bitcast_convert_type(hi, U32) & HI_MASK)
        ref[pl.ds(c, n, stride=TOK_SUBLANES), :] = word


def _load_packed_rows(ref, n, row0=0):
    words = [ref[pl.ds(row0 + c, n, stride=TOK_SUBLANES), :] for c in range(TOK_SUBLANES)]
    lo = [lax.bitcast_convert_type(w << 16, F32) for w in words]
    hi = [lax.bitcast_convert_type(w & HI_MASK, F32) for w in words]
    return lo + hi


def _rope_chunks(t, c, s1, s2):
    outs = []
    for j in range(t.shape[1] // LANES):
        tj = t[:, j * LANES:(j + 1) * LANES]
        outs.append(tj * c + pltpu.roll(tj, LANES - ROT_DIM // 2, 1) * s1
                    + pltpu.roll(tj, ROT_DIM // 2, 1) * s2)
    return outs[0] if len(outs) == 1 else jnp.concatenate(outs, axis=1)


def _proj_kernel(x_ref, g1_ref, w_ref, wq_ref, bd_ref, qg_ref, kg_ref, c_ref, s1_ref, s2_ref, cw_ref, cbias_ref,
                 q_ref, k_ref, v_ref, cb_ref, cp_ref, ue_ref, ga_ref, gc_ref, ubuf):
    x = x_ref[...]
    ms = jnp.mean(x * x, axis=-1, keepdims=True)
    h = (x * lax.rsqrt(ms + NORM_EPS) * g1_ref[...]).astype(BF16)

    def proj(a, b):
        return jnp.dot(h, w_ref[:, a:b], preferred_element_type=F32)

    c, s1, s2 = c_ref[...], s1_ref[...], s2_ref[...]

    ga_ref[...] = jax.nn.sigmoid(proj(_GA0, _GC0)).astype(BF16)
    gc_ref[...] = jax.nn.sigmoid(proj(_GC0, _END)).astype(BF16)
    cb_ref[...] = proj(_CB0, _CC0).astype(BF16)
    tm = x.shape[0]
    u = proj(_CC0, _CX0) * proj(_CX0, _GA0)
    ubuf[0:8, :] = jnp.zeros((8, CONV_DIM), F32)
    ubuf[8 + tm:16 + tm, :] = jnp.zeros((8, CONV_DIM), F32)
    ubuf[8:8 + tm, :] = u
    cw = cw_ref[...]
    cp_ref[...] = (ubuf[7:7 + tm, :] * cw[0:1, :] + u * cw[1:2, :] + ubuf[9:9 + tm, :] * cw[2:3, :]
                   + cbias_ref[...]).astype(BF16)
    ue_ref[...] = jnp.concatenate([u[0:8, :], u[tm - 8:tm, :]], axis=0).astype(BF16)
    v_ref[...] = proj(_V0, _CB0).astype(BF16)

    q = jnp.dot(h, wq_ref[...], preferred_element_type=F32)
    qms = jnp.dot((q * q).astype(BF16), bd_ref[...], preferred_element_type=F32)
    qg = jnp.concatenate([qg_ref[...]] * (ATTN_DIM // LANES), axis=1)
    qn = q * lax.rsqrt(qms + NORM_EPS) * qg
    q_ref[...] = (_rope_chunks(qn, c, s1, s2) * (HEAD_DIM ** -0.5 * LOG2E)).astype(BF16)

    k = proj(_K0, _V0)
    kms = jnp.dot((k * k).astype(BF16), bd_ref[0:KV_DIM, 0:KV_DIM], preferred_element_type=F32)
    kn = k * lax.rsqrt(kms + NORM_EPS) * kg_ref[...]
    k_ref[...] = _rope_chunks(kn, c, s1, s2).astype(BF16)


def _mix_kernel(sink_ref, x_ref, q_ref, kp_ref, kc_ref, kn_ref, vp_ref, vc_ref, vn_ref,
                cb_ref, cp_ref, uep_ref, uec_ref, uen_ref, ga_ref, gc_ref,
                cw_ref, wa_ref, wc_ref, wo_ref, g2_ref,
                wr_ref, br_ref,
                x1_ref, h2_ref, lg_ref,
                cscr, attn_scr, *, tq, tm):
    j = pl.program_id(1)
    nj = pl.num_programs(1)
    nsub = tq // WINDOW

    kbuf = jnp.concatenate([kp_ref[...], kc_ref[...], kn_ref[...]], axis=0)
    vbuf = jnp.concatenate([vp_ref[...], vc_ref[...], vn_ref[...]], axis=0)
    low_half = lax.broadcasted_iota(I32, (WINDOW, LANES), 1) < HEAD_DIM
    low_half3 = lax.broadcasted_iota(I32, (3 * WINDOW, LANES), 1) < HEAD_DIM
    qrow = lax.broadcasted_iota(I32, (WINDOW, WINDOW), 0)
    kcol = lax.broadcasted_iota(I32, (WINDOW, WINDOW), 1)
    bias_lo = jnp.where(kcol >= qrow, 0.0, MASK_VALUE)
    bias_hi = jnp.where(kcol <= qrow, 0.0, MASK_VALUE)
    one = jnp.ones((3 * WINDOW, LANES), BF16)
    def scores(n):
        qs = q_ref[n * WINDOW:(n + 1) * WINDOW, :]
        zero = jnp.zeros((WINDOW, LANES), BF16)
        q_stack = []
        for hd in range(N_HEADS):
            col = qs[:, (hd % 4) * LANES:(hd % 4 + 1) * LANES]
            keep = low_half if hd < 4 else jnp.logical_not(low_half)
            q_stack.append(jnp.where(keep, col, zero))
        q_stack = jnp.concatenate(q_stack, axis=0)
        kwin = kbuf[n * WINDOW:(n + 3) * WINDOW, :]
        return lax.dot_general(q_stack, kwin, (((1,), (1,)), ((), ())),
                               preferred_element_type=F32)

    s_next = scores(0)
    for n in range(nsub):
        blo = jnp.where(j > 0, bias_lo, MASK_VALUE) if n == 0 else bias_lo
        bhi = jnp.where(j < nj - 1, bias_hi, MASK_VALUE) if n == nsub - 1 else bias_hi
        s = s_next
        if n + 1 < nsub:
            s_next = scores(n + 1)
        vwin = vbuf[n * WINDOW:(n + 3) * WINDOW, :]
        p_list, es_list = [], []
        for hd in range(N_HEADS):
            sink = sink_ref[hd] * LOG2E
            rows = slice(hd * WINDOW, (hd + 1) * WINDOW)
            s0 = s[rows, 0:WINDOW] + blo
            s1 = s[rows, WINDOW:2 * WINDOW]
            s2 = s[rows, 2 * WINDOW:3 * WINDOW] + bhi
            m = jnp.max(jnp.maximum(jnp.maximum(s0, s1), s2), axis=-1, keepdims=True)
            m = jnp.maximum(m, sink)
            p_list.append(jnp.concatenate(
                [jnp.exp2(s0 - m).astype(BF16), jnp.exp2(s1 - m).astype(BF16),
                 jnp.exp2(s2 - m).astype(BF16)], axis=1))
            es_list.append(jnp.exp2(sink - m))
        half = N_HEADS // 2
        o_lo = jnp.dot(jnp.concatenate(p_list[:half], axis=0), jnp.where(low_half3, vwin, one),
                       preferred_element_type=F32)
        o_hi = jnp.dot(jnp.concatenate(p_list[half:], axis=0), jnp.where(low_half3, one, vwin),
                       preferred_element_type=F32)
        cols = []
        for c4 in range(half):
            rows = slice(c4 * WINDOW, (c4 + 1) * WINDOW)
            oa, ob = o_lo[rows, :], o_hi[rows, :]
            da = pltpu.roll(oa, HEAD_DIM, 1) + es_list[c4]
            db = pltpu.roll(ob, HEAD_DIM, 1) + es_list[c4 + half]
            cols.append(jnp.where(low_half, oa / da, ob / db))
        attn_scr[n * WINDOW:(n + 1) * WINDOW, :] = jnp.concatenate(cols, axis=1).astype(BF16)
    attn = attn_scr[...]

    cscr[...] = cp_ref[...].astype(F32)
    cw = cw_ref[...]
    uec = uec_ref[...].astype(F32)
    n_tiles = tq // tm
    for p in range(n_tiles):
        if p == 0:
            prev_last = uep_ref[...].astype(F32)[15:16, :] * (j > 0).astype(F32)
        else:
            prev_last = uec[16 * p - 1:16 * p, :]
        cscr[p * tm:p * tm + 1, :] = cscr[p * tm:p * tm + 1, :] + prev_last * cw[0:1, :]
        if p == n_tiles - 1:
            next_first = uen_ref[...].astype(F32)[0:1, :] * (j < nj - 1).astype(F32)
        else:
            next_first = uec[16 * (p + 1):16 * (p + 1) + 1, :]
        last = (p + 1) * tm - 1
        cscr[last:last + 1, :] = cscr[last:last + 1, :] + next_first * cw[2:3, :]
    cgate = (cb_ref[...].astype(F32) * cscr[...]).astype(BF16)

    conv_out = gc_ref[...].astype(F32) * jnp.dot(cgate, wc_ref[...], preferred_element_type=F32)
    merged = ga_ref[...].astype(F32) * jnp.dot(attn, wa_ref[...], preferred_element_type=F32) + conv_out
    x1 = x_ref[...] + jnp.dot(merged.astype(BF16), wo_ref[...], preferred_element_type=F32)
    x1_ref[...] = x1

    ms = jnp.mean(x1 * x1, axis=-1, keepdims=True)
    h2 = x1 * lax.rsqrt(ms + NORM_EPS) * g2_ref[...]
    _store_packed_rows(h2_ref, h2, tq)
    h_hi = h2.astype(BF16)
    h_lo = (h2 - h_hi.astype(F32)).astype(BF16)
    acc = jnp.dot(h_hi, wr_ref[...], preferred_element_type=F32)
    logits = (acc[:, 0:LANES] + acc[:, LANES:2 * LANES]
              + jnp.dot(h_lo, wr_ref[:, 0:LANES], preferred_element_type=F32) + br_ref[...])
    lg_ref[...] = logits


def _route_kernel(lg_ref, tri_ref, ri_ref, rf_ref, cnt_ref, base_ref, *, tr):
    @pl.when(pl.program_id(0) == 0)
    def _():
        base_ref[...] = jnp.zeros_like(base_ref)

    logits = lg_ref[...].T[0:ROUTER_ROWS, :]
    grow = lax.broadcasted_iota(I32, (8, tr), 0)
    gl = jnp.where(grow < N_GROUPS, logits[0:8, :], NEG_BIG)
    gmax = jnp.max(gl, axis=0, keepdims=True)
    grp = jnp.min(jnp.where(gl == gmax, grow, 8), axis=0, keepdims=True)
    p_grp = 1.0 / jnp.sum(jnp.exp(gl - gmax), axis=0, keepdims=True)
    el = logits[8:ROUTER_ROWS, :]
    erow = lax.broadcasted_iota(I32, (N_EXPERTS, tr), 0)
    in_grp = (erow // EXPERTS_PER_GROUP) == grp
    l_a = jnp.where(in_grp, el, NEG_BIG)
    l1 = jnp.max(l_a, axis=0, keepdims=True)
    i1 = jnp.min(jnp.where(l_a == l1, erow, N_EXPERTS), axis=0, keepdims=True)
    l_b = jnp.where(erow == i1, NEG_BIG, l_a)
    l2 = jnp.max(l_b, axis=0, keepdims=True)
    i2 = jnp.min(jnp.where(l_b == l2, erow, N_EXPERTS), axis=0, keepdims=True)
    r21 = jnp.exp(l2 - l1)
    t1 = 1.0 / (1.0 + r21)
    gate1 = p_grp * t1
    gate2 = p_grp * (r21 * t1)
    oh1 = (erow == i1).astype(F32)
    oh2 = (erow == i2).astype(F32)
    oh = oh1 + oh2
    base = base_ref[...][:, 0:1]
    r1, r2 = [], []
    for c in range(tr // RANK_CHUNK):
        cols = slice(c * RANK_CHUNK, (c + 1) * RANK_CHUNK)
        ohc = oh[:, cols]
        before = jnp.dot(ohc.astype(BF16), tri_ref[...], preferred_element_type=F32) + base
        r1.append(jnp.sum(oh1[:, cols] * before, axis=0, keepdims=True))
        r2.append(jnp.sum(oh2[:, cols] * before, axis=0, keepdims=True))
        base = base + jnp.sum(ohc, axis=1, keepdims=True)
    rank1 = jnp.concatenate(r1, axis=1)
    rank2 = jnp.concatenate(r2, axis=1)
    new_base = jnp.broadcast_to(base, base_ref.shape)
    base_ref[...] = new_base
    cnt_ref[...] = new_base
    zi = jnp.zeros((4, tr), I32)
    ri_ref[...] = jnp.concatenate([i1, i2, rank1.astype(I32), rank2.astype(I32), zi], axis=0)
    rf_ref[...] = jnp.concatenate([gate1, gate2, jnp.zeros((LANES - 2, tr), F32)], axis=0).T


def _moe_kernel(be_ref, first_ref, wslot_ref, nxt_ref, nused_ref, x_hbm, wg_hbm, wu_hbm, wd_hbm, y_ref,
                wg_s, wu_s, wd_s, xbuf, xsem, wst_g, wst_u, wst_d, wsem):
    i = pl.program_id(0)
    nused = nused_ref[0]
    blk_rows = MOE_ROWS * TOK_SUBLANES

    def x_copy(blk):
        slot = lax.rem(blk, MOE_X_SLOTS)
        return pltpu.make_async_copy(
            x_hbm.at[pl.ds(pl.multiple_of(blk * blk_rows, blk_rows), blk_rows)],
            xbuf.at[pl.ds(pl.multiple_of(slot * blk_rows, blk_rows), blk_rows)],
            xsem.at[slot])

    @pl.when(i == 0)
    def _():
        x_copy(0).start()

        @pl.when(nused > 1)
        def _():
            x_copy(1).start()

    @pl.when(i + 2 < nused)
    def _():
        x_copy(i + 2).start()

    def w_copies(e, slot):
        return (pltpu.make_async_copy(wg_hbm.at[e], wst_g.at[pl.ds(slot * D_MODEL, D_MODEL)], wsem.at[slot]),
                pltpu.make_async_copy(wu_hbm.at[e], wst_u.at[pl.ds(slot * D_MODEL, D_MODEL)], wsem.at[slot]),
                pltpu.make_async_copy(wd_hbm.at[e], wst_d.at[pl.ds(slot * D_FF, D_FF)], wsem.at[slot]))

    @pl.when(i == 0)
    def _():
        for cp in w_copies(be_ref[0], 0):
            cp.start(priority=1)

    def new_run(_, carry):
        slot = wslot_ref[i]
        for cp in w_copies(be_ref[i], slot):
            cp.wait()
        wg_s[...] = wst_g[pl.ds(pl.multiple_of(slot * D_MODEL, D_MODEL), D_MODEL), :].astype(BF16)
        wu_s[...] = wst_u[pl.ds(pl.multiple_of(slot * D_MODEL, D_MODEL), D_MODEL), :].astype(BF16)
        wd_s[...] = wst_d[pl.ds(pl.multiple_of(slot * D_FF, D_FF), D_FF), :].astype(BF16)

        @pl.when(nxt_ref[i] >= 0)
        def _():
            for cp in w_copies(nxt_ref[i], 1 - slot):
                cp.start(priority=1)
        return carry

    lax.fori_loop(0, first_ref[i], new_run, 0)

    @pl.when(i < nused)
    def _():
        x_copy(i).wait()
        row0 = pl.multiple_of(lax.rem(i, MOE_X_SLOTS) * blk_rows, blk_rows)
        xb = jnp.concatenate([c.astype(BF16) for c in _load_packed_rows(xbuf, MOE_ROWS, row0)], axis=1)
        g = jnp.dot(xb, wg_s[...], preferred_element_type=F32)
        u = jnp.dot(xb, wu_s[...], preferred_element_type=F32)
        hid = (g * jax.nn.sigmoid(g) * u).astype(BF16)
        _store_packed_rows(y_ref, jnp.dot(hid, wd_s[...], preferred_element_type=F32), MOE_ROWS)

    @pl.when(i >= nused)
    def _():
        y_ref[...] = jnp.zeros_like(y_ref)


def _combine_kernel(x1_ref, gate_ref, y0_ref, y1_ref, o_ref, *, tm):
    g = gate_ref[...]
    g0, g1 = g[:, 0:1], g[:, 1:2]
    y0 = _load_packed_rows(y0_ref, tm)
    y1 = _load_packed_rows(y1_ref, tm)
    for c in range(2 * TOK_SUBLANES):
        cols = slice(c * LANES, (c + 1) * LANES)
        o_ref[:, cols] = x1_ref[:, cols] + g0 * y0[c] + g1 * y1[c]


def _sc_windows(n_tokens):
    info = plsc.get_sparse_core_info()
    n_workers = info.num_cores * info.num_subcores
    assert n_tokens % (n_workers * SC_WINDOW * 2) == 0
    return info.num_cores, n_tokens // (n_workers * SC_WINDOW)


def _sc_dispatch(h3, d0, d1, n_rows):
    t = h3.shape[0]
    n_cores, nwin = _sc_windows(t)
    mesh = plsc.VectorSubcoreMesh(core_axis_name="c", subcore_axis_name="s")
    rows = pltpu.VMEM((SC_WINDOW,) + h3.shape[1:], h3.dtype)
    idx = pltpu.VMEM((nwin, SC_WINDOW), I32)

    @functools.partial(
        pl.kernel, mesh=mesh,
        out_type=jax.ShapeDtypeStruct((n_rows,) + h3.shape[1:], h3.dtype),
        scratch_types=[idx, idx, idx, rows, rows] + [pltpu.SemaphoreType.DMA] * 3,
        name="sc_dispatch")
    def run(h_hbm, d0_hbm, d1_hbm, it_hbm, o_hbm, i0, i1, it, buf_a, buf_b, sem_a, sem_b, sem_s):
        wbase = (lax.axis_index("s") * n_cores + lax.axis_index("c")) * nwin
        pltpu.sync_copy(d0_hbm.at[pl.ds(wbase, nwin)], i0)
        pltpu.sync_copy(d1_hbm.at[pl.ds(wbase, nwin)], i1)
        pltpu.sync_copy(it_hbm.at[pl.ds(wbase, nwin)], it)

        @pl.loop(0, nwin, step=2)
        def _(w):
            ga = pltpu.async_copy(h_hbm.at[it.at[w]], buf_a, sem_a)
            gb = pltpu.async_copy(h_hbm.at[it.at[w + 1]], buf_b, sem_b)
            ga.wait()
            s0 = pltpu.async_copy(buf_a, o_hbm.at[i0.at[w]], sem_s)
            s1 = pltpu.async_copy(buf_a, o_hbm.at[i1.at[w]], sem_s)
            gb.wait()
            s2 = pltpu.async_copy(buf_b, o_hbm.at[i0.at[w + 1]], sem_s)
            s3 = pltpu.async_copy(buf_b, o_hbm.at[i1.at[w + 1]], sem_s)
            s0.wait()
            s1.wait()
            s2.wait()
            s3.wait()

    win = lambda a: a.reshape(-1, SC_WINDOW)
    return run(h3, win(d0), win(d1), win(jnp.arange(t, dtype=I32)))


def _sc_gather(y3, d0, d1):
    t = d0.shape[0]
    n_cores, nwin = _sc_windows(t)
    mesh = plsc.VectorSubcoreMesh(core_axis_name="c", subcore_axis_name="s")
    rows = pltpu.VMEM((SC_WINDOW,) + y3.shape[1:], y3.dtype)
    idx = pltpu.VMEM((nwin, SC_WINDOW), I32)

    @functools.partial(
        pl.kernel, mesh=mesh,
        out_type=jax.ShapeDtypeStruct((2 * t,) + y3.shape[1:], y3.dtype),
        scratch_types=[idx] * 4 + [rows] * 4 + [pltpu.SemaphoreType.DMA] * 5,
        name="sc_gather")
    def run(y_hbm, d0_hbm, d1_hbm, ita_hbm, itb_hbm, o_hbm, i0, i1, ita, itb,
            b0, b1, b2, b3, m0, m1, m2, m3, ms):
        wbase = (lax.axis_index("s") * n_cores + lax.axis_index("c")) * nwin
        pltpu.sync_copy(d0_hbm.at[pl.ds(wbase, nwin)], i0)
        pltpu.sync_copy(d1_hbm.at[pl.ds(wbase, nwin)], i1)
        pltpu.sync_copy(ita_hbm.at[pl.ds(wbase, nwin)], ita)
        pltpu.sync_copy(itb_hbm.at[pl.ds(wbase, nwin)], itb)

        @pl.loop(0, nwin, step=2)
        def _(w):
            g0 = pltpu.async_copy(y_hbm.at[i0.at[w]], b0, m0)
            g1 = pltpu.async_copy(y_hbm.at[i1.at[w]], b1, m1)
            g2 = pltpu.async_copy(y_hbm.at[i0.at[w + 1]], b2, m2)
            g3 = pltpu.async_copy(y_hbm.at[i1.at[w + 1]], b3, m3)
            g0.wait()
            s0 = pltpu.async_copy(b0, o_hbm.at[ita.at[w]], ms)
            g1.wait()
            s1 = pltpu.async_copy(b1, o_hbm.at[itb.at[w]], ms)
            g2.wait()
            s2 = pltpu.async_copy(b2, o_hbm.at[ita.at[w + 1]], ms)
            g3.wait()
            s3 = pltpu.async_copy(b3, o_hbm.at[itb.at[w + 1]], ms)
            s0.wait()
            s1.wait()
            s2.wait()
            s3.wait()

    win = lambda a: a.reshape(-1, SC_WINDOW)
    iota = jnp.arange(2 * t, dtype=I32)
    return run(y3, win(d0), win(d1), win(iota[:t]), win(iota[t:]))


def _rope_lane_tables(seq_len):
    half = ROT_DIM // 2
    inv_freq = ROPE_THETA ** (-np.arange(0, ROT_DIM, 2, dtype=np.float64) / ROT_DIM)
    ang = np.arange(seq_len, dtype=np.float64)[:, None] * inv_freq[None, :]
    cos, sin = np.cos(ang), np.sin(ang)
    m = np.arange(LANES) % HEAD_DIM
    f = m % half
    c = np.where(m[None, :] < ROT_DIM, cos[:, f], 1.0)
    s1 = np.where(m[None, :] < half, -sin[:, f], 0.0)
    s2 = np.where((m[None, :] >= half) & (m[None, :] < ROT_DIM), sin[:, f], 0.0)
    return tuple(jnp.asarray(a, F32) for a in (c, s1, s2))


def _layer(x, norm1_g, w_in, q_norm_g, k_norm_g, attn_sink, conv_w, conv_b, w_attn_proj,
           w_conv_proj, w_out, norm2_g, w_rg, b_rg, w_re, b_re, w_gate, w_up, w_down):
    bsz, seq, d = x.shape
    t = bsz * seq
    tm = min(512, seq)
    tq = min(1024, seq)
    assert d == D_MODEL and seq % tq == 0 and seq % tm == 0 and tq % tm == 0 and tq % WINDOW == 0
    xf = x.reshape(t, d)

    hpg = N_HEADS // N_KV_HEADS
    wq_p = (w_in[:, :ATTN_DIM].reshape(d, N_KV_HEADS, hpg, HEAD_DIM).transpose(0, 2, 1, 3)
            .reshape(d, ATTN_DIM))
    wq_p = wq_p.astype(BF16)
    w_in_b = w_in.astype(BF16)
    wa_p = (w_attn_proj.reshape(N_KV_HEADS, hpg, HEAD_DIM, d).transpose(1, 0, 2, 3)
            .reshape(ATTN_DIM, d).astype(BF16))
    bd = jnp.asarray(np.kron(np.eye(N_HEADS), np.full((HEAD_DIM, HEAD_DIM), 1.0 / HEAD_DIM)), BF16)
    qg = jnp.tile(q_norm_g, LANES // HEAD_DIM).reshape(1, LANES)
    kg = jnp.tile(k_norm_g, LANES // HEAD_DIM).reshape(1, LANES)
    ctab, s1tab, s2tab = _rope_lane_tables(seq)
    nseq_t = seq // tm

    row = lambda w: pl.BlockSpec((tm, w), lambda i: (i, 0))
    full = lambda a: pl.BlockSpec(a.shape, lambda i: (0,) * a.ndim)
    tab = pl.BlockSpec((tm, LANES), lambda i: (i % nseq_t, 0))
    cbias = conv_b.reshape(1, CONV_DIM)
    q, k, v, cb, cp, ue, ga, gc = pl.pallas_call(
        _proj_kernel,
        grid=(t // tm,),
        in_specs=[row(d), full(norm1_g.reshape(1, d)), full(w_in_b), full(wq_p), full(bd), full(qg), full(kg),
                  tab, tab, tab, full(conv_w), full(cbias)],
        out_specs=[row(ATTN_DIM), row(KV_DIM), row(KV_DIM), row(CONV_DIM), row(CONV_DIM),
                   pl.BlockSpec((16, CONV_DIM), lambda i: (i, 0)), row(D_MODEL), row(D_MODEL)],
        out_shape=[jax.ShapeDtypeStruct((t, w), BF16) for w in (ATTN_DIM, KV_DIM, KV_DIM, CONV_DIM, CONV_DIM)]
                  + [jax.ShapeDtypeStruct((t // tm * 16, CONV_DIM), BF16),
                     jax.ShapeDtypeStruct((t, D_MODEL), BF16), jax.ShapeDtypeStruct((t, D_MODEL), BF16)],
        scratch_shapes=[pltpu.VMEM((tm + 16, CONV_DIM), F32)],
        compiler_params=pltpu.CompilerParams(dimension_semantics=("arbitrary",),
                                             vmem_limit_bytes=VMEM_LIMIT),
        name="proj",
    )(xf, norm1_g.reshape(1, d), w_in_b, wq_p, bd, qg, kg, ctab, s1tab, s2tab, conv_w, cbias)

    wr = jnp.concatenate([w_rg, jnp.zeros((d, 8 - N_GROUPS), F32), w_re,
                          jnp.zeros((d, LANES - ROUTER_ROWS), F32)], axis=1)
    wr_hi = wr.astype(BF16)
    wr2 = jnp.concatenate([wr_hi, (wr - wr_hi.astype(F32)).astype(BF16)], axis=1)
    br = jnp.concatenate([b_rg, jnp.zeros((8 - N_GROUPS,), F32), b_re,
                          jnp.zeros((LANES - ROUTER_ROWS,), F32)]).reshape(1, LANES)
    mix_params = (conv_w, wa_p, w_conv_proj.astype(BF16), w_out.astype(BF16),
                  norm2_g.reshape(1, d), wr2, br)

    out = _mixers_and_moe(xf, (q, k, v, cb, cp, ue, ga, gc), attn_sink, mix_params,
                          (w_gate, w_up, w_down), bsz, seq, tq, tm)
    return out.reshape(bsz, seq, d)


def _mixers_and_moe(xf, acts, attn_sink, mix_params, expert_w, bsz, seq, tq, tm):
    q, k, v, cb, cp, ue, ga, gc = acts
    d = D_MODEL
    t = bsz * seq
    nj = seq // tq
    wpb = tq // WINDOW
    nb_seq = seq // WINDOW
    tpt = tq // tm
    n_edge = t // tm
    row_in = lambda w: pl.BlockSpec((tq, w), lambda b, j: (b * nj + j, 0))
    row_out = row_in
    prev_blk = pl.BlockSpec((WINDOW, KV_DIM),
                            lambda b, j: (b * nb_seq + jnp.maximum(j * wpb - 1, 0), 0))
    next_blk = pl.BlockSpec((WINDOW, KV_DIM),
                            lambda b, j: (b * nb_seq + jnp.minimum((j + 1) * wpb, nb_seq - 1), 0))
    edge_prev = pl.BlockSpec((16, CONV_DIM),
                             lambda b, j: (jnp.maximum((b * nj + j) * tpt - 1, 0), 0))
    edge_cur = pl.BlockSpec((16 * tpt, CONV_DIM), lambda b, j: (b * nj + j, 0))
    edge_next = pl.BlockSpec((16, CONV_DIM),
                             lambda b, j: (jnp.minimum((b * nj + j + 1) * tpt, n_edge - 1), 0))
    full2 = lambda a: pl.BlockSpec(a.shape, lambda b, j: (0,) * a.ndim)

    x1, h2, logits = pl.pallas_call(
        functools.partial(_mix_kernel, tq=tq, tm=tm),
        grid=(bsz, nj),
        in_specs=[pl.BlockSpec(memory_space=pltpu.SMEM),
                  row_in(d), row_in(ATTN_DIM), prev_blk, row_in(KV_DIM), next_blk,
                  prev_blk, row_in(KV_DIM), next_blk,
                  row_in(CONV_DIM), row_in(CONV_DIM), edge_prev, edge_cur, edge_next,
                  row_in(D_MODEL), row_in(D_MODEL)]
                 + [full2(p) for p in mix_params],
        out_specs=[row_out(d), pl.BlockSpec((tq * TOK_SUBLANES, LANES), lambda b, j: (b * nj + j, 0)),
                   row_out(LANES)],
        out_shape=[jax.ShapeDtypeStruct((t, d), F32),
                   jax.ShapeDtypeStruct((t * TOK_SUBLANES, LANES), U32),
                   jax.ShapeDtypeStruct((t, LANES), F32)],
        scratch_shapes=[pltpu.VMEM((tq, CONV_DIM), F32), pltpu.VMEM((tq, ATTN_DIM), BF16)],
        compiler_params=pltpu.CompilerParams(dimension_semantics=("arbitrary", "arbitrary"),
                                             vmem_limit_bytes=VMEM_LIMIT),
        name="mix",
    )(attn_sink, xf, q, k, k, k, v, v, v, cb, cp, ue, ue, ue, ga, gc, *mix_params)

    tr = min(ROUTE_TILE, t)
    assert t % tr == 0 and tr % RANK_CHUNK == 0
    tri = jnp.asarray(np.triu(np.ones((RANK_CHUNK, RANK_CHUNK), np.float32), 1), BF16)
    colr = lambda r: pl.BlockSpec((r, tr), lambda i: (0, i))
    route_i, route_f, counts = pl.pallas_call(
        functools.partial(_route_kernel, tr=tr),
        grid=(t // tr,),
        in_specs=[pl.BlockSpec((tr, LANES), lambda i: (i, 0)), pl.BlockSpec(tri.shape, lambda i: (0, 0))],
        out_specs=[colr(8), pl.BlockSpec((tr, LANES), lambda i: (i, 0)),
                   pl.BlockSpec((N_EXPERTS, LANES), lambda i: (0, 0))],
        out_shape=[jax.ShapeDtypeStruct((8, t), I32), jax.ShapeDtypeStruct((t, LANES), F32),
                   jax.ShapeDtypeStruct((N_EXPERTS, LANES), F32)],
        scratch_shapes=[pltpu.VMEM((N_EXPERTS, LANES), F32)],
        compiler_params=pltpu.CompilerParams(dimension_semantics=("arbitrary",),
                                             vmem_limit_bytes=VMEM_LIMIT),
        name="route",
    )(logits, tri)

    n_rows = 2 * t + N_EXPERTS * MOE_ROWS
    nblk = n_rows // MOE_ROWS
    cnt = counts[:, 0].astype(I32)
    padded = ((cnt + MOE_ROWS - 1) // MOE_ROWS) * MOE_ROWS
    pad_ends = jnp.cumsum(padded)
    pad_starts = pad_ends - padded
    eids = jnp.arange(N_EXPERTS, dtype=I32)[:, None, None]
    dest = jnp.sum(jnp.where(route_i[None, 0:2] == eids, pad_starts[:, None, None], 0), axis=0) \
        + route_i[2:4]
    blk_start = jnp.arange(nblk, dtype=I32) * MOE_ROWS
    block_e = jnp.minimum(jnp.sum((pad_ends[None, :] <= blk_start[:, None]).astype(I32), axis=1),
                          N_EXPERTS - 1)
    nused = (pad_ends[-1:] // MOE_ROWS).astype(I32)

    x_pad = _sc_dispatch(h2.reshape(t, TOK_SUBLANES, LANES), dest[0], dest[1], n_rows)
    x_pad = x_pad.reshape(n_rows * TOK_SUBLANES, LANES)

    blk = jnp.arange(nblk, dtype=I32)
    first = (blk < nused[0]) & (block_e != jnp.concatenate([jnp.full((1,), -1, I32), block_e[:-1]]))
    wslot = ((jnp.cumsum(first.astype(I32)) - 1) % 2).astype(I32)
    later_first = first[None, :] & (blk[None, :] > blk[:, None])
    nxt = jnp.where(jnp.any(later_first, axis=1), block_e[jnp.argmax(later_first, axis=1)], -1).astype(I32)

    w_gate, w_up, w_down = expert_w
    blk_rows = MOE_ROWS * TOK_SUBLANES
    hbm = pl.BlockSpec(memory_space=pl.ANY)
    y_pad = pl.pallas_call(
        _moe_kernel,
        grid_spec=pltpu.PrefetchScalarGridSpec(
            num_scalar_prefetch=5,
            grid=(nblk,),
            in_specs=[hbm, hbm, hbm, hbm],
            out_specs=pl.BlockSpec((blk_rows, LANES),
                                   lambda i, be, fi, ws, nx, nu: (jnp.where(i < nu[0], i, nblk - 1), 0)),
            scratch_shapes=[pltpu.VMEM((d, D_FF), BF16), pltpu.VMEM((d, D_FF), BF16),
                            pltpu.VMEM((D_FF, d), BF16),
                            pltpu.VMEM((MOE_X_SLOTS * blk_rows, LANES), U32),
                            pltpu.SemaphoreType.DMA((MOE_X_SLOTS,)),
                            pltpu.VMEM((2 * d, D_FF), F32), pltpu.VMEM((2 * d, D_FF), F32),
                            pltpu.VMEM((2 * D_FF, d), F32), pltpu.SemaphoreType.DMA((2,))]),
        out_shape=jax.ShapeDtypeStruct((n_rows * TOK_SUBLANES, LANES), U32),
        compiler_params=pltpu.CompilerParams(dimension_semantics=("arbitrary",),
                                             vmem_limit_bytes=VMEM_LIMIT),
        name="moe",
    )(block_e, first.astype(I32), wslot, nxt, nused, x_pad, w_gate, w_up, w_down)

    yk = _sc_gather(y_pad.reshape(n_rows, TOK_SUBLANES, LANES), dest[0], dest[1])
    yk = yk.reshape(2 * t * TOK_SUBLANES, LANES)
    tc = min(512, t)
    nt = t // tc
    return pl.pallas_call(
        functools.partial(_combine_kernel, tm=tc),
        grid=(nt,),
        in_specs=[pl.BlockSpec((tc, d), lambda i: (i, 0)),
                  pl.BlockSpec((tc, LANES), lambda i: (i, 0)),
                  pl.BlockSpec((tc * TOK_SUBLANES, LANES), lambda i: (i, 0)),
                  pl.BlockSpec((tc * TOK_SUBLANES, LANES), lambda i: (nt + i, 0))],
        out_specs=pl.BlockSpec((tc, d), lambda i: (i, 0)),
        out_shape=jax.ShapeDtypeStruct((t, d), F32),
        compiler_params=pltpu.CompilerParams(dimension_semantics=("arbitrary",),
                                             vmem_limit_bytes=VMEM_LIMIT),
        name="combine",
    )(x1, route_f, yk, yk)


def kernel(x, norm1_g, w_in, q_norm_g, k_norm_g, attn_sink, conv_w, conv_b, w_attn_proj, w_conv_proj, w_out, norm2_g, w_router_group, b_router_group, w_router_expert, b_router_expert, w_gate_e, w_up_e, w_down_e):
    for l in range(norm1_g.shape[0]):
        x = _layer(x, norm1_g[l], w_in[l], q_norm_g[l], k_norm_g[l], attn_sink[l], conv_w[l],
                   conv_b[l], w_attn_proj[l], w_conv_proj[l], w_out[l], norm2_g[l],
                   w_router_group[l], b_router_group[l], w_router_expert[l], b_router_expert[l],
                   w_gate_e[l], w_up_e[l], w_down_e[l])
    return x
```
